```python
import math
import jax, jax.numpy as jnp
from jax import lax
import numpy as np

D_MODEL = 2048
BATCH = 8
SEQ = 2048
DEPTH = 2

HEAD_DIM = 64
H_FOX = 8
H_SB = 8
H_DSA = 8
W_FOX = H_FOX * HEAD_DIM
W_SB = H_SB * HEAD_DIM
W_DSA = H_DSA * HEAD_DIM
KV_RANK = 256
IDX_HEADS = 4
IDX_DIM = 64
TOPK_MAX = 256
N_BUCKETS = 32
MAX_DISTANCE = 128
Q_BLOCK = 128
D_FF = 7168
N_EXPERTS = 8
TOP_K_EXPERTS = 2
N_BRANCH = 3
EPS = 1e-6
N_DENSE = (DEPTH + 1) // 2
N_MOE = DEPTH // 2
SPLITS = (W_FOX, W_FOX, W_FOX, H_FOX, W_FOX,
          W_SB, W_SB, W_SB,
          W_DSA, KV_RANK, IDX_HEADS * IDX_DIM, IDX_DIM, IDX_HEADS,
          N_BRANCH * D_MODEL)
N_IN = sum(SPLITS)

kernel_name = "hybrid_fox_stickbreak_dsa_moe"


def rms_norm(x, g):
    xf = x.astype(jnp.float32)
    y = xf * lax.rsqrt(jnp.mean(xf * xf, axis=-1, keepdims=True) + EPS)
    return (y * g.astype(jnp.float32)).astype(x.dtype)


def t5_bucket(n):
    max_exact = N_BUCKETS // 2
    nf = jnp.maximum(n, 1).astype(jnp.float32)
    large = max_exact + (jnp.log(nf / max_exact) / math.log(MAX_DISTANCE / max_exact)
                         * (N_BUCKETS - max_exact)).astype(jnp.int32)
    large = jnp.minimum(large, N_BUCKETS - 1)
    return jnp.where(n < max_exact, n, large)


def to_blocks(a):
    b, s = a.shape[:2]
    a = a.reshape(b, s // Q_BLOCK, Q_BLOCK, *a.shape[2:])
    return jnp.moveaxis(a, 1, 0)


def from_blocks(o):
    o = jnp.moveaxis(o, 0, 1)
    return o.reshape(o.shape[0], -1, *o.shape[3:])


def block_starts(s):
    return jnp.arange(s // Q_BLOCK, dtype=jnp.int32) * Q_BLOCK


def forgetting_attention(q, k, v, log_f):
    s_len = q.shape[1]
    scale = HEAD_DIM ** -0.5
    c = jnp.cumsum(log_f, axis=1)
    c_k = jnp.transpose(c, (0, 2, 1))
    k_pos = jnp.arange(s_len)

    def block(args):
        qb, cb, start = args
        q_pos = start + jnp.arange(Q_BLOCK)
        logit = jnp.einsum('bqhd,bshd->bhqs', qb, k, preferred_element_type=jnp.float32) * scale
        logit = logit + jnp.transpose(cb, (0, 2, 1))[..., None] - c_k[:, :, None, :]
        logit = jnp.where(k_pos[None, :] <= q_pos[:, None], logit, -jnp.inf)
        p = jax.nn.softmax(logit, axis=-1)
        return jnp.einsum('bhqs,bshd->bqhd', p.astype(v.dtype), v)

    out = lax.map(block, (to_blocks(q), to_blocks(c), block_starts(s_len)))
    return from_blocks(out)


def stick_breaking_attention(q, k, v):
    s_len = q.shape[1]
    scale = HEAD_DIM ** -0.5
    k_pos = jnp.arange(s_len)

    def block(args):
        qb, start = args
        q_pos = start + jnp.arange(Q_BLOCK)
        z = jnp.einsum('bqhd,bshd->bhqs', qb, k, preferred_element_type=jnp.float32) * scale
        strict = k_pos[None, :] < q_pos[:, None]
        log_1m = jnp.where(strict, jax.nn.log_sigmoid(-z), 0.0)
        suffix = lax.cumsum(log_1m, axis=3, reverse=True) - log_1m
        a = jnp.where(strict, jnp.exp(jax.nn.log_sigmoid(z) + suffix), 0.0)
        return jnp.einsum('bhqs,bshd->bqhd', a.astype(v.dtype), v)

    out = lax.map(block, (to_blocks(q), block_starts(s_len)))
    return from_blocks(out)


def dsa_attention(q, k, v, q_idx, k_idx, w_idx, rel_bias):
    s_len = q.shape[1]
    n_sel = min(TOPK_MAX, s_len // 4)
    scale = HEAD_DIM ** -0.5
    idx_scale = (IDX_DIM ** -0.5) * (IDX_HEADS ** -0.5)
    k_pos = jnp.arange(s_len)

    def block(args):
        qb, qib, wb, start = args
        q_pos = start + jnp.arange(Q_BLOCK)
        dots = jnp.einsum('bqhd,bsd->bqhs', qib, k_idx, preferred_element_type=jnp.float32)
        score = jnp.einsum('bqhs,bqh->bqs', jax.nn.relu(dots), wb.astype(jnp.float32) * idx_scale)
        score = jnp.where((k_pos[None, :] <= q_pos[:, None])[None], score, -jnp.inf)
        _, sel = lax.top_k(score, n_sel)
        gather = jax.vmap(lambda kb, ib: kb[ib])
        k_sel = gather(k, sel)
        v_sel = gather(v, sel)
        dist = q_pos[None, :, None] - sel
        valid = dist >= 0
        bias = rel_bias[t5_bucket(jnp.maximum(dist, 0))]
        logit = jnp.einsum('bqhd,bqkhd->bhqk', qb, k_sel, preferred_element_type=jnp.float32) * scale
        logit = logit + jnp.transpose(bias, (0, 3, 1, 2)).astype(jnp.float32)
        logit = jnp.where(valid[:, None], logit, -jnp.inf)
        p = jax.nn.softmax(logit, axis=-1)
        return jnp.einsum('bhqk,bqkhd->bqhd', p.astype(v.dtype), v_sel)

    out = lax.map(block, (to_blocks(q), to_blocks(q_idx), to_blocks(w_idx), block_starts(s_len)))
    return from_blocks(out)


def hybrid_mixer(h, w_in, b_forget, q_norm, k_norm, kv_norm, w_ukv,
                 w_br_fox, w_br_sb, w_br_dsa, w_out, rel_bias):
    b, s, _ = h.shape
    z = h @ w_in
    offs, acc = [], 0
    for w in SPLITS[:-1]:
        acc += w
        offs.append(acc)
    (fq, fk, fv, ff, fo, sq, sk, sv, dq, dlat, diq, dik, diw, gates) = jnp.split(z, offs, axis=-1)
    heads = lambda t, n: t.reshape(b, s, n, HEAD_DIM)

    log_f = jax.nn.log_sigmoid(ff.astype(jnp.float32) + b_forget.astype(jnp.float32))
    y_fox = forgetting_attention(rms_norm(heads(fq, H_FOX), q_norm), rms_norm(heads(fk, H_FOX), k_norm),
                                 heads(fv, H_FOX), log_f)
    y_fox = y_fox.reshape(b, s, W_FOX) * jax.nn.sigmoid(fo)

    y_sb = stick_breaking_attention(heads(sq, H_SB), heads(sk, H_SB), heads(sv, H_SB)).reshape(b, s, W_SB)

    dk, dv = jnp.split(rms_norm(dlat, kv_norm) @ w_ukv, 2, axis=-1)
    y_dsa = dsa_attention(heads(dq, H_DSA), heads(dk, H_DSA), heads(dv, H_DSA),
                          diq.reshape(b, s, IDX_HEADS, IDX_DIM), dik, diw, rel_bias).reshape(b, s, W_DSA)

    g = jax.nn.sigmoid(gates).reshape(b, s, N_BRANCH, D_MODEL)
    m = (g[:, :, 0] * (y_fox @ w_br_fox) + g[:, :, 1] * (y_sb @ w_br_sb)
         + g[:, :, 2] * (y_dsa @ w_br_dsa))
    return m @ w_out


def swiglu(h, wg, wu, wd):
    return (jax.nn.silu(h @ wg) * (h @ wu)) @ wd


def moe_swiglu(h, w_router, wg, wu, wd):
    b, s, d = h.shape
    t = h.reshape(-1, d)
    logits = (t @ w_router).astype(jnp.float32)
    top_v, top_i = lax.top_k(logits, TOP_K_EXPERTS)
    top_w = jax.nn.softmax(top_v, axis=-1)
    comb = jnp.sum(jax.nn.one_hot(top_i, N_EXPERTS, dtype=jnp.float32) * top_w[..., None], axis=1)
    y = jnp.zeros_like(t)
    for e in range(N_EXPERTS):
        y = y + comb[:, e:e + 1].astype(t.dtype) * swiglu(t, wg[e], wu[e], wd[e])
    return y.reshape(b, s, d)


def setup_inputs(seed: int = 0) -> dict:
    key = jax.random.key(seed)
    ks = jax.random.split(key, 24)
    nrm = lambda k, shape, scale: jax.random.normal(k, shape, jnp.float32) * scale
    gain = lambda k, shape: 1.0 + nrm(k, shape, 0.02)
    return {
        "x": nrm(ks[0], (BATCH, SEQ, D_MODEL), 1.0),
        "w_in": nrm(ks[1], (DEPTH, D_MODEL, N_IN), D_MODEL ** -0.5),
        "b_forget": 2.0 + nrm(ks[2], (DEPTH, H_FOX), 0.1),
        "q_norm": gain(ks[3], (DEPTH, HEAD_DIM)),
        "k_norm": gain(ks[4], (DEPTH, HEAD_DIM)),
        "kv_norm": gain(ks[5], (DEPTH, KV_RANK)),
        "w_ukv": nrm(ks[6], (DEPTH, KV_RANK, 2 * W_DSA), KV_RANK ** -0.5),
        "w_br_fox": nrm(ks[7], (DEPTH, W_FOX, D_MODEL), W_FOX ** -0.5),
        "w_br_sb": nrm(ks[8], (DEPTH, W_SB, D_MODEL), W_SB ** -0.5),
        "w_br_dsa": nrm(ks[9], (DEPTH, W_DSA, D_MODEL), W_DSA ** -0.5),
        "w_out": nrm(ks[10], (DEPTH, D_MODEL, D_MODEL), D_MODEL ** -0.5),
        "rel_bias": nrm(ks[11], (N_BUCKETS, H_DSA), 0.5),
        "norm_mix": gain(ks[12], (DEPTH, D_MODEL)),
        "norm_ffn": gain(ks[13], (DEPTH, D_MODEL)),
        "w_ffn_gate": nrm(ks[14], (N_DENSE, D_MODEL, D_FF), D_MODEL ** -0.5),
        "w_ffn_up": nrm(ks[15], (N_DENSE, D_MODEL, D_FF), D_MODEL ** -0.5),
        "w_ffn_down": nrm(ks[16], (N_DENSE, D_FF, D_MODEL), D_FF ** -0.5),
        "w_router": nrm(ks[17], (N_MOE, D_MODEL, N_EXPERTS), D_MODEL ** -0.5),
        "w_moe_gate": nrm(ks[18], (N_MOE, N_EXPERTS, D_MODEL, D_FF), D_MODEL ** -0.5),
        "w_moe_up": nrm(ks[19], (N_MOE, N_EXPERTS, D_MODEL, D_FF), D_MODEL ** -0.5),
        "w_moe_down": nrm(ks[20], (N_MOE, N_EXPERTS, D_FF, D_MODEL), D_FF ** -0.5),
        "norm_final": gain(ks[21], (D_MODEL,)),
    }


def reference(x, w_in, b_forget, q_norm, k_norm, kv_norm, w_ukv, w_br_fox, w_br_sb, w_br_dsa,
              w_out, rel_bias, norm_mix, norm_ffn, w_ffn_gate, w_ffn_up, w_ffn_down,
              w_router, w_moe_gate, w_moe_up, w_moe_down, norm_final):
    for l in range(DEPTH):
        h = rms_norm(x, norm_mix[l])
        x = x + hybrid_mixer(h, w_in[l], b_forget[l], q_norm[l], k_norm[l], kv_norm[l], w_ukv[l],
                             w_br_fox[l], w_br_sb[l], w_br_dsa[l], w_out[l], rel_bias)
        h = rms_norm(x, norm_ffn[l])
        if l % 2 == 0:
            j = l // 2
            x = x + swiglu(h, w_ffn_gate[j], w_ffn_up[j], w_ffn_down[j])
        else:
            j = l // 2
            x = x + moe_swiglu(h, w_router[j], w_moe_gate[j], w_moe_up[j], w_moe_down[j])
    return rms_norm(x, norm_final)
```

```python
import functools
import math

import jax
import jax.numpy as jnp
import numpy as np
from jax import lax
from jax.experimental import pallas as pl
from jax.experimental.pallas import tpu as pltpu

F32 = jnp.float32
BF16 = jnp.bfloat16

HEAD_DIM = 64
N_HEADS = 8
W_ATT = N_HEADS * HEAD_DIM
KV_RANK = 256
IDX_HEADS = 4
IDX_DIM = 64
TOPK_MAX = 256
N_BUCKETS = 32
MAX_DISTANCE = 128
N_EXPERTS = 8
N_BRANCH = 3
EPS = 1e-6
LANES = 128
NEG = -1e30
INT_MIN = -2147483648

U_FQ, U_FK, U_FV, U_FO, U_SQ, U_SK, U_SV, U_DQ = 0, 4, 8, 12, 16, 20, 24, 28
U_DLAT, U_DIQ, U_DIK, U_SMALL, U_GATES = 32, 34, 36, 37, 40
N_UNITS = 88
Z_COLS = N_UNITS * LANES
SM_FF, SM_DIW = 0, 8

VMEM_LIMIT = 56 * 1024 * 1024


def _cparams(sem):
    return pltpu.CompilerParams(dimension_semantics=sem, vmem_limit_bytes=VMEM_LIMIT)


def _log_sigmoid(x):
    return jnp.minimum(x, 0.0) - jnp.log1p(jnp.exp(-jnp.abs(x)))


def _rmsnorm_kernel(x_ref, g_ref, o_ref):
    x = x_ref[...]
    ms = jnp.mean(x * x, axis=-1, keepdims=True)
    o_ref[...] = (x * lax.rsqrt(ms + EPS) * g_ref[...]).astype(o_ref.dtype)


def _rmsnorm(x, g, out_dtype, tm):
    m, d = x.shape
    return pl.pallas_call(
        _rmsnorm_kernel,
        grid=(m // tm,),
        in_specs=[pl.BlockSpec((tm, d), lambda i: (i, 0)), pl.BlockSpec((1, d), lambda i: (0, 0))],
        out_specs=pl.BlockSpec((tm, d), lambda i: (i, 0)),
        out_shape=jax.ShapeDtypeStruct((m, d), out_dtype),
        compiler_params=_cparams(("parallel",)),
        name="rmsnorm",
    )(x, g.reshape(1, d))


def _rmsnorm_router_kernel(x_ref, g_ref, wr_ref, o_ref, info_ref):
    x = x_ref[...]
    ms = jnp.mean(x * x, axis=-1, keepdims=True)
    h = x * lax.rsqrt(ms + EPS) * g_ref[...]
    o_ref[...] = h.astype(o_ref.dtype)
    logits = jnp.dot(h, wr_ref[...], precision=lax.Precision.HIGHEST, preferred_element_type=F32)
    lane = lax.broadcasted_iota(jnp.int32, logits.shape, 1).astype(F32)
    lg = jnp.where(lane < N_EXPERTS, logits, -jnp.inf)
    v1 = jnp.max(lg, axis=-1, keepdims=True)
    i1 = jnp.min(jnp.where(lg == v1, lane, float(LANES)), axis=-1, keepdims=True)
    lg2 = jnp.where(lane == i1, -jnp.inf, lg)
    v2 = jnp.max(lg2, axis=-1, keepdims=True)
    i2 = jnp.min(jnp.where(lg2 == v2, lane, float(LANES)), axis=-1, keepdims=True)
    e2 = jnp.exp(v2 - v1)
    w1 = 1.0 / (1.0 + e2)
    w2 = e2 / (1.0 + e2)
    info = jnp.where(lane == 0.0, w1, jnp.where(lane == 1.0, w2, jnp.where(
        lane == 2.0, i1, jnp.where(lane == 3.0, i2, 0.0))))
    info_ref[...] = info


def _rmsnorm_router(x, g, w_router, tm):
    m, d = x.shape
    wr = jnp.pad(w_router, ((0, 0), (0, LANES - w_router.shape[1])))
    return pl.pallas_call(
        _rmsnorm_router_kernel,
        grid=(m // tm,),
        in_specs=[pl.BlockSpec((tm, d), lambda i: (i, 0)), pl.BlockSpec((1, d), lambda i: (0, 0)),
                  pl.BlockSpec((d, LANES), lambda i: (0, 0))],
        out_specs=[pl.BlockSpec((tm, d), lambda i: (i, 0)), pl.BlockSpec((tm, LANES), lambda i: (i, 0))],
        out_shape=[jax.ShapeDtypeStruct((m, d), BF16), jax.ShapeDtypeStruct((m, LANES), F32)],
        compiler_params=_cparams(("parallel",)),
        name="rmsnorm_router",
    )(x, g.reshape(1, d), wr)


def _inproj_kernel(a_ref, w_ref, z_ref, zs_ref, *, small_tile):
    acc = jnp.dot(a_ref[...], w_ref[...], preferred_element_type=F32)
    z_ref[...] = acc.astype(z_ref.dtype)

    @pl.when(pl.program_id(1) == small_tile)
    def _():
        off = (U_SMALL * LANES) % acc.shape[1]
        zs_ref[...] = acc[:, off:off + LANES]


def _inproj(h, w, tm, tn):
    m, d = h.shape
    return pl.pallas_call(
        functools.partial(_inproj_kernel, small_tile=(U_SMALL * LANES) // tn),
        grid=(m // tm, Z_COLS // tn),
        in_specs=[pl.BlockSpec((tm, d), lambda i, j: (i, 0)), pl.BlockSpec((d, tn), lambda i, j: (0, j))],
        out_specs=[pl.BlockSpec((tm, tn), lambda i, j: (i, j)), pl.BlockSpec((tm, LANES), lambda i, j: (i, 0))],
        out_shape=[jax.ShapeDtypeStruct((m, Z_COLS), BF16), jax.ShapeDtypeStruct((m, LANES), F32)],
        compiler_params=_cparams(("parallel", "arbitrary")),
        name="inproj",
    )(h, w)


def _relayout_w_in(w):
    d = w.shape[0]
    o = 0
    parts = {}
    for name, width in (("fq", W_ATT), ("fk", W_ATT), ("fv", W_ATT), ("ff", N_HEADS), ("fo", W_ATT),
                        ("sq", W_ATT), ("sk", W_ATT), ("sv", W_ATT), ("dq", W_ATT), ("dlat", KV_RANK),
                        ("diq", IDX_HEADS * IDX_DIM), ("dik", IDX_DIM), ("diw", IDX_HEADS),
                        ("gates", None)):
        width = w.shape[1] - o if width is None else width
        parts[name] = w[:, o:o + width]
        o += width
    small = jnp.concatenate(
        [parts["ff"], parts["diw"], jnp.zeros((d, LANES - N_HEADS - IDX_HEADS), w.dtype)], axis=1)
    pad = jnp.zeros((d, (U_GATES - U_SMALL - 1) * LANES), w.dtype)
    out = jnp.concatenate(
        [parts["fq"], parts["fk"], parts["fv"], parts["fo"], parts["sq"], parts["sk"], parts["sv"],
         parts["dq"], parts["dlat"], parts["diq"], parts["dik"], parts["dik"], small, pad,
         parts["gates"]], axis=1)
    return out.astype(BF16)


def _prep_kernel(zq_ref, zk_ref, zl_ref, zs_ref, gq_ref, gk_ref, gkv_ref, bf_ref, wukv_ref,
                 grp_ref, eq_ref, ek_ref, oneq_ref, onek_ref,
                 qn_ref, kn_ref, aq_ref, ak_ref, dk_ref, dv_ref, carry_ref, *, ts):
    @pl.when(pl.program_id(1) == 0)
    def _():
        carry_ref[...] = jnp.zeros_like(carry_ref)

    def head_norm(z_ref, g_ref):
        x = z_ref[0].astype(F32)
        sq = x * x
        hi = sq.astype(BF16)
        lo = (sq - hi.astype(F32)).astype(BF16)
        ms = (jnp.dot(hi, grp_ref[...], preferred_element_type=F32)
              + jnp.dot(lo, grp_ref[...], preferred_element_type=F32)) * (1.0 / HEAD_DIM)
        return x * lax.rsqrt(ms + EPS) * g_ref[...]

    qn_ref[0] = head_norm(zq_ref, gq_ref).astype(BF16)
    kn_ref[0] = head_norm(zk_ref, gk_ref).astype(BF16)

    lane = lax.broadcasted_iota(jnp.int32, (ts, LANES), 1)
    lf = jnp.where(lane < N_HEADS, _log_sigmoid(zs_ref[0] + bf_ref[...]), 0.0)
    r = lax.broadcasted_iota(jnp.int32, (ts, ts), 0)
    c_ = lax.broadcasted_iota(jnp.int32, (ts, ts), 1)
    tri = (c_ <= r).astype(F32)
    c = jnp.dot(tri, lf, precision=lax.Precision.HIGHEST, preferred_element_type=F32) + carry_ref[...]
    carry_ref[...] = c[ts - 1:ts, :]
    c0 = c.astype(BF16)
    r1 = c - c0.astype(F32)
    c1 = r1.astype(BF16)
    c2 = (r1 - c1.astype(F32)).astype(BF16)
    pieces = (c0, c1, c2)
    aq = oneq_ref[...]
    ak = onek_ref[...]
    for k in range(3):
        aq = aq + jnp.dot(pieces[k], eq_ref[k], preferred_element_type=F32)
        ak = ak - jnp.dot(pieces[k], ek_ref[k], preferred_element_type=F32)
    aq_ref[0] = aq.astype(BF16)
    ak_ref[0] = ak.astype(BF16)

    lat = zl_ref[0, :, :KV_RANK].astype(F32)
    msl = jnp.mean(lat * lat, axis=-1, keepdims=True)
    latn = (lat * lax.rsqrt(msl + EPS) * gkv_ref[...]).astype(BF16)
    kv = jnp.dot(latn, wukv_ref[...], preferred_element_type=F32)
    dk_ref[0] = kv[:, :W_ATT].astype(BF16)
    dv_ref[0] = kv[:, W_ATT:].astype(BF16)


def _aug_constants():
    eq = np.zeros((3, LANES, LANES), np.float32)
    ek = np.zeros((3, LANES, LANES), np.float32)
    oneq = np.zeros((1, LANES), np.float32)
    onek = np.zeros((1, LANES), np.float32)
    for h in range(N_HEADS):
        for k in range(3):
            eq[k, h, 8 * h + k] = 1.0
            ek[k, h, 8 * h + 3 + k] = 1.0
            oneq[0, 8 * h + 3 + k] = 1.0
            onek[0, 8 * h + k] = 1.0
    grp = np.kron(np.eye(N_HEADS, dtype=np.float32), np.ones((HEAD_DIM, HEAD_DIM), np.float32))
    return (jnp.asarray(grp, BF16), jnp.asarray(eq, BF16), jnp.asarray(ek, BF16),
            jnp.asarray(oneq), jnp.asarray(onek))


def _prep(z3, zs3, q_norm, k_norm, kv_norm, b_forget, w_ukv, ts):
    b, s, _ = z3.shape
    grp, eq, ek, oneq, onek = _aug_constants()
    gq = (jnp.tile(q_norm, N_HEADS) * HEAD_DIM ** -0.5).reshape(1, W_ATT)
    gk = jnp.tile(k_norm, N_HEADS).reshape(1, W_ATT)
    bf = jnp.pad(b_forget, (SM_FF, LANES - N_HEADS - SM_FF)).reshape(1, LANES)
    const = lambda shape: pl.BlockSpec(shape, lambda bi, si: (0,) * len(shape))
    zblk = lambda unit: pl.BlockSpec((1, ts, W_ATT), lambda bi, si: (bi, si, unit // 4))
    seq_out = lambda w: pl.BlockSpec((1, ts, w), lambda bi, si: (bi, si, 0))
    return pl.pallas_call(
        functools.partial(_prep_kernel, ts=ts),
        grid=(b, s // ts),
        in_specs=[zblk(U_FQ), zblk(U_FK), zblk(U_DLAT),
                  pl.BlockSpec((1, ts, LANES), lambda bi, si: (bi, si, 0)),
                  const((1, W_ATT)), const((1, W_ATT)), const((1, KV_RANK)), const((1, LANES)),
                  const((KV_RANK, 2 * W_ATT)), const((W_ATT, W_ATT)),
                  const((3, LANES, LANES)), const((3, LANES, LANES)), const((1, LANES)), const((1, LANES))],
        out_specs=[seq_out(W_ATT), seq_out(W_ATT), seq_out(LANES), seq_out(LANES), seq_out(W_ATT), seq_out(W_ATT)],
        out_shape=[jax.ShapeDtypeStruct((b, s, W_ATT), BF16), jax.ShapeDtypeStruct((b, s, W_ATT), BF16),
                   jax.ShapeDtypeStruct((b, s, LANES), BF16), jax.ShapeDtypeStruct((b, s, LANES), BF16),
                   jax.ShapeDtypeStruct((b, s, W_ATT), BF16), jax.ShapeDtypeStruct((b, s, W_ATT), BF16)],
        scratch_shapes=[pltpu.VMEM((1, LANES), F32)],
        compiler_params=_cparams(("parallel", "arbitrary")),
        name="mixer_prep",
    )(z3, z3, z3, zs3, gq, gk, kv_norm.reshape(1, KV_RANK), bf, w_ukv.astype(BF16), grp, eq, ek, oneq, onek)


def _nt_dot(a, b):
    return lax.dot_general(a, b, (((1,), (1,)), ((), ())), preferred_element_type=F32)


def _half_mask(shape, half):
    lane = lax.broadcasted_iota(jnp.int32, shape, 1)
    return (lane >= HEAD_DIM * half) & (lane < HEAD_DIM * (half + 1))


def _fox_kernel(q_ref, k_ref, v_ref, aq_ref, ak_ref, fo_ref, o_ref, *, t):
    pair = pl.program_id(1)
    i = pl.program_id(2)
    q = q_ref[0].astype(F32)
    aq = aq_ref[0].astype(F32)
    lane = lax.broadcasted_iota(jnp.int32, (t, LANES), 1)
    row = lax.broadcasted_iota(jnp.int32, (t, t), 0)
    col = lax.broadcasted_iota(jnp.int32, (t, t), 1)
    causal = col <= row
    outs = []
    for half in range(2):
        head = 2 * pair + half
        qm = jnp.where(_half_mask((t, LANES), half), q, 0.0).astype(BF16)
        am = jnp.where((lane >= 8 * head) & (lane < 8 * head + 6), aq, 0.0).astype(BF16)
        qc = jnp.concatenate([qm, am], axis=1)

        def step(j, carry, diag):
            m, l, acc = carry
            ks = pl.multiple_of(j * t, t)
            kc = jnp.concatenate([k_ref[0, pl.ds(ks, t), :], ak_ref[0, pl.ds(ks, t), :]], axis=1)
            s = _nt_dot(qc, kc)
            if diag:
                s = jnp.where(causal, s, NEG)
            m_new = jnp.maximum(m, jnp.max(s, axis=-1, keepdims=True))
            alpha = jnp.exp(m - m_new)
            p = jnp.exp(s - m_new)
            l = alpha * l + jnp.sum(p, axis=-1, keepdims=True)
            acc = alpha * acc + jnp.dot(p.astype(BF16), v_ref[0, pl.ds(ks, t), :], preferred_element_type=F32)
            return m_new, l, acc

        init = (jnp.full((t, 1), NEG, F32), jnp.zeros((t, 1), F32), jnp.zeros((t, LANES), F32))
        carry = lax.fori_loop(0, i, functools.partial(step, diag=False), init)
        _, l, acc = step(i, carry, True)
        outs.append(acc / l)
    o = jnp.where(lane < HEAD_DIM, outs[0], outs[1])
    o_ref[0] = (o * jax.nn.sigmoid(fo_ref[0].astype(F32))).astype(o_ref.dtype)


def _fox(qn, kn, z3, aq, ak, t):
    b, s, _ = qn.shape
    npair = N_HEADS // 2
    qblk = lambda unit: pl.BlockSpec((1, t, LANES), lambda bi, p, i: (bi, i, unit + p))
    sblk = lambda unit: pl.BlockSpec((1, s, LANES), lambda bi, p, i: (bi, 0, unit + p))
    return pl.pallas_call(
        functools.partial(_fox_kernel, t=t),
        grid=(b, npair, s // t),
        in_specs=[qblk(0), sblk(0), sblk(U_FV),
                  pl.BlockSpec((1, t, LANES), lambda bi, p, i: (bi, i, 0)),
                  pl.BlockSpec((1, s, LANES), lambda bi, p, i: (bi, 0, 0)),
                  qblk(U_FO)],
        out_specs=qblk(0),
        out_shape=jax.ShapeDtypeStruct((b, s, W_ATT), BF16),
        compiler_params=_cparams(("parallel", "parallel", "arbitrary")),
        name="fox_attention",
    )(qn, kn, z3, aq, ak, z3)


def _sb_kernel(q_ref, k_ref, v_ref, o_ref, *, t):
    i = pl.program_id(2)
    q = q_ref[0].astype(F32) * (HEAD_DIM ** -0.5)
    lane = lax.broadcasted_iota(jnp.int32, (t, LANES), 1)
    row = lax.broadcasted_iota(jnp.int32, (t, t), 0)
    col = lax.broadcasted_iota(jnp.int32, (t, t), 1)
    strict = col < row
    after = (row > col).astype(BF16)
    outs = []
    for half in range(2):
        qm = jnp.where(_half_mask((t, LANES), half), q, 0.0).astype(BF16)

        def step(j, carry, diag):
            rsum, acc = carry
            ks = pl.multiple_of(j * t, t)
            z = _nt_dot(qm, k_ref[0, pl.ds(ks, t), :])
            lz = _log_sigmoid(z)
            l1m = lz - z
            if diag:
                l1m = jnp.where(strict, l1m, 0.0)
            hi = l1m.astype(BF16)
            lo = (l1m - hi.astype(F32)).astype(BF16)
            suffix = (jnp.dot(hi, after, preferred_element_type=F32)
                      + jnp.dot(lo, after, preferred_element_type=F32)) + rsum
            a = jnp.exp(lz + suffix)
            if diag:
                a = jnp.where(strict, a, 0.0)
            acc = acc + jnp.dot(a.astype(BF16), v_ref[0, pl.ds(ks, t), :], preferred_element_type=F32)
            return rsum + jnp.sum(l1m, axis=-1, keepdims=True), acc

        carry = step(i, (jnp.zeros((t, 1), F32), jnp.zeros((t, LANES), F32)), True)
        _, acc = lax.fori_loop(0, i, lambda n, c: step(i - 1 - n, c, False), carry)
        outs.append(acc)
    o_ref[0] = jnp.where(lane < HEAD_DIM, outs[0], outs[1]).astype(o_ref.dtype)


def _sb(z3, t):
    b, s, _ = z3.shape
    npair = N_HEADS // 2
    qblk = lambda unit: pl.BlockSpec((1, t, LANES), lambda bi, p, i: (bi, i, unit + p))
    sblk = lambda unit: pl.BlockSpec((1, s, LANES), lambda bi, p, i: (bi, 0, unit + p))
    return pl.pallas_call(
        functools.partial(_sb_kernel, t=t),
        grid=(b, npair, s // t),
        in_specs=[qblk(U_SQ), sblk(U_SK), sblk(U_SV)],
        out_specs=qblk(0),
        out_shape=jax.ShapeDtypeStruct((b, s, W_ATT), BF16),
        compiler_params=_cparams(("parallel", "parallel", "arbitrary")),
        name="stickbreak_attention",
    )(z3, z3, z3)


def _t5_bucket(n):
    max_exact = N_BUCKETS // 2
    nf = jnp.maximum(n, 1).astype(F32)
    large = max_exact + (jnp.log(nf / max_exact) / math.log(MAX_DISTANCE / max_exact)
                         * (N_BUCKETS - max_exact)).astype(jnp.int32)
    large = jnp.minimum(large, N_BUCKETS - 1)
    return jnp.where(n < max_exact, n, large)


def _bias_tiles_kernel(relb_ref, o_ref, *, t):
    h = pl.program_id(0)
    row = lax.broadcasted_iota(jnp.int32, (t, t), 0)
    col = lax.broadcasted_iota(jnp.int32, (t, t), 1)
    o_ref[0, 0] = jnp.full((t, t), relb_ref[N_BUCKETS - 1, h], F32)
    for slot, shift in ((1, t), (2, 0)):
        bucket = _t5_bucket(jnp.maximum(row - col + shift, 0))
        val = jnp.full((t, t), relb_ref[0, h], F32)
        for k in range(1, N_BUCKETS):
            val = jnp.where(bucket == k, relb_ref[k, h], val)
        o_ref[0, slot] = val


def _bias_tiles(rel_bias, t):
    assert t >= MAX_DISTANCE
    return pl.pallas_call(
        functools.partial(_bias_tiles_kernel, t=t),
        grid=(N_HEADS,),
        in_specs=[pl.BlockSpec(memory_space=pltpu.SMEM)],
        out_specs=pl.BlockSpec((1, 3, t, t), lambda h: (h, 0, 0, 0)),
        out_shape=jax.ShapeDtypeStruct((N_HEADS, 3, t, t), F32),
        compiler_params=_cparams(("arbitrary",)),
        name="t5_bias_tiles",
    )(rel_bias)


def _dsa_kernel(dq_ref, qi_ref, zs_ref, kidx_ref, dk_ref, dv_ref, bias_ref, o_ref, key_ref, madd_ref,
                *, t, n_sel):
    i = pl.program_id(1)
    nch = i + 1
    row = lax.broadcasted_iota(jnp.int32, (t, t), 0)
    col = lax.broadcasted_iota(jnp.int32, (t, t), 1)
    idx_scale = (IDX_DIM ** -0.5) * (IDX_HEADS ** -0.5)

    zs = zs_ref[0]
    qi = qi_ref[0].astype(F32)
    qih, wih = [], []
    for h in range(IDX_HEADS):
        blk = qi[:, (h // 2) * LANES:(h // 2 + 1) * LANES]
        qih.append(jnp.where(_half_mask((t, LANES), h % 2), blk, 0.0).astype(BF16))
        wih.append(zs[:, SM_DIW + h:SM_DIW + h + 1] * idx_scale)

    def score_chunk(j, _):
        ks = pl.multiple_of(j * t, t)
        kc = kidx_ref[0, pl.ds(ks, t), :]
        sc = jnp.zeros((t, t), F32)
        for h in range(IDX_HEADS):
            sc = sc + jnp.maximum(_nt_dot(qih[h], kc), 0.0) * wih[h]
        sc = jnp.where(sc == 0.0, 0.0, sc)
        sc = jnp.where(j * t + col <= i * t + row, sc, -jnp.inf)
        bits = pltpu.bitcast(sc, jnp.int32)
        key_ref[j] = jnp.where(bits < 0, bits ^ jnp.int32(0x7FFFFFFF), bits)
        return 0

    lax.fori_loop(0, nch, score_chunk, 0)

    def count_rows(pred):
        def body(j, c):
            return c + pred(key_ref[j]).astype(jnp.int32)
        per_lane = lax.fori_loop(0, nch, body, jnp.zeros((t, t), jnp.int32))
        return jnp.sum(per_lane.astype(F32), axis=-1, keepdims=True)

    def bit_step(n, thr):
        cand = thr + lax.shift_left(jnp.int32(1), 31 - n)
        cnt = count_rows(lambda k: k >= cand)
        return jnp.where(cnt >= float(n_sel), cand, thr)

    thr = lax.fori_loop(0, 32, bit_step, jnp.full((t, 1), INT_MIN, jnp.int32))
    need = float(n_sel) - count_rows(lambda k: k > thr)

    upto = (row <= col).astype(BF16)

    def mask_chunk(j, seen):
        key = key_ref[j]
        eq = key == thr
        rank = jnp.dot(eq.astype(F32).astype(BF16), upto, preferred_element_type=F32) + seen
        sel = (key > thr) | (eq & (rank <= need))
        sel = sel & (j * t + col <= i * t + row)
        madd_ref[j] = jnp.where(sel, 0.0, NEG)
        return rank[:, t - 1:t]

    lax.fori_loop(0, nch, mask_chunk, jnp.zeros((t, 1), F32))

    lane = lax.broadcasted_iota(jnp.int32, (t, LANES), 1)
    for pair in range(N_HEADS // 2):
        lo, hi_ = pair * LANES, (pair + 1) * LANES
        qp = dq_ref[0, :, lo:hi_].astype(F32) * (HEAD_DIM ** -0.5)
        outs = []
        for half in range(2):
            head = 2 * pair + half
            qm = jnp.where(_half_mask((t, LANES), half), qp, 0.0).astype(BF16)

            def step(j, carry):
                m, l, acc = carry
                ks = pl.multiple_of(j * t, t)
                slot = jnp.clip(j - i + 2, 0, 2)
                s = _nt_dot(qm, dk_ref[0, pl.ds(ks, t), lo:hi_]) + bias_ref[head, slot] + madd_ref[j]
                m_new = jnp.maximum(m, jnp.max(s, axis=-1, keepdims=True))
                alpha = jnp.exp(m - m_new)
                p = jnp.exp(s - m_new)
                l = alpha * l + jnp.sum(p, axis=-1, keepdims=True)
                acc = alpha * acc + jnp.dot(p.astype(BF16), dv_ref[0, pl.ds(ks, t), lo:hi_],
                                            preferred_element_type=F32)
                return m_new, l, acc

            init = (jnp.full((t, 1), NEG, F32), jnp.zeros((t, 1), F32), jnp.zeros((t, LANES), F32))
            _, l, acc = lax.fori_loop(0, nch, step, init)
            outs.append(acc / l)
        o_ref[0, :, lo:hi_] = jnp.where(lane < HEAD_DIM, outs[0], outs[1]).astype(o_ref.dtype)


def _dsa(z3, zs3, dk, dv, bias, t):
    b, s, _ = z3.shape
    n_sel = min(TOPK_MAX, s // 4)
    return pl.pallas_call(
        functools.partial(_dsa_kernel, t=t, n_sel=n_sel),
        grid=(b, s // t),
        in_specs=[pl.BlockSpec((1, t, W_ATT), lambda bi, i: (bi, i, U_DQ // 4)),
                  pl.BlockSpec((1, t, 2 * LANES), lambda bi, i: (bi, i, U_DIQ // 2)),
                  pl.BlockSpec((1, t, LANES), lambda bi, i: (bi, i, 0)),
                  pl.BlockSpec((1, s, LANES), lambda bi, i: (bi, 0, U_DIK)),
                  pl.BlockSpec((1, s, W_ATT), lambda bi, i: (bi, 0, 0)),
                  pl.BlockSpec((1, s, W_ATT), lambda bi, i: (bi, 0, 0)),
                  pl.BlockSpec((N_HEADS, 3, t, t), lambda bi, i: (0, 0, 0, 0))],
        out_specs=pl.BlockSpec((1, t, W_ATT), lambda bi, i: (bi, i, 0)),
        out_shape=jax.ShapeDtypeStruct((b, s, W_ATT), BF16),
        scratch_shapes=[pltpu.VMEM((s // t, t, t), jnp.int32), pltpu.VMEM((s // t, t, t), F32)],
        compiler_params=_cparams(("parallel", "arbitrary")),
        name="dsa_attention",
    )(z3, z3, zs3, z3, dk, dv, bias)


def _merge_kernel(yf_ref, ys_ref, yd_ref, wf_ref, ws_ref, wd_ref, g0_ref, g1_ref, g2_ref, o_ref):
    def branch(y_ref, w_ref, g_ref):
        proj = jnp.dot(y_ref[...], w_ref[...].astype(BF16), preferred_element_type=F32)
        return jax.nn.sigmoid(g_ref[...].astype(F32)) * proj

    o_ref[...] = (branch(yf_ref, wf_ref, g0_ref) + branch(ys_ref, ws_ref, g1_ref)
                  + branch(yd_ref, wd_ref, g2_ref)).astype(o_ref.dtype)


def _merge(yf, ys, yd, wf, ws, wd, z, d, tm, tn):
    m = yf.shape[0]
    yblk = pl.BlockSpec((tm, W_ATT), lambda i, j: (i, 0))
    wblk = pl.BlockSpec((W_ATT, tn), lambda i, j: (0, j))
    gblk = lambda g: pl.BlockSpec((tm, tn), lambda i, j: (i, (U_GATES * LANES + g * d) // tn + j))
    return pl.pallas_call(
        _merge_kernel,
        grid=(m // tm, d // tn),
        in_specs=[yblk, yblk, yblk, wblk, wblk, wblk, gblk(0), gblk(1), gblk(2)],
        out_specs=pl.BlockSpec((tm, tn), lambda i, j: (i, j)),
        out_shape=jax.ShapeDtypeStruct((m, d), BF16),
        compiler_params=_cparams(("parallel", "arbitrary")),
        name="branch_merge",
    )(yf, ys, yd, wf, ws, wd, z, z, z)


def _matmul_res_kernel(a_ref, w_ref, r_ref, o_ref):
    o_ref[...] = r_ref[...] + jnp.dot(a_ref[...], w_ref[...].astype(BF16), preferred_element_type=F32)


def _matmul_res(a, w, res, tm, tn):
    m, k = a.shape
    n = w.shape[1]
    return pl.pallas_call(
        _matmul_res_kernel,
        grid=(m // tm, n // tn),
        in_specs=[pl.BlockSpec((tm, k), lambda i, j: (i, 0)), pl.BlockSpec((k, tn), lambda i, j: (0, j)),
                  pl.BlockSpec((tm, tn), lambda i, j: (i, j))],
        out_specs=pl.BlockSpec((tm, tn), lambda i, j: (i, j)),
        out_shape=jax.ShapeDtypeStruct((m, n), F32),
        compiler_params=_cparams(("parallel", "arbitrary")),
        name="out_proj_residual",
    )(a, w, res)


def _ffn_kernel(te_ref, nv_ref, x_ref, wg_ref, wu_ref, wd_ref, *rest, grouped):
    o_ref = rest[-1]
    ti = pl.program_id(0)
    f = pl.program_id(1)

    @pl.when(f == 0)
    def _():
        o_ref[...] = jnp.zeros_like(o_ref)

    @pl.when(ti < nv_ref[0])
    def _():
        x = x_ref[...]
        g = jnp.dot(x, wg_ref[0].astype(BF16), preferred_element_type=F32)
        u = jnp.dot(x, wu_ref[0].astype(BF16), preferred_element_type=F32)
        a = (g * jax.nn.sigmoid(g) * u).astype(BF16)
        o_ref[...] += jnp.dot(a, wd_ref[0].astype(BF16), preferred_element_type=F32)

    if grouped:
        @pl.when(f == pl.num_programs(1) - 1)
        def _():
            o_ref[...] = o_ref[...] * rest[0][...]


def _ffn(x, wg, wu, wd, row_scale, tile_expert, n_valid, tm, fc):
    p, d = x.shape
    nf = wg.shape[2] // fc
    grouped = row_scale is not None

    def fidx(ti, f, nv):
        return jnp.where(ti < nv[0], f, nf - 1)

    in_specs = [pl.BlockSpec((tm, d), lambda ti, f, te, nv: (ti, 0)),
                pl.BlockSpec((1, d, fc), lambda ti, f, te, nv: (te[ti], 0, fidx(ti, f, nv))),
                pl.BlockSpec((1, d, fc), lambda ti, f, te, nv: (te[ti], 0, fidx(ti, f, nv))),
                pl.BlockSpec((1, fc, d), lambda ti, f, te, nv: (te[ti], fidx(ti, f, nv), 0))]
    args = [x, wg, wu, wd]
    if grouped:
        in_specs.append(pl.BlockSpec((tm, 1), lambda ti, f, te, nv: (ti, 0)))
        args.append(row_scale)
    grid_spec = pltpu.PrefetchScalarGridSpec(
        num_scalar_prefetch=2,
        grid=(p // tm, nf),
        in_specs=in_specs,
        out_specs=pl.BlockSpec((tm, d), lambda ti, f, te, nv: (ti, 0)),
    )
    return pl.pallas_call(
        functools.partial(_ffn_kernel, grouped=grouped),
        grid_spec=grid_spec,
        out_shape=jax.ShapeDtypeStruct((p, d), F32),
        compiler_params=_cparams(("arbitrary", "arbitrary")),
        name="grouped_swiglu" if grouped else "dense_swiglu",
    )(tile_expert, n_valid, *args)


def _moe(x2, h, info, wg, wu, wd, tm, fc):
    n, d = h.shape
    e = wg.shape[0]
    eid = jnp.concatenate([info[:, 2], info[:, 3]]).astype(jnp.int32)
    wts = jnp.concatenate([info[:, 0], info[:, 1]])
    tok = jnp.tile(jnp.arange(n, dtype=jnp.int32), 2)
    onehot = (eid[:, None] == jnp.arange(e, dtype=jnp.int32)[None, :]).astype(jnp.int32)
    rank = jnp.sum((jnp.cumsum(onehot, axis=0) - 1) * onehot, axis=1)
    counts = jnp.sum(onehot, axis=0)
    padded = ((counts + tm - 1) // tm) * tm
    ends = jnp.cumsum(padded)
    starts = ends - padded
    pos = starts[eid] + rank
    n_tiles = (2 * n) // tm + e
    p = n_tiles * tm
    src = jnp.zeros((p,), jnp.int32).at[pos].set(tok)
    scale = jnp.zeros((p,), F32).at[pos].set(wts)
    n_valid = (ends[e - 1] // tm).astype(jnp.int32)
    tile_start = jnp.arange(n_tiles, dtype=jnp.int32) * tm
    tile_start = jnp.minimum(tile_start, (n_valid - 1) * tm)
    tile_expert = jnp.sum((tile_start[:, None] >= ends[None, :]).astype(jnp.int32), axis=1)
    xg = jnp.take(h, src, axis=0)
    y = _ffn(xg, wg, wu, wd, scale.reshape(p, 1), tile_expert, n_valid.reshape(1), tm, fc)
    return x2 + jnp.take(y, pos[:n], axis=0) + jnp.take(y, pos[n:], axis=0)


def _tile(total, want):
    t = min(total, want)
    assert total % t == 0
    return t


def kernel(x, w_in, b_forget, q_norm, k_norm, kv_norm, w_ukv, w_br_fox, w_br_sb, w_br_dsa, w_out, rel_bias, norm_mix, norm_ffn, w_ffn_gate, w_ffn_up, w_ffn_down, w_router, w_moe_gate, w_moe_up, w_moe_down, norm_final):
    b, s, d = x.shape
    m = b * s
    depth = w_in.shape[0]
    ta = _tile(s, 256)
    tm_norm = _tile(m, 512)
    tm = _tile(m, 1024)
    tn = 512
    fc = 256
    tm_moe = _tile(2 * m, 1024)

    bias = _bias_tiles(rel_bias, ta)
    x2 = x.reshape(m, d)
    for l in range(depth):
        h = _rmsnorm(x2, norm_mix[l], BF16, tm_norm)
        z, zs = _inproj(h, _relayout_w_in(w_in[l]), tm, tn)
        z3 = z.reshape(b, s, Z_COLS)
        zs3 = zs.reshape(b, s, LANES)
        qn, kn, aq, ak, dk, dv = _prep(z3, zs3, q_norm[l], k_norm[l], kv_norm[l], b_forget[l], w_ukv[l], ta)
        y_fox = _fox(qn, kn, z3, aq, ak, ta).reshape(m, W_ATT)
        y_sb = _sb(z3, ta).reshape(m, W_ATT)
        y_dsa = _dsa(z3, zs3, dk, dv, bias, ta).reshape(m, W_ATT)
        mix = _merge(y_fox, y_sb, y_dsa, w_br_fox[l], w_br_sb[l], w_br_dsa[l], z, d, tm, tn)
        x2 = _matmul_res(mix, w_out[l], x2, tm, tn)
        j = l // 2
        if l % 2 == 0:
            h = _rmsnorm(x2, norm_ffn[l], BF16, tm_norm)
            n_tiles = m // tm
            x2 = x2 + _ffn(h, w_ffn_gate[j][None], w_ffn_up[j][None], w_ffn_down[j][None], None,
                           jnp.zeros((n_tiles,), jnp.int32), jnp.full((1,), n_tiles, jnp.int32), tm, fc)
        else:
            h, info = _rmsnorm_router(x2, norm_ffn[l], w_router[j], tm_norm)
            x2 = _moe(x2, h, info, w_moe_gate[j], w_moe_up[j], w_moe_down[j], tm_moe, fc)
    return _rmsnorm(x2, norm_final, F32, tm_norm).reshape(b, s, d)
```

```python
import functools
import math

import jax
import jax.numpy as jnp
import numpy as np
from jax import lax
from jax.experimental import pallas as pl
from jax.experimental.pallas import tpu as pltpu

F32 = jnp.float32
BF16 = jnp.bfloat16

HEAD_DIM = 64
N_HEADS = 8
N_PAIRS = N_HEADS // 2
W_ATT = N_HEADS * HEAD_DIM
KV_RANK = 256
IDX_HEADS = 4
IDX_DIM = 64
TOPK_MAX = 256
N_BUCKETS = 32
MAX_DISTANCE = 128
N_EXPERTS = 8
N_BRANCH = 3
EPS = 1e-6
LANES = 128
NEG = -1e30
INT_MIN = -2147483648
SB_UNDERFLOW = -104.0

U_FQ, U_FK, U_FV, U_FO, U_SQ, U_SK, U_SV, U_DQ = 0, 4, 8, 12, 16, 20, 24, 28
U_DLAT, U_DIQ, U_DIK, U_SMALL, U_GATES = 32, 34, 36, 37, 40
N_UNITS = 88
Z_COLS = N_UNITS * LANES
SM_FF, SM_DIW = 0, 8

VMEM_LIMIT = 56 * 1024 * 1024


def _cparams(sem):
    return pltpu.CompilerParams(dimension_semantics=sem, vmem_limit_bytes=VMEM_LIMIT)


def _log_sigmoid(x):
    return jnp.minimum(x, 0.0) - jnp.log1p(jnp.exp(-jnp.abs(x)))


def _rmsnorm_kernel(x_ref, g_ref, o_ref):
    x = x_ref[...]
    ms = jnp.mean(x * x, axis=-1, keepdims=True)
    o_ref[...] = (x * lax.rsqrt(ms + EPS) * g_ref[...]).astype(o_ref.dtype)


def _rmsnorm(x, g, out_dtype, tm):
    m, d = x.shape
    return pl.pallas_call(
        _rmsnorm_kernel,
        grid=(m // tm,),
        in_specs=[pl.BlockSpec((tm, d), lambda i: (i, 0)), pl.BlockSpec((1, d), lambda i: (0, 0))],
        out_specs=pl.BlockSpec((tm, d), lambda i: (i, 0)),
        out_shape=jax.ShapeDtypeStruct((m, d), out_dtype),
        compiler_params=_cparams(("parallel",)),
        name="rmsnorm",
    )(x, g.reshape(1, d))


def _rmsnorm_router_kernel(x_ref, g_ref, wr_ref, o_ref, info_ref):
    x = x_ref[...]
    ms = jnp.mean(x * x, axis=-1, keepdims=True)
    h = x * lax.rsqrt(ms + EPS) * g_ref[...]
    o_ref[...] = h.astype(o_ref.dtype)
    logits = jnp.dot(h, wr_ref[...], precision=lax.Precision.HIGHEST, preferred_element_type=F32)
    lane = lax.broadcasted_iota(jnp.int32, logits.shape, 1).astype(F32)
    lg = jnp.where(lane < N_EXPERTS, logits, -jnp.inf)
    v1 = jnp.max(lg, axis=-1, keepdims=True)
    i1 = jnp.min(jnp.where(lg == v1, lane, float(LANES)), axis=-1, keepdims=True)
    lg2 = jnp.where(lane == i1, -jnp.inf, lg)
    v2 = jnp.max(lg2, axis=-1, keepdims=True)
    i2 = jnp.min(jnp.where(lg2 == v2, lane, float(LANES)), axis=-1, keepdims=True)
    e2 = jnp.exp(v2 - v1)
    w1 = 1.0 / (1.0 + e2)
    w2 = e2 / (1.0 + e2)
    info = jnp.where(lane == 0.0, w1, jnp.where(lane == 1.0, w2, jnp.where(
        lane == 2.0, i1, jnp.where(lane == 3.0, i2, 0.0))))
    info_ref[...] = info


def _rmsnorm_router(x, g, w_router, tm):
    m, d = x.shape
    wr = jnp.pad(w_router, ((0, 0), (0, LANES - w_router.shape[1])))
    return pl.pallas_call(
        _rmsnorm_router_kernel,
        grid=(m // tm,),
        in_specs=[pl.BlockSpec((tm, d), lambda i: (i, 0)), pl.BlockSpec((1, d), lambda i: (0, 0)),
                  pl.BlockSpec((d, LANES), lambda i: (0, 0))],
        out_specs=[pl.BlockSpec((tm, d), lambda i: (i, 0)), pl.BlockSpec((tm, LANES), lambda i: (i, 0))],
        out_shape=[jax.ShapeDtypeStruct((m, d), F32), jax.ShapeDtypeStruct((m, LANES), F32)],
        compiler_params=_cparams(("parallel",)),
        name="rmsnorm_router",
    )(x, g.reshape(1, d), wr)


def _inproj_kernel(a_ref, w_ref, z_ref, zs_ref, *, small_tile):
    acc = jnp.dot(a_ref[...], w_ref[...], preferred_element_type=F32)
    z_ref[...] = acc.astype(z_ref.dtype)

    @pl.when(pl.program_id(1) == small_tile)
    def _():
        off = (U_SMALL * LANES) % acc.shape[1]
        zs_ref[...] = acc[:, off:off + LANES]


def _inproj(h, w, tm, tn):
    m, d = h.shape
    return pl.pallas_call(
        functools.partial(_inproj_kernel, small_tile=(U_SMALL * LANES) // tn),
        grid=(m // tm, Z_COLS // tn),
        in_specs=[pl.BlockSpec((tm, d), lambda i, j: (i, 0)), pl.BlockSpec((d, tn), lambda i, j: (0, j))],
        out_specs=[pl.BlockSpec((tm, tn), lambda i, j: (i, j)), pl.BlockSpec((tm, LANES), lambda i, j: (i, 0))],
        out_shape=[jax.ShapeDtypeStruct((m, Z_COLS), BF16), jax.ShapeDtypeStruct((m, LANES), F32)],
        compiler_params=_cparams(("parallel", "arbitrary")),
        name="inproj",
    )(h, w)


def _relayout_w_in(w):
    d = w.shape[0]
    o = 0
    parts = {}
    for name, width in (("fq", W_ATT), ("fk", W_ATT), ("fv", W_ATT), ("ff", N_HEADS), ("fo", W_ATT),
                        ("sq", W_ATT), ("sk", W_ATT), ("sv", W_ATT), ("dq", W_ATT), ("dlat", KV_RANK),
                        ("diq", IDX_HEADS * IDX_DIM), ("dik", IDX_DIM), ("diw", IDX_HEADS),
                        ("gates", None)):
        width = w.shape[1] - o if width is None else width
        parts[name] = w[:, o:o + width]
        o += width
    small = jnp.concatenate(
        [parts["ff"], parts["diw"], jnp.zeros((d, LANES - N_HEADS - IDX_HEADS), w.dtype)], axis=1)
    pad = jnp.zeros((d, (U_GATES - U_SMALL - 1) * LANES), w.dtype)
    out = jnp.concatenate(
        [parts["fq"], parts["fk"], parts["fv"], parts["fo"], parts["sq"], parts["sk"], parts["sv"],
         parts["dq"], parts["dlat"], parts["diq"], parts["dik"], parts["dik"], small, pad,
         parts["gates"]], axis=1)
    return out.astype(BF16)


def _prep_kernel(zq_ref, zk_ref, zl_ref, zs_ref, gq_ref, gk_ref, gkv_ref, bf_ref, wukv_ref,
                 grp_ref, eq_ref, ek_ref, oneq_ref, onek_ref,
                 qn_ref, kn_ref, aq_ref, ak_ref, dk_ref, dv_ref, carry_ref, *, ts):
    @pl.when(pl.program_id(1) == 0)
    def _():
        carry_ref[...] = jnp.zeros_like(carry_ref)

    def head_norm(z_ref, g_ref):
        x = z_ref[0].astype(F32)
        sq = x * x
        hi = sq.astype(BF16)
        lo = (sq - hi.astype(F32)).astype(BF16)
        ms = (jnp.dot(hi, grp_ref[...], preferred_element_type=F32)
              + jnp.dot(lo, grp_ref[...], preferred_element_type=F32)) * (1.0 / HEAD_DIM)
        return x * lax.rsqrt(ms + EPS) * g_ref[...]

    qn_ref[0] = head_norm(zq_ref, gq_ref).astype(BF16)
    kn_ref[0] = head_norm(zk_ref, gk_ref).astype(BF16)

    lane = lax.broadcasted_iota(jnp.int32, (ts, LANES), 1)
    lf = jnp.where(lane < N_HEADS, _log_sigmoid(zs_ref[0] + bf_ref[...]), 0.0)
    r = lax.broadcasted_iota(jnp.int32, (ts, ts), 0)
    c_ = lax.broadcasted_iota(jnp.int32, (ts, ts), 1)
    tri = (c_ <= r).astype(F32)
    c = jnp.dot(tri, lf, precision=lax.Precision.HIGHEST, preferred_element_type=F32) + carry_ref[...]
    carry_ref[...] = c[ts - 1:ts, :]
    c0 = c.astype(BF16)
    r1 = c - c0.astype(F32)
    c1 = r1.astype(BF16)
    c2 = (r1 - c1.astype(F32)).astype(BF16)
    pieces = (c0, c1, c2)
    aq = oneq_ref[...]
    ak = onek_ref[...]
    for k in range(3):
        aq = aq + jnp.dot(pieces[k], eq_ref[k], preferred_element_type=F32)
        ak = ak - jnp.dot(pieces[k], ek_ref[k], preferred_element_type=F32)
    aq_ref[0] = aq.astype(BF16)
    ak_ref[0] = ak.astype(BF16)

    lat = zl_ref[0, :, :KV_RANK].astype(F32)
    msl = jnp.mean(lat * lat, axis=-1, keepdims=True)
    latn = (lat * lax.rsqrt(msl + EPS) * gkv_ref[...]).astype(BF16)
    kv = jnp.dot(latn, wukv_ref[...], preferred_element_type=F32)
    dk_ref[0] = kv[:, :W_ATT].astype(BF16)
    dv_ref[0] = kv[:, W_ATT:].astype(BF16)


def _aug_constants():
    eq = np.zeros((3, LANES, LANES), np.float32)
    ek = np.zeros((3, LANES, LANES), np.float32)
    oneq = np.zeros((1, LANES), np.float32)
    onek = np.zeros((1, LANES), np.float32)
    for h in range(N_HEADS):
        for k in range(3):
            eq[k, h, 8 * h + k] = 1.0
            ek[k, h, 8 * h + 3 + k] = 1.0
            oneq[0, 8 * h + 3 + k] = 1.0
            onek[0, 8 * h + k] = 1.0
    grp = np.kron(np.eye(N_HEADS, dtype=np.float32), np.ones((HEAD_DIM, HEAD_DIM), np.float32))
    return (jnp.asarray(grp, BF16), jnp.asarray(eq, BF16), jnp.asarray(ek, BF16),
            jnp.asarray(oneq), jnp.asarray(onek))


def _prep(z3, zs3, q_norm, k_norm, kv_norm, b_forget, w_ukv, ts):
    b, s, _ = z3.shape
    grp, eq, ek, oneq, onek = _aug_constants()
    gq = (jnp.tile(q_norm, N_HEADS) * HEAD_DIM ** -0.5).reshape(1, W_ATT)
    gk = jnp.tile(k_norm, N_HEADS).reshape(1, W_ATT)
    bf = jnp.pad(b_forget, (SM_FF, LANES - N_HEADS - SM_FF)).reshape(1, LANES)
    const = lambda shape: pl.BlockSpec(shape, lambda bi, si: (0,) * len(shape))
    zblk = lambda unit: pl.BlockSpec((1, ts, W_ATT), lambda bi, si: (bi, si, unit // 4))
    seq_out = lambda w: pl.BlockSpec((1, ts, w), lambda bi, si: (bi, si, 0))
    return pl.pallas_call(
        functools.partial(_prep_kernel, ts=ts),
        grid=(b, s // ts),
        in_specs=[zblk(U_FQ), zblk(U_FK), zblk(U_DLAT),
                  pl.BlockSpec((1, ts, LANES), lambda bi, si: (bi, si, 0)),
                  const((1, W_ATT)), const((1, W_ATT)), const((1, KV_RANK)), const((1, LANES)),
                  const((KV_RANK, 2 * W_ATT)), const((W_ATT, W_ATT)),
                  const((3, LANES, LANES)), const((3, LANES, LANES)), const((1, LANES)), const((1, LANES))],
        out_specs=[seq_out(W_ATT), seq_out(W_ATT), seq_out(LANES), seq_out(LANES), seq_out(W_ATT), seq_out(W_ATT)],
        out_shape=[jax.ShapeDtypeStruct((b, s, W_ATT), BF16), jax.ShapeDtypeStruct((b, s, W_ATT), BF16),
                   jax.ShapeDtypeStruct((b, s, LANES), BF16), jax.ShapeDtypeStruct((b, s, LANES), BF16),
                   jax.ShapeDtypeStruct((b, s, W_ATT), BF16), jax.ShapeDtypeStruct((b, s, W_ATT), BF16)],
        scratch_shapes=[pltpu.VMEM((1, LANES), F32)],
        compiler_params=_cparams(("parallel", "arbitrary")),
        name="mixer_prep",
    )(z3, z3, z3, zs3, gq, gk, kv_norm.reshape(1, KV_RANK), bf, w_ukv.astype(BF16), grp, eq, ek, oneq, onek)


def _nt_dot(a, b):
    return lax.dot_general(a, b, (((1,), (1,)), ((), ())), preferred_element_type=F32)


def _half_mask(shape, half):
    lane = lax.broadcasted_iota(jnp.int32, shape, 1)
    return (lane >= HEAD_DIM * half) & (lane < HEAD_DIM * (half + 1))


def _value_tiles(v, t):
    b, s, _ = v.shape
    vt = v.reshape(b, s // t, t, N_PAIRS, LANES)
    return jnp.transpose(vt, (0, 3, 1, 4, 2))


def _pair_out(acc0, acc1, t):
    return jnp.transpose(jnp.concatenate([acc0, acc1], axis=0))


def _fox_kernel(q_ref, k_ref, vt_ref, aq_ref, ak_ref, fo_ref, o_ref, *, t):
    pair = pl.program_id(1)
    i = pl.program_id(2)
    q = q_ref[0].astype(F32)
    aq = aq_ref[0].astype(F32)
    lane = lax.broadcasted_iota(jnp.int32, (t, LANES), 1)
    krow = lax.broadcasted_iota(jnp.int32, (t, t), 0)
    qcol = lax.broadcasted_iota(jnp.int32, (t, t), 1)
    causal = krow <= qcol
    qcs = []
    for half in range(2):
        head = 2 * pair + half
        qm = jnp.where(_half_mask((t, LANES), half), q, 0.0).astype(BF16)
        am = jnp.where((lane >= 8 * head) & (lane < 8 * head + 6), aq, 0.0).astype(BF16)
        qcs.append(jnp.concatenate([qm, am], axis=1))

    def step(j, carry, diag):
        ks = pl.multiple_of(j * t, t)
        kc = jnp.concatenate([k_ref[0, pl.ds(ks, t), :], ak_ref[0, pl.ds(ks, t), :]], axis=1)
        vt = vt_ref[0, 0, j]
        new = []
        for half in range(2):
            m, l, acc = carry[half]
            s = _nt_dot(kc, qcs[half])
            if diag:
                s = jnp.where(causal, s, NEG)
            m_new = jnp.maximum(m, jnp.max(s, axis=0, keepdims=True))
            alpha = jnp.exp(m - m_new)
            p = jnp.exp(s - m_new)
            l = alpha * l + jnp.sum(p, axis=0, keepdims=True)
            pv = jnp.dot(vt[HEAD_DIM * half:HEAD_DIM * (half + 1), :], p.astype(BF16), preferred_element_type=F32)
            new.append((m_new, l, alpha * acc + pv))
        return tuple(new)

    init = (jnp.full((1, t), NEG, F32), jnp.zeros((1, t), F32), jnp.zeros((HEAD_DIM, t), F32))
    carry = lax.fori_loop(0, i, functools.partial(step, diag=False), (init, init))
    (_, l0, acc0), (_, l1, acc1) = step(i, carry, True)
    o = _pair_out(acc0 / l0, acc1 / l1, t)
    o_ref[0] = (o * jax.nn.sigmoid(fo_ref[0].astype(F32))).astype(o_ref.dtype)


def _fox(qn, kn, z3, aq, ak, t):
    b, s, _ = qn.shape
    vt = _value_tiles(z3[:, :, U_FV * LANES:(U_FV + 4) * LANES], t)
    qblk = lambda unit: pl.BlockSpec((1, t, LANES), lambda bi, p, i: (bi, i, unit + p))
    return pl.pallas_call(
        functools.partial(_fox_kernel, t=t),
        grid=(b, N_PAIRS, s // t),
        in_specs=[qblk(0),
                  pl.BlockSpec((1, s, LANES), lambda bi, p, i: (bi, 0, p)),
                  pl.BlockSpec((1, 1, s // t, LANES, t), lambda bi, p, i: (bi, p, 0, 0, 0)),
                  pl.BlockSpec((1, t, LANES), lambda bi, p, i: (bi, i, 0)),
                  pl.BlockSpec((1, s, LANES), lambda bi, p, i: (bi, 0, 0)),
                  qblk(U_FO)],
        out_specs=qblk(0),
        out_shape=jax.ShapeDtypeStruct((b, s, W_ATT), BF16),
        compiler_params=_cparams(("parallel", "parallel", "arbitrary")),
        name="fox_attention",
    )(qn, kn, vt, aq, ak, z3)


def _sb_kernel(q_ref, k_ref, vt_ref, o_ref, *, t):
    i = pl.program_id(2)
    q = q_ref[0].astype(F32) * (HEAD_DIM ** -0.5)
    krow = lax.broadcasted_iota(jnp.int32, (t, t), 0)
    qcol = lax.broadcasted_iota(jnp.int32, (t, t), 1)
    strict = krow < qcol
    after = (qcol > krow).astype(BF16)
    qms = [jnp.where(_half_mask((t, LANES), half), q, 0.0).astype(BF16) for half in range(2)]

    def step(j, carry, diag):
        ks = pl.multiple_of(j * t, t)
        k = k_ref[0, pl.ds(ks, t), :]
        vt = vt_ref[0, 0, j]
        new = []
        for half in range(2):
            rsum, acc = carry[half]
            z = _nt_dot(k, qms[half])
            lz = _log_sigmoid(z)
            l1m = lz - z
            if diag:
                l1m = jnp.where(strict, l1m, 0.0)
            hi = l1m.astype(BF16)
            lo = (l1m - hi.astype(F32)).astype(BF16)
            suffix = (jnp.dot(after, hi, preferred_element_type=F32)
                      + jnp.dot(after, lo, preferred_element_type=F32)) + rsum
            a = jnp.exp(lz + suffix)
            if diag:
                a = jnp.where(strict, a, 0.0)
            pv = jnp.dot(vt[HEAD_DIM * half:HEAD_DIM * (half + 1), :], a.astype(BF16), preferred_element_type=F32)
            new.append((rsum + jnp.sum(l1m, axis=0, keepdims=True), acc + pv))
        return tuple(new)

    zero = (jnp.zeros((1, t), F32), jnp.zeros((HEAD_DIM, t), F32))
    carry = step(i, (zero, zero), True)

    def more(c):
        n, ((r0, _), (r1, _)) = c
        return (n < i) & (jnp.maximum(jnp.max(r0), jnp.max(r1)) > SB_UNDERFLOW)

    def body(c):
        n, carry = c
        return n + 1, step(i - 1 - n, carry, False)

    _, ((_, acc0), (_, acc1)) = lax.while_loop(more, body, (jnp.int32(0), carry))
    o_ref[0] = _pair_out(acc0, acc1, t).astype(o_ref.dtype)


def _sb(z3, t):
    b, s, _ = z3.shape
    vt = _value_tiles(z3[:, :, U_SV * LANES:(U_SV + 4) * LANES], t)
    return pl.pallas_call(
        functools.partial(_sb_kernel, t=t),
        grid=(b, N_PAIRS, s // t),
        in_specs=[pl.BlockSpec((1, t, LANES), lambda bi, p, i: (bi, i, U_SQ + p)),
                  pl.BlockSpec((1, s, LANES), lambda bi, p, i: (bi, 0, U_SK + p)),
                  pl.BlockSpec((1, 1, s // t, LANES, t), lambda bi, p, i: (bi, p, 0, 0, 0))],
        out_specs=pl.BlockSpec((1, t, LANES), lambda bi, p, i: (bi, i, p)),
        out_shape=jax.ShapeDtypeStruct((b, s, W_ATT), BF16),
        compiler_params=_cparams(("parallel", "parallel", "arbitrary")),
        name="stickbreak_attention",
    )(z3, z3, vt)


def _t5_bucket(n):
    max_exact = N_BUCKETS // 2
    nf = jnp.maximum(n, 1).astype(F32)
    large = max_exact + (jnp.log(nf / max_exact) / math.log(MAX_DISTANCE / max_exact)
                         * (N_BUCKETS - max_exact)).astype(jnp.int32)
    large = jnp.minimum(large, N_BUCKETS - 1)
    return jnp.where(n < max_exact, n, large)


def _bias_tiles_kernel(relb_ref, o_ref, *, t):
    h = pl.program_id(0)
    krow = lax.broadcasted_iota(jnp.int32, (t, t), 0)
    qcol = lax.broadcasted_iota(jnp.int32, (t, t), 1)
    o_ref[0, 0] = jnp.full((t, t), relb_ref[N_BUCKETS - 1, h], F32)
    for slot, shift in ((1, t), (2, 0)):
        bucket = _t5_bucket(jnp.maximum(qcol - krow + shift, 0))
        val = jnp.full((t, t), relb_ref[0, h], F32)
        for k in range(1, N_BUCKETS):
            val = jnp.where(bucket == k, relb_ref[k, h], val)
        o_ref[0, slot] = val


def _bias_tiles(rel_bias, t):
    assert t >= MAX_DISTANCE
    return pl.pallas_call(
        functools.partial(_bias_tiles_kernel, t=t),
        grid=(N_HEADS,),
        in_specs=[pl.BlockSpec(memory_space=pltpu.SMEM)],
        out_specs=pl.BlockSpec((1, 3, t, t), lambda h: (h, 0, 0, 0)),
        out_shape=jax.ShapeDtypeStruct((N_HEADS, 3, t, t), F32),
        compiler_params=_cparams(("arbitrary",)),
        name="t5_bias_tiles",
    )(rel_bias)


def _dsa_kernel(dq_ref, qi_ref, zs_ref, kidx_ref, dk_ref, dvt_ref, bias_ref, o_ref, key_ref, madd_ref,
                *, t, n_sel):
    i = pl.program_id(1)
    nch = i + 1
    krow = lax.broadcasted_iota(jnp.int32, (t, t), 0)
    qcol = lax.broadcasted_iota(jnp.int32, (t, t), 1)
    idx_scale = (IDX_DIM ** -0.5) * (IDX_HEADS ** -0.5)

    zst = jnp.transpose(zs_ref[0])
    qi = qi_ref[0].astype(F32)
    qih, wih = [], []
    for h in range(IDX_HEADS):
        blk = qi[:, (h // 2) * LANES:(h // 2 + 1) * LANES]
        qih.append(jnp.where(_half_mask((t, LANES), h % 2), blk, 0.0).astype(BF16))
        wih.append(zst[SM_DIW + h:SM_DIW + h + 1, :] * idx_scale)

    def score_chunk(j, _):
        ks = pl.multiple_of(j * t, t)
        kc = kidx_ref[0, pl.ds(ks, t), :]
        sc = jnp.zeros((t, t), F32)
        for h in range(IDX_HEADS):
            sc = sc + jnp.maximum(_nt_dot(kc, qih[h]), 0.0) * wih[h]
        sc = jnp.where(sc == 0.0, 0.0, sc)
        sc = jnp.where(j * t + krow <= i * t + qcol, sc, -jnp.inf)
        bits = pltpu.bitcast(sc, jnp.int32)
        key_ref[j] = jnp.where(bits < 0, bits ^ jnp.int32(0x7FFFFFFF), bits)
        return 0

    lax.fori_loop(0, nch, score_chunk, 0)

    def count_keys(pred):
        def body(j, c):
            return c + jnp.sum(jnp.where(pred(key_ref[j]), 1.0, 0.0), axis=0, keepdims=True)
        return lax.fori_loop(0, nch, body, jnp.zeros((1, t), F32))

    def bit_step(n, thr):
        cand = thr + lax.shift_left(jnp.int32(1), 31 - n)
        cnt = count_keys(lambda k: k >= cand)
        return jnp.where(cnt >= float(n_sel), cand, thr)

    thr = lax.fori_loop(0, 32, bit_step, jnp.full((1, t), INT_MIN, jnp.int32))
    need = float(n_sel) - count_keys(lambda k: k > thr)

    upto = (qcol <= krow).astype(BF16)

    def mask_chunk(j, seen):
        key = key_ref[j]
        eq = key == thr
        rank = jnp.dot(upto, jnp.where(eq, 1.0, 0.0).astype(BF16), preferred_element_type=F32) + seen
        sel = (key > thr) | (eq & (rank <= need))
        sel = sel & (j * t + krow <= i * t + qcol)
        madd_ref[j] = jnp.where(sel, 0.0, NEG)
        return rank[t - 1:t, :]

    lax.fori_loop(0, nch, mask_chunk, jnp.zeros((1, t), F32))

    for pair in range(N_PAIRS):
        lo, hi_ = pair * LANES, (pair + 1) * LANES
        qp = dq_ref[0, :, lo:hi_].astype(F32) * (HEAD_DIM ** -0.5)
        qms = [jnp.where(_half_mask((t, LANES), half), qp, 0.0).astype(BF16) for half in range(2)]

        def step(j, carry):
            ks = pl.multiple_of(j * t, t)
            slot = jnp.clip(j - i + 2, 0, 2)
            k = dk_ref[0, pl.ds(ks, t), lo:hi_]
            vt = dvt_ref[0, pair, j]
            madd = madd_ref[j]
            new = []
            for half in range(2):
                m, l, acc = carry[half]
                s = _nt_dot(k, qms[half]) + bias_ref[2 * pair + half, slot] + madd
                m_new = jnp.maximum(m, jnp.max(s, axis=0, keepdims=True))
                alpha = jnp.exp(m - m_new)
                p = jnp.exp(s - m_new)
                l = alpha * l + jnp.sum(p, axis=0, keepdims=True)
                pv = jnp.dot(vt[HEAD_DIM * half:HEAD_DIM * (half + 1), :], p.astype(BF16),
                             preferred_element_type=F32)
                new.append((m_new, l, alpha * acc + pv))
            return tuple(new)

        init = (jnp.full((1, t), NEG, F32), jnp.zeros((1, t), F32), jnp.zeros((HEAD_DIM, t), F32))
        (_, l0, acc0), (_, l1, acc1) = lax.fori_loop(0, nch, step, (init, init))
        o_ref[0, :, lo:hi_] = _pair_out(acc0 / l0, acc1 / l1, t).astype(o_ref.dtype)


def _dsa(z3, zs3, dk, dv, bias, t):
    b, s, _ = z3.shape
    n_sel = min(TOPK_MAX, s // 4)
    dvt = _value_tiles(dv, t)
    return pl.pallas_call(
        functools.partial(_dsa_kernel, t=t, n_sel=n_sel),
        grid=(b, s // t),
        in_specs=[pl.BlockSpec((1, t, W_ATT), lambda bi, i: (bi, i, U_DQ // 4)),
                  pl.BlockSpec((1, t, 2 * LANES), lambda bi, i: (bi, i, U_DIQ // 2)),
                  pl.BlockSpec((1, t, LANES), lambda bi, i: (bi, i, 0)),
                  pl.BlockSpec((1, s, LANES), lambda bi, i: (bi, 0, U_DIK)),
                  pl.BlockSpec((1, s, W_ATT), lambda bi, i: (bi, 0, 0)),
                  pl.BlockSpec((1, N_PAIRS, s // t, LANES, t), lambda bi, i: (bi, 0, 0, 0, 0)),
                  pl.BlockSpec((N_HEADS, 3, t, t), lambda bi, i: (0, 0, 0, 0))],
        out_specs=pl.BlockSpec((1, t, W_ATT), lambda bi, i: (bi, i, 0)),
        out_shape=jax.ShapeDtypeStruct((b, s, W_ATT), BF16),
        scratch_shapes=[pltpu.VMEM((s // t, t, t), jnp.int32), pltpu.VMEM((s // t, t, t), F32)],
        compiler_params=_cparams(("parallel", "arbitrary")),
        name="dsa_attention",
    )(z3, z3, zs3, z3, dk, dvt, bias)


def _merge_kernel(yf_ref, ys_ref, yd_ref, wf_ref, ws_ref, wd_ref, g0_ref, g1_ref, g2_ref, o_ref):
    def branch(y_ref, w_ref, g_ref):
        proj = jnp.dot(y_ref[...], w_ref[...].astype(BF16), preferred_element_type=F32)
        return jax.nn.sigmoid(g_ref[...].astype(F32)) * proj

    o_ref[...] = (branch(yf_ref, wf_ref, g0_ref) + branch(ys_ref, ws_ref, g1_ref)
                  + branch(yd_ref, wd_ref, g2_ref)).astype(o_ref.dtype)


def _merge(yf, ys, yd, wf, ws, wd, z, d, tm, tn):
    m = yf.shape[0]
    yblk = pl.BlockSpec((tm, W_ATT), lambda i, j: (i, 0))
    wblk = pl.BlockSpec((W_ATT, tn), lambda i, j: (0, j))
    gblk = lambda g: pl.BlockSpec((tm, tn), lambda i, j: (i, (U_GATES * LANES + g * d) // tn + j))
    return pl.pallas_call(
        _merge_kernel,
        grid=(m // tm, d // tn),
        in_specs=[yblk, yblk, yblk, wblk, wblk, wblk, gblk(0), gblk(1), gblk(2)],
        out_specs=pl.BlockSpec((tm, tn), lambda i, j: (i, j)),
        out_shape=jax.ShapeDtypeStruct((m, d), BF16),
        compiler_params=_cparams(("parallel", "arbitrary")),
        name="branch_merge",
    )(yf, ys, yd, wf, ws, wd, z, z, z)


def _matmul_res_kernel(a_ref, w_ref, r_ref, o_ref):
    o_ref[...] = r_ref[...] + jnp.dot(a_ref[...], w_ref[...].astype(BF16), preferred_element_type=F32)


def _matmul_res(a, w, res, tm, tn):
    m, k = a.shape
    n = w.shape[1]
    return pl.pallas_call(
        _matmul_res_kernel,
        grid=(m // tm, n // tn),
        in_specs=[pl.BlockSpec((tm, k), lambda i, j: (i, 0)), pl.BlockSpec((k, tn), lambda i, j: (0, j)),
                  pl.BlockSpec((tm, tn), lambda i, j: (i, j))],
        out_specs=pl.BlockSpec((tm, tn), lambda i, j: (i, j)),
        out_shape=jax.ShapeDtypeStruct((m, n), F32),
        compiler_params=_cparams(("parallel", "arbitrary")),
        name="out_proj_residual",
    )(a, w, res)


def _row_copy(src_hbm, row, dst_vmem, slot, sem):
    return pltpu.make_async_copy(src_hbm.at[pl.ds(row, 1)], dst_vmem.at[pl.ds(slot, 1)], sem)


def _ffn_kernel(te_ref, nv_ref, src_ref, x_ref, wg_ref, wu_ref, wd_ref, *rest, grouped, tm):
    ti = pl.program_id(0)
    f = pl.program_id(1)
    if grouped:
        scale_ref, o_ref, gbuf, xs, sem = rest
    else:
        (o_ref,) = rest

    @pl.when(f == 0)
    def _():
        o_ref[...] = jnp.zeros_like(o_ref)

    @pl.when(ti < nv_ref[0])
    def _():
        if grouped:
            @pl.when(f == 0)
            def _():
                def issue(r, _):
                    _row_copy(x_ref, src_ref[ti * tm + r], gbuf, r, sem).start()
                    return 0

                def wait(r, _):
                    _row_copy(x_ref, 0, gbuf, r, sem).wait()
                    return 0

                lax.fori_loop(0, tm, issue, 0)
                lax.fori_loop(0, tm, wait, 0)
                xs[...] = gbuf[...].astype(BF16)

            x = xs[...]
        else:
            x = x_ref[...]
        g = jnp.dot(x, wg_ref[0].astype(BF16), preferred_element_type=F32)
        u = jnp.dot(x, wu_ref[0].astype(BF16), preferred_element_type=F32)
        a = (g * jax.nn.sigmoid(g) * u).astype(BF16)
        o_ref[...] += jnp.dot(a, wd_ref[0].astype(BF16), preferred_element_type=F32)

    if grouped:
        @pl.when(f == pl.num_programs(1) - 1)
        def _():
            o_ref[...] = o_ref[...] * scale_ref[...]


def _ffn(x, wg, wu, wd, tile_expert, n_valid, tm, fc, src=None, row_scale=None):
    grouped = src is not None
    d = x.shape[1]
    p = src.shape[0] if grouped else x.shape[0]
    nf = wg.shape[2] // fc

    def fidx(ti, f, nv):
        return jnp.where(ti < nv[0], f, nf - 1)

    wspecs = [pl.BlockSpec((1, d, fc), lambda ti, f, te, nv, sr: (te[ti], 0, fidx(ti, f, nv))),
              pl.BlockSpec((1, d, fc), lambda ti, f, te, nv, sr: (te[ti], 0, fidx(ti, f, nv))),
              pl.BlockSpec((1, fc, d), lambda ti, f, te, nv, sr: (te[ti], fidx(ti, f, nv), 0))]
    if grouped:
        in_specs = [pl.BlockSpec(memory_space=pl.ANY)] + wspecs + [
            pl.BlockSpec((tm, 1), lambda ti, f, te, nv, sr: (ti, 0))]
        args = [x, wg, wu, wd, row_scale]
        scratch = [pltpu.VMEM((tm, d), F32), pltpu.VMEM((tm, d), BF16), pltpu.SemaphoreType.DMA(())]
    else:
        in_specs = [pl.BlockSpec((tm, d), lambda ti, f, te, nv, sr: (ti, 0))] + wspecs
        args = [x, wg, wu, wd]
        scratch = []
        src = jnp.zeros((1,), jnp.int32)
    grid_spec = pltpu.PrefetchScalarGridSpec(
        num_scalar_prefetch=3,
        grid=(p // tm, nf),
        in_specs=in_specs,
        out_specs=pl.BlockSpec((tm, d), lambda ti, f, te, nv, sr: (ti, 0)),
        scratch_shapes=scratch,
    )
    return pl.pallas_call(
        functools.partial(_ffn_kernel, grouped=grouped, tm=tm),
        grid_spec=grid_spec,
        out_shape=jax.ShapeDtypeStruct((p, d), F32),
        compiler_params=_cparams(("arbitrary", "arbitrary")),
        name="grouped_swiglu" if grouped else "dense_swiglu",
    )(tile_expert, n_valid, src, *args)


def _combine_kernel(pos_ref, x_ref, y_ref, o_ref, buf, sem, *, tt, n):
    base = pl.program_id(0) * tt

    def issue(r, _):
        for k in range(2):
            _row_copy(y_ref, pos_ref[k * n + base + r], buf.at[k], r, sem).start()
        return 0

    def wait(r, _):
        for k in range(2):
            _row_copy(y_ref, 0, buf.at[k], r, sem).wait()
        return 0

    lax.fori_loop(0, tt, issue, 0)
    lax.fori_loop(0, tt, wait, 0)
    o_ref[...] = x_ref[...] + (buf[0] + buf[1])


def _combine(x2, y, pos, tt):
    n, d = x2.shape
    grid_spec = pltpu.PrefetchScalarGridSpec(
        num_scalar_prefetch=1,
        grid=(n // tt,),
        in_specs=[pl.BlockSpec((tt, d), lambda i, ps: (i, 0)), pl.BlockSpec(memory_space=pl.ANY)],
        out_specs=pl.BlockSpec((tt, d), lambda i, ps: (i, 0)),
        scratch_shapes=[pltpu.VMEM((2, tt, d), F32), pltpu.SemaphoreType.DMA(())],
    )
    return pl.pallas_call(
        functools.partial(_combine_kernel, tt=tt, n=n),
        grid_spec=grid_spec,
        out_shape=jax.ShapeDtypeStruct((n, d), F32),
        compiler_params=_cparams(("arbitrary",)),
        name="moe_combine",
    )(pos, x2, y)


def _moe(x2, h, info, wg, wu, wd, tm, fc):
    n, d = h.shape
    e = wg.shape[0]
    eid = jnp.concatenate([info[:, 2], info[:, 3]]).astype(jnp.int32)
    wts = jnp.concatenate([info[:, 0], info[:, 1]])
    tok = jnp.tile(jnp.arange(n, dtype=jnp.int32), 2)
    onehot = (eid[:, None] == jnp.arange(e, dtype=jnp.int32)[None, :]).astype(jnp.int32)
    rank = jnp.sum((jnp.cumsum(onehot, axis=0) - 1) * onehot, axis=1)
    counts = jnp.sum(onehot, axis=0)
    padded = ((counts + tm - 1) // tm) * tm
    ends = jnp.cumsum(padded)
    starts = ends - padded
    pos = starts[eid] + rank
    n_tiles = (2 * n) // tm + e
    p = n_tiles * tm
    src = jnp.zeros((p,), jnp.int32).at[pos].set(tok)
    scale = jnp.zeros((p,), F32).at[pos].set(wts)
    n_valid = (ends[e - 1] // tm).astype(jnp.int32)
    tile_start = jnp.arange(n_tiles, dtype=jnp.int32) * tm
    tile_start = jnp.minimum(tile_start, (n_valid - 1) * tm)
    tile_expert = jnp.sum((tile_start[:, None] >= ends[None, :]).astype(jnp.int32), axis=1)
    y = _ffn(h, wg, wu, wd, tile_expert, n_valid.reshape(1), tm, fc, src=src, row_scale=scale.reshape(p, 1))
    return _combine(x2, y, pos, _tile(n, 256))


def _tile(total, want):
    t = min(total, want)
    assert total % t == 0
    return t


def kernel(x, w_in, b_forget, q_norm, k_norm, kv_norm, w_ukv, w_br_fox, w_br_sb, w_br_dsa, w_out, rel_bias, norm_mix, norm_ffn, w_ffn_gate, w_ffn_up, w_ffn_down, w_router, w_moe_gate, w_moe_up, w_moe_down, norm_final):
    b, s, d = x.shape
    m = b * s
    depth = w_in.shape[0]
    ta = _tile(s, 256)
    tm_norm = _tile(m, 512)
    tm = _tile(m, 1024)
    tn = 512
    fc = 256
    tm_moe = _tile(2 * m, 1024)

    bias = _bias_tiles(rel_bias, ta)
    x2 = x.reshape(m, d)
    for l in range(depth):
        h = _rmsnorm(x2, norm_mix[l], BF16, tm_norm)
        z, zs = _inproj(h, _relayout_w_in(w_in[l]), tm, tn)
        z3 = z.reshape(b, s, Z_COLS)
        zs3 = zs.reshape(b, s, LANES)
        qn, kn, aq, ak, dk, dv = _prep(z3, zs3, q_norm[l], k_norm[l], kv_norm[l], b_forget[l], w_ukv[l], ta)
        y_fox = _fox(qn, kn, z3, aq, ak, ta).reshape(m, W_ATT)
        y_sb = _sb(z3, ta).reshape(m, W_ATT)
        y_dsa = _dsa(z3, zs3, dk, dv, bias, ta).reshape(m, W_ATT)
        mix = _merge(y_fox, y_sb, y_dsa, w_br_fox[l], w_br_sb[l], w_br_dsa[l], z, d, tm, tn)
        x2 = _matmul_res(mix, w_out[l], x2, tm, tn)
        j = l // 2
        if l % 2 == 0:
            h = _rmsnorm(x2, norm_ffn[l], BF16, tm_norm)
            n_tiles = m // tm
            x2 = x2 + _ffn(h, w_ffn_gate[j][None], w_ffn_up[j][None], w_ffn_down[j][None],
                           jnp.zeros((n_tiles,), jnp.int32), jnp.full((1,), n_tiles, jnp.int32), tm, fc)
        else:
            h, info = _rmsnorm_router(x2, norm_ffn[l], w_router[j], tm_norm)
            x2 = _moe(x2, h, info, w_moe_gate[j], w_moe_up[j], w_moe_down[j], tm_moe, fc)
    return _rmsnorm(x2, norm_final, F32, tm_norm).reshape(b, s, d)
```

```python
import functools
import math

import jax
import jax.numpy as jnp
import numpy as np
from jax import lax
from jax.experimental import pallas as pl
from jax.experimental.pallas import tpu as pltpu

F32 = jnp.float32
BF16 = jnp.bfloat16

HEAD_DIM = 64
N_HEADS = 8
N_PAIRS = N_HEADS // 2
W_ATT = N_HEADS * HEAD_DIM
KV_RANK = 256
IDX_HEADS = 4
IDX_DIM = 64
TOPK_MAX = 256
N_BUCKETS = 32
MAX_DISTANCE = 128
N_EXPERTS = 8
N_BRANCH = 3
EPS = 1e-6
LANES = 128
NEG = -1e30
INT_MIN = -2147483648
SB_UNDERFLOW = -104.0

U_FQ, U_FK, U_FV, U_FO, U_SQ, U_SK, U_SV, U_DQ = 0, 4, 8, 12, 16, 20, 24, 28
U_DLAT, U_DIQ, U_DIK, U_SMALL, U_GATES = 32, 34, 36, 37, 40
N_UNITS = 88
Z_COLS = N_UNITS * LANES
SM_FF, SM_DIW = 0, 8

VMEM_LIMIT = 56 * 1024 * 1024


def _cparams(sem):
    return pltpu.CompilerParams(dimension_semantics=sem, vmem_limit_bytes=VMEM_LIMIT)


def _log_sigmoid(x):
    return jnp.minimum(x, 0.0) - jnp.log1p(jnp.exp(-jnp.abs(x)))


def _rmsnorm_kernel(x_ref, g_ref, o_ref):
    x = x_ref[...]
    ms = jnp.mean(x * x, axis=-1, keepdims=True)
    o_ref[...] = (x * lax.rsqrt(ms + EPS) * g_ref[...]).astype(o_ref.dtype)


def _rmsnorm(x, g, out_dtype, tm):
    m, d = x.shape
    return pl.pallas_call(
        _rmsnorm_kernel,
        grid=(m // tm,),
        in_specs=[pl.BlockSpec((tm, d), lambda i: (i, 0)), pl.BlockSpec((1, d), lambda i: (0, 0))],
        out_specs=pl.BlockSpec((tm, d), lambda i: (i, 0)),
        out_shape=jax.ShapeDtypeStruct((m, d), out_dtype),
        compiler_params=_cparams(("parallel",)),
        name="rmsnorm",
    )(x, g.reshape(1, d))


def _rmsnorm_router_kernel(x_ref, g_ref, wr_ref, o_ref, info_ref):
    x = x_ref[...]
    ms = jnp.mean(x * x, axis=-1, keepdims=True)
    h = x * lax.rsqrt(ms + EPS) * g_ref[...]
    o_ref[...] = h.astype(o_ref.dtype)
    logits = jnp.dot(h, wr_ref[...], precision=lax.Precision.HIGHEST, preferred_element_type=F32)
    lane = lax.broadcasted_iota(jnp.int32, logits.shape, 1).astype(F32)
    lg = jnp.where(lane < N_EXPERTS, logits, -jnp.inf)
    v1 = jnp.max(lg, axis=-1, keepdims=True)
    i1 = jnp.min(jnp.where(lg == v1, lane, float(LANES)), axis=-1, keepdims=True)
    lg2 = jnp.where(lane == i1, -jnp.inf, lg)
    v2 = jnp.max(lg2, axis=-1, keepdims=True)
    i2 = jnp.min(jnp.where(lg2 == v2, lane, float(LANES)), axis=-1, keepdims=True)
    e2 = jnp.exp(v2 - v1)
    w1 = 1.0 / (1.0 + e2)
    w2 = e2 / (1.0 + e2)
    info = jnp.where(lane == 0.0, w1, jnp.where(lane == 1.0, w2, jnp.where(
        lane == 2.0, i1, jnp.where(lane == 3.0, i2, 0.0))))
    info_ref[...] = info


def _rmsnorm_router(x, g, w_router, tm):
    m, d = x.shape
    wr = jnp.pad(w_router, ((0, 0), (0, LANES - w_router.shape[1])))
    return pl.pallas_call(
        _rmsnorm_router_kernel,
        grid=(m // tm,),
        in_specs=[pl.BlockSpec((tm, d), lambda i: (i, 0)), pl.BlockSpec((1, d), lambda i: (0, 0)),
                  pl.BlockSpec((d, LANES), lambda i: (0, 0))],
        out_specs=[pl.BlockSpec((tm, d), lambda i: (i, 0)), pl.BlockSpec((tm, LANES), lambda i: (i, 0))],
        out_shape=[jax.ShapeDtypeStruct((m, d), F32), jax.ShapeDtypeStruct((m, LANES), F32)],
        compiler_params=_cparams(("parallel",)),
        name="rmsnorm_router",
    )(x, g.reshape(1, d), wr)


def _inproj_kernel(a_ref, w_ref, z_ref, zs_ref, *, small_tile):
    acc = jnp.dot(a_ref[...], w_ref[...], preferred_element_type=F32)
    z_ref[...] = acc.astype(z_ref.dtype)

    @pl.when(pl.program_id(1) == small_tile)
    def _():
        off = (U_SMALL * LANES) % acc.shape[1]
        zs_ref[...] = acc[:, off:off + LANES]


def _inproj(h, w, tm, tn):
    m, d = h.shape
    return pl.pallas_call(
        functools.partial(_inproj_kernel, small_tile=(U_SMALL * LANES) // tn),
        grid=(m // tm, Z_COLS // tn),
        in_specs=[pl.BlockSpec((tm, d), lambda i, j: (i, 0)), pl.BlockSpec((d, tn), lambda i, j: (0, j))],
        out_specs=[pl.BlockSpec((tm, tn), lambda i, j: (i, j)), pl.BlockSpec((tm, LANES), lambda i, j: (i, 0))],
        out_shape=[jax.ShapeDtypeStruct((m, Z_COLS), BF16), jax.ShapeDtypeStruct((m, LANES), F32)],
        compiler_params=_cparams(("parallel", "arbitrary")),
        name="inproj",
    )(h, w)


def _w_in_segments(d_model):
    src, o = {}, 0
    for name, width in (("fq", W_ATT), ("fk", W_ATT), ("fv", W_ATT), ("ff", N_HEADS), ("fo", W_ATT),
                        ("sq", W_ATT), ("sk", W_ATT), ("sv", W_ATT), ("dq", W_ATT), ("dlat", KV_RANK),
                        ("diq", IDX_HEADS * IDX_DIM), ("dik", IDX_DIM), ("diw", IDX_HEADS),
                        ("gates", N_BRANCH * d_model)):
        src[name] = (o, width)
        o += width
    dst = {"fq": U_FQ, "fk": U_FK, "fv": U_FV, "fo": U_FO, "sq": U_SQ, "sk": U_SK, "sv": U_SV,
           "dq": U_DQ, "dlat": U_DLAT, "diq": U_DIQ, "gates": U_GATES}
    segs = [(unit * LANES, *src[name]) for name, unit in dst.items()]
    segs += [(U_DIK * LANES, *src["dik"]), (U_DIK * LANES + IDX_DIM, *src["dik"]),
             (U_SMALL * LANES + SM_FF, *src["ff"]), (U_SMALL * LANES + SM_DIW, *src["diw"])]
    return segs, o


def _relayout_kernel(w_ref, o_ref, *, segs, chunk):
    o_ref[...] = jnp.zeros_like(o_ref)
    for dst, src, width in segs:
        for c in range(0, width, chunk):
            cw = min(chunk, width - c)
            o_ref[:, dst + c:dst + c + cw] = w_ref[0, :, src + c:src + c + cw].astype(o_ref.dtype)


def _relayout_w_in(w_in, layer, tr):
    _, d, n_in = w_in.shape
    segs, total = _w_in_segments(d)
    assert total == n_in and U_GATES * LANES + N_BRANCH * d == Z_COLS
    return pl.pallas_call(
        functools.partial(_relayout_kernel, segs=segs, chunk=4 * LANES),
        grid=(d // tr,),
        in_specs=[pl.BlockSpec((1, tr, n_in), lambda i: (layer, i, 0))],
        out_specs=pl.BlockSpec((tr, Z_COLS), lambda i: (i, 0)),
        out_shape=jax.ShapeDtypeStruct((d, Z_COLS), BF16),
        compiler_params=_cparams(("parallel",)),
        name="w_in_relayout",
    )(w_in)


def _prep_kernel(zq_ref, zk_ref, zl_ref, zs_ref, gq_ref, gk_ref, gkv_ref, bf_ref, wukv_ref,
                 grp_ref, eq_ref, ek_ref, oneq_ref, onek_ref,
                 qn_ref, kn_ref, aq_ref, ak_ref, dk_ref, dv_ref, carry_ref, *, ts):
    @pl.when(pl.program_id(1) == 0)
    def _():
        carry_ref[...] = jnp.zeros_like(carry_ref)

    def head_norm(z_ref, g_ref):
        x = z_ref[0].astype(F32)
        sq = x * x
        hi = sq.astype(BF16)
        lo = (sq - hi.astype(F32)).astype(BF16)
        ms = (jnp.dot(hi, grp_ref[...], preferred_element_type=F32)
              + jnp.dot(lo, grp_ref[...], preferred_element_type=F32)) * (1.0 / HEAD_DIM)
        return x * lax.rsqrt(ms + EPS) * g_ref[...]

    qn_ref[0] = head_norm(zq_ref, gq_ref).astype(BF16)
    kn_ref[0] = head_norm(zk_ref, gk_ref).astype(BF16)

    lane = lax.broadcasted_iota(jnp.int32, (ts, LANES), 1)
    lf = jnp.where(lane < N_HEADS, _log_sigmoid(zs_ref[0] + bf_ref[...]), 0.0)
    r = lax.broadcasted_iota(jnp.int32, (ts, ts), 0)
    c_ = lax.broadcasted_iota(jnp.int32, (ts, ts), 1)
    tri = (c_ <= r).astype(F32)
    c = jnp.dot(tri, lf, precision=lax.Precision.HIGHEST, preferred_element_type=F32) + carry_ref[...]
    carry_ref[...] = c[ts - 1:ts, :]
    c0 = c.astype(BF16)
    r1 = c - c0.astype(F32)
    c1 = r1.astype(BF16)
    c2 = (r1 - c1.astype(F32)).astype(BF16)
    pieces = (c0, c1, c2)
    aq = oneq_ref[...]
    ak = onek_ref[...]
    for k in range(3):
        aq = aq + jnp.dot(pieces[k], eq_ref[k], preferred_element_type=F32)
        ak = ak - jnp.dot(pieces[k], ek_ref[k], preferred_element_type=F32)
    aq_ref[0] = aq.astype(BF16)
    ak_ref[0] = ak.astype(BF16)

    lat = zl_ref[0, :, :KV_RANK].astype(F32)
    msl = jnp.mean(lat * lat, axis=-1, keepdims=True)
    latn = (lat * lax.rsqrt(msl + EPS) * gkv_ref[...]).astype(BF16)
    kv = jnp.dot(latn, wukv_ref[...], preferred_element_type=F32)
    dk_ref[0] = kv[:, :W_ATT].astype(BF16)
    dv_ref[0] = kv[:, W_ATT:].astype(BF16)


def _aug_constants():
    eq = np.zeros((3, LANES, LANES), np.float32)
    ek = np.zeros((3, LANES, LANES), np.float32)
    oneq = np.zeros((1, LANES), np.float32)
    onek = np.zeros((1, LANES), np.float32)
    for h in range(N_HEADS):
        for k in range(3):
            eq[k, h, 8 * h + k] = 1.0
            ek[k, h, 8 * h + 3 + k] = 1.0
            oneq[0, 8 * h + 3 + k] = 1.0
            onek[0, 8 * h + k] = 1.0
    grp = np.kron(np.eye(N_HEADS, dtype=np.float32), np.ones((HEAD_DIM, HEAD_DIM), np.float32))
    return (jnp.asarray(grp, BF16), jnp.asarray(eq, BF16), jnp.asarray(ek, BF16),
            jnp.asarray(oneq), jnp.asarray(onek))


def _prep(z3, zs3, q_norm, k_norm, kv_norm, b_forget, w_ukv, ts):
    b, s, _ = z3.shape
    grp, eq, ek, oneq, onek = _aug_constants()
    gq = (jnp.tile(q_norm, N_HEADS) * HEAD_DIM ** -0.5).reshape(1, W_ATT)
    gk = jnp.tile(k_norm, N_HEADS).reshape(1, W_ATT)
    bf = jnp.pad(b_forget, (SM_FF, LANES - N_HEADS - SM_FF)).reshape(1, LANES)
    const = lambda shape: pl.BlockSpec(shape, lambda bi, si: (0,) * len(shape))
    zblk = lambda unit: pl.BlockSpec((1, ts, W_ATT), lambda bi, si: (bi, si, unit // 4))
    seq_out = lambda w: pl.BlockSpec((1, ts, w), lambda bi, si: (bi, si, 0))
    return pl.pallas_call(
        functools.partial(_prep_kernel, ts=ts),
        grid=(b, s // ts),
        in_specs=[zblk(U_FQ), zblk(U_FK), zblk(U_DLAT),
                  pl.BlockSpec((1, ts, LANES), lambda bi, si: (bi, si, 0)),
                  const((1, W_ATT)), const((1, W_ATT)), const((1, KV_RANK)), const((1, LANES)),
                  const((KV_RANK, 2 * W_ATT)), const((W_ATT, W_ATT)),
                  const((3, LANES, LANES)), const((3, LANES, LANES)), const((1, LANES)), const((1, LANES))],
        out_specs=[seq_out(W_ATT), seq_out(W_ATT), seq_out(LANES), seq_out(LANES), seq_out(W_ATT), seq_out(W_ATT)],
        out_shape=[jax.ShapeDtypeStruct((b, s, W_ATT), BF16), jax.ShapeDtypeStruct((b, s, W_ATT), BF16),
                   jax.ShapeDtypeStruct((b, s, LANES), BF16), jax.ShapeDtypeStruct((b, s, LANES), BF16),
                   jax.ShapeDtypeStruct((b, s, W_ATT), BF16), jax.ShapeDtypeStruct((b, s, W_ATT), BF16)],
        scratch_shapes=[pltpu.VMEM((1, LANES), F32)],
        compiler_params=_cparams(("parallel", "arbitrary")),
        name="mixer_prep",
    )(z3, z3, z3, zs3, gq, gk, kv_norm.reshape(1, KV_RANK), bf, w_ukv.astype(BF16), grp, eq, ek, oneq, onek)


def _nt_dot(a, b):
    return lax.dot_general(a, b, (((1,), (1,)), ((), ())), preferred_element_type=F32)


def _half_mask(shape, half):
    lane = lax.broadcasted_iota(jnp.int32, shape, 1)
    return (lane >= HEAD_DIM * half) & (lane < HEAD_DIM * (half + 1))


def _value_tiles(v, t):
    b, s, _ = v.shape
    vt = v.reshape(b, s // t, t, N_PAIRS, LANES)
    return jnp.transpose(vt, (0, 3, 1, 4, 2))


def _pair_out(acc0, acc1):
    return jnp.transpose(jnp.concatenate([acc0, acc1], axis=0))


def _online_softmax_pair(last, scores, weighted_values, mask_last, t):
    def soft(s, m, l):
        m_new = jnp.maximum(m, jnp.max(s, axis=0, keepdims=True))
        alpha = jnp.exp(m - m_new)
        p = jnp.exp(s - m_new)
        return m_new, alpha * l + jnp.sum(p, axis=0, keepdims=True), alpha, p.astype(BF16)

    def drain(j, p, alpha, acc):
        return tuple(alpha[h] * acc[h] + weighted_values(j, h, p[h]) for h in range(2))

    def body(n, c):
        s, p_prev, a_prev, m, l, acc = c
        s_next = scores(n + 1)
        acc = drain(jnp.maximum(n - 1, 0), p_prev, a_prev, acc)
        r = [soft(s[h], m[h], l[h]) for h in range(2)]
        return (s_next, (r[0][3], r[1][3]), (r[0][2], r[1][2]), (r[0][0], r[1][0]), (r[0][1], r[1][1]), acc)

    two = lambda x: (x, x)
    init = (scores(0), two(jnp.zeros((t, t), BF16)), two(jnp.ones((1, t), F32)),
            two(jnp.full((1, t), NEG, F32)), two(jnp.zeros((1, t), F32)), two(jnp.zeros((HEAD_DIM, t), F32)))
    s, p_prev, a_prev, m, l, acc = lax.fori_loop(0, last, body, init)
    acc = drain(jnp.maximum(last - 1, 0), p_prev, a_prev, acc)
    if mask_last is not None:
        s = tuple(mask_last(x) for x in s)
    r = [soft(s[h], m[h], l[h]) for h in range(2)]
    acc = drain(last, (r[0][3], r[1][3]), (r[0][2], r[1][2]), acc)
    return acc[0] / r[0][1], acc[1] / r[1][1]


def _fox_kernel(q_ref, k_ref, vt_ref, aq_ref, ak_ref, fo_ref, o_ref, *, t):
    pair = pl.program_id(1)
    i = pl.program_id(2)
    q = q_ref[0].astype(F32)
    aq = aq_ref[0].astype(F32)
    lane = lax.broadcasted_iota(jnp.int32, (t, LANES), 1)
    krow = lax.broadcasted_iota(jnp.int32, (t, t), 0)
    qcol = lax.broadcasted_iota(jnp.int32, (t, t), 1)
    causal = krow <= qcol
    qcs = []
    for half in range(2):
        head = 2 * pair + half
        qm = jnp.where(_half_mask((t, LANES), half), q, 0.0).astype(BF16)
        am = jnp.where((lane >= 8 * head) & (lane < 8 * head + 6), aq, 0.0).astype(BF16)
        qcs.append(jnp.concatenate([qm, am], axis=1))

    def scores(j):
        ks = pl.multiple_of(j * t, t)
        kc = jnp.concatenate([k_ref[0, pl.ds(ks, t), :], ak_ref[0, pl.ds(ks, t), :]], axis=1)
        return tuple(_nt_dot(kc, qcs[half]) for half in range(2))

    def weighted_values(j, half, p):
        return jnp.dot(vt_ref[0, 0, j, HEAD_DIM * half:HEAD_DIM * (half + 1), :], p, preferred_element_type=F32)

    o0, o1 = _online_softmax_pair(i, scores, weighted_values, lambda s: jnp.where(causal, s, NEG), t)
    o = _pair_out(o0, o1)
    o_ref[0] = (o * jax.nn.sigmoid(fo_ref[0].astype(F32))).astype(o_ref.dtype)


def _fox(qn, kn, z3, aq, ak, t):
    b, s, _ = qn.shape
    vt = _value_tiles(z3[:, :, U_FV * LANES:(U_FV + 4) * LANES], t)
    qblk = lambda unit: pl.BlockSpec((1, t, LANES), lambda bi, p, i: (bi, i, unit + p))
    return pl.pallas_call(
        functools.partial(_fox_kernel, t=t),
        grid=(b, N_PAIRS, s // t),
        in_specs=[qblk(0),
                  pl.BlockSpec((1, s, LANES), lambda bi, p, i: (bi, 0, p)),
                  pl.BlockSpec((1, 1, s // t, LANES, t), lambda bi, p, i: (bi, p, 0, 0, 0)),
                  pl.BlockSpec((1, t, LANES), lambda bi, p, i: (bi, i, 0)),
                  pl.BlockSpec((1, s, LANES), lambda bi, p, i: (bi, 0, 0)),
                  qblk(U_FO)],
        out_specs=qblk(0),
        out_shape=jax.ShapeDtypeStruct((b, s, W_ATT), BF16),
        compiler_params=_cparams(("parallel", "parallel", "arbitrary")),
        name="fox_attention",
    )(qn, kn, vt, aq, ak, z3)


def _sb_kernel(q_ref, k_ref, vt_ref, o_ref, *, t):
    i = pl.program_id(2)
    q = q_ref[0].astype(F32) * (HEAD_DIM ** -0.5)
    krow = lax.broadcasted_iota(jnp.int32, (t, t), 0)
    qcol = lax.broadcasted_iota(jnp.int32, (t, t), 1)
    strict = krow < qcol
    after = (qcol > krow).astype(BF16)
    qms = [jnp.where(_half_mask((t, LANES), half), q, 0.0).astype(BF16) for half in range(2)]

    def step(j, carry, diag):
        ks = pl.multiple_of(j * t, t)
        k = k_ref[0, pl.ds(ks, t), :]
        vt = vt_ref[0, 0, j]
        new = []
        for half in range(2):
            rsum, acc = carry[half]
            z = _nt_dot(k, qms[half])
            lz = _log_sigmoid(z)
            l1m = lz - z
            if diag:
                l1m = jnp.where(strict, l1m, 0.0)
            hi = l1m.astype(BF16)
            lo = (l1m - hi.astype(F32)).astype(BF16)
            suffix = (jnp.dot(after, hi, preferred_element_type=F32)
                      + jnp.dot(after, lo, preferred_element_type=F32)) + rsum
            a = jnp.exp(lz + suffix)
            if diag:
                a = jnp.where(strict, a, 0.0)
            pv = jnp.dot(vt[HEAD_DIM * half:HEAD_DIM * (half + 1), :], a.astype(BF16), preferred_element_type=F32)
            new.append((rsum + jnp.sum(l1m, axis=0, keepdims=True), acc + pv))
        return tuple(new)

    zero = (jnp.zeros((1, t), F32), jnp.zeros((HEAD_DIM, t), F32))
    carry = step(i, (zero, zero), True)

    def more(c):
        n, ((r0, _), (r1, _)) = c
        return (n < i) & (jnp.maximum(jnp.max(r0), jnp.max(r1)) > SB_UNDERFLOW)

    def body(c):
        n, carry = c
        return n + 1, step(i - 1 - n, carry, False)

    _, ((_, acc0), (_, acc1)) = lax.while_loop(more, body, (jnp.int32(0), carry))
    o_ref[0] = _pair_out(acc0, acc1).astype(o_ref.dtype)


def _sb(z3, t):
    b, s, _ = z3.shape
    vt = _value_tiles(z3[:, :, U_SV * LANES:(U_SV + 4) * LANES], t)
    return pl.pallas_call(
        functools.partial(_sb_kernel, t=t),
        grid=(b, N_PAIRS, s // t),
        in_specs=[pl.BlockSpec((1, t, LANES), lambda bi, p, i: (bi, i, U_SQ + p)),
                  pl.BlockSpec((1, s, LANES), lambda bi, p, i: (bi, 0, U_SK + p)),
                  pl.BlockSpec((1, 1, s // t, LANES, t), lambda bi, p, i: (bi, p, 0, 0, 0))],
        out_specs=pl.BlockSpec((1, t, LANES), lambda bi, p, i: (bi, i, p)),
        out_shape=jax.ShapeDtypeStruct((b, s, W_ATT), BF16),
        compiler_params=_cparams(("parallel", "parallel", "arbitrary")),
        name="stickbreak_attention",
    )(z3, z3, vt)


def _t5_bucket(n):
    max_exact = N_BUCKETS // 2
    nf = jnp.maximum(n, 1).astype(F32)
    large = max_exact + (jnp.log(nf / max_exact) / math.log(MAX_DISTANCE / max_exact)
                         * (N_BUCKETS - max_exact)).astype(jnp.int32)
    large = jnp.minimum(large, N_BUCKETS - 1)
    return jnp.where(n < max_exact, n, large)


def _bias_tiles_kernel(relb_ref, o_ref, *, t):
    h = pl.program_id(0)
    krow = lax.broadcasted_iota(jnp.int32, (t, t), 0)
    qcol = lax.broadcasted_iota(jnp.int32, (t, t), 1)
    o_ref[0, 0] = jnp.full((t, t), relb_ref[N_BUCKETS - 1, h], F32)
    for slot, shift in ((1, t), (2, 0)):
        bucket = _t5_bucket(jnp.maximum(qcol - krow + shift, 0))
        val = jnp.full((t, t), relb_ref[0, h], F32)
        for k in range(1, N_BUCKETS):
            val = jnp.where(bucket == k, relb_ref[k, h], val)
        o_ref[0, slot] = val


def _bias_tiles(rel_bias, t):
    assert t >= MAX_DISTANCE
    return pl.pallas_call(
        functools.partial(_bias_tiles_kernel, t=t),
        grid=(N_HEADS,),
        in_specs=[pl.BlockSpec(memory_space=pltpu.SMEM)],
        out_specs=pl.BlockSpec((1, 3, t, t), lambda h: (h, 0, 0, 0)),
        out_shape=jax.ShapeDtypeStruct((N_HEADS, 3, t, t), F32),
        compiler_params=_cparams(("arbitrary",)),
        name="t5_bias_tiles",
    )(rel_bias)


def _dsa_kernel(dq_ref, qi_ref, zs_ref, kidx_ref, dk_ref, dvt_ref, bias_ref, o_ref, key_ref, madd_ref,
                *, t, n_sel):
    i = pl.program_id(1)
    nch = i + 1
    krow = lax.broadcasted_iota(jnp.int32, (t, t), 0)
    qcol = lax.broadcasted_iota(jnp.int32, (t, t), 1)
    idx_scale = (IDX_DIM ** -0.5) * (IDX_HEADS ** -0.5)

    zst = jnp.transpose(zs_ref[0])
    qi = qi_ref[0].astype(F32)
    qih, wih = [], []
    for h in range(IDX_HEADS):
        blk = qi[:, (h // 2) * LANES:(h // 2 + 1) * LANES]
        qih.append(jnp.where(_half_mask((t, LANES), h % 2), blk, 0.0).astype(BF16))
        wih.append(zst[SM_DIW + h:SM_DIW + h + 1, :] * idx_scale)

    def score_chunk(j, _):
        ks = pl.multiple_of(j * t, t)
        kc = kidx_ref[0, pl.ds(ks, t), :]
        sc = jnp.zeros((t, t), F32)
        for h in range(IDX_HEADS):
            sc = sc + jnp.maximum(_nt_dot(kc, qih[h]), 0.0) * wih[h]
        sc = jnp.where(sc == 0.0, 0.0, sc)
        sc = jnp.where(j * t + krow <= i * t + qcol, sc, -jnp.inf)
        bits = pltpu.bitcast(sc, jnp.int32)
        key_ref[j] = jnp.where(bits < 0, bits ^ jnp.int32(0x7FFFFFFF), bits)
        return 0

    lax.fori_loop(0, nch, score_chunk, 0)

    def count_keys(pred):
        def body(j, c):
            return c + jnp.sum(jnp.where(pred(key_ref[j]), 1.0, 0.0), axis=0, keepdims=True)
        return lax.fori_loop(0, nch, body, jnp.zeros((1, t), F32))

    def bit_step(n, thr):
        cand = thr + lax.shift_left(jnp.int32(1), 31 - n)
        cnt = count_keys(lambda k: k >= cand)
        return jnp.where(cnt >= float(n_sel), cand, thr)

    thr = lax.fori_loop(0, 32, bit_step, jnp.full((1, t), INT_MIN, jnp.int32))
    need = float(n_sel) - count_keys(lambda k: k > thr)

    upto = (qcol <= krow).astype(BF16)

    def mask_chunk(j, seen):
        key = key_ref[j]
        eq = key == thr
        rank = jnp.dot(upto, jnp.where(eq, 1.0, 0.0).astype(BF16), preferred_element_type=F32) + seen
        sel = (key > thr) | (eq & (rank <= need))
        sel = sel & (j * t + krow <= i * t + qcol)
        madd_ref[j] = jnp.where(sel, 0.0, NEG)
        return rank[t - 1:t, :]

    lax.fori_loop(0, nch, mask_chunk, jnp.zeros((1, t), F32))

    for pair in range(N_PAIRS):
        lo, hi_ = pair * LANES, (pair + 1) * LANES
        qp = dq_ref[0, :, lo:hi_].astype(F32) * (HEAD_DIM ** -0.5)
        qms = [jnp.where(_half_mask((t, LANES), half), qp, 0.0).astype(BF16) for half in range(2)]

        def scores(j, pair=pair, lo=lo, hi_=hi_, qms=qms):
            ks = pl.multiple_of(j * t, t)
            slot = jnp.clip(j - i + 2, 0, 2)
            k = dk_ref[0, pl.ds(ks, t), lo:hi_]
            madd = madd_ref[j]
            return tuple(_nt_dot(k, qms[half]) + bias_ref[2 * pair + half, slot] + madd for half in range(2))

        def weighted_values(j, half, p, pair=pair):
            return jnp.dot(dvt_ref[0, pair, j, HEAD_DIM * half:HEAD_DIM * (half + 1), :], p,
                           preferred_element_type=F32)

        o0, o1 = _online_softmax_pair(i, scores, weighted_values, None, t)
        o_ref[0, :, lo:hi_] = _pair_out(o0, o1).astype(o_ref.dtype)


def _dsa(z3, zs3, dk, dv, bias, t):
    b, s, _ = z3.shape
    n_sel = min(TOPK_MAX, s // 4)
    dvt = _value_tiles(dv, t)
    return pl.pallas_call(
        functools.partial(_dsa_kernel, t=t, n_sel=n_sel),
        grid=(b, s // t),
        in_specs=[pl.BlockSpec((1, t, W_ATT), lambda bi, i: (bi, i, U_DQ // 4)),
                  pl.BlockSpec((1, t, 2 * LANES), lambda bi, i: (bi, i, U_DIQ // 2)),
                  pl.BlockSpec((1, t, LANES), lambda bi, i: (bi, i, 0)),
                  pl.BlockSpec((1, s, LANES), lambda bi, i: (bi, 0, U_DIK)),
                  pl.BlockSpec((1, s, W_ATT), lambda bi, i: (bi, 0, 0)),
                  pl.BlockSpec((1, N_PAIRS, s // t, LANES, t), lambda bi, i: (bi, 0, 0, 0, 0)),
                  pl.BlockSpec((N_HEADS, 3, t, t), lambda bi, i: (0, 0, 0, 0))],
        out_specs=pl.BlockSpec((1, t, W_ATT), lambda bi, i: (bi, i, 0)),
        out_shape=jax.ShapeDtypeStruct((b, s, W_ATT), BF16),
        scratch_shapes=[pltpu.VMEM((s // t, t, t), jnp.int32), pltpu.VMEM((s // t, t, t), F32)],
        compiler_params=_cparams(("parallel", "arbitrary")),
        name="dsa_attention",
    )(z3, z3, zs3, z3, dk, dvt, bias)


def _merge_kernel(yf_ref, ys_ref, yd_ref, wf_ref, ws_ref, wd_ref, g0_ref, g1_ref, g2_ref, o_ref):
    def branch(y_ref, w_ref, g_ref):
        proj = jnp.dot(y_ref[...], w_ref[...].astype(BF16), preferred_element_type=F32)
        return jax.nn.sigmoid(g_ref[...].astype(F32)) * proj

    o_ref[...] = (branch(yf_ref, wf_ref, g0_ref) + branch(ys_ref, ws_ref, g1_ref)
                  + branch(yd_ref, wd_ref, g2_ref)).astype(o_ref.dtype)


def _merge(yf, ys, yd, wf, ws, wd, z, d, tm, tn):
    m = yf.shape[0]
    yblk = pl.BlockSpec((tm, W_ATT), lambda i, j: (i, 0))
    wblk = pl.BlockSpec((W_ATT, tn), lambda i, j: (0, j))
    gblk = lambda g: pl.BlockSpec((tm, tn), lambda i, j: (i, (U_GATES * LANES + g * d) // tn + j))
    return pl.pallas_call(
        _merge_kernel,
        grid=(m // tm, d // tn),
        in_specs=[yblk, yblk, yblk, wblk, wblk, wblk, gblk(0), gblk(1), gblk(2)],
        out_specs=pl.BlockSpec((tm, tn), lambda i, j: (i, j)),
        out_shape=jax.ShapeDtypeStruct((m, d), BF16),
        compiler_params=_cparams(("parallel", "arbitrary")),
        name="branch_merge",
    )(yf, ys, yd, wf, ws, wd, z, z, z)


def _matmul_res_kernel(a_ref, w_ref, r_ref, o_ref):
    o_ref[...] = r_ref[...] + jnp.dot(a_ref[...], w_ref[...].astype(BF16), preferred_element_type=F32)


def _matmul_res(a, w, res, tm, tn):
    m, k = a.shape
    n = w.shape[1]
    return pl.pallas_call(
        _matmul_res_kernel,
        grid=(m // tm, n // tn),
        in_specs=[pl.BlockSpec((tm, k), lambda i, j: (i, 0)), pl.BlockSpec((k, tn), lambda i, j: (0, j)),
                  pl.BlockSpec((tm, tn), lambda i, j: (i, j))],
        out_specs=pl.BlockSpec((tm, tn), lambda i, j: (i, j)),
        out_shape=jax.ShapeDtypeStruct((m, n), F32),
        compiler_params=_cparams(("parallel", "arbitrary")),
        name="out_proj_residual",
    )(a, w, res)


def _row_copy(src_hbm, row, dst_vmem, slot, sem):
    return pltpu.make_async_copy(src_hbm.at[pl.ds(row, 1)], dst_vmem.at[pl.ds(slot, 1)], sem)


def _ffn_kernel(te_ref, nv_ref, src_ref, x_ref, wg_ref, wu_ref, wd_ref, *rest, grouped, tm, nf):
    ti = pl.program_id(0)
    f = pl.program_id(1)
    if grouped:
        scale_ref, o_ref, gbuf, xs, sem = rest
    else:
        (o_ref,) = rest

    @pl.when(f == 0)
    def _():
        o_ref[...] = jnp.zeros_like(o_ref)

    if grouped:
        per = pl.cdiv(tm, nf - 1)

        def issue_rows(tile, lo, n):
            def issue(r, _):
                _row_copy(x_ref, src_ref[tile * tm + lo + r], gbuf, lo + r, sem).start()
                return 0
            lax.fori_loop(0, n, issue, 0)

        @pl.when((ti == 0) & (f == 0))
        def _():
            issue_rows(0, 0, tm)

        @pl.when((f == 0) & (ti < nv_ref[0]))
        def _():
            def wait(r, _):
                _row_copy(x_ref, 0, gbuf, r, sem).wait()
                return 0
            lax.fori_loop(0, tm, wait, 0)
            xs[...] = gbuf[...].astype(BF16)

        @pl.when((f > 0) & (ti + 1 < nv_ref[0]))
        def _():
            lo = (f - 1) * per
            issue_rows(ti + 1, lo, jnp.clip(tm - lo, 0, per))

    @pl.when(ti < nv_ref[0])
    def _():
        x = xs[...] if grouped else x_ref[...]
        g = jnp.dot(x, wg_ref[0].astype(BF16), preferred_element_type=F32)
        u = jnp.dot(x, wu_ref[0].astype(BF16), preferred_element_type=F32)
        a = (g * jax.nn.sigmoid(g) * u).astype(BF16)
        o_ref[...] += jnp.dot(a, wd_ref[0].astype(BF16), preferred_element_type=F32)

    if grouped:
        @pl.when(f == nf - 1)
        def _():
            o_ref[...] = o_ref[...] * scale_ref[...]


def _ffn(x, wg, wu, wd, tile_expert, n_valid, tm, fc, src=None, row_scale=None):
    grouped = src is not None
    d = x.shape[1]
    p = src.shape[0] if grouped else x.shape[0]
    nf = wg.shape[2] // fc

    def fidx(ti, f, nv):
        return jnp.where(ti < nv[0], f, nf - 1)

    wspecs = [pl.BlockSpec((1, d, fc), lambda ti, f, te, nv, sr: (te[ti], 0, fidx(ti, f, nv))),
              pl.BlockSpec((1, d, fc), lambda ti, f, te, nv, sr: (te[ti], 0, fidx(ti, f, nv))),
              pl.BlockSpec((1, fc, d), lambda ti, f, te, nv, sr: (te[ti], fidx(ti, f, nv), 0))]
    if grouped:
        in_specs = [pl.BlockSpec(memory_space=pl.ANY)] + wspecs + [
            pl.BlockSpec((tm, 1), lambda ti, f, te, nv, sr: (ti, 0))]
        args = [x, wg, wu, wd, row_scale]
        scratch = [pltpu.VMEM((tm, d), F32), pltpu.VMEM((tm, d), BF16), pltpu.SemaphoreType.DMA(())]
    else:
        in_specs = [pl.BlockSpec((tm, d), lambda ti, f, te, nv, sr: (ti, 0))] + wspecs
        args = [x, wg, wu, wd]
        scratch = []
        src = jnp.zeros((1,), jnp.int32)
    grid_spec = pltpu.PrefetchScalarGridSpec(
        num_scalar_prefetch=3,
        grid=(p // tm, nf),
        in_specs=in_specs,
        out_specs=pl.BlockSpec((tm, d), lambda ti, f, te, nv, sr: (ti, 0)),
        scratch_shapes=scratch,
    )
    return pl.pallas_call(
        functools.partial(_ffn_kernel, grouped=grouped, tm=tm, nf=nf),
        grid_spec=grid_spec,
        out_shape=jax.ShapeDtypeStruct((p, d), F32),
        compiler_params=_cparams(("arbitrary", "arbitrary")),
        name="grouped_swiglu" if grouped else "dense_swiglu",
    )(tile_expert, n_valid, src, *args)


def _combine_kernel(pos_ref, x_ref, y_ref, g_ref, o_ref, buf, sem, *, tt, n, norm):
    i = pl.program_id(0)

    def issue_tile(tile, slot):
        def issue(r, _):
            for k in range(2):
                _row_copy(y_ref, pos_ref[k * n + tile * tt + r], buf.at[slot, k], r, sem.at[slot]).start()
            return 0
        lax.fori_loop(0, tt, issue, 0)

    @pl.when(i == 0)
    def _():
        issue_tile(0, 0)

    @pl.when(i + 1 < pl.num_programs(0))
    def _():
        issue_tile(i + 1, (i + 1) % 2)

    slot = i % 2

    def wait(r, _):
        for k in range(2):
            _row_copy(y_ref, 0, buf.at[slot, k], r, sem.at[slot]).wait()
        return 0

    lax.fori_loop(0, tt, wait, 0)
    o = x_ref[...] + (buf[slot, 0] + buf[slot, 1])
    if norm:
        o = o * lax.rsqrt(jnp.mean(o * o, axis=-1, keepdims=True) + EPS) * g_ref[...]
    o_ref[...] = o


def _combine(x2, y, pos, tt, gain=None):
    n, d = x2.shape
    norm = gain is not None
    grid_spec = pltpu.PrefetchScalarGridSpec(
        num_scalar_prefetch=1,
        grid=(n // tt,),
        in_specs=[pl.BlockSpec((tt, d), lambda i, ps: (i, 0)), pl.BlockSpec(memory_space=pl.ANY),
                  pl.BlockSpec((1, d), lambda i, ps: (0, 0))],
        out_specs=pl.BlockSpec((tt, d), lambda i, ps: (i, 0)),
        scratch_shapes=[pltpu.VMEM((2, 2, tt, d), F32), pltpu.SemaphoreType.DMA((2,))],
    )
    return pl.pallas_call(
        functools.partial(_combine_kernel, tt=tt, n=n, norm=norm),
        grid_spec=grid_spec,
        out_shape=jax.ShapeDtypeStruct((n, d), F32),
        compiler_params=_cparams(("arbitrary",)),
        name="moe_combine",
    )(pos, x2, y, (gain if norm else jnp.ones((d,), F32)).reshape(1, d))


def _moe(x2, h, info, wg, wu, wd, tm, fc, out_gain=None):
    n, d = h.shape
    e = wg.shape[0]
    eid = jnp.concatenate([info[:, 2], info[:, 3]]).astype(jnp.int32)
    wts = jnp.concatenate([info[:, 0], info[:, 1]])
    tok = jnp.tile(jnp.arange(n, dtype=jnp.int32), 2)
    onehot = (eid[:, None] == jnp.arange(e, dtype=jnp.int32)[None, :]).astype(jnp.int32)
    rank = jnp.sum((jnp.cumsum(onehot, axis=0) - 1) * onehot, axis=1)
    counts = jnp.sum(onehot, axis=0)
    padded = ((counts + tm - 1) // tm) * tm
    ends = jnp.cumsum(padded)
    starts = ends - padded
    pos = starts[eid] + rank
    n_tiles = (2 * n) // tm + e
    p = n_tiles * tm
    src = jnp.zeros((p,), jnp.int32).at[pos].set(tok)
    scale = jnp.zeros((p,), F32).at[pos].set(wts)
    n_valid = (ends[e - 1] // tm).astype(jnp.int32)
    tile_start = jnp.arange(n_tiles, dtype=jnp.int32) * tm
    tile_start = jnp.minimum(tile_start, (n_valid - 1) * tm)
    tile_expert = jnp.sum((tile_start[:, None] >= ends[None, :]).astype(jnp.int32), axis=1)
    y = _ffn(h, wg, wu, wd, tile_expert, n_valid.reshape(1), tm, fc, src=src, row_scale=scale.reshape(p, 1))
    return _combine(x2, y, pos, _tile(n, 256), out_gain)


def _tile(total, want):
    t = min(total, want)
    assert total % t == 0
    return t


def kernel(x, w_in, b_forget, q_norm, k_norm, kv_norm, w_ukv, w_br_fox, w_br_sb, w_br_dsa, w_out, rel_bias, norm_mix, norm_ffn, w_ffn_gate, w_ffn_up, w_ffn_down, w_router, w_moe_gate, w_moe_up, w_moe_down, norm_final):
    b, s, d = x.shape
    m = b * s
    depth = w_in.shape[0]
    ta = _tile(s, 256)
    tm_norm = _tile(m, 512)
    tm = _tile(m, 1024)
    tn = 512
    fc = 256
    tm_moe = _tile(2 * m, 1024)

    bias = _bias_tiles(rel_bias, ta)
    x2 = x.reshape(m, d)
    for l in range(depth):
        h = _rmsnorm(x2, norm_mix[l], BF16, tm_norm)
        z, zs = _inproj(h, _relayout_w_in(w_in, l, _tile(d, 256)), tm, tn)
        z3 = z.reshape(b, s, Z_COLS)
        zs3 = zs.reshape(b, s, LANES)
        qn, kn, aq, ak, dk, dv = _prep(z3, zs3, q_norm[l], k_norm[l], kv_norm[l], b_forget[l], w_ukv[l], ta)
        y_fox = _fox(qn, kn, z3, aq, ak, ta).reshape(m, W_ATT)
        y_sb = _sb(z3, ta).reshape(m, W_ATT)
        y_dsa = _dsa(z3, zs3, dk, dv, bias, ta).reshape(m, W_ATT)
        mix = _merge(y_fox, y_sb, y_dsa, w_br_fox[l], w_br_sb[l], w_br_dsa[l], z, d, tm, tn)
        x2 = _matmul_res(mix, w_out[l], x2, tm, tn)
        j = l // 2
        if l % 2 == 0:
            h = _rmsnorm(x2, norm_ffn[l], BF16, tm_norm)
            n_tiles = m // tm
            x2 = x2 + _ffn(h, w_ffn_gate[j][None], w_ffn_up[j][None], w_ffn_down[j][None],
                           jnp.zeros((n_tiles,), jnp.int32), jnp.full((1,), n_tiles, jnp.int32), tm, fc)
        else:
            h, info = _rmsnorm_router(x2, norm_ffn[l], w_router[j], tm_norm)
            last = l == depth - 1
            x2 = _moe(x2, h, info, w_moe_gate[j], w_moe_up[j], w_moe_down[j], tm_moe, fc,
                      norm_final if last else None)
    if depth % 2 == 1:
        x2 = _rmsnorm(x2, norm_final, F32, tm_norm)
    return x2.reshape(b, s, d)
```

```python
import functools
import math

import jax
import jax.numpy as jnp
import numpy as np
from jax import lax
from jax.experimental import pallas as pl
from jax.experimental.pallas import tpu as pltpu

F32 = jnp.float32
BF16 = jnp.bfloat16

HEAD_DIM = 64
N_HEADS = 8
N_PAIRS = N_HEADS // 2
W_ATT = N_HEADS * HEAD_DIM
KV_RANK = 256
IDX_HEADS = 4
IDX_DIM = 64
TOPK_MAX = 256
N_BUCKETS = 32
MAX_DISTANCE = 128
N_EXPERTS = 8
N_BRANCH = 3
EPS = 1e-6
LANES = 128
NEG = -1e30
INT_MIN = -2147483648
SB_UNDERFLOW = -104.0

U_FQ, U_FK, U_FV, U_FO, U_SQ, U_SK, U_SV, U_DQ = 0, 4, 8, 12, 16, 20, 24, 28
U_DLAT, U_DIQ, U_DIK, U_SMALL, U_GATES = 32, 34, 36, 37, 40
N_UNITS = 88
Z_COLS = N_UNITS * LANES
SM_FF, SM_DIW = 0, 8

VMEM_LIMIT = 56 * 1024 * 1024


def _cparams(sem, row_gather=False):
    return pltpu.CompilerParams(dimension_semantics=sem, vmem_limit_bytes=VMEM_LIMIT,
                                disable_bounds_checks=row_gather)


def _log_sigmoid(x):
    return jnp.minimum(x, 0.0) - jnp.log1p(jnp.exp(-jnp.abs(x)))


def _rmsnorm_kernel(x_ref, g_ref, o_ref):
    x = x_ref[...]
    ms = jnp.mean(x * x, axis=-1, keepdims=True)
    o_ref[...] = (x * lax.rsqrt(ms + EPS) * g_ref[...]).astype(o_ref.dtype)


def _rmsnorm(x, g, out_dtype, tm):
    m, d = x.shape
    return pl.pallas_call(
        _rmsnorm_kernel,
        grid=(m // tm,),
        in_specs=[pl.BlockSpec((tm, d), lambda i: (i, 0)), pl.BlockSpec((1, d), lambda i: (0, 0))],
        out_specs=pl.BlockSpec((tm, d), lambda i: (i, 0)),
        out_shape=jax.ShapeDtypeStruct((m, d), out_dtype),
        compiler_params=_cparams(("parallel",)),
        name="rmsnorm",
    )(x, g.reshape(1, d))


def _add_rmsnorm_kernel(x_ref, y_ref, g_ref, xo_ref, h_ref):
    x = x_ref[...] + y_ref[...]
    xo_ref[...] = x
    ms = jnp.mean(x * x, axis=-1, keepdims=True)
    h_ref[...] = (x * lax.rsqrt(ms + EPS) * g_ref[...]).astype(h_ref.dtype)


def _add_rmsnorm(x, y, g, tm):
    m, d = x.shape
    blk = pl.BlockSpec((tm, d), lambda i: (i, 0))
    return pl.pallas_call(
        _add_rmsnorm_kernel,
        grid=(m // tm,),
        in_specs=[blk, blk, pl.BlockSpec((1, d), lambda i: (0, 0))],
        out_specs=[blk, blk],
        out_shape=[jax.ShapeDtypeStruct((m, d), F32), jax.ShapeDtypeStruct((m, d), BF16)],
        compiler_params=_cparams(("parallel",)),
        name="add_rmsnorm",
    )(x, y, g.reshape(1, d))


def _rmsnorm_router_kernel(x_ref, g_ref, wr_ref, o_ref, info_ref):
    x = x_ref[...]
    ms = jnp.mean(x * x, axis=-1, keepdims=True)
    h = x * lax.rsqrt(ms + EPS) * g_ref[...]
    o_ref[...] = h.astype(o_ref.dtype)
    logits = jnp.dot(h, wr_ref[...], precision=lax.Precision.HIGHEST, preferred_element_type=F32)
    lane = lax.broadcasted_iota(jnp.int32, logits.shape, 1).astype(F32)
    lg = jnp.where(lane < N_EXPERTS, logits, -jnp.inf)
    v1 = jnp.max(lg, axis=-1, keepdims=True)
    i1 = jnp.min(jnp.where(lg == v1, lane, float(LANES)), axis=-1, keepdims=True)
    lg2 = jnp.where(lane == i1, -jnp.inf, lg)
    v2 = jnp.max(lg2, axis=-1, keepdims=True)
    i2 = jnp.min(jnp.where(lg2 == v2, lane, float(LANES)), axis=-1, keepdims=True)
    e2 = jnp.exp(v2 - v1)
    w1 = 1.0 / (1.0 + e2)
    w2 = e2 / (1.0 + e2)
    info = jnp.where(lane == 0.0, w1, jnp.where(lane == 1.0, w2, jnp.where(
        lane == 2.0, i1, jnp.where(lane == 3.0, i2, 0.0))))
    info_ref[...] = info


def _rmsnorm_router(x, g, w_router, tm):
    m, d = x.shape
    wr = jnp.pad(w_router, ((0, 0), (0, LANES - w_router.shape[1])))
    return pl.pallas_call(
        _rmsnorm_router_kernel,
        grid=(m // tm,),
        in_specs=[pl.BlockSpec((tm, d), lambda i: (i, 0)), pl.BlockSpec((1, d), lambda i: (0, 0)),
                  pl.BlockSpec((d, LANES), lambda i: (0, 0))],
        out_specs=[pl.BlockSpec((tm, d), lambda i: (i, 0)), pl.BlockSpec((tm, LANES), lambda i: (i, 0))],
        out_shape=[jax.ShapeDtypeStruct((m, d), F32), jax.ShapeDtypeStruct((m, LANES), F32)],
        compiler_params=_cparams(("parallel",)),
        name="rmsnorm_router",
    )(x, g.reshape(1, d), wr)


def _inproj_kernel(a_ref, w_ref, z_ref, zs_ref, *, small_tile):
    acc = jnp.dot(a_ref[...], w_ref[...], preferred_element_type=F32)
    z_ref[...] = acc.astype(z_ref.dtype)

    @pl.when(pl.program_id(1) == small_tile)
    def _():
        off = (U_SMALL * LANES) % acc.shape[1]
        zs_ref[...] = acc[:, off:off + LANES]


def _inproj(h, w, tm, tn):
    m, d = h.shape
    return pl.pallas_call(
        functools.partial(_inproj_kernel, small_tile=(U_SMALL * LANES) // tn),
        grid=(m // tm, Z_COLS // tn),
        in_specs=[pl.BlockSpec((tm, d), lambda i, j: (i, 0)), pl.BlockSpec((d, tn), lambda i, j: (0, j))],
        out_specs=[pl.BlockSpec((tm, tn), lambda i, j: (i, j)), pl.BlockSpec((tm, LANES), lambda i, j: (i, 0))],
        out_shape=[jax.ShapeDtypeStruct((m, Z_COLS), BF16), jax.ShapeDtypeStruct((m, LANES), F32)],
        compiler_params=_cparams(("parallel", "arbitrary")),
        name="inproj",
    )(h, w)


W_IN_ORDER = (("fq", W_ATT), ("fk", W_ATT), ("fv", W_ATT), ("ff", N_HEADS), ("fo", W_ATT),
              ("sq", W_ATT), ("sk", W_ATT), ("sv", W_ATT), ("dq", W_ATT), ("dlat", KV_RANK),
              ("diq", IDX_HEADS * IDX_DIM), ("dik", IDX_DIM), ("diw", IDX_HEADS))
RELAYOUT_COLS = 4 * LANES


def _relayout_kernel(tbl_ref, w_ref, ff_ref, o_ref, *, layer, special):
    j = pl.program_id(0)

    @pl.when(j != special)
    def _():
        o_ref[...] = jnp.transpose(w_ref[:, layer, :]).astype(o_ref.dtype)

    @pl.when(j == special)
    def _():
        xt = jnp.transpose(w_ref[0:LANES, layer, :])
        fft = jnp.transpose(ff_ref[:, layer, :])
        lane = lax.broadcasted_iota(jnp.int32, xt.shape, 1)
        dik2 = jnp.where(lane < IDX_DIM, xt, pltpu.roll(xt, IDX_DIM, axis=1))
        diw = pltpu.roll(xt, LANES - IDX_DIM + SM_DIW, axis=1)
        small = jnp.where(lane < N_HEADS, fft,
                          jnp.where((lane >= SM_DIW) & (lane < SM_DIW + IDX_HEADS), diw, 0.0))
        o_ref[:, 0:LANES] = dik2.astype(o_ref.dtype)
        o_ref[:, LANES:2 * LANES] = small.astype(o_ref.dtype)
        o_ref[:, 2 * LANES:] = jnp.zeros((o_ref.shape[0], o_ref.shape[1] - 2 * LANES), o_ref.dtype)


def _relayout_w_in(w_in, layer):
    _, d, n_in = w_in.shape
    src, o = {}, 0
    for name, width in W_IN_ORDER + (("gates", N_BRANCH * d),):
        src[name] = o
        o += width
    assert o == n_in and U_GATES * LANES + N_BRANCH * d == Z_COLS and SM_FF == 0
    assert src["diq"] == src["dlat"] + KV_RANK and src["diw"] == src["dik"] + IDX_DIM
    units = {U_FQ: "fq", U_FK: "fk", U_FV: "fv", U_FO: "fo", U_SQ: "sq", U_SK: "sk", U_SV: "sv",
             U_DQ: "dq", U_DLAT: "dlat", U_DIK: "dik"}
    per = RELAYOUT_COLS // LANES
    tbl = [src[units[u]] for u in range(0, U_GATES, per)]
    tbl += [src["gates"] + k * RELAYOUT_COLS for k in range(N_BRANCH * d // RELAYOUT_COLS)]
    elem = lambda rows: (pl.Element(rows), pl.Element(w_in.shape[0]), pl.Element(d))
    grid_spec = pltpu.PrefetchScalarGridSpec(
        num_scalar_prefetch=1,
        grid=(Z_COLS // RELAYOUT_COLS,),
        in_specs=[pl.BlockSpec(elem(RELAYOUT_COLS), lambda j, tbl: (tbl[j], 0, 0)),
                  pl.BlockSpec(elem(LANES), lambda j, tbl: (src["ff"], 0, 0))],
        out_specs=pl.BlockSpec((d, RELAYOUT_COLS), lambda j, tbl: (0, j)),
    )
    wt = jnp.transpose(w_in, (2, 0, 1))
    return pl.pallas_call(
        functools.partial(_relayout_kernel, layer=layer, special=U_DIK // per),
        grid_spec=grid_spec,
        out_shape=jax.ShapeDtypeStruct((d, Z_COLS), BF16),
        compiler_params=_cparams(("arbitrary",)),
        name="w_in_relayout",
    )(jnp.asarray(tbl, jnp.int32), wt, wt)


def _prep_kernel(zq_ref, zk_ref, zl_ref, zs_ref, gq_ref, gk_ref, gkv_ref, bf_ref, wukv_ref,
                 grp_ref, eq_ref, ek_ref, oneq_ref, onek_ref,
                 qn_ref, kn_ref, aq_ref, ak_ref, dk_ref, dv_ref, carry_ref, *, ts):
    @pl.when(pl.program_id(1) == 0)
    def _():
        carry_ref[...] = jnp.zeros_like(carry_ref)

    def head_norm(z_ref, g_ref):
        x = z_ref[0].astype(F32)
        sq = x * x
        hi = sq.astype(BF16)
        lo = (sq - hi.astype(F32)).astype(BF16)
        ms = (jnp.dot(hi, grp_ref[...], preferred_element_type=F32)
              + jnp.dot(lo, grp_ref[...], preferred_element_type=F32)) * (1.0 / HEAD_DIM)
        return x * lax.rsqrt(ms + EPS) * g_ref[...]

    qn_ref[0] = head_norm(zq_ref, gq_ref).astype(BF16)
    kn_ref[0] = head_norm(zk_ref, gk_ref).astype(BF16)

    lane = lax.broadcasted_iota(jnp.int32, (ts, LANES), 1)
    lf = jnp.where(lane < N_HEADS, _log_sigmoid(zs_ref[0] + bf_ref[...]), 0.0)
    r = lax.broadcasted_iota(jnp.int32, (ts, ts), 0)
    c_ = lax.broadcasted_iota(jnp.int32, (ts, ts), 1)
    tri = (c_ <= r).astype(F32)
    c = jnp.dot(tri, lf, precision=lax.Precision.HIGHEST, preferred_element_type=F32) + carry_ref[...]
    carry_ref[...] = c[ts - 1:ts, :]
    c0 = c.astype(BF16)
    r1 = c - c0.astype(F32)
    c1 = r1.astype(BF16)
    c2 = (r1 - c1.astype(F32)).astype(BF16)
    pieces = (c0, c1, c2)
    aq = oneq_ref[...]
    ak = onek_ref[...]
    for k in range(3):
        aq = aq + jnp.dot(pieces[k], eq_ref[k], preferred_element_type=F32)
        ak = ak - jnp.dot(pieces[k], ek_ref[k], preferred_element_type=F32)
    aq_ref[0] = aq.astype(BF16)
    ak_ref[0] = ak.astype(BF16)

    lat = zl_ref[0, :, :KV_RANK].astype(F32)
    msl = jnp.mean(lat * lat, axis=-1, keepdims=True)
    latn = (lat * lax.rsqrt(msl + EPS) * gkv_ref[...]).astype(BF16)
    kv = jnp.dot(latn, wukv_ref[...], preferred_element_type=F32)
    dk_ref[0] = kv[:, :W_ATT].astype(BF16)
    dv_ref[0] = kv[:, W_ATT:].astype(BF16)


def _aug_constants():
    eq = np.zeros((3, LANES, LANES), np.float32)
    ek = np.zeros((3, LANES, LANES), np.float32)
    oneq = np.zeros((1, LANES), np.float32)
    onek = np.zeros((1, LANES), np.float32)
    for h in range(N_HEADS):
        for k in range(3):
            eq[k, h, 8 * h + k] = 1.0
            ek[k, h, 8 * h + 3 + k] = 1.0
            oneq[0, 8 * h + 3 + k] = 1.0
            onek[0, 8 * h + k] = 1.0
    grp = np.kron(np.eye(N_HEADS, dtype=np.float32), np.ones((HEAD_DIM, HEAD_DIM), np.float32))
    return (jnp.asarray(grp, BF16), jnp.asarray(eq, BF16), jnp.asarray(ek, BF16),
            jnp.asarray(oneq), jnp.asarray(onek))


def _prep(z3, zs3, q_norm, k_norm, kv_norm, b_forget, w_ukv, ts):
    b, s, _ = z3.shape
    grp, eq, ek, oneq, onek = _aug_constants()
    gq = (jnp.tile(q_norm, N_HEADS) * HEAD_DIM ** -0.5).reshape(1, W_ATT)
    gk = jnp.tile(k_norm, N_HEADS).reshape(1, W_ATT)
    bf = jnp.pad(b_forget, (SM_FF, LANES - N_HEADS - SM_FF)).reshape(1, LANES)
    const = lambda shape: pl.BlockSpec(shape, lambda bi, si: (0,) * len(shape))
    zblk = lambda unit: pl.BlockSpec((1, ts, W_ATT), lambda bi, si: (bi, si, unit // 4))
    seq_out = lambda w: pl.BlockSpec((1, ts, w), lambda bi, si: (bi, si, 0))
    return pl.pallas_call(
        functools.partial(_prep_kernel, ts=ts),
        grid=(b, s // ts),
        in_specs=[zblk(U_FQ), zblk(U_FK), zblk(U_DLAT),
                  pl.BlockSpec((1, ts, LANES), lambda bi, si: (bi, si, 0)),
                  const((1, W_ATT)), const((1, W_ATT)), const((1, KV_RANK)), const((1, LANES)),
                  const((KV_RANK, 2 * W_ATT)), const((W_ATT, W_ATT)),
                  const((3, LANES, LANES)), const((3, LANES, LANES)), const((1, LANES)), const((1, LANES))],
        out_specs=[seq_out(W_ATT), seq_out(W_ATT), seq_out(LANES), seq_out(LANES), seq_out(W_ATT), seq_out(W_ATT)],
        out_shape=[jax.ShapeDtypeStruct((b, s, W_ATT), BF16), jax.ShapeDtypeStruct((b, s, W_ATT), BF16),
                   jax.ShapeDtypeStruct((b, s, LANES), BF16), jax.ShapeDtypeStruct((b, s, LANES), BF16),
                   jax.ShapeDtypeStruct((b, s, W_ATT), BF16), jax.ShapeDtypeStruct((b, s, W_ATT), BF16)],
        scratch_shapes=[pltpu.VMEM((1, LANES), F32)],
        compiler_params=_cparams(("parallel", "arbitrary")),
        name="mixer_prep",
    )(z3, z3, z3, zs3, gq, gk, kv_norm.reshape(1, KV_RANK), bf, w_ukv.astype(BF16), grp, eq, ek, oneq, onek)


def _nt_dot(a, b):
    return lax.dot_general(a, b, (((1,), (1,)), ((), ())), preferred_element_type=F32)


def _half_mask(shape, half):
    lane = lax.broadcasted_iota(jnp.int32, shape, 1)
    return (lane >= HEAD_DIM * half) & (lane < HEAD_DIM * (half + 1))


def _value_tiles(v, t):
    b, s, _ = v.shape
    vt = v.reshape(b, s // t, t, N_PAIRS, LANES)
    return jnp.transpose(vt, (0, 3, 1, 4, 2))


def _pair_out(acc0, acc1):
    return jnp.transpose(jnp.concatenate([acc0, acc1], axis=0))


def _online_softmax_pair(last, scores, weighted_values, mask_last, t):
    def soft(s, m, l):
        m_new = jnp.maximum(m, jnp.max(s, axis=0, keepdims=True))
        alpha = jnp.exp(m - m_new)
        p = jnp.exp(s - m_new)
        return m_new, alpha * l + jnp.sum(p, axis=0, keepdims=True), alpha, p.astype(BF16)

    def drain(j, p, alpha, acc):
        return tuple(alpha[h] * acc[h] + weighted_values(j, h, p[h]) for h in range(2))

    def body(n, c):
        s, p_prev, a_prev, m, l, acc = c
        s_next = scores(n + 1)
        acc = drain(jnp.maximum(n - 1, 0), p_prev, a_prev, acc)
        r = [soft(s[h], m[h], l[h]) for h in range(2)]
        return (s_next, (r[0][3], r[1][3]), (r[0][2], r[1][2]), (r[0][0], r[1][0]), (r[0][1], r[1][1]), acc)

    two = lambda x: (x, x)
    init = (scores(0), two(jnp.zeros((t, t), BF16)), two(jnp.ones((1, t), F32)),
            two(jnp.full((1, t), NEG, F32)), two(jnp.zeros((1, t), F32)), two(jnp.zeros((HEAD_DIM, t), F32)))
    s, p_prev, a_prev, m, l, acc = lax.fori_loop(0, last, body, init)
    acc = drain(jnp.maximum(last - 1, 0), p_prev, a_prev, acc)
    if mask_last is not None:
        s = tuple(mask_last(x) for x in s)
    r = [soft(s[h], m[h], l[h]) for h in range(2)]
    acc = drain(last, (r[0][3], r[1][3]), (r[0][2], r[1][2]), acc)
    return acc[0] / r[0][1], acc[1] / r[1][1]


def _fox_kernel(q_ref, k_ref, vt_ref, aq_ref, ak_ref, fo_ref, o_ref, *, t):
    pair = pl.program_id(1)
    i = pl.program_id(2)
    q = q_ref[0].astype(F32)
    aq = aq_ref[0].astype(F32)
    lane = lax.broadcasted_iota(jnp.int32, (t, LANES), 1)
    krow = lax.broadcasted_iota(jnp.int32, (t, t), 0)
    qcol = lax.broadcasted_iota(jnp.int32, (t, t), 1)
    causal = krow <= qcol
    qcs = []
    for half in range(2):
        head = 2 * pair + half
        qm = jnp.where(_half_mask((t, LANES), half), q, 0.0).astype(BF16)
        am = jnp.where((lane >= 8 * head) & (lane < 8 * head + 6), aq, 0.0).astype(BF16)
        qcs.append(jnp.concatenate([qm, am], axis=1))

    def scores(j):
        ks = pl.multiple_of(j * t, t)
        kc = jnp.concatenate([k_ref[0, pl.ds(ks, t), :], ak_ref[0, pl.ds(ks, t), :]], axis=1)
        return tuple(_nt_dot(kc, qcs[half]) for half in range(2))

    def weighted_values(j, half, p):
        return jnp.dot(vt_ref[0, 0, j, HEAD_DIM * half:HEAD_DIM * (half + 1), :], p, preferred_element_type=F32)

    o0, o1 = _online_softmax_pair(i, scores, weighted_values, lambda s: jnp.where(causal, s, NEG), t)
    o = _pair_out(o0, o1)
    o_ref[0] = (o * jax.nn.sigmoid(fo_ref[0].astype(F32))).astype(o_ref.dtype)


def _fox(qn, kn, z3, aq, ak, t):
    b, s, _ = qn.shape
    vt = _value_tiles(z3[:, :, U_FV * LANES:(U_FV + 4) * LANES], t)
    qblk = lambda unit: pl.BlockSpec((1, t, LANES), lambda bi, p, i: (bi, i, unit + p))
    return pl.pallas_call(
        functools.partial(_fox_kernel, t=t),
        grid=(b, N_PAIRS, s // t),
        in_specs=[qblk(0),
                  pl.BlockSpec((1, s, LANES), lambda bi, p, i: (bi, 0, p)),
                  pl.BlockSpec((1, 1, s // t, LANES, t), lambda bi, p, i: (bi, p, 0, 0, 0)),
                  pl.BlockSpec((1, t, LANES), lambda bi, p, i: (bi, i, 0)),
                  pl.BlockSpec((1, s, LANES), lambda bi, p, i: (bi, 0, 0)),
                  qblk(U_FO)],
        out_specs=qblk(0),
        out_shape=jax.ShapeDtypeStruct((b, s, W_ATT), BF16),
        compiler_params=_cparams(("parallel", "parallel", "arbitrary")),
        name="fox_attention",
    )(qn, kn, vt, aq, ak, z3)


def _sb_kernel(q_ref, k_ref, vt_ref, o_ref, *, t):
    i = pl.program_id(2)
    q = q_ref[0].astype(F32) * (HEAD_DIM ** -0.5)
    krow = lax.broadcasted_iota(jnp.int32, (t, t), 0)
    qcol = lax.broadcasted_iota(jnp.int32, (t, t), 1)
    strict = krow < qcol
    after = (qcol > krow).astype(BF16)
    qms = [jnp.where(_half_mask((t, LANES), half), q, 0.0).astype(BF16) for half in range(2)]

    def local(j, diag):
        ks = pl.multiple_of(j * t, t)
        k = k_ref[0, pl.ds(ks, t), :]
        out = []
        for half in range(2):
            z = _nt_dot(k, qms[half])
            lz = _log_sigmoid(z)
            l1m = lz - z
            if diag:
                l1m = jnp.where(strict, l1m, 0.0)
            hi = l1m.astype(BF16)
            lo = (l1m - hi.astype(F32)).astype(BF16)
            suffix = (jnp.dot(after, hi, preferred_element_type=F32)
                      + jnp.dot(after, lo, preferred_element_type=F32))
            out.append((lz + suffix, jnp.sum(l1m, axis=0, keepdims=True)))
        return out

    def finish(j, loc, carry, keep):
        new = []
        for half in range(2):
            logw, colsum = loc[half]
            rsum, acc = carry[half]
            a = jnp.exp(logw + rsum)
            if keep is not None:
                a = jnp.where(keep, a, 0.0)
            pv = jnp.dot(vt_ref[0, 0, j, HEAD_DIM * half:HEAD_DIM * (half + 1), :], a.astype(BF16),
                         preferred_element_type=F32)
            new.append((rsum + colsum, acc + pv))
        return tuple(new)

    prev = jnp.maximum(i - 1, 0)
    loc_diag = local(i, True)
    loc_prev = local(prev, False)
    zero = (jnp.zeros((1, t), F32), jnp.zeros((HEAD_DIM, t), F32))
    carry = finish(i, loc_diag, (zero, zero), strict)
    carry = finish(prev, loc_prev, carry, i > 0)

    def more(c):
        n, ((r0, _), (r1, _)) = c
        return (n < i) & (jnp.maximum(jnp.max(r0), jnp.max(r1)) > SB_UNDERFLOW)

    def body(c):
        n, carry = c
        j = i - 1 - n
        return n + 1, finish(j, local(j, False), carry, None)

    _, ((_, acc0), (_, acc1)) = lax.while_loop(more, body, (jnp.int32(1), carry))
    o_ref[0] = _pair_out(acc0, acc1).astype(o_ref.dtype)


def _sb(z3, t):
    b, s, _ = z3.shape
    vt = _value_tiles(z3[:, :, U_SV * LANES:(U_SV + 4) * LANES], t)
    return pl.pallas_call(
        functools.partial(_sb_kernel, t=t),
        grid=(b, N_PAIRS, s // t),
        in_specs=[pl.BlockSpec((1, t, LANES), lambda bi, p, i: (bi, i, U_SQ + p)),
                  pl.BlockSpec((1, s, LANES), lambda bi, p, i: (bi, 0, U_SK + p)),
                  pl.BlockSpec((1, 1, s // t, LANES, t), lambda bi, p, i: (bi, p, 0, 0, 0))],
        out_specs=pl.BlockSpec((1, t, LANES), lambda bi, p, i: (bi, i, p)),
        out_shape=jax.ShapeDtypeStruct((b, s, W_ATT), BF16),
        compiler_params=_cparams(("parallel", "parallel", "arbitrary")),
        name="stickbreak_attention",
    )(z3, z3, vt)


def _t5_bucket(n):
    max_exact = N_BUCKETS // 2
    nf = jnp.maximum(n, 1).astype(F32)
    large = max_exact + (jnp.log(nf / max_exact) / math.log(MAX_DISTANCE / max_exact)
                         * (N_BUCKETS - max_exact)).astype(jnp.int32)
    large = jnp.minimum(large, N_BUCKETS - 1)
    return jnp.where(n < max_exact, n, large)


def _bias_tiles_kernel(relb_ref, o_ref, *, t):
    h = pl.program_id(0)
    krow = lax.broadcasted_iota(jnp.int32, (t, t), 0)
    qcol = lax.broadcasted_iota(jnp.int32, (t, t), 1)
    o_ref[0, 0] = jnp.full((t, t), relb_ref[N_BUCKETS - 1, h], F32)
    for slot, shift in ((1, t), (2, 0)):
        bucket = _t5_bucket(jnp.maximum(qcol - krow + shift, 0))
        val = jnp.full((t, t), relb_ref[0, h], F32)
        for k in range(1, N_BUCKETS):
            val = jnp.where(bucket == k, relb_ref[k, h], val)
        o_ref[0, slot] = val


def _bias_tiles(rel_bias, t):
    assert t >= MAX_DISTANCE
    return pl.pallas_call(
        functools.partial(_bias_tiles_kernel, t=t),
        grid=(N_HEADS,),
        in_specs=[pl.BlockSpec(memory_space=pltpu.SMEM)],
        out_specs=pl.BlockSpec((1, 3, t, t), lambda h: (h, 0, 0, 0)),
        out_shape=jax.ShapeDtypeStruct((N_HEADS, 3, t, t), F32),
        compiler_params=_cparams(("arbitrary",)),
        name="t5_bias_tiles",
    )(rel_bias)


def _dsa_kernel(dq_ref, qi_ref, zs_ref, kidx_ref, dk_ref, dvt_ref, bias_ref, o_ref, key_ref, madd_ref,
                *, t, n_sel):
    i = pl.program_id(1)
    nch = i + 1
    krow = lax.broadcasted_iota(jnp.int32, (t, t), 0)
    qcol = lax.broadcasted_iota(jnp.int32, (t, t), 1)
    idx_scale = (IDX_DIM ** -0.5) * (IDX_HEADS ** -0.5)

    zst = jnp.transpose(zs_ref[0])
    qi = qi_ref[0].astype(F32)
    qih, wih = [], []
    for h in range(IDX_HEADS):
        blk = qi[:, (h // 2) * LANES:(h // 2 + 1) * LANES]
        qih.append(jnp.where(_half_mask((t, LANES), h % 2), blk, 0.0).astype(BF16))
        wih.append(zst[SM_DIW + h:SM_DIW + h + 1, :] * idx_scale)

    def score_chunk(j, _):
        ks = pl.multiple_of(j * t, t)
        kc = kidx_ref[0, pl.ds(ks, t), :]
        sc = jnp.zeros((t, t), F32)
        for h in range(IDX_HEADS):
            sc = sc + jnp.maximum(_nt_dot(kc, qih[h]), 0.0) * wih[h]
        sc = jnp.where(sc == 0.0, 0.0, sc)
        sc = jnp.where(j * t + krow <= i * t + qcol, sc, -jnp.inf)
        bits = pltpu.bitcast(sc, jnp.int32)
        key_ref[j] = jnp.where(bits < 0, bits ^ jnp.int32(0x7FFFFFFF), bits)
        return 0

    lax.fori_loop(0, nch, score_chunk, 0)

    def count_keys(pred):
        def body(j, c):
            return c + jnp.sum(jnp.where(pred(key_ref[j]), 1.0, 0.0), axis=0, keepdims=True)
        return lax.fori_loop(0, nch, body, jnp.zeros((1, t), F32))

    def bit_step(n, thr):
        cand = thr + lax.shift_left(jnp.int32(1), 31 - n)
        cnt = count_keys(lambda k: k >= cand)
        return jnp.where(cnt >= float(n_sel), cand, thr)

    thr = lax.fori_loop(0, 32, bit_step, jnp.full((1, t), INT_MIN, jnp.int32))
    need = float(n_sel) - count_keys(lambda k: k > thr)

    upto = (qcol <= krow).astype(BF16)

    def mask_chunk(j, seen):
        key = key_ref[j]
        eq = key == thr
        rank = jnp.dot(upto, jnp.where(eq, 1.0, 0.0).astype(BF16), preferred_element_type=F32) + seen
        sel = (key > thr) | (eq & (rank <= need))
        sel = sel & (j * t + krow <= i * t + qcol)
        madd_ref[j] = jnp.where(sel, 0.0, NEG)
        return rank[t - 1:t, :]

    lax.fori_loop(0, nch, mask_chunk, jnp.zeros((1, t), F32))

    for pair in range(N_PAIRS):
        lo, hi_ = pair * LANES, (pair + 1) * LANES
        qp = dq_ref[0, :, lo:hi_].astype(F32) * (HEAD_DIM ** -0.5)
        qms = [jnp.where(_half_mask((t, LANES), half), qp, 0.0).astype(BF16) for half in range(2)]

        def scores(j, pair=pair, lo=lo, hi_=hi_, qms=qms):
            ks = pl.multiple_of(j * t, t)
            slot = jnp.clip(j - i + 2, 0, 2)
            k = dk_ref[0, pl.ds(ks, t), lo:hi_]
            madd = madd_ref[j]
            return tuple(_nt_dot(k, qms[half]) + bias_ref[2 * pair + half, slot] + madd for half in range(2))

        def weighted_values(j, half, p, pair=pair):
            return jnp.dot(dvt_ref[0, pair, j, HEAD_DIM * half:HEAD_DIM * (half + 1), :], p,
                           preferred_element_type=F32)

        o0, o1 = _online_softmax_pair(i, scores, weighted_values, None, t)
        o_ref[0, :, lo:hi_] = _pair_out(o0, o1).astype(o_ref.dtype)


def _dsa(z3, zs3, dk, dv, bias, t):
    b, s, _ = z3.shape
    n_sel = min(TOPK_MAX, s // 4)
    dvt = _value_tiles(dv, t)
    return pl.pallas_call(
        functools.partial(_dsa_kernel, t=t, n_sel=n_sel),
        grid=(b, s // t),
        in_specs=[pl.BlockSpec((1, t, W_ATT), lambda bi, i: (bi, i, U_DQ // 4)),
                  pl.BlockSpec((1, t, 2 * LANES), lambda bi, i: (bi, i, U_DIQ // 2)),
                  pl.BlockSpec((1, t, LANES), lambda bi, i: (bi, i, 0)),
                  pl.BlockSpec((1, s, LANES), lambda bi, i: (bi, 0, U_DIK)),
                  pl.BlockSpec((1, s, W_ATT), lambda bi, i: (bi, 0, 0)),
                  pl.BlockSpec((1, N_PAIRS, s // t, LANES, t), lambda bi, i: (bi, 0, 0, 0, 0)),
                  pl.BlockSpec((N_HEADS, 3, t, t), lambda bi, i: (0, 0, 0, 0))],
        out_specs=pl.BlockSpec((1, t, W_ATT), lambda bi, i: (bi, i, 0)),
        out_shape=jax.ShapeDtypeStruct((b, s, W_ATT), BF16),
        scratch_shapes=[pltpu.VMEM((s // t, t, t), jnp.int32), pltpu.VMEM((s // t, t, t), F32)],
        compiler_params=_cparams(("parallel", "arbitrary")),
        name="dsa_attention",
    )(z3, z3, zs3, z3, dk, dvt, bias)


def _merge_kernel(yf_ref, ys_ref, yd_ref, wf_ref, ws_ref, wd_ref, g0_ref, g1_ref, g2_ref, o_ref):
    def branch(y_ref, w_ref, g_ref):
        proj = jnp.dot(y_ref[...], w_ref[...].astype(BF16), preferred_element_type=F32)
        return jax.nn.sigmoid(g_ref[...].astype(F32)) * proj

    o_ref[...] = (branch(yf_ref, wf_ref, g0_ref) + branch(ys_ref, ws_ref, g1_ref)
                  + branch(yd_ref, wd_ref, g2_ref)).astype(o_ref.dtype)


def _merge(yf, ys, yd, wf, ws, wd, z, d, tm, tn):
    m = yf.shape[0]
    yblk = pl.BlockSpec((tm, W_ATT), lambda i, j: (i, 0))
    wblk = pl.BlockSpec((W_ATT, tn), lambda i, j: (0, j))
    gblk = lambda g: pl.BlockSpec((tm, tn), lambda i, j: (i, (U_GATES * LANES + g * d) // tn + j))
    return pl.pallas_call(
        _merge_kernel,
        grid=(m // tm, d // tn),
        in_specs=[yblk, yblk, yblk, wblk, wblk, wblk, gblk(0), gblk(1), gblk(2)],
        out_specs=pl.BlockSpec((tm, tn), lambda i, j: (i, j)),
        out_shape=jax.ShapeDtypeStruct((m, d), BF16),
        compiler_params=_cparams(("parallel", "arbitrary")),
        name="branch_merge",
    )(yf, ys, yd, wf, ws, wd, z, z, z)


def _matmul_res_kernel(a_ref, w_ref, r_ref, o_ref):
    o_ref[...] = r_ref[...] + jnp.dot(a_ref[...], w_ref[...].astype(BF16), preferred_element_type=F32)


def _matmul_res(a, w, res, tm, tn):
    m, k = a.shape
    n = w.shape[1]
    return pl.pallas_call(
        _matmul_res_kernel,
        grid=(m // tm, n // tn),
        in_specs=[pl.BlockSpec((tm, k), lambda i, j: (i, 0)), pl.BlockSpec((k, tn), lambda i, j: (0, j)),
                  pl.BlockSpec((tm, tn), lambda i, j: (i, j))],
        out_specs=pl.BlockSpec((tm, tn), lambda i, j: (i, j)),
        out_shape=jax.ShapeDtypeStruct((m, n), F32),
        compiler_params=_cparams(("parallel", "arbitrary")),
        name="out_proj_residual",
    )(a, w, res)


def _row_copy(src_hbm, row, dst_vmem, slot, sem):
    return pltpu.make_async_copy(src_hbm.at[pl.ds(row, 1)], dst_vmem.at[pl.ds(slot, 1)], sem)


def _ffn_kernel(te_ref, nv_ref, src_ref, x_ref, wg_ref, wu_ref, wd_ref, *rest, grouped, tm, nf):
    ti = pl.program_id(0)
    f = pl.program_id(1)
    if grouped:
        o_ref, gbuf, xs, sem = rest
    else:
        (o_ref,) = rest

    @pl.when(f == 0)
    def _():
        o_ref[...] = jnp.zeros_like(o_ref)

    if grouped:
        per = pl.cdiv(tm, nf - 1)

        def issue_rows(tile, lo, n):
            def issue(r, _):
                _row_copy(x_ref, src_ref[tile * tm + lo + r], gbuf, lo + r, sem).start()
                return 0
            lax.fori_loop(0, n, issue, 0)

        @pl.when((ti == 0) & (f == 0))
        def _():
            issue_rows(0, 0, tm)

        @pl.when((f == 0) & (ti < nv_ref[0]))
        def _():
            def wait(r, _):
                _row_copy(x_ref, 0, gbuf, r, sem).wait()
                return 0
            lax.fori_loop(0, tm, wait, 0)
            xs[...] = gbuf[...].astype(BF16)

        @pl.when((f > 0) & (ti + 1 < nv_ref[0]))
        def _():
            lo = (f - 1) * per
            issue_rows(ti + 1, lo, jnp.clip(tm - lo, 0, per))

    @pl.when(ti < nv_ref[0])
    def _():
        x = xs[...] if grouped else x_ref[...]
        g = jnp.dot(x, wg_ref[0].astype(BF16), preferred_element_type=F32)
        u = jnp.dot(x, wu_ref[0].astype(BF16), preferred_element_type=F32)
        a = (g * jax.nn.sigmoid(g) * u).astype(BF16)
        o_ref[...] += jnp.dot(a, wd_ref[0].astype(BF16), preferred_element_type=F32)


def _ffn(x, wg, wu, wd, tile_expert, n_valid, tm, fc, src=None):
    grouped = src is not None
    d = x.shape[1]
    p = src.shape[0] if grouped else x.shape[0]
    nf = wg.shape[2] // fc

    def chunk(ti, f, nv):
        return jnp.where(ti < nv[0], f, nf - 1)

    wspecs = [pl.BlockSpec((1, d, fc), lambda ti, f, te, nv, sr: (te[ti], 0, chunk(ti, f, nv))),
              pl.BlockSpec((1, d, fc), lambda ti, f, te, nv, sr: (te[ti], 0, chunk(ti, f, nv))),
              pl.BlockSpec((1, fc, d), lambda ti, f, te, nv, sr: (te[ti], chunk(ti, f, nv), 0))]
    scratch = []
    if grouped:
        in_specs = [pl.BlockSpec(memory_space=pl.ANY)] + wspecs
        args = [x, wg, wu, wd]
        scratch += [pltpu.VMEM((tm, d), F32), pltpu.VMEM((tm, d), BF16), pltpu.SemaphoreType.DMA(())]
    else:
        in_specs = [pl.BlockSpec((tm, d), lambda ti, f, te, nv, sr: (ti, 0))] + wspecs
        args = [x, wg, wu, wd]
        src = jnp.zeros((1,), jnp.int32)
    grid_spec = pltpu.PrefetchScalarGridSpec(
        num_scalar_prefetch=3,
        grid=(p // tm, nf),
        in_specs=in_specs,
        out_specs=pl.BlockSpec((tm, d), lambda ti, f, te, nv, sr: (ti, 0)),
        scratch_shapes=scratch,
    )
    return pl.pallas_call(
        functools.partial(_ffn_kernel, grouped=grouped, tm=tm, nf=nf),
        grid_spec=grid_spec,
        out_shape=jax.ShapeDtypeStruct((p, d), F32),
        compiler_params=_cparams(("arbitrary", "arbitrary"), row_gather=grouped),
        name="grouped_swiglu" if grouped else "dense_swiglu",
    )(tile_expert, n_valid, src, *args)


def _combine_kernel(pos_ref, x_ref, info_ref, y_ref, g_ref, o_ref, buf, sem, *, tt, n, norm):
    i = pl.program_id(0)

    def issue_tile(tile, slot):
        def issue(r, _):
            for k in range(2):
                _row_copy(y_ref, pos_ref[k * n + tile * tt + r], buf.at[slot, k], r, sem.at[slot]).start()
            return 0
        lax.fori_loop(0, tt, issue, 0)

    @pl.when(i == 0)
    def _():
        issue_tile(0, 0)

    @pl.when(i + 1 < pl.num_programs(0))
    def _():
        issue_tile(i + 1, (i + 1) % 2)

    slot = i % 2

    def wait(r, _):
        for k in range(2):
            _row_copy(y_ref, 0, buf.at[slot, k], r, sem.at[slot]).wait()
        return 0

    lax.fori_loop(0, tt, wait, 0)
    info = info_ref[...]
    o = x_ref[...] + (info[:, 0:1] * buf[slot, 0] + info[:, 1:2] * buf[slot, 1])
    if norm:
        o = o * lax.rsqrt(jnp.mean(o * o, axis=-1, keepdims=True) + EPS) * g_ref[...]
    o_ref[...] = o


def _combine(x2, info, y, pos, tt, gain=None):
    n, d = x2.shape
    norm = gain is not None
    grid_spec = pltpu.PrefetchScalarGridSpec(
        num_scalar_prefetch=1,
        grid=(n // tt,),
        in_specs=[pl.BlockSpec((tt, d), lambda i, ps: (i, 0)), pl.BlockSpec((tt, LANES), lambda i, ps: (i, 0)),
                  pl.BlockSpec(memory_space=pl.ANY), pl.BlockSpec((1, d), lambda i, ps: (0, 0))],
        out_specs=pl.BlockSpec((tt, d), lambda i, ps: (i, 0)),
        scratch_shapes=[pltpu.VMEM((2, 2, tt, d), F32), pltpu.SemaphoreType.DMA((2,))],
    )
    return pl.pallas_call(
        functools.partial(_combine_kernel, tt=tt, n=n, norm=norm),
        grid_spec=grid_spec,
        out_shape=jax.ShapeDtypeStruct((n, d), F32),
        compiler_params=_cparams(("arbitrary",), row_gather=True),
        name="moe_combine",
    )(pos, x2, info, y, (gain if norm else jnp.ones((d,), F32)).reshape(1, d))


def _moe(x2, h, info, wg, wu, wd, tm, fc, out_gain=None):
    n, d = h.shape
    e = wg.shape[0]
    eid = jnp.concatenate([info[:, 2], info[:, 3]]).astype(jnp.int32)
    tok = jnp.tile(jnp.arange(n, dtype=jnp.int32), 2)
    onehot = (eid[:, None] == jnp.arange(e, dtype=jnp.int32)[None, :]).astype(jnp.int32)
    rank = jnp.sum((jnp.cumsum(onehot, axis=0) - 1) * onehot, axis=1)
    counts = jnp.sum(onehot, axis=0)
    padded = ((counts + tm - 1) // tm) * tm
    ends = jnp.cumsum(padded)
    starts = ends - padded
    pos = starts[eid] + rank
    n_tiles = (2 * n) // tm + e
    p = n_tiles * tm
    src = jnp.zeros((p,), jnp.int32).at[pos].set(tok)
    n_valid = (ends[e - 1] // tm).astype(jnp.int32)
    tile_start = jnp.arange(n_tiles, dtype=jnp.int32) * tm
    tile_start = jnp.minimum(tile_start, (n_valid - 1) * tm)
    tile_expert = jnp.sum((tile_start[:, None] >= ends[None, :]).astype(jnp.int32), axis=1)
    y = _ffn(h, wg, wu, wd, tile_expert, n_valid.reshape(1), tm, fc, src=src)
    return _combine(x2, info, y, pos, _tile(n, 256), out_gain)


def _tile(total, want):
    t = min(total, want)
    assert total % t == 0
    return t


def kernel(x, w_in, b_forget, q_norm, k_norm, kv_norm, w_ukv, w_br_fox, w_br_sb, w_br_dsa, w_out, rel_bias, norm_mix, norm_ffn, w_ffn_gate, w_ffn_up, w_ffn_down, w_router, w_moe_gate, w_moe_up, w_moe_down, norm_final):
    b, s, d = x.shape
    m = b * s
    depth = w_in.shape[0]
    ta = _tile(s, 256)
    tm_norm = _tile(m, 512)
    tm = _tile(m, 1024)
    tn = 512
    fc = 256
    tm_moe = _tile(2 * m, 1024)

    bias = _bias_tiles(rel_bias, ta)
    x2 = x.reshape(m, d)
    pending = None
    for l in range(depth):
        if pending is None:
            h = _rmsnorm(x2, norm_mix[l], BF16, tm_norm)
        else:
            x2, h = _add_rmsnorm(x2, pending, norm_mix[l], tm_norm)
            pending = None
        z, zs = _inproj(h, _relayout_w_in(w_in, l), tm, 2 * tn)
        z3 = z.reshape(b, s, Z_COLS)
        zs3 = zs.reshape(b, s, LANES)
        qn, kn, aq, ak, dk, dv = _prep(z3, zs3, q_norm[l], k_norm[l], kv_norm[l], b_forget[l], w_ukv[l], ta)
        y_fox = _fox(qn, kn, z3, aq, ak, ta).reshape(m, W_ATT)
        y_sb = _sb(z3, ta).reshape(m, W_ATT)
        y_dsa = _dsa(z3, zs3, dk, dv, bias, ta).reshape(m, W_ATT)
        mix = _merge(y_fox, y_sb, y_dsa, w_br_fox[l], w_br_sb[l], w_br_dsa[l], z, d, tm, tn)
        x2 = _matmul_res(mix, w_out[l], x2, tm, tn)
        j = l // 2
        if l % 2 == 0:
            h = _rmsnorm(x2, norm_ffn[l], BF16, tm_norm)
            n_tiles = m // tm
            pending = _ffn(h, w_ffn_gate[j][None], w_ffn_up[j][None], w_ffn_down[j][None],
                           jnp.zeros((n_tiles,), jnp.int32), jnp.full((1,), n_tiles, jnp.int32), tm, fc)
        else:
            h, info = _rmsnorm_router(x2, norm_ffn[l], w_router[j], tm_norm)
            last = l == depth - 1
            x2 = _moe(x2, h, info, w_moe_gate[j], w_moe_up[j], w_moe_down[j], tm_moe, fc,
                      norm_final if last else None)
    if pending is not None:
        x2 = _rmsnorm(x2 + pending, norm_final, F32, tm_norm)
    return x2.reshape(b, s, d)
```

```python
import functools
import math

import jax
import jax.numpy as jnp
import numpy as np
from jax import lax
from jax.experimental import pallas as pl
from jax.experimental.pallas import tpu as pltpu

F32 = jnp.float32
BF16 = jnp.bfloat16

HEAD_DIM = 64
N_HEADS = 8
N_PAIRS = N_HEADS // 2
W_ATT = N_HEADS * HEAD_DIM
KV_RANK = 256
IDX_HEADS = 4
IDX_DIM = 64
TOPK_MAX = 256
N_BUCKETS = 32
MAX_DISTANCE = 128
N_EXPERTS = 8
N_BRANCH = 3
EPS = 1e-6
LANES = 128
NEG = -1e30
INT_MIN = -2147483648
SB_UNDERFLOW = -104.0

U_FQ, U_FK, U_FV, U_FO, U_SQ, U_SK, U_SV, U_DQ = 0, 4, 8, 12, 16, 20, 24, 28
U_DLAT, U_DIQ, U_DIK, U_SMALL, U_GATES = 32, 34, 36, 37, 40
N_UNITS = 88
Z_COLS = N_UNITS * LANES
SM_FF, SM_DIW = 0, 8

VMEM_LIMIT = 56 * 1024 * 1024


def _cparams(sem):
    return pltpu.CompilerParams(dimension_semantics=sem, vmem_limit_bytes=VMEM_LIMIT)


def _log_sigmoid(x):
    return jnp.minimum(x, 0.0) - jnp.log1p(jnp.exp(-jnp.abs(x)))


def _rmsnorm_kernel(x_ref, g_ref, o_ref):
    x = x_ref[...]
    ms = jnp.mean(x * x, axis=-1, keepdims=True)
    o_ref[...] = (x * lax.rsqrt(ms + EPS) * g_ref[...]).astype(o_ref.dtype)


def _rmsnorm(x, g, out_dtype, tm):
    m, d = x.shape
    return pl.pallas_call(
        _rmsnorm_kernel,
        grid=(m // tm,),
        in_specs=[pl.BlockSpec((tm, d), lambda i: (i, 0)), pl.BlockSpec((1, d), lambda i: (0, 0))],
        out_specs=pl.BlockSpec((tm, d), lambda i: (i, 0)),
        out_shape=jax.ShapeDtypeStruct((m, d), out_dtype),
        compiler_params=_cparams(("parallel",)),
        name="rmsnorm",
    )(x, g.reshape(1, d))


def _store_token_major(o_ref, x):
    for s in range(x.shape[1] // LANES):
        o_ref[:, s, :] = x[:, s * LANES:(s + 1) * LANES].astype(o_ref.dtype)


def _load_token_major(buf_ref, rows, s):
    per = buf_ref.shape[0] // rows
    return buf_ref[pl.ds(s, rows, stride=per), :]


def _add_rmsnorm_kernel(x_ref, y_ref, g_ref, xo_ref, h_ref):
    x = x_ref[...] + y_ref[...]
    xo_ref[...] = x
    ms = jnp.mean(x * x, axis=-1, keepdims=True)
    h_ref[...] = (x * lax.rsqrt(ms + EPS) * g_ref[...]).astype(h_ref.dtype)


def _add_rmsnorm(x, y, g, tm):
    m, d = x.shape
    blk = pl.BlockSpec((tm, d), lambda i: (i, 0))
    return pl.pallas_call(
        _add_rmsnorm_kernel,
        grid=(m // tm,),
        in_specs=[blk, blk, pl.BlockSpec((1, d), lambda i: (0, 0))],
        out_specs=[blk, blk],
        out_shape=[jax.ShapeDtypeStruct((m, d), F32), jax.ShapeDtypeStruct((m, d), BF16)],
        compiler_params=_cparams(("parallel",)),
        name="add_rmsnorm",
    )(x, y, g.reshape(1, d))


def _rmsnorm_router_kernel(x_ref, g_ref, wr_ref, o_ref, info_ref):
    x = x_ref[...]
    ms = jnp.mean(x * x, axis=-1, keepdims=True)
    h = x * lax.rsqrt(ms + EPS) * g_ref[...]
    _store_token_major(o_ref, h)
    logits = jnp.dot(h, wr_ref[...], precision=lax.Precision.HIGHEST, preferred_element_type=F32)
    lane = lax.broadcasted_iota(jnp.int32, logits.shape, 1).astype(F32)
    lg = jnp.where(lane < N_EXPERTS, logits, -jnp.inf)
    v1 = jnp.max(lg, axis=-1, keepdims=True)
    i1 = jnp.min(jnp.where(lg == v1, lane, float(LANES)), axis=-1, keepdims=True)
    lg2 = jnp.where(lane == i1, -jnp.inf, lg)
    v2 = jnp.max(lg2, axis=-1, keepdims=True)
    i2 = jnp.min(jnp.where(lg2 == v2, lane, float(LANES)), axis=-1, keepdims=True)
    e2 = jnp.exp(v2 - v1)
    w1 = 1.0 / (1.0 + e2)
    w2 = e2 / (1.0 + e2)
    info = jnp.where(lane == 0.0, w1, jnp.where(lane == 1.0, w2, jnp.where(
        lane == 2.0, i1, jnp.where(lane == 3.0, i2, 0.0))))
    info_ref[...] = info


def _rmsnorm_router(x, g, w_router, tm):
    m, d = x.shape
    wr = jnp.pad(w_router, ((0, 0), (0, LANES - w_router.shape[1])))
    return pl.pallas_call(
        _rmsnorm_router_kernel,
        grid=(m // tm,),
        in_specs=[pl.BlockSpec((tm, d), lambda i: (i, 0)), pl.BlockSpec((1, d), lambda i: (0, 0)),
                  pl.BlockSpec((d, LANES), lambda i: (0, 0))],
        out_specs=[pl.BlockSpec((tm, d // LANES, LANES), lambda i: (i, 0, 0)),
                   pl.BlockSpec((tm, LANES), lambda i: (i, 0))],
        out_shape=[jax.ShapeDtypeStruct((m, d // LANES, LANES), F32), jax.ShapeDtypeStruct((m, LANES), F32)],
        compiler_params=_cparams(("parallel",)),
        name="rmsnorm_router",
    )(x, g.reshape(1, d), wr)


def _inproj_kernel(a_ref, w_ref, z_ref, zs_ref, *, small_tile):
    acc = jnp.dot(a_ref[...], w_ref[...], preferred_element_type=F32)
    z_ref[...] = acc.astype(z_ref.dtype)

    @pl.when(pl.program_id(1) == small_tile)
    def _():
        off = (U_SMALL * LANES) % acc.shape[1]
        zs_ref[...] = acc[:, off:off + LANES]


def _inproj(h, w, tm, tn):
    m, d = h.shape
    return pl.pallas_call(
        functools.partial(_inproj_kernel, small_tile=(U_SMALL * LANES) // tn),
        grid=(m // tm, Z_COLS // tn),
        in_specs=[pl.BlockSpec((tm, d), lambda i, j: (i, 0)), pl.BlockSpec((d, tn), lambda i, j: (0, j))],
        out_specs=[pl.BlockSpec((tm, tn), lambda i, j: (i, j)), pl.BlockSpec((tm, LANES), lambda i, j: (i, 0))],
        out_shape=[jax.ShapeDtypeStruct((m, Z_COLS), BF16), jax.ShapeDtypeStruct((m, LANES), F32)],
        compiler_params=_cparams(("parallel", "arbitrary")),
        name="inproj",
    )(h, w)


W_IN_ORDER = (("fq", W_ATT), ("fk", W_ATT), ("fv", W_ATT), ("ff", N_HEADS), ("fo", W_ATT),
              ("sq", W_ATT), ("sk", W_ATT), ("sv", W_ATT), ("dq", W_ATT), ("dlat", KV_RANK),
              ("diq", IDX_HEADS * IDX_DIM), ("dik", IDX_DIM), ("diw", IDX_HEADS))
RELAYOUT_COLS = 4 * LANES


def _relayout_kernel(tbl_ref, w_ref, ff_ref, o_ref, *, layer, special):
    j = pl.program_id(0)

    @pl.when(j != special)
    def _():
        o_ref[...] = jnp.transpose(w_ref[:, layer, :]).astype(o_ref.dtype)

    @pl.when(j == special)
    def _():
        xt = jnp.transpose(w_ref[0:LANES, layer, :])
        fft = jnp.transpose(ff_ref[:, layer, :])
        lane = lax.broadcasted_iota(jnp.int32, xt.shape, 1)
        dik2 = jnp.where(lane < IDX_DIM, xt, pltpu.roll(xt, IDX_DIM, axis=1))
        diw = pltpu.roll(xt, LANES - IDX_DIM + SM_DIW, axis=1)
        small = jnp.where(lane < N_HEADS, fft,
                          jnp.where((lane >= SM_DIW) & (lane < SM_DIW + IDX_HEADS), diw, 0.0))
        o_ref[:, 0:LANES] = dik2.astype(o_ref.dtype)
        o_ref[:, LANES:2 * LANES] = small.astype(o_ref.dtype)
        o_ref[:, 2 * LANES:] = jnp.zeros((o_ref.shape[0], o_ref.shape[1] - 2 * LANES), o_ref.dtype)


def _relayout_w_in(w_in, layer):
    _, d, n_in = w_in.shape
    src, o = {}, 0
    for name, width in W_IN_ORDER + (("gates", N_BRANCH * d),):
        src[name] = o
        o += width
    assert o == n_in and U_GATES * LANES + N_BRANCH * d == Z_COLS and SM_FF == 0
    assert src["diq"] == src["dlat"] + KV_RANK and src["diw"] == src["dik"] + IDX_DIM
    units = {U_FQ: "fq", U_FK: "fk", U_FV: "fv", U_FO: "fo", U_SQ: "sq", U_SK: "sk", U_SV: "sv",
             U_DQ: "dq", U_DLAT: "dlat", U_DIK: "dik"}
    per = RELAYOUT_COLS // LANES
    tbl = [src[units[u]] for u in range(0, U_GATES, per)]
    tbl += [src["gates"] + k * RELAYOUT_COLS for k in range(N_BRANCH * d // RELAYOUT_COLS)]
    elem = lambda rows: (pl.Element(rows), pl.Element(w_in.shape[0]), pl.Element(d))
    grid_spec = pltpu.PrefetchScalarGridSpec(
        num_scalar_prefetch=1,
        grid=(Z_COLS // RELAYOUT_COLS,),
        in_specs=[pl.BlockSpec(elem(RELAYOUT_COLS), lambda j, tbl: (tbl[j], 0, 0)),
                  pl.BlockSpec(elem(LANES), lambda j, tbl: (src["ff"], 0, 0))],
        out_specs=pl.BlockSpec((d, RELAYOUT_COLS), lambda j, tbl: (0, j)),
    )
    wt = jnp.transpose(w_in, (2, 0, 1))
    return pl.pallas_call(
        functools.partial(_relayout_kernel, layer=layer, special=U_DIK // per),
        grid_spec=grid_spec,
        out_shape=jax.ShapeDtypeStruct((d, Z_COLS), BF16),
        compiler_params=_cparams(("arbitrary",)),
        name="w_in_relayout",
    )(jnp.asarray(tbl, jnp.int32), wt, wt)


def _prep_kernel(zq_ref, zk_ref, zl_ref, zs_ref, gq_ref, gk_ref, gkv_ref, bf_ref, wukv_ref,
                 grp_ref, eq_ref, ek_ref, oneq_ref, onek_ref,
                 qn_ref, kn_ref, aq_ref, ak_ref, dk_ref, dv_ref, carry_ref, *, ts):
    @pl.when(pl.program_id(1) == 0)
    def _():
        carry_ref[...] = jnp.zeros_like(carry_ref)

    def head_norm(z_ref, g_ref):
        x = z_ref[0].astype(F32)
        sq = x * x
        hi = sq.astype(BF16)
        lo = (sq - hi.astype(F32)).astype(BF16)
        ms = (jnp.dot(hi, grp_ref[...], preferred_element_type=F32)
              + jnp.dot(lo, grp_ref[...], preferred_element_type=F32)) * (1.0 / HEAD_DIM)
        return x * lax.rsqrt(ms + EPS) * g_ref[...]

    qn_ref[0] = head_norm(zq_ref, gq_ref).astype(BF16)
    kn_ref[0] = head_norm(zk_ref, gk_ref).astype(BF16)

    lane = lax.broadcasted_iota(jnp.int32, (ts, LANES), 1)
    lf = jnp.where(lane < N_HEADS, _log_sigmoid(zs_ref[0] + bf_ref[...]), 0.0)
    r = lax.broadcasted_iota(jnp.int32, (ts, ts), 0)
    c_ = lax.broadcasted_iota(jnp.int32, (ts, ts), 1)
    tri = (c_ <= r).astype(F32)
    c = jnp.dot(tri, lf, precision=lax.Precision.HIGHEST, preferred_element_type=F32) + carry_ref[...]
    carry_ref[...] = c[ts - 1:ts, :]
    c0 = c.astype(BF16)
    r1 = c - c0.astype(F32)
    c1 = r1.astype(BF16)
    c2 = (r1 - c1.astype(F32)).astype(BF16)
    pieces = (c0, c1, c2)
    aq = oneq_ref[...]
    ak = onek_ref[...]
    for k in range(3):
        aq = aq + jnp.dot(pieces[k], eq_ref[k], preferred_element_type=F32)
        ak = ak - jnp.dot(pieces[k], ek_ref[k], preferred_element_type=F32)
    aq_ref[0] = aq.astype(BF16)
    ak_ref[0] = ak.astype(BF16)

    lat = zl_ref[0, :, :KV_RANK].astype(F32)
    msl = jnp.mean(lat * lat, axis=-1, keepdims=True)
    latn = (lat * lax.rsqrt(msl + EPS) * gkv_ref[...]).astype(BF16)
    kv = jnp.dot(latn, wukv_ref[...], preferred_element_type=F32)
    dk_ref[0] = kv[:, :W_ATT].astype(BF16)
    dv_ref[0] = kv[:, W_ATT:].astype(BF16)


def _aug_constants():
    eq = np.zeros((3, LANES, LANES), np.float32)
    ek = np.zeros((3, LANES, LANES), np.float32)
    oneq = np.zeros((1, LANES), np.float32)
    onek = np.zeros((1, LANES), np.float32)
    for h in range(N_HEADS):
        for k in range(3):
            eq[k, h, 8 * h + k] = 1.0
            ek[k, h, 8 * h + 3 + k] = 1.0
            oneq[0, 8 * h + 3 + k] = 1.0
            onek[0, 8 * h + k] = 1.0
    grp = np.kron(np.eye(N_HEADS, dtype=np.float32), np.ones((HEAD_DIM, HEAD_DIM), np.float32))
    return (jnp.asarray(grp, BF16), jnp.asarray(eq, BF16), jnp.asarray(ek, BF16),
            jnp.asarray(oneq), jnp.asarray(onek))


def _prep(z3, zs3, q_norm, k_norm, kv_norm, b_forget, w_ukv, ts):
    b, s, _ = z3.shape
    grp, eq, ek, oneq, onek = _aug_constants()
    gq = (jnp.tile(q_norm, N_HEADS) * HEAD_DIM ** -0.5).reshape(1, W_ATT)
    gk = jnp.tile(k_norm, N_HEADS).reshape(1, W_ATT)
    bf = jnp.pad(b_forget, (SM_FF, LANES - N_HEADS - SM_FF)).reshape(1, LANES)
    const = lambda shape: pl.BlockSpec(shape, lambda bi, si: (0,) * len(shape))
    zblk = lambda unit: pl.BlockSpec((1, ts, W_ATT), lambda bi, si: (bi, si, unit // 4))
    seq_out = lambda w: pl.BlockSpec((1, ts, w), lambda bi, si: (bi, si, 0))
    return pl.pallas_call(
        functools.partial(_prep_kernel, ts=ts),
        grid=(b, s // ts),
        in_specs=[zblk(U_FQ), zblk(U_FK), zblk(U_DLAT),
                  pl.BlockSpec((1, ts, LANES), lambda bi, si: (bi, si, 0)),
                  const((1, W_ATT)), const((1, W_ATT)), const((1, KV_RANK)), const((1, LANES)),
                  const((KV_RANK, 2 * W_ATT)), const((W_ATT, W_ATT)),
                  const((3, LANES, LANES)), const((3, LANES, LANES)), const((1, LANES)), const((1, LANES))],
        out_specs=[seq_out(W_ATT), seq_out(W_ATT), seq_out(LANES), seq_out(LANES), seq_out(W_ATT), seq_out(W_ATT)],
        out_shape=[jax.ShapeDtypeStruct((b, s, W_ATT), BF16), jax.ShapeDtypeStruct((b, s, W_ATT), BF16),
                   jax.ShapeDtypeStruct((b, s, LANES), BF16), jax.ShapeDtypeStruct((b, s, LANES), BF16),
                   jax.ShapeDtypeStruct((b, s, W_ATT), BF16), jax.ShapeDtypeStruct((b, s, W_ATT), BF16)],
        scratch_shapes=[pltpu.VMEM((1, LANES), F32)],
        compiler_params=_cparams(("parallel", "arbitrary")),
        name="mixer_prep",
    )(z3, z3, z3, zs3, gq, gk, kv_norm.reshape(1, KV_RANK), bf, w_ukv.astype(BF16), grp, eq, ek, oneq, onek)


def _nt_dot(a, b):
    return lax.dot_general(a, b, (((1,), (1,)), ((), ())), preferred_element_type=F32)


def _half_mask(shape, half):
    lane = lax.broadcasted_iota(jnp.int32, shape, 1)
    return (lane >= HEAD_DIM * half) & (lane < HEAD_DIM * (half + 1))


def _value_tiles(v, t):
    b, s, _ = v.shape
    vt = v.reshape(b, s // t, t, N_PAIRS, LANES)
    return jnp.transpose(vt, (0, 3, 1, 4, 2))


def _pair_out(acc0, acc1):
    return jnp.transpose(jnp.concatenate([acc0, acc1], axis=0))


def _online_softmax_pair(last, scores, weighted_values, mask_last, t):
    def soft(s, m, l):
        m_new = jnp.maximum(m, jnp.max(s, axis=0, keepdims=True))
        alpha = jnp.exp(m - m_new)
        p = jnp.exp(s - m_new)
        return m_new, alpha * l + jnp.sum(p, axis=0, keepdims=True), alpha, p.astype(BF16)

    def drain(j, p, alpha, acc):
        return tuple(alpha[h] * acc[h] + weighted_values(j, h, p[h]) for h in range(2))

    def body(n, c):
        s, p_prev, a_prev, m, l, acc = c
        s_next = scores(n + 1)
        acc = drain(jnp.maximum(n - 1, 0), p_prev, a_prev, acc)
        r = [soft(s[h], m[h], l[h]) for h in range(2)]
        return (s_next, (r[0][3], r[1][3]), (r[0][2], r[1][2]), (r[0][0], r[1][0]), (r[0][1], r[1][1]), acc)

    two = lambda x: (x, x)
    init = (scores(0), two(jnp.zeros((t, t), BF16)), two(jnp.ones((1, t), F32)),
            two(jnp.full((1, t), NEG, F32)), two(jnp.zeros((1, t), F32)), two(jnp.zeros((HEAD_DIM, t), F32)))
    s, p_prev, a_prev, m, l, acc = lax.fori_loop(0, last, body, init)
    acc = drain(jnp.maximum(last - 1, 0), p_prev, a_prev, acc)
    if mask_last is not None:
        s = tuple(mask_last(x) for x in s)
    r = [soft(s[h], m[h], l[h]) for h in range(2)]
    acc = drain(last, (r[0][3], r[1][3]), (r[0][2], r[1][2]), acc)
    return acc[0] / r[0][1], acc[1] / r[1][1]


def _fox_kernel(q_ref, k_ref, vt_ref, aq_ref, ak_ref, fo_ref, o_ref, *, t):
    pair = pl.program_id(1)
    i = pl.program_id(2)
    q = q_ref[0].astype(F32)
    aq = aq_ref[0].astype(F32)
    lane = lax.broadcasted_iota(jnp.int32, (t, LANES), 1)
    krow = lax.broadcasted_iota(jnp.int32, (t, t), 0)
    qcol = lax.broadcasted_iota(jnp.int32, (t, t), 1)
    causal = krow <= qcol
    qcs = []
    for half in range(2):
        head = 2 * pair + half
        qm = jnp.where(_half_mask((t, LANES), half), q, 0.0).astype(BF16)
        am = jnp.where((lane >= 8 * head) & (lane < 8 * head + 6), aq, 0.0).astype(BF16)
        qcs.append(jnp.concatenate([qm, am], axis=1))

    def scores(j):
        ks = pl.multiple_of(j * t, t)
        kc = jnp.concatenate([k_ref[0, pl.ds(ks, t), :], ak_ref[0, pl.ds(ks, t), :]], axis=1)
        return tuple(_nt_dot(kc, qcs[half]) for half in range(2))

    def weighted_values(j, half, p):
        return jnp.dot(vt_ref[0, 0, j, HEAD_DIM * half:HEAD_DIM * (half + 1), :], p, preferred_element_type=F32)

    o0, o1 = _online_softmax_pair(i, scores, weighted_values, lambda s: jnp.where(causal, s, NEG), t)
    o = _pair_out(o0, o1)
    o_ref[0] = (o * jax.nn.sigmoid(fo_ref[0].astype(F32))).astype(o_ref.dtype)


def _fox(qn, kn, z3, aq, ak, t):
    b, s, _ = qn.shape
    vt = _value_tiles(z3[:, :, U_FV * LANES:(U_FV + 4) * LANES], t)
    qblk = lambda unit: pl.BlockSpec((1, t, LANES), lambda bi, p, i: (bi, i, unit + p))
    return pl.pallas_call(
        functools.partial(_fox_kernel, t=t),
        grid=(b, N_PAIRS, s // t),
        in_specs=[qblk(0),
                  pl.BlockSpec((1, s, LANES), lambda bi, p, i: (bi, 0, p)),
                  pl.BlockSpec((1, 1, s // t, LANES, t), lambda bi, p, i: (bi, p, 0, 0, 0)),
                  pl.BlockSpec((1, t, LANES), lambda bi, p, i: (bi, i, 0)),
                  pl.BlockSpec((1, s, LANES), lambda bi, p, i: (bi, 0, 0)),
                  qblk(U_FO)],
        out_specs=qblk(0),
        out_shape=jax.ShapeDtypeStruct((b, s, W_ATT), BF16),
        compiler_params=_cparams(("parallel", "parallel", "arbitrary")),
        name="fox_attention",
    )(qn, kn, vt, aq, ak, z3)


def _sb_kernel(q_ref, k_ref, vt_ref, o_ref, *, t):
    i = pl.program_id(2)
    q = q_ref[0].astype(F32) * (HEAD_DIM ** -0.5)
    krow = lax.broadcasted_iota(jnp.int32, (t, t), 0)
    qcol = lax.broadcasted_iota(jnp.int32, (t, t), 1)
    strict = krow < qcol
    after = (qcol > krow).astype(BF16)
    qms = [jnp.where(_half_mask((t, LANES), half), q, 0.0).astype(BF16) for half in range(2)]

    def local(j, diag):
        ks = pl.multiple_of(j * t, t)
        k = k_ref[0, pl.ds(ks, t), :]
        out = []
        for half in range(2):
            z = _nt_dot(k, qms[half])
            lz = _log_sigmoid(z)
            l1m = lz - z
            if diag:
                l1m = jnp.where(strict, l1m, 0.0)
            hi = l1m.astype(BF16)
            lo = (l1m - hi.astype(F32)).astype(BF16)
            suffix = (jnp.dot(after, hi, preferred_element_type=F32)
                      + jnp.dot(after, lo, preferred_element_type=F32))
            out.append((lz + suffix, jnp.sum(l1m, axis=0, keepdims=True)))
        return out

    def finish(j, loc, carry, keep):
        new = []
        for half in range(2):
            logw, colsum = loc[half]
            rsum, acc = carry[half]
            a = jnp.exp(logw + rsum)
            if keep is not None:
                a = jnp.where(keep, a, 0.0)
            pv = jnp.dot(vt_ref[0, 0, j, HEAD_DIM * half:HEAD_DIM * (half + 1), :], a.astype(BF16),
                         preferred_element_type=F32)
            new.append((rsum + colsum, acc + pv))
        return tuple(new)

    prev = jnp.maximum(i - 1, 0)
    loc_diag = local(i, True)
    loc_prev = local(prev, False)
    zero = (jnp.zeros((1, t), F32), jnp.zeros((HEAD_DIM, t), F32))
    carry = finish(i, loc_diag, (zero, zero), strict)
    carry = finish(prev, loc_prev, carry, i > 0)

    def more(c):
        n, ((r0, _), (r1, _)) = c
        return (n < i) & (jnp.maximum(jnp.max(r0), jnp.max(r1)) > SB_UNDERFLOW)

    def body(c):
        n, carry = c
        j = i - 1 - n
        return n + 1, finish(j, local(j, False), carry, None)

    _, ((_, acc0), (_, acc1)) = lax.while_loop(more, body, (jnp.int32(1), carry))
    o_ref[0] = _pair_out(acc0, acc1).astype(o_ref.dtype)


def _sb(z3, t):
    b, s, _ = z3.shape
    vt = _value_tiles(z3[:, :, U_SV * LANES:(U_SV + 4) * LANES], t)
    return pl.pallas_call(
        functools.partial(_sb_kernel, t=t),
        grid=(b, N_PAIRS, s // t),
        in_specs=[pl.BlockSpec((1, t, LANES), lambda bi, p, i: (bi, i, U_SQ + p)),
                  pl.BlockSpec((1, s, LANES), lambda bi, p, i: (bi, 0, U_SK + p)),
                  pl.BlockSpec((1, 1, s // t, LANES, t), lambda bi, p, i: (bi, p, 0, 0, 0))],
        out_specs=pl.BlockSpec((1, t, LANES), lambda bi, p, i: (bi, i, p)),
        out_shape=jax.ShapeDtypeStruct((b, s, W_ATT), BF16),
        compiler_params=_cparams(("parallel", "parallel", "arbitrary")),
        name="stickbreak_attention",
    )(z3, z3, vt)


def _t5_bucket(n):
    max_exact = N_BUCKETS // 2
    nf = jnp.maximum(n, 1).astype(F32)
    large = max_exact + (jnp.log(nf / max_exact) / math.log(MAX_DISTANCE / max_exact)
                         * (N_BUCKETS - max_exact)).astype(jnp.int32)
    large = jnp.minimum(large, N_BUCKETS - 1)
    return jnp.where(n < max_exact, n, large)


def _bias_tiles_kernel(relb_ref, o_ref, *, t):
    h = pl.program_id(0)
    krow = lax.broadcasted_iota(jnp.int32, (t, t), 0)
    qcol = lax.broadcasted_iota(jnp.int32, (t, t), 1)
    o_ref[0, 0] = jnp.full((t, t), relb_ref[N_BUCKETS - 1, h], F32)
    for slot, shift in ((1, t), (2, 0)):
        bucket = _t5_bucket(jnp.maximum(qcol - krow + shift, 0))
        val = jnp.full((t, t), relb_ref[0, h], F32)
        for k in range(1, N_BUCKETS):
            val = jnp.where(bucket == k, relb_ref[k, h], val)
        o_ref[0, slot] = val


def _bias_tiles(rel_bias, t):
    assert t >= MAX_DISTANCE
    return pl.pallas_call(
        functools.partial(_bias_tiles_kernel, t=t),
        grid=(N_HEADS,),
        in_specs=[pl.BlockSpec(memory_space=pltpu.SMEM)],
        out_specs=pl.BlockSpec((1, 3, t, t), lambda h: (h, 0, 0, 0)),
        out_shape=jax.ShapeDtypeStruct((N_HEADS, 3, t, t), F32),
        compiler_params=_cparams(("arbitrary",)),
        name="t5_bias_tiles",
    )(rel_bias)


def _dsa_kernel(dq_ref, qi_ref, zs_ref, kidx_ref, dk_ref, dvt_ref, bias_ref, o_ref, key_ref, madd_ref,
                *, t, n_sel):
    i = pl.program_id(1)
    nch = i + 1
    krow = lax.broadcasted_iota(jnp.int32, (t, t), 0)
    qcol = lax.broadcasted_iota(jnp.int32, (t, t), 1)
    idx_scale = (IDX_DIM ** -0.5) * (IDX_HEADS ** -0.5)

    zst = jnp.transpose(zs_ref[0])
    qi = qi_ref[0].astype(F32)
    qih, wih = [], []
    for h in range(IDX_HEADS):
        blk = qi[:, (h // 2) * LANES:(h // 2 + 1) * LANES]
        qih.append(jnp.where(_half_mask((t, LANES), h % 2), blk, 0.0).astype(BF16))
        wih.append(zst[SM_DIW + h:SM_DIW + h + 1, :] * idx_scale)

    def score_chunk(j, _):
        ks = pl.multiple_of(j * t, t)
        kc = kidx_ref[0, pl.ds(ks, t), :]
        sc = jnp.zeros((t, t), F32)
        for h in range(IDX_HEADS):
            sc = sc + jnp.maximum(_nt_dot(kc, qih[h]), 0.0) * wih[h]
        sc = jnp.where(sc == 0.0, 0.0, sc)
        sc = jnp.where(j * t + krow <= i * t + qcol, sc, -jnp.inf)
        bits = pltpu.bitcast(sc, jnp.int32)
        key_ref[j] = jnp.where(bits < 0, bits ^ jnp.int32(0x7FFFFFFF), bits)
        return 0

    lax.fori_loop(0, nch, score_chunk, 0)

    def count_keys(pred):
        def body(j, c):
            return c + jnp.sum(jnp.where(pred(key_ref[j]), 1.0, 0.0), axis=0, keepdims=True)
        return lax.fori_loop(0, nch, body, jnp.zeros((1, t), F32))

    def bit_step(n, thr):
        cand = thr + lax.shift_left(jnp.int32(1), 31 - n)
        cnt = count_keys(lambda k: k >= cand)
        return jnp.where(cnt >= float(n_sel), cand, thr)

    thr = lax.fori_loop(0, 32, bit_step, jnp.full((1, t), INT_MIN, jnp.int32))
    need = float(n_sel) - count_keys(lambda k: k > thr)

    upto = (qcol <= krow).astype(BF16)

    def mask_chunk(j, seen):
        key = key_ref[j]
        eq = key == thr
        rank = jnp.dot(upto, jnp.where(eq, 1.0, 0.0).astype(BF16), preferred_element_type=F32) + seen
        sel = (key > thr) | (eq & (rank <= need))
        sel = sel & (j * t + krow <= i * t + qcol)
        madd_ref[j] = jnp.where(sel, 0.0, NEG)
        return rank[t - 1:t, :]

    lax.fori_loop(0, nch, mask_chunk, jnp.zeros((1, t), F32))

    for pair in range(N_PAIRS):
        lo, hi_ = pair * LANES, (pair + 1) * LANES
        qp = dq_ref[0, :, lo:hi_].astype(F32) * (HEAD_DIM ** -0.5)
        qms = [jnp.where(_half_mask((t, LANES), half), qp, 0.0).astype(BF16) for half in range(2)]

        def scores(j, pair=pair, lo=lo, hi_=hi_, qms=qms):
            ks = pl.multiple_of(j * t, t)
            slot = jnp.clip(j - i + 2, 0, 2)
            k = dk_ref[0, pl.ds(ks, t), lo:hi_]
            madd = madd_ref[j]
            return tuple(_nt_dot(k, qms[half]) + bias_ref[2 * pair + half, slot] + madd for half in range(2))

        def weighted_values(j, half, p, pair=pair):
            return jnp.dot(dvt_ref[0, pair, j, HEAD_DIM * half:HEAD_DIM * (half + 1), :], p,
                           preferred_element_type=F32)

        o0, o1 = _online_softmax_pair(i, scores, weighted_values, None, t)
        o_ref[0, :, lo:hi_] = _pair_out(o0, o1).astype(o_ref.dtype)


def _dsa(z3, zs3, dk, dv, bias, t):
    b, s, _ = z3.shape
    n_sel = min(TOPK_MAX, s // 4)
    dvt = _value_tiles(dv, t)
    return pl.pallas_call(
        functools.partial(_dsa_kernel, t=t, n_sel=n_sel),
        grid=(b, s // t),
        in_specs=[pl.BlockSpec((1, t, W_ATT), lambda bi, i: (bi, i, U_DQ // 4)),
                  pl.BlockSpec((1, t, 2 * LANES), lambda bi, i: (bi, i, U_DIQ // 2)),
                  pl.BlockSpec((1, t, LANES), lambda bi, i: (bi, i, 0)),
                  pl.BlockSpec((1, s, LANES), lambda bi, i: (bi, 0, U_DIK)),
                  pl.BlockSpec((1, s, W_ATT), lambda bi, i: (bi, 0, 0)),
                  pl.BlockSpec((1, N_PAIRS, s // t, LANES, t), lambda bi, i: (bi, 0, 0, 0, 0)),
                  pl.BlockSpec((N_HEADS, 3, t, t), lambda bi, i: (0, 0, 0, 0))],
        out_specs=pl.BlockSpec((1, t, W_ATT), lambda bi, i: (bi, i, 0)),
        out_shape=jax.ShapeDtypeStruct((b, s, W_ATT), BF16),
        scratch_shapes=[pltpu.VMEM((s // t, t, t), jnp.int32), pltpu.VMEM((s // t, t, t), F32)],
        compiler_params=_cparams(("parallel", "arbitrary")),
        name="dsa_attention",
    )(z3, z3, zs3, z3, dk, dvt, bias)


def _merge_kernel(yf_ref, ys_ref, yd_ref, wf_ref, ws_ref, wd_ref, g0_ref, g1_ref, g2_ref, o_ref):
    def branch(y_ref, w_ref, g_ref):
        proj = jnp.dot(y_ref[...], w_ref[...].astype(BF16), preferred_element_type=F32)
        return jax.nn.sigmoid(g_ref[...].astype(F32)) * proj

    o_ref[...] = (branch(yf_ref, wf_ref, g0_ref) + branch(ys_ref, ws_ref, g1_ref)
                  + branch(yd_ref, wd_ref, g2_ref)).astype(o_ref.dtype)


def _merge(yf, ys, yd, wf, ws, wd, z, d, tm, tn):
    m = yf.shape[0]
    yblk = pl.BlockSpec((tm, W_ATT), lambda i, j: (i, 0))
    wblk = pl.BlockSpec((W_ATT, tn), lambda i, j: (0, j))
    gblk = lambda g: pl.BlockSpec((tm, tn), lambda i, j: (i, (U_GATES * LANES + g * d) // tn + j))
    return pl.pallas_call(
        _merge_kernel,
        grid=(m // tm, d // tn),
        in_specs=[yblk, yblk, yblk, wblk, wblk, wblk, gblk(0), gblk(1), gblk(2)],
        out_specs=pl.BlockSpec((tm, tn), lambda i, j: (i, j)),
        out_shape=jax.ShapeDtypeStruct((m, d), BF16),
        compiler_params=_cparams(("parallel", "arbitrary")),
        name="branch_merge",
    )(yf, ys, yd, wf, ws, wd, z, z, z)


def _matmul_res_kernel(a_ref, w_ref, r_ref, o_ref):
    o_ref[...] = r_ref[...] + jnp.dot(a_ref[...], w_ref[...].astype(BF16), preferred_element_type=F32)


def _matmul_res(a, w, res, tm, tn):
    m, k = a.shape
    n = w.shape[1]
    return pl.pallas_call(
        _matmul_res_kernel,
        grid=(m // tm, n // tn),
        in_specs=[pl.BlockSpec((tm, k), lambda i, j: (i, 0)), pl.BlockSpec((k, tn), lambda i, j: (0, j)),
                  pl.BlockSpec((tm, tn), lambda i, j: (i, j))],
        out_specs=pl.BlockSpec((tm, tn), lambda i, j: (i, j)),
        out_shape=jax.ShapeDtypeStruct((m, n), F32),
        compiler_params=_cparams(("parallel", "arbitrary")),
        name="out_proj_residual",
    )(a, w, res)


def _token_copy(src_hbm, row, dst_vmem, slot, sem):
    per = src_hbm.shape[1]
    return pltpu.make_async_copy(src_hbm.at[row], dst_vmem.at[pl.ds(pl.multiple_of(slot * per, per), per)], sem)


def _ffn_kernel(te_ref, nv_ref, src_ref, x_ref, wg_ref, wu_ref, wd_ref, *rest, grouped, tm, nf):
    ti = pl.program_id(0)
    f = pl.program_id(1)
    if grouped:
        o_ref, acc, gbuf, xs, sem = rest
    else:
        (o_ref,) = rest
        acc = o_ref

    @pl.when(f == 0)
    def _():
        acc[...] = jnp.zeros_like(acc)

    if grouped:
        per = pl.cdiv(tm, nf - 1)

        def issue_rows(tile, lo, n):
            def issue(r, _):
                _token_copy(x_ref, src_ref[tile * tm + lo + r], gbuf, lo + r, sem).start()
                return 0
            lax.fori_loop(0, n, issue, 0)

        @pl.when((ti == 0) & (f == 0))
        def _():
            issue_rows(0, 0, tm)

        @pl.when((f == 0) & (ti < nv_ref[0]))
        def _():
            def wait(r, _):
                _token_copy(x_ref, 0, gbuf, r, sem).wait()
                return 0
            lax.fori_loop(0, tm, wait, 0)
            for s in range(xs.shape[1] // LANES):
                xs[:, s * LANES:(s + 1) * LANES] = _load_token_major(gbuf, tm, s).astype(BF16)

        @pl.when((f > 0) & (ti + 1 < nv_ref[0]))
        def _():
            lo = (f - 1) * per
            issue_rows(ti + 1, lo, jnp.clip(tm - lo, 0, per))

    @pl.when(ti < nv_ref[0])
    def _():
        x = xs[...] if grouped else x_ref[...]
        g = jnp.dot(x, wg_ref[0].astype(BF16), preferred_element_type=F32)
        u = jnp.dot(x, wu_ref[0].astype(BF16), preferred_element_type=F32)
        a = (g * jax.nn.sigmoid(g) * u).astype(BF16)
        acc[...] += jnp.dot(a, wd_ref[0].astype(BF16), preferred_element_type=F32)

    if grouped:
        @pl.when(f == nf - 1)
        def _():
            _store_token_major(o_ref, acc[...])


def _ffn(x, wg, wu, wd, tile_expert, n_valid, tm, fc, src=None):
    grouped = src is not None
    d = wg.shape[1]
    p = src.shape[0] if grouped else x.shape[0]
    nf = wg.shape[2] // fc
    per = d // LANES

    def chunk(ti, f, nv):
        return jnp.where(ti < nv[0], f, nf - 1)

    wspecs = [pl.BlockSpec((1, d, fc), lambda ti, f, te, nv, sr: (te[ti], 0, chunk(ti, f, nv))),
              pl.BlockSpec((1, d, fc), lambda ti, f, te, nv, sr: (te[ti], 0, chunk(ti, f, nv))),
              pl.BlockSpec((1, fc, d), lambda ti, f, te, nv, sr: (te[ti], chunk(ti, f, nv), 0))]
    scratch = []
    if grouped:
        in_specs = [pl.BlockSpec(memory_space=pl.ANY)] + wspecs
        args = [x, wg, wu, wd]
        scratch += [pltpu.VMEM((tm, d), F32), pltpu.VMEM((tm * per, LANES), F32), pltpu.VMEM((tm, d), BF16),
                    pltpu.SemaphoreType.DMA(())]
        out_spec = pl.BlockSpec((tm, per, LANES), lambda ti, f, te, nv, sr: (ti, 0, 0))
        out_shape = jax.ShapeDtypeStruct((p, per, LANES), F32)
    else:
        in_specs = [pl.BlockSpec((tm, d), lambda ti, f, te, nv, sr: (ti, 0))] + wspecs
        args = [x, wg, wu, wd]
        src = jnp.zeros((1,), jnp.int32)
        out_spec = pl.BlockSpec((tm, d), lambda ti, f, te, nv, sr: (ti, 0))
        out_shape = jax.ShapeDtypeStruct((p, d), F32)
    grid_spec = pltpu.PrefetchScalarGridSpec(
        num_scalar_prefetch=3,
        grid=(p // tm, nf),
        in_specs=in_specs,
        out_specs=out_spec,
        scratch_shapes=scratch,
    )
    return pl.pallas_call(
        functools.partial(_ffn_kernel, grouped=grouped, tm=tm, nf=nf),
        grid_spec=grid_spec,
        out_shape=out_shape,
        compiler_params=_cparams(("arbitrary", "arbitrary")),
        name="grouped_swiglu" if grouped else "dense_swiglu",
    )(tile_expert, n_valid, src, *args)


def _combine_kernel(pos_ref, x_ref, info_ref, y_ref, g_ref, o_ref, buf, sem, *, tt, n, norm):
    i = pl.program_id(0)

    def issue_tile(tile, slot):
        def issue(r, _):
            for k in range(2):
                _token_copy(y_ref, pos_ref[k * n + tile * tt + r], buf.at[slot, k], r, sem.at[slot]).start()
            return 0
        lax.fori_loop(0, tt, issue, 0)

    @pl.when(i == 0)
    def _():
        issue_tile(0, 0)

    @pl.when(i + 1 < pl.num_programs(0))
    def _():
        issue_tile(i + 1, (i + 1) % 2)

    slot = i % 2

    def wait(r, _):
        for k in range(2):
            _token_copy(y_ref, 0, buf.at[slot, k], r, sem.at[slot]).wait()
        return 0

    lax.fori_loop(0, tt, wait, 0)
    info = info_ref[...]
    w0, w1 = info[:, 0:1], info[:, 1:2]
    d = o_ref.shape[1]
    sumsq = jnp.zeros((tt, 1), F32)
    for s in range(d // LANES):
        cols = slice(s * LANES, (s + 1) * LANES)
        o = x_ref[:, cols] + (w0 * _load_token_major(buf.at[slot, 0], tt, s)
                              + w1 * _load_token_major(buf.at[slot, 1], tt, s))
        o_ref[:, cols] = o
        sumsq = sumsq + jnp.sum(o * o, axis=-1, keepdims=True)
    if norm:
        o_ref[...] = o_ref[...] * lax.rsqrt(sumsq * (1.0 / d) + EPS) * g_ref[...]


def _combine(x2, info, y, pos, tt, gain=None):
    n, d = x2.shape
    norm = gain is not None
    grid_spec = pltpu.PrefetchScalarGridSpec(
        num_scalar_prefetch=1,
        grid=(n // tt,),
        in_specs=[pl.BlockSpec((tt, d), lambda i, ps: (i, 0)), pl.BlockSpec((tt, LANES), lambda i, ps: (i, 0)),
                  pl.BlockSpec(memory_space=pl.ANY), pl.BlockSpec((1, d), lambda i, ps: (0, 0))],
        out_specs=pl.BlockSpec((tt, d), lambda i, ps: (i, 0)),
        scratch_shapes=[pltpu.VMEM((2, 2, tt * (d // LANES), LANES), F32), pltpu.SemaphoreType.DMA((2,))],
    )
    return pl.pallas_call(
        functools.partial(_combine_kernel, tt=tt, n=n, norm=norm),
        grid_spec=grid_spec,
        out_shape=jax.ShapeDtypeStruct((n, d), F32),
        compiler_params=_cparams(("arbitrary",)),
        name="moe_combine",
    )(pos, x2, info, y, (gain if norm else jnp.ones((d,), F32)).reshape(1, d))


def _moe(x2, h, info, wg, wu, wd, tm, fc, out_gain=None):
    n = h.shape[0]
    e = wg.shape[0]
    eid = jnp.concatenate([info[:, 2], info[:, 3]]).astype(jnp.int32)
    tok = jnp.tile(jnp.arange(n, dtype=jnp.int32), 2)
    onehot = (eid[:, None] == jnp.arange(e, dtype=jnp.int32)[None, :]).astype(jnp.int32)
    rank = jnp.sum((jnp.cumsum(onehot, axis=0) - 1) * onehot, axis=1)
    counts = jnp.sum(onehot, axis=0)
    padded = ((counts + tm - 1) // tm) * tm
    ends = jnp.cumsum(padded)
    starts = ends - padded
    pos = starts[eid] + rank
    n_tiles = (2 * n) // tm + e
    p = n_tiles * tm
    src = jnp.zeros((p,), jnp.int32).at[pos].set(tok)
    n_valid = (ends[e - 1] // tm).astype(jnp.int32)
    tile_start = jnp.arange(n_tiles, dtype=jnp.int32) * tm
    tile_start = jnp.minimum(tile_start, (n_valid - 1) * tm)
    tile_expert = jnp.sum((tile_start[:, None] >= ends[None, :]).astype(jnp.int32), axis=1)
    y = _ffn(h, wg, wu, wd, tile_expert, n_valid.reshape(1), tm, fc, src=src)
    return _combine(x2, info, y, pos, _tile(n, 256), out_gain)


def _tile(total, want):
    t = min(total, want)
    assert total % t == 0
    return t


def kernel(x, w_in, b_forget, q_norm, k_norm, kv_norm, w_ukv, w_br_fox, w_br_sb, w_br_dsa, w_out, rel_bias, norm_mix, norm_ffn, w_ffn_gate, w_ffn_up, w_ffn_down, w_router, w_moe_gate, w_moe_up, w_moe_down, norm_final):
    b, s, d = x.shape
    m = b * s
    depth = w_in.shape[0]
    ta = _tile(s, 256)
    tm_norm = _tile(m, 512)
    tm = _tile(m, 1024)
    tn = 512
    fc = 256
    tm_moe = _tile(2 * m, 512)
    fc_moe = 512

    bias = _bias_tiles(rel_bias, ta)
    x2 = x.reshape(m, d)
    pending = None
    for l in range(depth):
        if pending is None:
            h = _rmsnorm(x2, norm_mix[l], BF16, tm_norm)
        else:
            x2, h = _add_rmsnorm(x2, pending, norm_mix[l], tm_norm)
            pending = None
        z, zs = _inproj(h, _relayout_w_in(w_in, l), tm, 2 * tn)
        z3 = z.reshape(b, s, Z_COLS)
        zs3 = zs.reshape(b, s, LANES)
        qn, kn, aq, ak, dk, dv = _prep(z3, zs3, q_norm[l], k_norm[l], kv_norm[l], b_forget[l], w_ukv[l], ta)
        y_fox = _fox(qn, kn, z3, aq, ak, ta).reshape(m, W_ATT)
        y_sb = _sb(z3, ta).reshape(m, W_ATT)
        y_dsa = _dsa(z3, zs3, dk, dv, bias, ta).reshape(m, W_ATT)
        mix = _merge(y_fox, y_sb, y_dsa, w_br_fox[l], w_br_sb[l], w_br_dsa[l], z, d, tm, tn)
        x2 = _matmul_res(mix, w_out[l], x2, tm, tn)
        j = l // 2
        if l % 2 == 0:
            h = _rmsnorm(x2, norm_ffn[l], BF16, tm_norm)
            n_tiles = m // tm
            pending = _ffn(h, w_ffn_gate[j][None], w_ffn_up[j][None], w_ffn_down[j][None],
                           jnp.zeros((n_tiles,), jnp.int32), jnp.full((1,), n_tiles, jnp.int32), tm, fc)
        else:
            h, info = _rmsnorm_router(x2, norm_ffn[l], w_router[j], tm_norm)
            last = l == depth - 1
            x2 = _moe(x2, h, info, w_moe_gate[j], w_moe_up[j], w_moe_down[j], tm_moe, fc_moe,
                      norm_final if last else None)
    if pending is not None:
        x2 = _rmsnorm(x2 + pending, norm_final, F32, tm_norm)
    return x2.reshape(b, s, d)
```

```python
import functools
import math

import jax
import jax.numpy as jnp
import numpy as np
from jax import lax
from jax.experimental import pallas as pl
from jax.experimental.pallas import tpu as pltpu

F32 = jnp.float32
BF16 = jnp.bfloat16

HEAD_DIM = 64
N_HEADS = 8
N_PAIRS = N_HEADS // 2
W_ATT = N_HEADS * HEAD_DIM
KV_RANK = 256
IDX_HEADS = 4
IDX_DIM = 64
TOPK_MAX = 256
N_BUCKETS = 32
MAX_DISTANCE = 128
N_EXPERTS = 8
N_BRANCH = 3
EPS = 1e-6
LANES = 128
NEG = -1e30
SB_UNDERFLOW = -104.0
ISSUE_UNROLL = 8

U_FQ, U_FK, U_FV, U_FO, U_SQ, U_SK, U_SV, U_DQ = 0, 4, 8, 12, 16, 20, 24, 28
U_DLAT, U_DIQ, U_DIK, U_SMALL, U_GATES = 32, 34, 36, 37, 40
N_UNITS = 88
Z_COLS = N_UNITS * LANES
SM_FF, SM_DIW = 0, 8

VMEM_LIMIT = 56 * 1024 * 1024


def _cparams(sem):
    return pltpu.CompilerParams(dimension_semantics=sem, vmem_limit_bytes=VMEM_LIMIT)


def _log_sigmoid(x):
    return jnp.minimum(x, 0.0) - jnp.log1p(jnp.exp(-jnp.abs(x)))


def _rmsnorm_kernel(x_ref, g_ref, o_ref):
    x = x_ref[...]
    ms = jnp.mean(x * x, axis=-1, keepdims=True)
    o_ref[...] = (x * lax.rsqrt(ms + EPS) * g_ref[...]).astype(o_ref.dtype)


def _rmsnorm(x, g, out_dtype, tm):
    m, d = x.shape
    return pl.pallas_call(
        _rmsnorm_kernel,
        grid=(m // tm,),
        in_specs=[pl.BlockSpec((tm, d), lambda i: (i, 0)), pl.BlockSpec((1, d), lambda i: (0, 0))],
        out_specs=pl.BlockSpec((tm, d), lambda i: (i, 0)),
        out_shape=jax.ShapeDtypeStruct((m, d), out_dtype),
        compiler_params=_cparams(("parallel",)),
        name="rmsnorm",
    )(x, g.reshape(1, d))


def _add_rmsnorm_kernel(x_ref, y_ref, g_ref, xo_ref, h_ref):
    x = x_ref[...] + y_ref[...]
    xo_ref[...] = x
    ms = jnp.mean(x * x, axis=-1, keepdims=True)
    h_ref[...] = (x * lax.rsqrt(ms + EPS) * g_ref[...]).astype(h_ref.dtype)


def _add_rmsnorm(x, y, g, tm):
    m, d = x.shape
    blk = pl.BlockSpec((tm, d), lambda i: (i, 0))
    return pl.pallas_call(
        _add_rmsnorm_kernel,
        grid=(m // tm,),
        in_specs=[blk, blk, pl.BlockSpec((1, d), lambda i: (0, 0))],
        out_specs=[blk, blk],
        out_shape=[jax.ShapeDtypeStruct((m, d), F32), jax.ShapeDtypeStruct((m, d), BF16)],
        compiler_params=_cparams(("parallel",)),
        name="add_rmsnorm",
    )(x, y, g.reshape(1, d))


def _rmsnorm_router_kernel(x_ref, g_ref, wr_ref, o_ref, info_ref):
    x = x_ref[...]
    ms = jnp.mean(x * x, axis=-1, keepdims=True)
    h = x * lax.rsqrt(ms + EPS) * g_ref[...]
    o_ref[...] = h.astype(o_ref.dtype)
    logits = jnp.dot(h, wr_ref[...], precision=lax.Precision.HIGHEST, preferred_element_type=F32)
    lane = lax.broadcasted_iota(jnp.int32, logits.shape, 1).astype(F32)
    lg = jnp.where(lane < N_EXPERTS, logits, -jnp.inf)
    v1 = jnp.max(lg, axis=-1, keepdims=True)
    i1 = jnp.min(jnp.where(lg == v1, lane, float(LANES)), axis=-1, keepdims=True)
    lg2 = jnp.where(lane == i1, -jnp.inf, lg)
    v2 = jnp.max(lg2, axis=-1, keepdims=True)
    i2 = jnp.min(jnp.where(lg2 == v2, lane, float(LANES)), axis=-1, keepdims=True)
    e2 = jnp.exp(v2 - v1)
    w1 = 1.0 / (1.0 + e2)
    w2 = e2 / (1.0 + e2)
    info = jnp.where(lane == 0.0, w1, jnp.where(lane == 1.0, w2, jnp.where(
        lane == 2.0, i1, jnp.where(lane == 3.0, i2, 0.0))))
    info_ref[...] = info


def _rmsnorm_router(x, g, w_router, tm):
    m, d = x.shape
    wr = jnp.pad(w_router, ((0, 0), (0, LANES - w_router.shape[1])))
    return pl.pallas_call(
        _rmsnorm_router_kernel,
        grid=(m // tm,),
        in_specs=[pl.BlockSpec((tm, d), lambda i: (i, 0)), pl.BlockSpec((1, d), lambda i: (0, 0)),
                  pl.BlockSpec((d, LANES), lambda i: (0, 0))],
        out_specs=[pl.BlockSpec((tm, d), lambda i: (i, 0)), pl.BlockSpec((tm, LANES), lambda i: (i, 0))],
        out_shape=[jax.ShapeDtypeStruct((m, d), F32), jax.ShapeDtypeStruct((m, LANES), F32)],
        compiler_params=_cparams(("parallel",)),
        name="rmsnorm_router",
    )(x, g.reshape(1, d), wr)


def _inproj_kernel(a_ref, w_ref, z_ref, zs_ref, *, small_tile):
    acc = jnp.dot(a_ref[...], w_ref[...], preferred_element_type=F32)
    z_ref[...] = acc.astype(z_ref.dtype)

    @pl.when(pl.program_id(1) == small_tile)
    def _():
        off = (U_SMALL * LANES) % acc.shape[1]
        zs_ref[...] = acc[:, off:off + LANES]


def _inproj(h, w, tm, tn):
    m, d = h.shape
    return pl.pallas_call(
        functools.partial(_inproj_kernel, small_tile=(U_SMALL * LANES) // tn),
        grid=(m // tm, Z_COLS // tn),
        in_specs=[pl.BlockSpec((tm, d), lambda i, j: (i, 0)), pl.BlockSpec((d, tn), lambda i, j: (0, j))],
        out_specs=[pl.BlockSpec((tm, tn), lambda i, j: (i, j)), pl.BlockSpec((tm, LANES), lambda i, j: (i, 0))],
        out_shape=[jax.ShapeDtypeStruct((m, Z_COLS), BF16), jax.ShapeDtypeStruct((m, LANES), F32)],
        compiler_params=_cparams(("parallel", "arbitrary")),
        name="inproj",
    )(h, w)


W_IN_ORDER = (("fq", W_ATT), ("fk", W_ATT), ("fv", W_ATT), ("ff", N_HEADS), ("fo", W_ATT),
              ("sq", W_ATT), ("sk", W_ATT), ("sv", W_ATT), ("dq", W_ATT), ("dlat", KV_RANK),
              ("diq", IDX_HEADS * IDX_DIM), ("dik", IDX_DIM), ("diw", IDX_HEADS))
RELAYOUT_COLS = 4 * LANES


def _relayout_kernel(tbl_ref, w_ref, ff_ref, o_ref, *, layer, special):
    j = pl.program_id(0)

    @pl.when(j != special)
    def _():
        o_ref[...] = jnp.transpose(w_ref[:, layer, :]).astype(o_ref.dtype)

    @pl.when(j == special)
    def _():
        xt = jnp.transpose(w_ref[0:LANES, layer, :])
        fft = jnp.transpose(ff_ref[:, layer, :])
        lane = lax.broadcasted_iota(jnp.int32, xt.shape, 1)
        dik2 = jnp.where(lane < IDX_DIM, xt, pltpu.roll(xt, IDX_DIM, axis=1))
        diw = pltpu.roll(xt, LANES - IDX_DIM + SM_DIW, axis=1)
        small = jnp.where(lane < N_HEADS, fft,
                          jnp.where((lane >= SM_DIW) & (lane < SM_DIW + IDX_HEADS), diw, 0.0))
        o_ref[:, 0:LANES] = dik2.astype(o_ref.dtype)
        o_ref[:, LANES:2 * LANES] = small.astype(o_ref.dtype)
        o_ref[:, 2 * LANES:] = jnp.zeros((o_ref.shape[0], o_ref.shape[1] - 2 * LANES), o_ref.dtype)


def _relayout_w_in(w_in, layer):
    _, d, n_in = w_in.shape
    src, o = {}, 0
    for name, width in W_IN_ORDER + (("gates", N_BRANCH * d),):
        src[name] = o
        o += width
    assert o == n_in and U_GATES * LANES + N_BRANCH * d == Z_COLS and SM_FF == 0
    assert src["diq"] == src["dlat"] + KV_RANK and src["diw"] == src["dik"] + IDX_DIM
    units = {U_FQ: "fq", U_FK: "fk", U_FV: "fv", U_FO: "fo", U_SQ: "sq", U_SK: "sk", U_SV: "sv",
             U_DQ: "dq", U_DLAT: "dlat", U_DIK: "dik"}
    per = RELAYOUT_COLS // LANES
    tbl = [src[units[u]] for u in range(0, U_GATES, per)]
    tbl += [src["gates"] + k * RELAYOUT_COLS for k in range(N_BRANCH * d // RELAYOUT_COLS)]
    elem = lambda rows: (pl.Element(rows), pl.Element(w_in.shape[0]), pl.Element(d))
    grid_spec = pltpu.PrefetchScalarGridSpec(
        num_scalar_prefetch=1,
        grid=(Z_COLS // RELAYOUT_COLS,),
        in_specs=[pl.BlockSpec(elem(RELAYOUT_COLS), lambda j, tbl: (tbl[j], 0, 0)),
                  pl.BlockSpec(elem(LANES), lambda j, tbl: (src["ff"], 0, 0))],
        out_specs=pl.BlockSpec((d, RELAYOUT_COLS), lambda j, tbl: (0, j)),
    )
    wt = jnp.transpose(w_in, (2, 0, 1))
    return pl.pallas_call(
        functools.partial(_relayout_kernel, layer=layer, special=U_DIK // per),
        grid_spec=grid_spec,
        out_shape=jax.ShapeDtypeStruct((d, Z_COLS), BF16),
        compiler_params=_cparams(("arbitrary",)),
        name="w_in_relayout",
    )(jnp.asarray(tbl, jnp.int32), wt, wt)


def _prep_kernel(zq_ref, zk_ref, zl_ref, zs_ref, gq_ref, gk_ref, gkv_ref, bf_ref, wukv_ref,
                 grp_ref, eq_ref, ek_ref, oneq_ref, onek_ref,
                 qn_ref, kn_ref, aq_ref, ak_ref, dk_ref, dv_ref, carry_ref, *, ts):
    @pl.when(pl.program_id(1) == 0)
    def _():
        carry_ref[...] = jnp.zeros_like(carry_ref)

    def head_norm(z_ref, g_ref):
        x = z_ref[0].astype(F32)
        sq = x * x
        hi = sq.astype(BF16)
        lo = (sq - hi.astype(F32)).astype(BF16)
        ms = (jnp.dot(hi, grp_ref[...], preferred_element_type=F32)
              + jnp.dot(lo, grp_ref[...], preferred_element_type=F32)) * (1.0 / HEAD_DIM)
        return x * lax.rsqrt(ms + EPS) * g_ref[...]

    qn_ref[0] = head_norm(zq_ref, gq_ref).astype(BF16)
    kn_ref[0] = head_norm(zk_ref, gk_ref).astype(BF16)

    lane = lax.broadcasted_iota(jnp.int32, (ts, LANES), 1)
    lf = jnp.where(lane < N_HEADS, _log_sigmoid(zs_ref[0] + bf_ref[...]), 0.0)
    r = lax.broadcasted_iota(jnp.int32, (ts, ts), 0)
    c_ = lax.broadcasted_iota(jnp.int32, (ts, ts), 1)
    tri = (c_ <= r).astype(F32)
    c = jnp.dot(tri, lf, precision=lax.Precision.HIGHEST, preferred_element_type=F32) + carry_ref[...]
    carry_ref[...] = c[ts - 1:ts, :]
    c0 = c.astype(BF16)
    r1 = c - c0.astype(F32)
    c1 = r1.astype(BF16)
    c2 = (r1 - c1.astype(F32)).astype(BF16)
    pieces = (c0, c1, c2)
    aq = oneq_ref[...]
    ak = onek_ref[...]
    for k in range(3):
        aq = aq + jnp.dot(pieces[k], eq_ref[k], preferred_element_type=F32)
        ak = ak - jnp.dot(pieces[k], ek_ref[k], preferred_element_type=F32)
    aq_ref[0] = aq.astype(BF16)
    ak_ref[0] = ak.astype(BF16)

    lat = zl_ref[0, :, :KV_RANK].astype(F32)
    msl = jnp.mean(lat * lat, axis=-1, keepdims=True)
    latn = (lat * lax.rsqrt(msl + EPS) * gkv_ref[...]).astype(BF16)
    kv = jnp.dot(latn, wukv_ref[...], preferred_element_type=F32)
    dk_ref[0] = kv[:, :W_ATT].astype(BF16)
    dv_ref[0] = kv[:, W_ATT:].astype(BF16)


def _aug_constants():
    eq = np.zeros((3, LANES, LANES), np.float32)
    ek = np.zeros((3, LANES, LANES), np.float32)
    oneq = np.zeros((1, LANES), np.float32)
    onek = np.zeros((1, LANES), np.float32)
    for h in range(N_HEADS):
        for k in range(3):
            eq[k, h, 8 * h + k] = 1.0
            ek[k, h, 8 * h + 3 + k] = 1.0
            oneq[0, 8 * h + 3 + k] = 1.0
            onek[0, 8 * h + k] = 1.0
    grp = np.kron(np.eye(N_HEADS, dtype=np.float32), np.ones((HEAD_DIM, HEAD_DIM), np.float32))
    return (jnp.asarray(grp, BF16), jnp.asarray(eq, BF16), jnp.asarray(ek, BF16),
            jnp.asarray(oneq), jnp.asarray(onek))


def _prep(z3, zs3, q_norm, k_norm, kv_norm, b_forget, w_ukv, ts):
    b, s, _ = z3.shape
    grp, eq, ek, oneq, onek = _aug_constants()
    gq = (jnp.tile(q_norm, N_HEADS) * HEAD_DIM ** -0.5).reshape(1, W_ATT)
    gk = jnp.tile(k_norm, N_HEADS).reshape(1, W_ATT)
    bf = jnp.pad(b_forget, (SM_FF, LANES - N_HEADS - SM_FF)).reshape(1, LANES)
    const = lambda shape: pl.BlockSpec(shape, lambda bi, si: (0,) * len(shape))
    zblk = lambda unit: pl.BlockSpec((1, ts, W_ATT), lambda bi, si: (bi, si, unit // 4))
    seq_out = lambda w: pl.BlockSpec((1, ts, w), lambda bi, si: (bi, si, 0))
    return pl.pallas_call(
        functools.partial(_prep_kernel, ts=ts),
        grid=(b, s // ts),
        in_specs=[zblk(U_FQ), zblk(U_FK), zblk(U_DLAT),
                  pl.BlockSpec((1, ts, LANES), lambda bi, si: (bi, si, 0)),
                  const((1, W_ATT)), const((1, W_ATT)), const((1, KV_RANK)), const((1, LANES)),
                  const((KV_RANK, 2 * W_ATT)), const((W_ATT, W_ATT)),
                  const((3, LANES, LANES)), const((3, LANES, LANES)), const((1, LANES)), const((1, LANES))],
        out_specs=[seq_out(W_ATT), seq_out(W_ATT), seq_out(LANES), seq_out(LANES), seq_out(W_ATT), seq_out(W_ATT)],
        out_shape=[jax.ShapeDtypeStruct((b, s, W_ATT), BF16), jax.ShapeDtypeStruct((b, s, W_ATT), BF16),
                   jax.ShapeDtypeStruct((b, s, LANES), BF16), jax.ShapeDtypeStruct((b, s, LANES), BF16),
                   jax.ShapeDtypeStruct((b, s, W_ATT), BF16), jax.ShapeDtypeStruct((b, s, W_ATT), BF16)],
        scratch_shapes=[pltpu.VMEM((1, LANES), F32)],
        compiler_params=_cparams(("parallel", "arbitrary")),
        name="mixer_prep",
    )(z3, z3, z3, zs3, gq, gk, kv_norm.reshape(1, KV_RANK), bf, w_ukv.astype(BF16), grp, eq, ek, oneq, onek)


def _nt_dot(a, b):
    return lax.dot_general(a, b, (((1,), (1,)), ((), ())), preferred_element_type=F32)


def _half_mask(shape, half):
    lane = lax.broadcasted_iota(jnp.int32, shape, 1)
    return (lane >= HEAD_DIM * half) & (lane < HEAD_DIM * (half + 1))


def _value_tiles(v, t):
    b, s, _ = v.shape
    vt = v.reshape(b, s // t, t, N_PAIRS, LANES)
    return jnp.transpose(vt, (0, 3, 1, 4, 2))


def _pair_out(acc0, acc1):
    return jnp.transpose(jnp.concatenate([acc0, acc1], axis=0))


def _online_softmax_pair(last, scores, weighted_values, mask_last, t):
    def soft(s, m, l):
        m_new = jnp.maximum(m, jnp.max(s, axis=0, keepdims=True))
        alpha = jnp.exp(m - m_new)
        p = jnp.exp(s - m_new)
        return m_new, alpha * l + jnp.sum(p, axis=0, keepdims=True), alpha, p.astype(BF16)

    def drain(j, p, alpha, acc):
        return tuple(alpha[h] * acc[h] + weighted_values(j, h, p[h]) for h in range(2))

    def body(n, c):
        s, p_prev, a_prev, m, l, acc = c
        s_next = scores(n + 1)
        acc = drain(jnp.maximum(n - 1, 0), p_prev, a_prev, acc)
        r = [soft(s[h], m[h], l[h]) for h in range(2)]
        return (s_next, (r[0][3], r[1][3]), (r[0][2], r[1][2]), (r[0][0], r[1][0]), (r[0][1], r[1][1]), acc)

    two = lambda x: (x, x)
    init = (scores(0), two(jnp.zeros((t, t), BF16)), two(jnp.ones((1, t), F32)),
            two(jnp.full((1, t), NEG, F32)), two(jnp.zeros((1, t), F32)), two(jnp.zeros((HEAD_DIM, t), F32)))
    s, p_prev, a_prev, m, l, acc = lax.fori_loop(0, last, body, init)
    acc = drain(jnp.maximum(last - 1, 0), p_prev, a_prev, acc)
    if mask_last is not None:
        s = tuple(mask_last(x) for x in s)
    r = [soft(s[h], m[h], l[h]) for h in range(2)]
    acc = drain(last, (r[0][3], r[1][3]), (r[0][2], r[1][2]), acc)
    return acc[0] / r[0][1], acc[1] / r[1][1]


def _fox_kernel(q_ref, k_ref, vt_ref, aq_ref, ak_ref, fo_ref, o_ref, *, t):
    pair = pl.program_id(1)
    i = pl.program_id(2)
    q = q_ref[0].astype(F32)
    aq = aq_ref[0].astype(F32)
    lane = lax.broadcasted_iota(jnp.int32, (t, LANES), 1)
    krow = lax.broadcasted_iota(jnp.int32, (t, t), 0)
    qcol = lax.broadcasted_iota(jnp.int32, (t, t), 1)
    causal = krow <= qcol
    qcs = []
    for half in range(2):
        head = 2 * pair + half
        qm = jnp.where(_half_mask((t, LANES), half), q, 0.0).astype(BF16)
        am = jnp.where((lane >= 8 * head) & (lane < 8 * head + 6), aq, 0.0).astype(BF16)
        qcs.append(jnp.concatenate([qm, am], axis=1))

    def scores(j):
        ks = pl.multiple_of(j * t, t)
        kc = jnp.concatenate([k_ref[0, pl.ds(ks, t), :], ak_ref[0, pl.ds(ks, t), :]], axis=1)
        return tuple(_nt_dot(kc, qcs[half]) for half in range(2))

    def weighted_values(j, half, p):
        return jnp.dot(vt_ref[0, 0, j, HEAD_DIM * half:HEAD_DIM * (half + 1), :], p, preferred_element_type=F32)

    o0, o1 = _online_softmax_pair(i, scores, weighted_values, lambda s: jnp.where(causal, s, NEG), t)
    o = _pair_out(o0, o1)
    o_ref[0] = (o * jax.nn.sigmoid(fo_ref[0].astype(F32))).astype(o_ref.dtype)


def _fox(qn, kn, z3, aq, ak, t):
    b, s, _ = qn.shape
    vt = _value_tiles(z3[:, :, U_FV * LANES:(U_FV + 4) * LANES], t)
    qblk = lambda unit: pl.BlockSpec((1, t, LANES), lambda bi, p, i: (bi, i, unit + p))
    return pl.pallas_call(
        functools.partial(_fox_kernel, t=t),
        grid=(b, N_PAIRS, s // t),
        in_specs=[qblk(0),
                  pl.BlockSpec((1, s, LANES), lambda bi, p, i: (bi, 0, p)),
                  pl.BlockSpec((1, 1, s // t, LANES, t), lambda bi, p, i: (bi, p, 0, 0, 0)),
                  pl.BlockSpec((1, t, LANES), lambda bi, p, i: (bi, i, 0)),
                  pl.BlockSpec((1, s, LANES), lambda bi, p, i: (bi, 0, 0)),
                  qblk(U_FO)],
        out_specs=qblk(0),
        out_shape=jax.ShapeDtypeStruct((b, s, W_ATT), BF16),
        compiler_params=_cparams(("parallel", "parallel", "arbitrary")),
        name="fox_attention",
    )(qn, kn, vt, aq, ak, z3)


def _sb_kernel(q_ref, k_ref, vt_ref, o_ref, *, t):
    i = pl.program_id(2)
    q = q_ref[0].astype(F32) * (HEAD_DIM ** -0.5)
    krow = lax.broadcasted_iota(jnp.int32, (t, t), 0)
    qcol = lax.broadcasted_iota(jnp.int32, (t, t), 1)
    strict = krow < qcol
    after = (qcol > krow).astype(BF16)
    qms = [jnp.where(_half_mask((t, LANES), half), q, 0.0).astype(BF16) for half in range(2)]

    def local(j, diag):
        ks = pl.multiple_of(j * t, t)
        k = k_ref[0, pl.ds(ks, t), :]
        out = []
        for half in range(2):
            z = _nt_dot(k, qms[half])
            lz = _log_sigmoid(z)
            l1m = lz - z
            if diag:
                l1m = jnp.where(strict, l1m, 0.0)
            hi = l1m.astype(BF16)
            lo = (l1m - hi.astype(F32)).astype(BF16)
            suffix = (jnp.dot(after, hi, preferred_element_type=F32)
                      + jnp.dot(after, lo, preferred_element_type=F32))
            out.append((lz + suffix, jnp.sum(l1m, axis=0, keepdims=True)))
        return out

    def finish(j, loc, carry, keep):
        new = []
        for half in range(2):
            logw, colsum = loc[half]
            rsum, acc = carry[half]
            a = jnp.exp(logw + rsum)
            if keep is not None:
                a = jnp.where(keep, a, 0.0)
            pv = jnp.dot(vt_ref[0, 0, j, HEAD_DIM * half:HEAD_DIM * (half + 1), :], a.astype(BF16),
                         preferred_element_type=F32)
            new.append((rsum + colsum, acc + pv))
        return tuple(new)

    prev = jnp.maximum(i - 1, 0)
    loc_diag = local(i, True)
    loc_prev = local(prev, False)
    zero = (jnp.zeros((1, t), F32), jnp.zeros((HEAD_DIM, t), F32))
    carry = finish(i, loc_diag, (zero, zero), strict)
    carry = finish(prev, loc_prev, carry, i > 0)

    def more(c):
        n, ((r0, _), (r1, _)) = c
        return (n < i) & (jnp.maximum(jnp.max(r0), jnp.max(r1)) > SB_UNDERFLOW)

    def body(c):
        n, carry = c
        j = i - 1 - n
        return n + 1, finish(j, local(j, False), carry, None)

    _, ((_, acc0), (_, acc1)) = lax.while_loop(more, body, (jnp.int32(1), carry))
    o_ref[0] = _pair_out(acc0, acc1).astype(o_ref.dtype)


def _sb(z3, t):
    b, s, _ = z3.shape
    vt = _value_tiles(z3[:, :, U_SV * LANES:(U_SV + 4) * LANES], t)
    return pl.pallas_call(
        functools.partial(_sb_kernel, t=t),
        grid=(b, N_PAIRS, s // t),
        in_specs=[pl.BlockSpec((1, t, LANES), lambda bi, p, i: (bi, i, U_SQ + p)),
                  pl.BlockSpec((1, s, LANES), lambda bi, p, i: (bi, 0, U_SK + p)),
                  pl.BlockSpec((1, 1, s // t, LANES, t), lambda bi, p, i: (bi, p, 0, 0, 0))],
        out_specs=pl.BlockSpec((1, t, LANES), lambda bi, p, i: (bi, i, p)),
        out_shape=jax.ShapeDtypeStruct((b, s, W_ATT), BF16),
        compiler_params=_cparams(("parallel", "parallel", "arbitrary")),
        name="stickbreak_attention",
    )(z3, z3, vt)


def _t5_bucket(n):
    max_exact = N_BUCKETS // 2
    nf = jnp.maximum(n, 1).astype(F32)
    large = max_exact + (jnp.log(nf / max_exact) / math.log(MAX_DISTANCE / max_exact)
                         * (N_BUCKETS - max_exact)).astype(jnp.int32)
    large = jnp.minimum(large, N_BUCKETS - 1)
    return jnp.where(n < max_exact, n, large)


def _bias_tiles_kernel(relb_ref, o_ref, *, t):
    h = pl.program_id(0)
    krow = lax.broadcasted_iota(jnp.int32, (t, t), 0)
    qcol = lax.broadcasted_iota(jnp.int32, (t, t), 1)
    o_ref[0, 0] = jnp.full((t, t), relb_ref[N_BUCKETS - 1, h], F32)
    for slot, shift in ((1, t), (2, 0)):
        bucket = _t5_bucket(jnp.maximum(qcol - krow + shift, 0))
        val = jnp.full((t, t), relb_ref[0, h], F32)
        for k in range(1, N_BUCKETS):
            val = jnp.where(bucket == k, relb_ref[k, h], val)
        o_ref[0, slot] = val


def _bias_tiles(rel_bias, t):
    assert t >= MAX_DISTANCE
    return pl.pallas_call(
        functools.partial(_bias_tiles_kernel, t=t),
        grid=(N_HEADS,),
        in_specs=[pl.BlockSpec(memory_space=pltpu.SMEM)],
        out_specs=pl.BlockSpec((1, 3, t, t), lambda h: (h, 0, 0, 0)),
        out_shape=jax.ShapeDtypeStruct((N_HEADS, 3, t, t), F32),
        compiler_params=_cparams(("arbitrary",)),
        name="t5_bias_tiles",
    )(rel_bias)


def _dsa_kernel(dq_ref, qi_ref, zs_ref, kidx_ref, dk_ref, dvt_ref, bias_ref, o_ref, key_ref, hi_ref, lo_ref,
                madd_ref, *, t, n_sel):
    i = pl.program_id(1)
    nch = i + 1
    krow = lax.broadcasted_iota(jnp.int32, (t, t), 0)
    qcol = lax.broadcasted_iota(jnp.int32, (t, t), 1)
    idx_scale = (IDX_DIM ** -0.5) * (IDX_HEADS ** -0.5)

    zst = jnp.transpose(zs_ref[0])
    qi = qi_ref[0].astype(F32)
    qih, wih = [], []
    for h in range(IDX_HEADS):
        blk = qi[:, (h // 2) * LANES:(h // 2 + 1) * LANES]
        qih.append(jnp.where(_half_mask((t, LANES), h % 2), blk, 0.0).astype(BF16))
        wih.append(zst[SM_DIW + h:SM_DIW + h + 1, :] * idx_scale)

    def score_chunk(j, _):
        ks = pl.multiple_of(j * t, t)
        kc = kidx_ref[0, pl.ds(ks, t), :]
        sc = jnp.zeros((t, t), F32)
        for h in range(IDX_HEADS):
            sc = sc + jnp.maximum(_nt_dot(kc, qih[h]), 0.0) * wih[h]
        sc = jnp.where(sc == 0.0, 0.0, sc)
        sc = jnp.where(j * t + krow <= i * t + qcol, sc, -jnp.inf)
        bits = pltpu.bitcast(sc, jnp.int32)
        key = jnp.where(bits < 0, bits ^ jnp.int32(0x7FFFFFFF), bits)
        key_ref[j] = key
        hi_ref[j] = (key >> 16).astype(jnp.int16)
        return 0

    lax.fori_loop(0, nch, score_chunk, 0)

    i16 = jnp.int16
    rows16 = 16

    def count16(ref, pred):
        def body(j, c):
            ind = jnp.where(pred(ref[j]), i16(1), i16(0))
            part = ind[0:rows16]
            for r in range(1, t // rows16):
                part = part + ind[r * rows16:(r + 1) * rows16]
            return c + part
        per_lane = lax.fori_loop(0, nch, body, jnp.zeros((rows16, t), i16))
        return jnp.sum(per_lane.astype(F32), axis=0, keepdims=True)

    def search16(ref, base):
        def bit_step(n, thr):
            cand = thr + lax.shift_left(jnp.int32(1), 15 - n)
            c16 = cand.astype(i16)
            cnt = base + count16(ref, lambda v: v >= c16)
            return jnp.where(cnt >= float(n_sel), cand, thr)
        return lax.fori_loop(0, 16, bit_step, jnp.full((1, t), -32768, jnp.int32))

    thr_hi = search16(hi_ref, 0.0)
    thr_hi16 = thr_hi.astype(i16)
    above = count16(hi_ref, lambda v: v > thr_hi16)

    def low_halves(j, _):
        low = ((key_ref[j] & 0xFFFF) - 32768).astype(i16)
        lo_ref[j] = jnp.where(hi_ref[j] == thr_hi16, low, i16(-32768))
        return 0

    lax.fori_loop(0, nch, low_halves, 0)
    thr_lo = search16(lo_ref, above)
    thr = lax.shift_left(thr_hi, 16) | (thr_lo + 32768)

    def count_keys(pred):
        def body(j, c):
            return c + jnp.sum(jnp.where(pred(key_ref[j]), 1.0, 0.0), axis=0, keepdims=True)
        return lax.fori_loop(0, nch, body, jnp.zeros((1, t), F32))

    need = float(n_sel) - count_keys(lambda k: k > thr)

    upto = (qcol <= krow).astype(BF16)

    def mask_chunk(j, seen):
        key = key_ref[j]
        eq = key == thr
        rank = jnp.dot(upto, jnp.where(eq, 1.0, 0.0).astype(BF16), preferred_element_type=F32) + seen
        sel = (key > thr) | (eq & (rank <= need))
        sel = sel & (j * t + krow <= i * t + qcol)
        madd_ref[j] = jnp.where(sel, 0.0, NEG)
        return rank[t - 1:t, :]

    lax.fori_loop(0, nch, mask_chunk, jnp.zeros((1, t), F32))

    for pair in range(N_PAIRS):
        lo, hi_ = pair * LANES, (pair + 1) * LANES
        qp = dq_ref[0, :, lo:hi_].astype(F32) * (HEAD_DIM ** -0.5)
        qms = [jnp.where(_half_mask((t, LANES), half), qp, 0.0).astype(BF16) for half in range(2)]

        def scores(j, pair=pair, lo=lo, hi_=hi_, qms=qms):
            ks = pl.multiple_of(j * t, t)
            slot = jnp.clip(j - i + 2, 0, 2)
            k = dk_ref[0, pl.ds(ks, t), lo:hi_]
            madd = madd_ref[j]
            return tuple(_nt_dot(k, qms[half]) + bias_ref[2 * pair + half, slot] + madd for half in range(2))

        def weighted_values(j, half, p, pair=pair):
            return jnp.dot(dvt_ref[0, pair, j, HEAD_DIM * half:HEAD_DIM * (half + 1), :], p,
                           preferred_element_type=F32)

        o0, o1 = _online_softmax_pair(i, scores, weighted_values, None, t)
        o_ref[0, :, lo:hi_] = _pair_out(o0, o1).astype(o_ref.dtype)


def _dsa(z3, zs3, dk, dv, bias, t):
    b, s, _ = z3.shape
    n_sel = min(TOPK_MAX, s // 4)
    dvt = _value_tiles(dv, t)
    return pl.pallas_call(
        functools.partial(_dsa_kernel, t=t, n_sel=n_sel),
        grid=(b, s // t),
        in_specs=[pl.BlockSpec((1, t, W_ATT), lambda bi, i: (bi, i, U_DQ // 4)),
                  pl.BlockSpec((1, t, 2 * LANES), lambda bi, i: (bi, i, U_DIQ // 2)),
                  pl.BlockSpec((1, t, LANES), lambda bi, i: (bi, i, 0)),
                  pl.BlockSpec((1, s, LANES), lambda bi, i: (bi, 0, U_DIK)),
                  pl.BlockSpec((1, s, W_ATT), lambda bi, i: (bi, 0, 0)),
                  pl.BlockSpec((1, N_PAIRS, s // t, LANES, t), lambda bi, i: (bi, 0, 0, 0, 0)),
                  pl.BlockSpec((N_HEADS, 3, t, t), lambda bi, i: (0, 0, 0, 0))],
        out_specs=pl.BlockSpec((1, t, W_ATT), lambda bi, i: (bi, i, 0)),
        out_shape=jax.ShapeDtypeStruct((b, s, W_ATT), BF16),
        scratch_shapes=[pltpu.VMEM((s // t, t, t), jnp.int32), pltpu.VMEM((s // t, t, t), jnp.int16),
                        pltpu.VMEM((s // t, t, t), jnp.int16), pltpu.VMEM((s // t, t, t), F32)],
        compiler_params=_cparams(("parallel", "arbitrary")),
        name="dsa_attention",
    )(z3, z3, zs3, z3, dk, dvt, bias)


def _merge_kernel(yf_ref, ys_ref, yd_ref, wf_ref, ws_ref, wd_ref, g0_ref, g1_ref, g2_ref, o_ref):
    def branch(y_ref, w_ref, g_ref):
        proj = jnp.dot(y_ref[...], w_ref[...].astype(BF16), preferred_element_type=F32)
        return jax.nn.sigmoid(g_ref[...].astype(F32)) * proj

    o_ref[...] = (branch(yf_ref, wf_ref, g0_ref) + branch(ys_ref, ws_ref, g1_ref)
                  + branch(yd_ref, wd_ref, g2_ref)).astype(o_ref.dtype)


def _merge(yf, ys, yd, wf, ws, wd, z, d, tm, tn):
    m = yf.shape[0]
    yblk = pl.BlockSpec((tm, W_ATT), lambda i, j: (i, 0))
    wblk = pl.BlockSpec((W_ATT, tn), lambda i, j: (0, j))
    gblk = lambda g: pl.BlockSpec((tm, tn), lambda i, j: (i, (U_GATES * LANES + g * d) // tn + j))
    return pl.pallas_call(
        _merge_kernel,
        grid=(m // tm, d // tn),
        in_specs=[yblk, yblk, yblk, wblk, wblk, wblk, gblk(0), gblk(1), gblk(2)],
        out_specs=pl.BlockSpec((tm, tn), lambda i, j: (i, j)),
        out_shape=jax.ShapeDtypeStruct((m, d), BF16),
        compiler_params=_cparams(("parallel", "arbitrary")),
        name="branch_merge",
    )(yf, ys, yd, wf, ws, wd, z, z, z)


def _matmul_res_kernel(a_ref, w_ref, r_ref, o_ref):
    o_ref[...] = r_ref[...] + jnp.dot(a_ref[...], w_ref[...].astype(BF16), preferred_element_type=F32)


def _matmul_res(a, w, res, tm, tn):
    m, k = a.shape
    n = w.shape[1]
    return pl.pallas_call(
        _matmul_res_kernel,
        grid=(m // tm, n // tn),
        in_specs=[pl.BlockSpec((tm, k), lambda i, j: (i, 0)), pl.BlockSpec((k, tn), lambda i, j: (0, j)),
                  pl.BlockSpec((tm, tn), lambda i, j: (i, j))],
        out_specs=pl.BlockSpec((tm, tn), lambda i, j: (i, j)),
        out_shape=jax.ShapeDtypeStruct((m, n), F32),
        compiler_params=_cparams(("parallel", "arbitrary")),
        name="out_proj_residual",
    )(a, w, res)


def _row_copy(src_hbm, row, dst_vmem, slot, sem):
    return pltpu.make_async_copy(src_hbm.at[pl.ds(row, 1)], dst_vmem.at[pl.ds(slot, 1)], sem)


def _ffn_kernel(te_ref, nv_ref, src_ref, x_ref, wg_ref, wu_ref, wd_ref, *rest, grouped, tm, nf):
    ti = pl.program_id(0)
    f = pl.program_id(1)
    if grouped:
        o_ref, gbuf, xs, sem = rest
    else:
        (o_ref,) = rest

    @pl.when(f == 0)
    def _():
        o_ref[...] = jnp.zeros_like(o_ref)

    if grouped:
        steps = max(k for k in range(1, nf) if tm % k == 0)
        per = tm // steps

        def issue_rows(tile, lo, n):
            def issue(r, _):
                _row_copy(x_ref, src_ref[tile * tm + lo + r], gbuf, lo + r, sem).start()
                return 0
            lax.fori_loop(0, n, issue, 0, unroll=ISSUE_UNROLL)

        @pl.when((ti == 0) & (f == 0))
        def _():
            issue_rows(0, 0, tm)

        @pl.when((f == 0) & (ti < nv_ref[0]))
        def _():
            def wait(r, _):
                _row_copy(x_ref, 0, gbuf, r, sem).wait()
                return 0
            lax.fori_loop(0, tm, wait, 0)
            xs[...] = gbuf[...].astype(BF16)

        @pl.when((f > 0) & (f <= steps) & (ti + 1 < nv_ref[0]))
        def _():
            issue_rows(ti + 1, (f - 1) * per, per)

    @pl.when(ti < nv_ref[0])
    def _():
        x = xs[...] if grouped else x_ref[...]
        g = jnp.dot(x, wg_ref[0].astype(BF16), preferred_element_type=F32)
        u = jnp.dot(x, wu_ref[0].astype(BF16), preferred_element_type=F32)
        a = (g * jax.nn.sigmoid(g) * u).astype(BF16)
        o_ref[...] += jnp.dot(a, wd_ref[0].astype(BF16), preferred_element_type=F32)


def _ffn(x, wg, wu, wd, tile_expert, n_valid, tm, fc, src=None):
    grouped = src is not None
    d = x.shape[1]
    p = src.shape[0] if grouped else x.shape[0]
    nf = wg.shape[2] // fc

    def chunk(ti, f, nv):
        return jnp.where(ti < nv[0], f, nf - 1)

    wspecs = [pl.BlockSpec((1, d, fc), lambda ti, f, te, nv, sr: (te[ti], 0, chunk(ti, f, nv))),
              pl.BlockSpec((1, d, fc), lambda ti, f, te, nv, sr: (te[ti], 0, chunk(ti, f, nv))),
              pl.BlockSpec((1, fc, d), lambda ti, f, te, nv, sr: (te[ti], chunk(ti, f, nv), 0))]
    scratch = []
    if grouped:
        in_specs = [pl.BlockSpec(memory_space=pl.ANY)] + wspecs
        args = [x, wg, wu, wd]
        scratch += [pltpu.VMEM((tm, d), F32), pltpu.VMEM((tm, d), BF16), pltpu.SemaphoreType.DMA(())]
    else:
        in_specs = [pl.BlockSpec((tm, d), lambda ti, f, te, nv, sr: (ti, 0))] + wspecs
        args = [x, wg, wu, wd]
        src = jnp.zeros((1,), jnp.int32)
    grid_spec = pltpu.PrefetchScalarGridSpec(
        num_scalar_prefetch=3,
        grid=(p // tm, nf),
        in_specs=in_specs,
        out_specs=pl.BlockSpec((tm, d), lambda ti, f, te, nv, sr: (ti, 0)),
        scratch_shapes=scratch,
    )
    return pl.pallas_call(
        functools.partial(_ffn_kernel, grouped=grouped, tm=tm, nf=nf),
        grid_spec=grid_spec,
        out_shape=jax.ShapeDtypeStruct((p, d), F32),
        compiler_params=_cparams(("arbitrary", "arbitrary")),
        name="grouped_swiglu" if grouped else "dense_swiglu",
    )(tile_expert, n_valid, src, *args)


def _combine_kernel(pos_ref, x_ref, info_ref, y_ref, g_ref, o_ref, buf, sem, *, tt, n, norm):
    i = pl.program_id(0)

    def issue_tile(tile, slot):
        def issue(r, _):
            for k in range(2):
                _row_copy(y_ref, pos_ref[k * n + tile * tt + r], buf.at[slot, k], r, sem.at[slot]).start()
            return 0
        lax.fori_loop(0, tt, issue, 0, unroll=ISSUE_UNROLL)

    @pl.when(i == 0)
    def _():
        issue_tile(0, 0)

    @pl.when(i + 1 < pl.num_programs(0))
    def _():
        issue_tile(i + 1, (i + 1) % 2)

    slot = i % 2

    def wait(r, _):
        for k in range(2):
            _row_copy(y_ref, 0, buf.at[slot, k], r, sem.at[slot]).wait()
        return 0

    lax.fori_loop(0, tt, wait, 0)
    info = info_ref[...]
    o = x_ref[...] + (info[:, 0:1] * buf[slot, 0] + info[:, 1:2] * buf[slot, 1])
    if norm:
        o = o * lax.rsqrt(jnp.mean(o * o, axis=-1, keepdims=True) + EPS) * g_ref[...]
    o_ref[...] = o


def _combine(x2, info, y, pos, tt, gain=None):
    n, d = x2.shape
    norm = gain is not None
    grid_spec = pltpu.PrefetchScalarGridSpec(
        num_scalar_prefetch=1,
        grid=(n // tt,),
        in_specs=[pl.BlockSpec((tt, d), lambda i, ps: (i, 0)), pl.BlockSpec((tt, LANES), lambda i, ps: (i, 0)),
                  pl.BlockSpec(memory_space=pl.ANY), pl.BlockSpec((1, d), lambda i, ps: (0, 0))],
        out_specs=pl.BlockSpec((tt, d), lambda i, ps: (i, 0)),
        scratch_shapes=[pltpu.VMEM((2, 2, tt, d), F32), pltpu.SemaphoreType.DMA((2,))],
    )
    return pl.pallas_call(
        functools.partial(_combine_kernel, tt=tt, n=n, norm=norm),
        grid_spec=grid_spec,
        out_shape=jax.ShapeDtypeStruct((n, d), F32),
        compiler_params=_cparams(("arbitrary",)),
        name="moe_combine",
    )(pos, x2, info, y, (gain if norm else jnp.ones((d,), F32)).reshape(1, d))


def _moe(x2, h, info, wg, wu, wd, tm, fc, out_gain=None):
    n = h.shape[0]
    e = wg.shape[0]
    eid = jnp.concatenate([info[:, 2], info[:, 3]]).astype(jnp.int32)
    tok = jnp.tile(jnp.arange(n, dtype=jnp.int32), 2)
    onehot = (eid[:, None] == jnp.arange(e, dtype=jnp.int32)[None, :]).astype(jnp.int32)
    rank = jnp.sum((jnp.cumsum(onehot, axis=0) - 1) * onehot, axis=1)
    counts = jnp.sum(onehot, axis=0)
    padded = ((counts + tm - 1) // tm) * tm
    ends = jnp.cumsum(padded)
    starts = ends - padded
    pos = starts[eid] + rank
    n_tiles = (2 * n) // tm + e
    p = n_tiles * tm
    src = jnp.zeros((p,), jnp.int32).at[pos].set(tok)
    n_valid = (ends[e - 1] // tm).astype(jnp.int32)
    tile_start = jnp.arange(n_tiles, dtype=jnp.int32) * tm
    tile_start = jnp.minimum(tile_start, (n_valid - 1) * tm)
    tile_expert = jnp.sum((tile_start[:, None] >= ends[None, :]).astype(jnp.int32), axis=1)
    y = _ffn(h, wg, wu, wd, tile_expert, n_valid.reshape(1), tm, fc, src=src)
    return _combine(x2, info, y, pos, _tile(n, 256), out_gain)


def _tile(total, want):
    t = min(total, want)
    assert total % t == 0
    return t


def kernel(x, w_in, b_forget, q_norm, k_norm, kv_norm, w_ukv, w_br_fox, w_br_sb, w_br_dsa, w_out, rel_bias, norm_mix, norm_ffn, w_ffn_gate, w_ffn_up, w_ffn_down, w_router, w_moe_gate, w_moe_up, w_moe_down, norm_final):
    b, s, d = x.shape
    m = b * s
    depth = w_in.shape[0]
    ta = _tile(s, 256)
    tm_norm = _tile(m, 512)
    tm = _tile(m, 1024)
    tn = 512
    fc = 256
    tm_moe = _tile(2 * m, 1024)

    bias = _bias_tiles(rel_bias, ta)
    x2 = x.reshape(m, d)
    pending = None
    for l in range(depth):
        if pending is None:
            h = _rmsnorm(x2, norm_mix[l], BF16, tm_norm)
        else:
            x2, h = _add_rmsnorm(x2, pending, norm_mix[l], tm_norm)
            pending = None
        z, zs = _inproj(h, _relayout_w_in(w_in, l), tm, 2 * tn)
        z3 = z.reshape(b, s, Z_COLS)
        zs3 = zs.reshape(b, s, LANES)
        qn, kn, aq, ak, dk, dv = _prep(z3, zs3, q_norm[l], k_norm[l], kv_norm[l], b_forget[l], w_ukv[l], ta)
        y_fox = _fox(qn, kn, z3, aq, ak, ta).reshape(m, W_ATT)
        y_sb = _sb(z3, ta).reshape(m, W_ATT)
        y_dsa = _dsa(z3, zs3, dk, dv, bias, ta).reshape(m, W_ATT)
        mix = _merge(y_fox, y_sb, y_dsa, w_br_fox[l], w_br_sb[l], w_br_dsa[l], z, d, tm, tn)
        x2 = _matmul_res(mix, w_out[l], x2, tm, tn)
        j = l // 2
        if l % 2 == 0:
            h = _rmsnorm(x2, norm_ffn[l], BF16, tm_norm)
            n_tiles = m // tm
            pending = _ffn(h, w_ffn_gate[j][None], w_ffn_up[j][None], w_ffn_down[j][None],
                           jnp.zeros((n_tiles,), jnp.int32), jnp.full((1,), n_tiles, jnp.int32), tm, fc)
        else:
            h, info = _rmsnorm_router(x2, norm_ffn[l], w_router[j], tm_norm)
            last = l == depth - 1
            x2 = _moe(x2, h, info, w_moe_gate[j], w_moe_up[j], w_moe_down[j], tm_moe, fc,
                      norm_final if last else None)
    if pending is not None:
        x2 = _rmsnorm(x2 + pending, norm_final, F32, tm_norm)
    return x2.reshape(b, s, d)
```

```python
import functools
import math

import jax
import jax.numpy as jnp
import numpy as np
from jax import lax
from jax.experimental import pallas as pl
from jax.experimental.pallas import tpu as pltpu

F32 = jnp.float32
BF16 = jnp.bfloat16

HEAD_DIM = 64
N_HEADS = 8
N_PAIRS = N_HEADS // 2
W_ATT = N_HEADS * HEAD_DIM
KV_RANK = 256
IDX_HEADS = 4
IDX_DIM = 64
TOPK_MAX = 256
N_BUCKETS = 32
MAX_DISTANCE = 128
N_EXPERTS = 8
N_BRANCH = 3
EPS = 1e-6
LANES = 128
NEG = -1e30
SB_UNDERFLOW = -104.0
ISSUE_UNROLL = 8
SUB_ROWS = 256

U_FQ, U_FK, U_FV, U_FO, U_SQ, U_SK, U_SV, U_DQ = 0, 4, 8, 12, 16, 20, 24, 28
U_DLAT, U_DIQ, U_DIK, U_SMALL, U_GATES = 32, 34, 36, 37, 40
N_UNITS = 88
Z_COLS = N_UNITS * LANES
SM_FF, SM_DIW = 0, 8

VMEM_LIMIT = 56 * 1024 * 1024


def _cparams(sem):
    return pltpu.CompilerParams(dimension_semantics=sem, vmem_limit_bytes=VMEM_LIMIT)


def _log_sigmoid(x):
    return jnp.minimum(x, 0.0) - jnp.log1p(jnp.exp(-jnp.abs(x)))


def _rmsnorm_kernel(x_ref, g_ref, o_ref):
    x = x_ref[...]
    ms = jnp.mean(x * x, axis=-1, keepdims=True)
    o_ref[...] = (x * lax.rsqrt(ms + EPS) * g_ref[...]).astype(o_ref.dtype)


def _rmsnorm(x, g, out_dtype, tm):
    m, d = x.shape
    return pl.pallas_call(
        _rmsnorm_kernel,
        grid=(m // tm,),
        in_specs=[pl.BlockSpec((tm, d), lambda i: (i, 0)), pl.BlockSpec((1, d), lambda i: (0, 0))],
        out_specs=pl.BlockSpec((tm, d), lambda i: (i, 0)),
        out_shape=jax.ShapeDtypeStruct((m, d), out_dtype),
        compiler_params=_cparams(("parallel",)),
        name="rmsnorm",
    )(x, g.reshape(1, d))


def _add_rmsnorm_kernel(x_ref, y_ref, g_ref, xo_ref, h_ref):
    x = x_ref[...] + y_ref[...]
    xo_ref[...] = x
    ms = jnp.mean(x * x, axis=-1, keepdims=True)
    h_ref[...] = (x * lax.rsqrt(ms + EPS) * g_ref[...]).astype(h_ref.dtype)


def _add_rmsnorm(x, y, g, tm):
    m, d = x.shape
    blk = pl.BlockSpec((tm, d), lambda i: (i, 0))
    return pl.pallas_call(
        _add_rmsnorm_kernel,
        grid=(m // tm,),
        in_specs=[blk, blk, pl.BlockSpec((1, d), lambda i: (0, 0))],
        out_specs=[blk, blk],
        out_shape=[jax.ShapeDtypeStruct((m, d), F32), jax.ShapeDtypeStruct((m, d), BF16)],
        compiler_params=_cparams(("parallel",)),
        name="add_rmsnorm",
    )(x, y, g.reshape(1, d))


def _rmsnorm_router_kernel(x_ref, g_ref, wr_ref, o_ref, info_ref):
    x = x_ref[...]
    ms = jnp.mean(x * x, axis=-1, keepdims=True)
    h = x * lax.rsqrt(ms + EPS) * g_ref[...]
    o_ref[...] = h.astype(o_ref.dtype)
    logits = jnp.dot(h, wr_ref[...], precision=lax.Precision.HIGHEST, preferred_element_type=F32)
    lane = lax.broadcasted_iota(jnp.int32, logits.shape, 1).astype(F32)
    lg = jnp.where(lane < N_EXPERTS, logits, -jnp.inf)
    v1 = jnp.max(lg, axis=-1, keepdims=True)
    i1 = jnp.min(jnp.where(lg == v1, lane, float(LANES)), axis=-1, keepdims=True)
    lg2 = jnp.where(lane == i1, -jnp.inf, lg)
    v2 = jnp.max(lg2, axis=-1, keepdims=True)
    i2 = jnp.min(jnp.where(lg2 == v2, lane, float(LANES)), axis=-1, keepdims=True)
    e2 = jnp.exp(v2 - v1)
    w1 = 1.0 / (1.0 + e2)
    w2 = e2 / (1.0 + e2)
    info = jnp.where(lane == 0.0, w1, jnp.where(lane == 1.0, w2, jnp.where(
        lane == 2.0, i1, jnp.where(lane == 3.0, i2, 0.0))))
    info_ref[...] = info


def _rmsnorm_router(x, g, w_router, tm):
    m, d = x.shape
    wr = jnp.pad(w_router, ((0, 0), (0, LANES - w_router.shape[1])))
    return pl.pallas_call(
        _rmsnorm_router_kernel,
        grid=(m // tm,),
        in_specs=[pl.BlockSpec((tm, d), lambda i: (i, 0)), pl.BlockSpec((1, d), lambda i: (0, 0)),
                  pl.BlockSpec((d, LANES), lambda i: (0, 0))],
        out_specs=[pl.BlockSpec((tm, d), lambda i: (i, 0)), pl.BlockSpec((tm, LANES), lambda i: (i, 0))],
        out_shape=[jax.ShapeDtypeStruct((m, d), F32), jax.ShapeDtypeStruct((m, LANES), F32)],
        compiler_params=_cparams(("parallel",)),
        name="rmsnorm_router",
    )(x, g.reshape(1, d), wr)


def _inproj_kernel(a_ref, w_ref, z_ref, zs_ref, *, small_tile):
    acc = jnp.dot(a_ref[...], w_ref[...], preferred_element_type=F32)
    z_ref[...] = acc.astype(z_ref.dtype)

    @pl.when(pl.program_id(1) == small_tile)
    def _():
        off = (U_SMALL * LANES) % acc.shape[1]
        zs_ref[...] = acc[:, off:off + LANES]


def _inproj(h, w, tm, tn):
    m, d = h.shape
    return pl.pallas_call(
        functools.partial(_inproj_kernel, small_tile=(U_SMALL * LANES) // tn),
        grid=(m // tm, Z_COLS // tn),
        in_specs=[pl.BlockSpec((tm, d), lambda i, j: (i, 0)), pl.BlockSpec((d, tn), lambda i, j: (0, j))],
        out_specs=[pl.BlockSpec((tm, tn), lambda i, j: (i, j)), pl.BlockSpec((tm, LANES), lambda i, j: (i, 0))],
        out_shape=[jax.ShapeDtypeStruct((m, Z_COLS), BF16), jax.ShapeDtypeStruct((m, LANES), F32)],
        compiler_params=_cparams(("parallel", "arbitrary")),
        name="inproj",
    )(h, w)


W_IN_ORDER = (("fq", W_ATT), ("fk", W_ATT), ("fv", W_ATT), ("ff", N_HEADS), ("fo", W_ATT),
              ("sq", W_ATT), ("sk", W_ATT), ("sv", W_ATT), ("dq", W_ATT), ("dlat", KV_RANK),
              ("diq", IDX_HEADS * IDX_DIM), ("dik", IDX_DIM), ("diw", IDX_HEADS))
RELAYOUT_COLS = 4 * LANES


def _relayout_kernel(tbl_ref, w_ref, ff_ref, o_ref, *, layer, special):
    j = pl.program_id(0)

    @pl.when(j != special)
    def _():
        o_ref[...] = jnp.transpose(w_ref[:, layer, :]).astype(o_ref.dtype)

    @pl.when(j == special)
    def _():
        xt = jnp.transpose(w_ref[0:LANES, layer, :])
        fft = jnp.transpose(ff_ref[:, layer, :])
        lane = lax.broadcasted_iota(jnp.int32, xt.shape, 1)
        dik2 = jnp.where(lane < IDX_DIM, xt, pltpu.roll(xt, IDX_DIM, axis=1))
        diw = pltpu.roll(xt, LANES - IDX_DIM + SM_DIW, axis=1)
        small = jnp.where(lane < N_HEADS, fft,
                          jnp.where((lane >= SM_DIW) & (lane < SM_DIW + IDX_HEADS), diw, 0.0))
        o_ref[:, 0:LANES] = dik2.astype(o_ref.dtype)
        o_ref[:, LANES:2 * LANES] = small.astype(o_ref.dtype)
        o_ref[:, 2 * LANES:] = jnp.zeros((o_ref.shape[0], o_ref.shape[1] - 2 * LANES), o_ref.dtype)


def _relayout_w_in(w_in, layer):
    _, d, n_in = w_in.shape
    src, o = {}, 0
    for name, width in W_IN_ORDER + (("gates", N_BRANCH * d),):
        src[name] = o
        o += width
    assert o == n_in and U_GATES * LANES + N_BRANCH * d == Z_COLS and SM_FF == 0
    assert src["diq"] == src["dlat"] + KV_RANK and src["diw"] == src["dik"] + IDX_DIM
    units = {U_FQ: "fq", U_FK: "fk", U_FV: "fv", U_FO: "fo", U_SQ: "sq", U_SK: "sk", U_SV: "sv",
             U_DQ: "dq", U_DLAT: "dlat", U_DIK: "dik"}
    per = RELAYOUT_COLS // LANES
    tbl = [src[units[u]] for u in range(0, U_GATES, per)]
    tbl += [src["gates"] + k * RELAYOUT_COLS for k in range(N_BRANCH * d // RELAYOUT_COLS)]
    elem = lambda rows: (pl.Element(rows), pl.Element(w_in.shape[0]), pl.Element(d))
    grid_spec = pltpu.PrefetchScalarGridSpec(
        num_scalar_prefetch=1,
        grid=(Z_COLS // RELAYOUT_COLS,),
        in_specs=[pl.BlockSpec(elem(RELAYOUT_COLS), lambda j, tbl: (tbl[j], 0, 0)),
                  pl.BlockSpec(elem(LANES), lambda j, tbl: (src["ff"], 0, 0))],
        out_specs=pl.BlockSpec((d, RELAYOUT_COLS), lambda j, tbl: (0, j)),
    )
    wt = jnp.transpose(w_in, (2, 0, 1))
    return pl.pallas_call(
        functools.partial(_relayout_kernel, layer=layer, special=U_DIK // per),
        grid_spec=grid_spec,
        out_shape=jax.ShapeDtypeStruct((d, Z_COLS), BF16),
        compiler_params=_cparams(("arbitrary",)),
        name="w_in_relayout",
    )(jnp.asarray(tbl, jnp.int32), wt, wt)


def _prep_kernel(zq_ref, zk_ref, zl_ref, zfv_ref, zsv_ref, zs_ref, gq_ref, gk_ref, gkv_ref, bf_ref, wukv_ref,
                 grp_ref, eq_ref, ek_ref, oneq_ref, onek_ref,
                 qn_ref, kn_ref, aq_ref, ak_ref, dk_ref, fvt_ref, svt_ref, dvt_ref, carry_ref, *, ts):
    def value_tile(v):
        return jnp.transpose(v).reshape(N_PAIRS, LANES, ts).astype(BF16)

    fvt_ref[0, :, 0] = value_tile(zfv_ref[0].astype(F32))
    svt_ref[0, :, 0] = value_tile(zsv_ref[0].astype(F32))

    @pl.when(pl.program_id(1) == 0)
    def _():
        carry_ref[...] = jnp.zeros_like(carry_ref)

    def head_norm(z_ref, g_ref):
        x = z_ref[0].astype(F32)
        sq = x * x
        hi = sq.astype(BF16)
        lo = (sq - hi.astype(F32)).astype(BF16)
        ms = (jnp.dot(hi, grp_ref[...], preferred_element_type=F32)
              + jnp.dot(lo, grp_ref[...], preferred_element_type=F32)) * (1.0 / HEAD_DIM)
        return x * lax.rsqrt(ms + EPS) * g_ref[...]

    qn_ref[0] = head_norm(zq_ref, gq_ref).astype(BF16)
    kn_ref[0] = head_norm(zk_ref, gk_ref).astype(BF16)

    lane = lax.broadcasted_iota(jnp.int32, (ts, LANES), 1)
    lf = jnp.where(lane < N_HEADS, _log_sigmoid(zs_ref[0] + bf_ref[...]), 0.0)
    r = lax.broadcasted_iota(jnp.int32, (ts, ts), 0)
    c_ = lax.broadcasted_iota(jnp.int32, (ts, ts), 1)
    tri = (c_ <= r).astype(F32)
    c = jnp.dot(tri, lf, precision=lax.Precision.HIGHEST, preferred_element_type=F32) + carry_ref[...]
    carry_ref[...] = c[ts - 1:ts, :]
    c0 = c.astype(BF16)
    r1 = c - c0.astype(F32)
    c1 = r1.astype(BF16)
    c2 = (r1 - c1.astype(F32)).astype(BF16)
    pieces = (c0, c1, c2)
    aq = oneq_ref[...]
    ak = onek_ref[...]
    for k in range(3):
        aq = aq + jnp.dot(pieces[k], eq_ref[k], preferred_element_type=F32)
        ak = ak - jnp.dot(pieces[k], ek_ref[k], preferred_element_type=F32)
    aq_ref[0] = aq.astype(BF16)
    ak_ref[0] = ak.astype(BF16)

    lat = zl_ref[0, :, :KV_RANK].astype(F32)
    msl = jnp.mean(lat * lat, axis=-1, keepdims=True)
    latn = (lat * lax.rsqrt(msl + EPS) * gkv_ref[...]).astype(BF16)
    kv = jnp.dot(latn, wukv_ref[...], preferred_element_type=F32)
    dk_ref[0] = kv[:, :W_ATT].astype(BF16)
    dvt_ref[0, :, 0] = value_tile(kv[:, W_ATT:].astype(BF16).astype(F32))


def _aug_constants():
    eq = np.zeros((3, LANES, LANES), np.float32)
    ek = np.zeros((3, LANES, LANES), np.float32)
    oneq = np.zeros((1, LANES), np.float32)
    onek = np.zeros((1, LANES), np.float32)
    for h in range(N_HEADS):
        for k in range(3):
            eq[k, h, 8 * h + k] = 1.0
            ek[k, h, 8 * h + 3 + k] = 1.0
            oneq[0, 8 * h + 3 + k] = 1.0
            onek[0, 8 * h + k] = 1.0
    grp = np.kron(np.eye(N_HEADS, dtype=np.float32), np.ones((HEAD_DIM, HEAD_DIM), np.float32))
    return (jnp.asarray(grp, BF16), jnp.asarray(eq, BF16), jnp.asarray(ek, BF16),
            jnp.asarray(oneq), jnp.asarray(onek))


def _prep(z3, zs3, q_norm, k_norm, kv_norm, b_forget, w_ukv, ts):
    b, s, _ = z3.shape
    grp, eq, ek, oneq, onek = _aug_constants()
    gq = (jnp.tile(q_norm, N_HEADS) * HEAD_DIM ** -0.5).reshape(1, W_ATT)
    gk = jnp.tile(k_norm, N_HEADS).reshape(1, W_ATT)
    bf = jnp.pad(b_forget, (SM_FF, LANES - N_HEADS - SM_FF)).reshape(1, LANES)
    const = lambda shape: pl.BlockSpec(shape, lambda bi, si: (0,) * len(shape))
    zblk = lambda unit: pl.BlockSpec((1, ts, W_ATT), lambda bi, si: (bi, si, unit // 4))
    seq_out = lambda w: pl.BlockSpec((1, ts, w), lambda bi, si: (bi, si, 0))
    vt_out = pl.BlockSpec((1, N_PAIRS, 1, LANES, ts), lambda bi, si: (bi, 0, si, 0, 0))
    vt_shape = jax.ShapeDtypeStruct((b, N_PAIRS, s // ts, LANES, ts), BF16)
    return pl.pallas_call(
        functools.partial(_prep_kernel, ts=ts),
        grid=(b, s // ts),
        in_specs=[zblk(U_FQ), zblk(U_FK), zblk(U_DLAT), zblk(U_FV), zblk(U_SV),
                  pl.BlockSpec((1, ts, LANES), lambda bi, si: (bi, si, 0)),
                  const((1, W_ATT)), const((1, W_ATT)), const((1, KV_RANK)), const((1, LANES)),
                  const((KV_RANK, 2 * W_ATT)), const((W_ATT, W_ATT)),
                  const((3, LANES, LANES)), const((3, LANES, LANES)), const((1, LANES)), const((1, LANES))],
        out_specs=[seq_out(W_ATT), seq_out(W_ATT), seq_out(LANES), seq_out(LANES), seq_out(W_ATT),
                   vt_out, vt_out, vt_out],
        out_shape=[jax.ShapeDtypeStruct((b, s, W_ATT), BF16), jax.ShapeDtypeStruct((b, s, W_ATT), BF16),
                   jax.ShapeDtypeStruct((b, s, LANES), BF16), jax.ShapeDtypeStruct((b, s, LANES), BF16),
                   jax.ShapeDtypeStruct((b, s, W_ATT), BF16), vt_shape, vt_shape, vt_shape],
        scratch_shapes=[pltpu.VMEM((1, LANES), F32)],
        compiler_params=_cparams(("parallel", "arbitrary")),
        name="mixer_prep",
    )(z3, z3, z3, z3, z3, zs3, gq, gk, kv_norm.reshape(1, KV_RANK), bf, w_ukv.astype(BF16), grp, eq, ek, oneq, onek)


def _nt_dot(a, b):
    return lax.dot_general(a, b, (((1,), (1,)), ((), ())), preferred_element_type=F32)


def _half_mask(shape, half):
    lane = lax.broadcasted_iota(jnp.int32, shape, 1)
    return (lane >= HEAD_DIM * half) & (lane < HEAD_DIM * (half + 1))


def _pair_out(acc0, acc1):
    return jnp.transpose(jnp.concatenate([acc0, acc1], axis=0))


def _online_softmax_pair(last, scores, weighted_values, mask_last, t):
    def soft(s, m, l):
        m_new = jnp.maximum(m, jnp.max(s, axis=0, keepdims=True))
        alpha = jnp.exp(m - m_new)
        p = jnp.exp(s - m_new)
        return m_new, alpha * l + jnp.sum(p, axis=0, keepdims=True), alpha, p.astype(BF16)

    def drain(j, p, alpha, acc):
        return tuple(alpha[h] * acc[h] + weighted_values(j, h, p[h]) for h in range(2))

    def body(n, c):
        s, p_prev, a_prev, m, l, acc = c
        s_next = scores(n + 1)
        acc = drain(jnp.maximum(n - 1, 0), p_prev, a_prev, acc)
        r = [soft(s[h], m[h], l[h]) for h in range(2)]
        return (s_next, (r[0][3], r[1][3]), (r[0][2], r[1][2]), (r[0][0], r[1][0]), (r[0][1], r[1][1]), acc)

    two = lambda x: (x, x)
    init = (scores(0), two(jnp.zeros((t, t), BF16)), two(jnp.ones((1, t), F32)),
            two(jnp.full((1, t), NEG, F32)), two(jnp.zeros((1, t), F32)), two(jnp.zeros((HEAD_DIM, t), F32)))
    s, p_prev, a_prev, m, l, acc = lax.fori_loop(0, last, body, init)
    acc = drain(jnp.maximum(last - 1, 0), p_prev, a_prev, acc)
    if mask_last is not None:
        s = tuple(mask_last(x) for x in s)
    r = [soft(s[h], m[h], l[h]) for h in range(2)]
    acc = drain(last, (r[0][3], r[1][3]), (r[0][2], r[1][2]), acc)
    return acc[0] / r[0][1], acc[1] / r[1][1]


def _fox_kernel(q_ref, k_ref, vt_ref, aq_ref, ak_ref, fo_ref, o_ref, *, t):
    pair = pl.program_id(1)
    i = pl.program_id(2)
    q = q_ref[0].astype(F32)
    aq = aq_ref[0].astype(F32)
    lane = lax.broadcasted_iota(jnp.int32, (t, LANES), 1)
    krow = lax.broadcasted_iota(jnp.int32, (t, t), 0)
    qcol = lax.broadcasted_iota(jnp.int32, (t, t), 1)
    causal = krow <= qcol
    qcs = []
    for half in range(2):
        head = 2 * pair + half
        qm = jnp.where(_half_mask((t, LANES), half), q, 0.0).astype(BF16)
        am = jnp.where((lane >= 8 * head) & (lane < 8 * head + 6), aq, 0.0).astype(BF16)
        qcs.append(jnp.concatenate([qm, am], axis=1))

    def scores(j):
        ks = pl.multiple_of(j * t, t)
        kc = jnp.concatenate([k_ref[0, pl.ds(ks, t), :], ak_ref[0, pl.ds(ks, t), :]], axis=1)
        return tuple(_nt_dot(kc, qcs[half]) for half in range(2))

    def weighted_values(j, half, p):
        return jnp.dot(vt_ref[0, 0, j, HEAD_DIM * half:HEAD_DIM * (half + 1), :], p, preferred_element_type=F32)

    o0, o1 = _online_softmax_pair(i, scores, weighted_values, lambda s: jnp.where(causal, s, NEG), t)
    o = _pair_out(o0, o1)
    o_ref[0] = (o * jax.nn.sigmoid(fo_ref[0].astype(F32))).astype(o_ref.dtype)


def _fox(qn, kn, vt, aq, ak, z3, t):
    b, s, _ = qn.shape
    qblk = lambda unit: pl.BlockSpec((1, t, LANES), lambda bi, p, i: (bi, i, unit + p))
    return pl.pallas_call(
        functools.partial(_fox_kernel, t=t),
        grid=(b, N_PAIRS, s // t),
        in_specs=[qblk(0),
                  pl.BlockSpec((1, s, LANES), lambda bi, p, i: (bi, 0, p)),
                  pl.BlockSpec((1, 1, s // t, LANES, t), lambda bi, p, i: (bi, p, 0, 0, 0)),
                  pl.BlockSpec((1, t, LANES), lambda bi, p, i: (bi, i, 0)),
                  pl.BlockSpec((1, s, LANES), lambda bi, p, i: (bi, 0, 0)),
                  qblk(U_FO)],
        out_specs=qblk(0),
        out_shape=jax.ShapeDtypeStruct((b, s, W_ATT), BF16),
        compiler_params=_cparams(("parallel", "parallel", "arbitrary")),
        name="fox_attention",
    )(qn, kn, vt, aq, ak, z3)


def _sb_kernel(q_ref, k_ref, vt_ref, o_ref, *, t):
    i = pl.program_id(2)
    q = q_ref[0].astype(F32) * (HEAD_DIM ** -0.5)
    krow = lax.broadcasted_iota(jnp.int32, (t, t), 0)
    qcol = lax.broadcasted_iota(jnp.int32, (t, t), 1)
    strict = krow < qcol
    after = (qcol > krow).astype(BF16)
    qms = [jnp.where(_half_mask((t, LANES), half), q, 0.0).astype(BF16) for half in range(2)]

    def local(j, diag):
        ks = pl.multiple_of(j * t, t)
        k = k_ref[0, pl.ds(ks, t), :]
        out = []
        for half in range(2):
            z = _nt_dot(k, qms[half])
            lz = _log_sigmoid(z)
            l1m = lz - z
            if diag:
                l1m = jnp.where(strict, l1m, 0.0)
            hi = l1m.astype(BF16)
            lo = (l1m - hi.astype(F32)).astype(BF16)
            suffix = (jnp.dot(after, hi, preferred_element_type=F32)
                      + jnp.dot(after, lo, preferred_element_type=F32))
            out.append((lz + suffix, jnp.sum(l1m, axis=0, keepdims=True)))
        return out

    def finish(j, loc, carry, keep):
        new = []
        for half in range(2):
            logw, colsum = loc[half]
            rsum, acc = carry[half]
            a = jnp.exp(logw + rsum)
            if keep is not None:
                a = jnp.where(keep, a, 0.0)
            pv = jnp.dot(vt_ref[0, 0, j, HEAD_DIM * half:HEAD_DIM * (half + 1), :], a.astype(BF16),
                         preferred_element_type=F32)
            new.append((rsum + colsum, acc + pv))
        return tuple(new)

    prev = jnp.maximum(i - 1, 0)
    loc_diag = local(i, True)
    loc_prev = local(prev, False)
    zero = (jnp.zeros((1, t), F32), jnp.zeros((HEAD_DIM, t), F32))
    carry = finish(i, loc_diag, (zero, zero), strict)
    carry = finish(prev, loc_prev, carry, i > 0)

    def more(c):
        n, ((r0, _), (r1, _)) = c
        return (n < i) & (jnp.maximum(jnp.max(r0), jnp.max(r1)) > SB_UNDERFLOW)

    def body(c):
        n, carry = c
        j = i - 1 - n
        return n + 1, finish(j, local(j, False), carry, None)

    _, ((_, acc0), (_, acc1)) = lax.while_loop(more, body, (jnp.int32(1), carry))
    o_ref[0] = _pair_out(acc0, acc1).astype(o_ref.dtype)


def _sb(z3, vt, t):
    b, s, _ = z3.shape
    return pl.pallas_call(
        functools.partial(_sb_kernel, t=t),
        grid=(b, N_PAIRS, s // t),
        in_specs=[pl.BlockSpec((1, t, LANES), lambda bi, p, i: (bi, i, U_SQ + p)),
                  pl.BlockSpec((1, s, LANES), lambda bi, p, i: (bi, 0, U_SK + p)),
                  pl.BlockSpec((1, 1, s // t, LANES, t), lambda bi, p, i: (bi, p, 0, 0, 0))],
        out_specs=pl.BlockSpec((1, t, LANES), lambda bi, p, i: (bi, i, p)),
        out_shape=jax.ShapeDtypeStruct((b, s, W_ATT), BF16),
        compiler_params=_cparams(("parallel", "parallel", "arbitrary")),
        name="stickbreak_attention",
    )(z3, z3, vt)


def _t5_bucket(n):
    max_exact = N_BUCKETS // 2
    nf = jnp.maximum(n, 1).astype(F32)
    large = max_exact + (jnp.log(nf / max_exact) / math.log(MAX_DISTANCE / max_exact)
                         * (N_BUCKETS - max_exact)).astype(jnp.int32)
    large = jnp.minimum(large, N_BUCKETS - 1)
    return jnp.where(n < max_exact, n, large)


def _bias_tiles_kernel(relb_ref, o_ref, *, t):
    h = pl.program_id(0)
    krow = lax.broadcasted_iota(jnp.int32, (t, t), 0)
    qcol = lax.broadcasted_iota(jnp.int32, (t, t), 1)
    o_ref[0, 0] = jnp.full((t, t), relb_ref[N_BUCKETS - 1, h], F32)
    for slot, shift in ((1, t), (2, 0)):
        bucket = _t5_bucket(jnp.maximum(qcol - krow + shift, 0))
        val = jnp.full((t, t), relb_ref[0, h], F32)
        for k in range(1, N_BUCKETS):
            val = jnp.where(bucket == k, relb_ref[k, h], val)
        o_ref[0, slot] = val


def _bias_tiles(rel_bias, t):
    assert t >= MAX_DISTANCE
    return pl.pallas_call(
        functools.partial(_bias_tiles_kernel, t=t),
        grid=(N_HEADS,),
        in_specs=[pl.BlockSpec(memory_space=pltpu.SMEM)],
        out_specs=pl.BlockSpec((1, 3, t, t), lambda h: (h, 0, 0, 0)),
        out_shape=jax.ShapeDtypeStruct((N_HEADS, 3, t, t), F32),
        compiler_params=_cparams(("arbitrary",)),
        name="t5_bias_tiles",
    )(rel_bias)


def _dsa_kernel(dq_ref, qi_ref, zs_ref, kidx_ref, dk_ref, dvt_ref, bias_ref, o_ref, key_ref, hi_ref, lo_ref,
                madd_ref, *, t, n_sel):
    i = pl.program_id(1)
    nch = i + 1
    krow = lax.broadcasted_iota(jnp.int32, (t, t), 0)
    qcol = lax.broadcasted_iota(jnp.int32, (t, t), 1)
    idx_scale = (IDX_DIM ** -0.5) * (IDX_HEADS ** -0.5)

    zst = jnp.transpose(zs_ref[0])
    qi = qi_ref[0].astype(F32)
    qih, wih = [], []
    for h in range(IDX_HEADS):
        blk = qi[:, (h // 2) * LANES:(h // 2 + 1) * LANES]
        qih.append(jnp.where(_half_mask((t, LANES), h % 2), blk, 0.0).astype(BF16))
        wih.append(zst[SM_DIW + h:SM_DIW + h + 1, :] * idx_scale)

    def score_chunk(j, _):
        ks = pl.multiple_of(j * t, t)
        kc = kidx_ref[0, pl.ds(ks, t), :]
        sc = jnp.zeros((t, t), F32)
        for h in range(IDX_HEADS):
            sc = sc + jnp.maximum(_nt_dot(kc, qih[h]), 0.0) * wih[h]
        sc = jnp.where(sc == 0.0, 0.0, sc)
        sc = jnp.where(j * t + krow <= i * t + qcol, sc, -jnp.inf)
        bits = pltpu.bitcast(sc, jnp.int32)
        key = jnp.where(bits < 0, bits ^ jnp.int32(0x7FFFFFFF), bits)
        key_ref[j] = key
        hi_ref[j] = (key >> 16).astype(jnp.int16)
        return 0

    lax.fori_loop(0, nch, score_chunk, 0)

    i16 = jnp.int16
    rows16 = 16

    def count16(ref, pred):
        def body(j, c):
            ind = jnp.where(pred(ref[j]), i16(1), i16(0))
            part = ind[0:rows16]
            for r in range(1, t // rows16):
                part = part + ind[r * rows16:(r + 1) * rows16]
            return c + part
        per_lane = lax.fori_loop(0, nch, body, jnp.zeros((rows16, t), i16))
        return jnp.sum(per_lane.astype(F32), axis=0, keepdims=True)

    def search16(ref, base):
        def bit_step(n, thr):
            cand = thr + lax.shift_left(jnp.int32(1), 15 - n)
            c16 = cand.astype(i16)
            cnt = base + count16(ref, lambda v: v >= c16)
            return jnp.where(cnt >= float(n_sel), cand, thr)
        return lax.fori_loop(0, 16, bit_step, jnp.full((1, t), -32768, jnp.int32))

    thr_hi = search16(hi_ref, 0.0)
    thr_hi16 = thr_hi.astype(i16)
    above = count16(hi_ref, lambda v: v > thr_hi16)

    def low_halves(j, _):
        low = ((key_ref[j] & 0xFFFF) - 32768).astype(i16)
        lo_ref[j] = jnp.where(hi_ref[j] == thr_hi16, low, i16(-32768))
        return 0

    lax.fori_loop(0, nch, low_halves, 0)
    thr_lo = search16(lo_ref, above)
    thr = lax.shift_left(thr_hi, 16) | (thr_lo + 32768)

    def count_keys(pred):
        def body(j, c):
            return c + jnp.sum(jnp.where(pred(key_ref[j]), 1.0, 0.0), axis=0, keepdims=True)
        return lax.fori_loop(0, nch, body, jnp.zeros((1, t), F32))

    need = float(n_sel) - count_keys(lambda k: k > thr)

    upto = (qcol <= krow).astype(BF16)

    def mask_chunk(j, seen):
        key = key_ref[j]
        eq = key == thr
        rank = jnp.dot(upto, jnp.where(eq, 1.0, 0.0).astype(BF16), preferred_element_type=F32) + seen
        sel = (key > thr) | (eq & (rank <= need))
        sel = sel & (j * t + krow <= i * t + qcol)
        madd_ref[j] = jnp.where(sel, 0.0, NEG)
        return rank[t - 1:t, :]

    lax.fori_loop(0, nch, mask_chunk, jnp.zeros((1, t), F32))

    for pair in range(N_PAIRS):
        lo, hi_ = pair * LANES, (pair + 1) * LANES
        qp = dq_ref[0, :, lo:hi_].astype(F32) * (HEAD_DIM ** -0.5)
        qms = [jnp.where(_half_mask((t, LANES), half), qp, 0.0).astype(BF16) for half in range(2)]

        def scores(j, pair=pair, lo=lo, hi_=hi_, qms=qms):
            ks = pl.multiple_of(j * t, t)
            slot = jnp.clip(j - i + 2, 0, 2)
            k = dk_ref[0, pl.ds(ks, t), lo:hi_]
            madd = madd_ref[j]
            return tuple(_nt_dot(k, qms[half]) + bias_ref[2 * pair + half, slot] + madd for half in range(2))

        def weighted_values(j, half, p, pair=pair):
            return jnp.dot(dvt_ref[0, pair, j, HEAD_DIM * half:HEAD_DIM * (half + 1), :], p,
                           preferred_element_type=F32)

        o0, o1 = _online_softmax_pair(i, scores, weighted_values, None, t)
        o_ref[0, :, lo:hi_] = _pair_out(o0, o1).astype(o_ref.dtype)


def _dsa(z3, zs3, dk, dvt, bias, t):
    b, s, _ = z3.shape
    n_sel = min(TOPK_MAX, s // 4)
    return pl.pallas_call(
        functools.partial(_dsa_kernel, t=t, n_sel=n_sel),
        grid=(b, s // t),
        in_specs=[pl.BlockSpec((1, t, W_ATT), lambda bi, i: (bi, i, U_DQ // 4)),
                  pl.BlockSpec((1, t, 2 * LANES), lambda bi, i: (bi, i, U_DIQ // 2)),
                  pl.BlockSpec((1, t, LANES), lambda bi, i: (bi, i, 0)),
                  pl.BlockSpec((1, s, LANES), lambda bi, i: (bi, 0, U_DIK)),
                  pl.BlockSpec((1, s, W_ATT), lambda bi, i: (bi, 0, 0)),
                  pl.BlockSpec((1, N_PAIRS, s // t, LANES, t), lambda bi, i: (bi, 0, 0, 0, 0)),
                  pl.BlockSpec((N_HEADS, 3, t, t), lambda bi, i: (0, 0, 0, 0))],
        out_specs=pl.BlockSpec((1, t, W_ATT), lambda bi, i: (bi, i, 0)),
        out_shape=jax.ShapeDtypeStruct((b, s, W_ATT), BF16),
        scratch_shapes=[pltpu.VMEM((s // t, t, t), jnp.int32), pltpu.VMEM((s // t, t, t), jnp.int16),
                        pltpu.VMEM((s // t, t, t), jnp.int16), pltpu.VMEM((s // t, t, t), F32)],
        compiler_params=_cparams(("parallel", "arbitrary")),
        name="dsa_attention",
    )(z3, z3, zs3, z3, dk, dvt, bias)


def _merge_kernel(yf_ref, ys_ref, yd_ref, wf_ref, ws_ref, wd_ref, g0_ref, g1_ref, g2_ref, o_ref):
    def branch(y_ref, w_ref, g_ref):
        proj = jnp.dot(y_ref[...], w_ref[...].astype(BF16), preferred_element_type=F32)
        return jax.nn.sigmoid(g_ref[...].astype(F32)) * proj

    o_ref[...] = (branch(yf_ref, wf_ref, g0_ref) + branch(ys_ref, ws_ref, g1_ref)
                  + branch(yd_ref, wd_ref, g2_ref)).astype(o_ref.dtype)


def _merge(yf, ys, yd, wf, ws, wd, z, d, tm, tn):
    m = yf.shape[0]
    yblk = pl.BlockSpec((tm, W_ATT), lambda i, j: (i, 0))
    wblk = pl.BlockSpec((W_ATT, tn), lambda i, j: (0, j))
    gblk = lambda g: pl.BlockSpec((tm, tn), lambda i, j: (i, (U_GATES * LANES + g * d) // tn + j))
    return pl.pallas_call(
        _merge_kernel,
        grid=(m // tm, d // tn),
        in_specs=[yblk, yblk, yblk, wblk, wblk, wblk, gblk(0), gblk(1), gblk(2)],
        out_specs=pl.BlockSpec((tm, tn), lambda i, j: (i, j)),
        out_shape=jax.ShapeDtypeStruct((m, d), BF16),
        compiler_params=_cparams(("parallel", "arbitrary")),
        name="branch_merge",
    )(yf, ys, yd, wf, ws, wd, z, z, z)


def _matmul_res_kernel(a_ref, w_ref, r_ref, o_ref):
    o_ref[...] = r_ref[...] + jnp.dot(a_ref[...], w_ref[...].astype(BF16), preferred_element_type=F32)


def _matmul_res(a, w, res, tm, tn):
    m, k = a.shape
    n = w.shape[1]
    return pl.pallas_call(
        _matmul_res_kernel,
        grid=(m // tm, n // tn),
        in_specs=[pl.BlockSpec((tm, k), lambda i, j: (i, 0)), pl.BlockSpec((k, tn), lambda i, j: (0, j)),
                  pl.BlockSpec((tm, tn), lambda i, j: (i, j))],
        out_specs=pl.BlockSpec((tm, tn), lambda i, j: (i, j)),
        out_shape=jax.ShapeDtypeStruct((m, n), F32),
        compiler_params=_cparams(("parallel", "arbitrary")),
        name="out_proj_residual",
    )(a, w, res)


def _row_copy(src_hbm, row, dst_vmem, slot, sem):
    return pltpu.make_async_copy(src_hbm.at[pl.ds(row, 1)], dst_vmem.at[pl.ds(slot, 1)], sem)


def _ffn_kernel(te_ref, nv_ref, src_ref, rows_ref, x_ref, wg_ref, wu_ref, wd_ref, *rest, grouped, tm, nf):
    ti = pl.program_id(0)
    f = pl.program_id(1)
    if grouped:
        o_ref, gbuf, xs, wgb, wub, wdb, sem = rest
    else:
        (o_ref,) = rest

    @pl.when(f == 0)
    def _():
        o_ref[...] = jnp.zeros_like(o_ref)

    if grouped:
        steps = max(k for k in range(1, nf) if tm % k == 0)
        per = tm // steps

        def issue_rows(tile, lo, n):
            def issue(r, _):
                _row_copy(x_ref, src_ref[tile * tm + lo + r], gbuf, lo + r, sem).start()
                return 0
            lax.fori_loop(0, n, issue, 0, unroll=ISSUE_UNROLL)

        @pl.when((ti == 0) & (f == 0))
        def _():
            issue_rows(0, 0, tm)

        @pl.when((f == 0) & (ti < nv_ref[0]))
        def _():
            def wait(r, _):
                _row_copy(x_ref, 0, gbuf, r, sem).wait()
                return 0
            lax.fori_loop(0, tm, wait, 0, unroll=ISSUE_UNROLL)
            xs[...] = gbuf[...].astype(BF16)

        @pl.when((f > 0) & (f <= steps) & (ti + 1 < nv_ref[0]))
        def _():
            issue_rows(ti + 1, (f - 1) * per, per)

    def swiglu(x, wgate, wup, wdown):
        g = jnp.dot(x, wgate, preferred_element_type=F32)
        u = jnp.dot(x, wup, preferred_element_type=F32)
        a = (g * jax.nn.sigmoid(g) * u).astype(BF16)
        return jnp.dot(a, wdown, preferred_element_type=F32)

    def whole_tile(x):
        o_ref[...] += swiglu(x, wg_ref[0].astype(BF16), wu_ref[0].astype(BF16), wd_ref[0].astype(BF16))

    @pl.when(ti < nv_ref[0])
    def _():
        if not grouped:
            whole_tile(x_ref[...])
            return
        rows = rows_ref[ti]

        @pl.when(rows > tm - SUB_ROWS)
        def _():
            whole_tile(xs[...])

        @pl.when(rows <= tm - SUB_ROWS)
        def _():
            wgb[...] = wg_ref[0].astype(BF16)
            wub[...] = wu_ref[0].astype(BF16)
            wdb[...] = wd_ref[0].astype(BF16)
            for lo in range(0, tm, SUB_ROWS):
                @pl.when(lo < rows)
                def _(lo=lo):
                    o_ref[lo:lo + SUB_ROWS, :] += swiglu(xs[lo:lo + SUB_ROWS, :], wgb[...], wub[...], wdb[...])


def _ffn(x, wg, wu, wd, tile_expert, n_valid, tm, fc, src=None, tile_rows=None):
    grouped = src is not None
    d = x.shape[1]
    p = src.shape[0] if grouped else x.shape[0]
    nf = wg.shape[2] // fc

    def chunk(ti, f, nv):
        return jnp.where(ti < nv[0], f, nf - 1)

    wspecs = [pl.BlockSpec((1, d, fc), lambda ti, f, te, nv, sr, rw: (te[ti], 0, chunk(ti, f, nv))),
              pl.BlockSpec((1, d, fc), lambda ti, f, te, nv, sr, rw: (te[ti], 0, chunk(ti, f, nv))),
              pl.BlockSpec((1, fc, d), lambda ti, f, te, nv, sr, rw: (te[ti], chunk(ti, f, nv), 0))]
    scratch = []
    if grouped:
        in_specs = [pl.BlockSpec(memory_space=pl.ANY)] + wspecs
        args = [x, wg, wu, wd]
        scratch += [pltpu.VMEM((tm, d), F32), pltpu.VMEM((tm, d), BF16), pltpu.VMEM((d, fc), BF16),
                    pltpu.VMEM((d, fc), BF16), pltpu.VMEM((fc, d), BF16), pltpu.SemaphoreType.DMA(())]
    else:
        in_specs = [pl.BlockSpec((tm, d), lambda ti, f, te, nv, sr, rw: (ti, 0))] + wspecs
        args = [x, wg, wu, wd]
        src = tile_rows = jnp.zeros((1,), jnp.int32)
    grid_spec = pltpu.PrefetchScalarGridSpec(
        num_scalar_prefetch=4,
        grid=(p // tm, nf),
        in_specs=in_specs,
        out_specs=pl.BlockSpec((tm, d), lambda ti, f, te, nv, sr, rw: (ti, 0)),
        scratch_shapes=scratch,
    )
    return pl.pallas_call(
        functools.partial(_ffn_kernel, grouped=grouped, tm=tm, nf=nf),
        grid_spec=grid_spec,
        out_shape=jax.ShapeDtypeStruct((p, d), F32),
        compiler_params=_cparams(("arbitrary", "arbitrary")),
        name="grouped_swiglu" if grouped else "dense_swiglu",
    )(tile_expert, n_valid, src, tile_rows, *args)


def _combine_kernel(pos_ref, x_ref, info_ref, y_ref, g_ref, o_ref, buf, sem, *, tt, n, norm):
    i = pl.program_id(0)

    def issue_tile(tile, slot):
        def issue(r, _):
            for k in range(2):
                _row_copy(y_ref, pos_ref[k * n + tile * tt + r], buf.at[slot, k], r, sem.at[slot]).start()
            return 0
        lax.fori_loop(0, tt, issue, 0, unroll=ISSUE_UNROLL)

    @pl.when(i == 0)
    def _():
        issue_tile(0, 0)

    @pl.when(i + 1 < pl.num_programs(0))
    def _():
        issue_tile(i + 1, (i + 1) % 2)

    slot = i % 2

    def wait(r, _):
        for k in range(2):
            _row_copy(y_ref, 0, buf.at[slot, k], r, sem.at[slot]).wait()
        return 0

    lax.fori_loop(0, tt, wait, 0)
    info = info_ref[...]
    o = x_ref[...] + (info[:, 0:1] * buf[slot, 0] + info[:, 1:2] * buf[slot, 1])
    if norm:
        o = o * lax.rsqrt(jnp.mean(o * o, axis=-1, keepdims=True) + EPS) * g_ref[...]
    o_ref[...] = o


def _combine(x2, info, y, pos, tt, gain=None):
    n, d = x2.shape
    norm = gain is not None
    grid_spec = pltpu.PrefetchScalarGridSpec(
        num_scalar_prefetch=1,
        grid=(n // tt,),
        in_specs=[pl.BlockSpec((tt, d), lambda i, ps: (i, 0)), pl.BlockSpec((tt, LANES), lambda i, ps: (i, 0)),
                  pl.BlockSpec(memory_space=pl.ANY), pl.BlockSpec((1, d), lambda i, ps: (0, 0))],
        out_specs=pl.BlockSpec((tt, d), lambda i, ps: (i, 0)),
        scratch_shapes=[pltpu.VMEM((2, 2, tt, d), F32), pltpu.SemaphoreType.DMA((2,))],
    )
    return pl.pallas_call(
        functools.partial(_combine_kernel, tt=tt, n=n, norm=norm),
        grid_spec=grid_spec,
        out_shape=jax.ShapeDtypeStruct((n, d), F32),
        compiler_params=_cparams(("arbitrary",)),
        name="moe_combine",
    )(pos, x2, info, y, (gain if norm else jnp.ones((d,), F32)).reshape(1, d))


def _moe(x2, h, info, wg, wu, wd, tm, fc, out_gain=None):
    n = h.shape[0]
    e = wg.shape[0]
    eid = jnp.concatenate([info[:, 2], info[:, 3]]).astype(jnp.int32)
    tok = jnp.tile(jnp.arange(n, dtype=jnp.int32), 2)
    onehot = (eid[:, None] == jnp.arange(e, dtype=jnp.int32)[None, :]).astype(jnp.int32)
    rank = jnp.sum((jnp.cumsum(onehot, axis=0) - 1) * onehot, axis=1)
    counts = jnp.sum(onehot, axis=0)
    padded = ((counts + tm - 1) // tm) * tm
    ends = jnp.cumsum(padded)
    starts = ends - padded
    pos = starts[eid] + rank
    n_tiles = (2 * n) // tm + e
    p = n_tiles * tm
    src = jnp.zeros((p,), jnp.int32).at[pos].set(tok)
    n_valid = (ends[e - 1] // tm).astype(jnp.int32)
    tile_start = jnp.arange(n_tiles, dtype=jnp.int32) * tm
    tile_start = jnp.minimum(tile_start, (n_valid - 1) * tm)
    tile_expert = jnp.sum((tile_start[:, None] >= ends[None, :]).astype(jnp.int32), axis=1)
    tile_rows = jnp.clip((starts + counts)[tile_expert] - tile_start, 0, tm).astype(jnp.int32)
    y = _ffn(h, wg, wu, wd, tile_expert, n_valid.reshape(1), tm, fc, src=src, tile_rows=tile_rows)
    return _combine(x2, info, y, pos, _tile(n, 256), out_gain)


def _tile(total, want):
    t = min(total, want)
    assert total % t == 0
    return t


def kernel(x, w_in, b_forget, q_norm, k_norm, kv_norm, w_ukv, w_br_fox, w_br_sb, w_br_dsa, w_out, rel_bias, norm_mix, norm_ffn, w_ffn_gate, w_ffn_up, w_ffn_down, w_router, w_moe_gate, w_moe_up, w_moe_down, norm_final):
    b, s, d = x.shape
    m = b * s
    depth = w_in.shape[0]
    ta = _tile(s, 256)
    tm_norm = _tile(m, 512)
    tm = _tile(m, 1024)
    tn = 512
    fc = 256
    tm_moe = _tile(2 * m, 1024)

    bias = _bias_tiles(rel_bias, ta)
    x2 = x.reshape(m, d)
    pending = None
    for l in range(depth):
        if pending is None:
            h = _rmsnorm(x2, norm_mix[l], BF16, tm_norm)
        else:
            x2, h = _add_rmsnorm(x2, pending, norm_mix[l], tm_norm)
            pending = None
        z, zs = _inproj(h, _relayout_w_in(w_in, l), tm, 2 * tn)
        z3 = z.reshape(b, s, Z_COLS)
        zs3 = zs.reshape(b, s, LANES)
        qn, kn, aq, ak, dk, fvt, svt, dvt = _prep(z3, zs3, q_norm[l], k_norm[l], kv_norm[l], b_forget[l],
                                                  w_ukv[l], ta)
        y_fox = _fox(qn, kn, fvt, aq, ak, z3, ta).reshape(m, W_ATT)
        y_sb = _sb(z3, svt, ta).reshape(m, W_ATT)
        y_dsa = _dsa(z3, zs3, dk, dvt, bias, ta).reshape(m, W_ATT)
        mix = _merge(y_fox, y_sb, y_dsa, w_br_fox[l], w_br_sb[l], w_br_dsa[l], z, d, tm, tn)
        x2 = _matmul_res(mix, w_out[l], x2, tm, tn)
        j = l // 2
        if l % 2 == 0:
            h = _rmsnorm(x2, norm_ffn[l], BF16, tm_norm)
            n_tiles = m // tm
            pending = _ffn(h, w_ffn_gate[j][None], w_ffn_up[j][None], w_ffn_down[j][None],
                           jnp.zeros((n_tiles,), jnp.int32), jnp.full((1,), n_tiles, jnp.int32), tm, fc)
        else:
            h, info = _rmsnorm_router(x2, norm_ffn[l], w_router[j], tm_norm)
            last = l == depth - 1
            x2 = _moe(x2, h, info, w_moe_gate[j], w_moe_up[j], w_moe_down[j], tm_moe, fc,
                      norm_final if last else None)
    if pending is not None:
        x2 = _rmsnorm(x2 + pending, norm_final, F32, tm_norm)
    return x2.reshape(b, s, d)
```

```python
import functools
import math

import jax
import jax.numpy as jnp
import numpy as np
from jax import lax
from jax.experimental import pallas as pl
from jax.experimental.pallas import tpu as pltpu

F32 = jnp.float32
BF16 = jnp.bfloat16

HEAD_DIM = 64
N_HEADS = 8
N_PAIRS = N_HEADS // 2
W_ATT = N_HEADS * HEAD_DIM
KV_RANK = 256
IDX_HEADS = 4
IDX_DIM = 64
TOPK_MAX = 256
N_BUCKETS = 32
MAX_DISTANCE = 128
N_EXPERTS = 8
N_BRANCH = 3
EPS = 1e-6
LANES = 128
NEG = -1e30
SB_UNDERFLOW = -104.0
ISSUE_UNROLL = 8
SUB_ROWS = 256

U_FQ, U_FK, U_FV, U_FO, U_SQ, U_SK, U_SV, U_DQ = 0, 4, 8, 12, 16, 20, 24, 28
U_DLAT, U_DIQ, U_DIK, U_SMALL, U_GATES = 32, 34, 36, 37, 40
N_UNITS = 88
Z_COLS = N_UNITS * LANES
SM_FF, SM_DIW = 0, 8

VMEM_LIMIT = 56 * 1024 * 1024


def _cparams(sem):
    return pltpu.CompilerParams(dimension_semantics=sem, vmem_limit_bytes=VMEM_LIMIT)


def _log_sigmoid(x):
    return jnp.minimum(x, 0.0) - jnp.log1p(jnp.exp(-jnp.abs(x)))


def _rmsnorm_kernel(x_ref, g_ref, o_ref):
    x = x_ref[...]
    ms = jnp.mean(x * x, axis=-1, keepdims=True)
    o_ref[...] = (x * lax.rsqrt(ms + EPS) * g_ref[...]).astype(o_ref.dtype)


def _rmsnorm(x, g, out_dtype, tm):
    m, d = x.shape
    return pl.pallas_call(
        _rmsnorm_kernel,
        grid=(m // tm,),
        in_specs=[pl.BlockSpec((tm, d), lambda i: (i, 0)), pl.BlockSpec((1, d), lambda i: (0, 0))],
        out_specs=pl.BlockSpec((tm, d), lambda i: (i, 0)),
        out_shape=jax.ShapeDtypeStruct((m, d), out_dtype),
        compiler_params=_cparams(("parallel",)),
        name="rmsnorm",
    )(x, g.reshape(1, d))


def _add_rmsnorm_kernel(x_ref, y_ref, g_ref, xo_ref, h_ref):
    x = x_ref[...] + y_ref[...]
    xo_ref[...] = x
    ms = jnp.mean(x * x, axis=-1, keepdims=True)
    h_ref[...] = (x * lax.rsqrt(ms + EPS) * g_ref[...]).astype(h_ref.dtype)


def _add_rmsnorm(x, y, g, tm):
    m, d = x.shape
    blk = pl.BlockSpec((tm, d), lambda i: (i, 0))
    return pl.pallas_call(
        _add_rmsnorm_kernel,
        grid=(m // tm,),
        in_specs=[blk, blk, pl.BlockSpec((1, d), lambda i: (0, 0))],
        out_specs=[blk, blk],
        out_shape=[jax.ShapeDtypeStruct((m, d), F32), jax.ShapeDtypeStruct((m, d), BF16)],
        compiler_params=_cparams(("parallel",)),
        name="add_rmsnorm",
    )(x, y, g.reshape(1, d))


def _rmsnorm_router_kernel(x_ref, g_ref, wr_ref, o_ref, info_ref):
    x = x_ref[...]
    ms = jnp.mean(x * x, axis=-1, keepdims=True)
    h = x * lax.rsqrt(ms + EPS) * g_ref[...]
    o_ref[...] = h.astype(o_ref.dtype)
    logits = jnp.dot(h, wr_ref[...], precision=lax.Precision.HIGHEST, preferred_element_type=F32)
    lane = lax.broadcasted_iota(jnp.int32, logits.shape, 1).astype(F32)
    lg = jnp.where(lane < N_EXPERTS, logits, -jnp.inf)
    v1 = jnp.max(lg, axis=-1, keepdims=True)
    i1 = jnp.min(jnp.where(lg == v1, lane, float(LANES)), axis=-1, keepdims=True)
    lg2 = jnp.where(lane == i1, -jnp.inf, lg)
    v2 = jnp.max(lg2, axis=-1, keepdims=True)
    i2 = jnp.min(jnp.where(lg2 == v2, lane, float(LANES)), axis=-1, keepdims=True)
    e2 = jnp.exp(v2 - v1)
    w1 = 1.0 / (1.0 + e2)
    w2 = e2 / (1.0 + e2)
    info = jnp.where(lane == 0.0, w1, jnp.where(lane == 1.0, w2, jnp.where(
        lane == 2.0, i1, jnp.where(lane == 3.0, i2, 0.0))))
    info_ref[...] = info


def _rmsnorm_router(x, g, w_router, tm):
    m, d = x.shape
    wr = jnp.pad(w_router, ((0, 0), (0, LANES - w_router.shape[1])))
    return pl.pallas_call(
        _rmsnorm_router_kernel,
        grid=(m // tm,),
        in_specs=[pl.BlockSpec((tm, d), lambda i: (i, 0)), pl.BlockSpec((1, d), lambda i: (0, 0)),
                  pl.BlockSpec((d, LANES), lambda i: (0, 0))],
        out_specs=[pl.BlockSpec((tm, d), lambda i: (i, 0)), pl.BlockSpec((tm, LANES), lambda i: (i, 0))],
        out_shape=[jax.ShapeDtypeStruct((m, d), F32), jax.ShapeDtypeStruct((m, LANES), F32)],
        compiler_params=_cparams(("parallel",)),
        name="rmsnorm_router",
    )(x, g.reshape(1, d), wr)


def _inproj_kernel(a_ref, w_ref, z_ref, zs_ref, *, small_tile):
    acc = jnp.dot(a_ref[...], w_ref[...], preferred_element_type=F32)
    z_ref[...] = acc.astype(z_ref.dtype)

    @pl.when(pl.program_id(1) == small_tile)
    def _():
        off = (U_SMALL * LANES) % acc.shape[1]
        zs_ref[...] = acc[:, off:off + LANES]


def _inproj(h, w, tm, tn):
    m, d = h.shape
    return pl.pallas_call(
        functools.partial(_inproj_kernel, small_tile=(U_SMALL * LANES) // tn),
        grid=(m // tm, Z_COLS // tn),
        in_specs=[pl.BlockSpec((tm, d), lambda i, j: (i, 0)), pl.BlockSpec((d, tn), lambda i, j: (0, j))],
        out_specs=[pl.BlockSpec((tm, tn), lambda i, j: (i, j)), pl.BlockSpec((tm, LANES), lambda i, j: (i, 0))],
        out_shape=[jax.ShapeDtypeStruct((m, Z_COLS), BF16), jax.ShapeDtypeStruct((m, LANES), F32)],
        compiler_params=_cparams(("parallel", "arbitrary")),
        name="inproj",
    )(h, w)


W_IN_ORDER = (("fq", W_ATT), ("fk", W_ATT), ("fv", W_ATT), ("ff", N_HEADS), ("fo", W_ATT),
              ("sq", W_ATT), ("sk", W_ATT), ("sv", W_ATT), ("dq", W_ATT), ("dlat", KV_RANK),
              ("diq", IDX_HEADS * IDX_DIM), ("dik", IDX_DIM), ("diw", IDX_HEADS))
RELAYOUT_COLS = 4 * LANES


def _relayout_kernel(tbl_ref, w_ref, ff_ref, o_ref, *, layer, special):
    j = pl.program_id(0)

    @pl.when(j != special)
    def _():
        o_ref[...] = jnp.transpose(w_ref[:, layer, :]).astype(o_ref.dtype)

    @pl.when(j == special)
    def _():
        xt = jnp.transpose(w_ref[0:LANES, layer, :])
        fft = jnp.transpose(ff_ref[:, layer, :])
        lane = lax.broadcasted_iota(jnp.int32, xt.shape, 1)
        dik2 = jnp.where(lane < IDX_DIM, xt, pltpu.roll(xt, IDX_DIM, axis=1))
        diw = pltpu.roll(xt, LANES - IDX_DIM + SM_DIW, axis=1)
        small = jnp.where(lane < N_HEADS, fft,
                          jnp.where((lane >= SM_DIW) & (lane < SM_DIW + IDX_HEADS), diw, 0.0))
        o_ref[:, 0:LANES] = dik2.astype(o_ref.dtype)
        o_ref[:, LANES:2 * LANES] = small.astype(o_ref.dtype)
        o_ref[:, 2 * LANES:] = jnp.zeros((o_ref.shape[0], o_ref.shape[1] - 2 * LANES), o_ref.dtype)


def _relayout_w_in(w_in, layer):
    _, d, n_in = w_in.shape
    src, o = {}, 0
    for name, width in W_IN_ORDER + (("gates", N_BRANCH * d),):
        src[name] = o
        o += width
    assert o == n_in and U_GATES * LANES + N_BRANCH * d == Z_COLS and SM_FF == 0
    assert src["diq"] == src["dlat"] + KV_RANK and src["diw"] == src["dik"] + IDX_DIM
    units = {U_FQ: "fq", U_FK: "fk", U_FV: "fv", U_FO: "fo", U_SQ: "sq", U_SK: "sk", U_SV: "sv",
             U_DQ: "dq", U_DLAT: "dlat", U_DIK: "dik"}
    per = RELAYOUT_COLS // LANES
    tbl = [src[units[u]] for u in range(0, U_GATES, per)]
    tbl += [src["gates"] + k * RELAYOUT_COLS for k in range(N_BRANCH * d // RELAYOUT_COLS)]
    elem = lambda rows: (pl.Element(rows), pl.Element(w_in.shape[0]), pl.Element(d))
    grid_spec = pltpu.PrefetchScalarGridSpec(
        num_scalar_prefetch=1,
        grid=(Z_COLS // RELAYOUT_COLS,),
        in_specs=[pl.BlockSpec(elem(RELAYOUT_COLS), lambda j, tbl: (tbl[j], 0, 0)),
                  pl.BlockSpec(elem(LANES), lambda j, tbl: (src["ff"], 0, 0))],
        out_specs=pl.BlockSpec((d, RELAYOUT_COLS), lambda j, tbl: (0, j)),
    )
    wt = jnp.transpose(w_in, (2, 0, 1))
    return pl.pallas_call(
        functools.partial(_relayout_kernel, layer=layer, special=U_DIK // per),
        grid_spec=grid_spec,
        out_shape=jax.ShapeDtypeStruct((d, Z_COLS), BF16),
        compiler_params=_cparams(("arbitrary",)),
        name="w_in_relayout",
    )(jnp.asarray(tbl, jnp.int32), wt, wt)


def _prep_kernel(zq_ref, zk_ref, zl_ref, zfv_ref, zsv_ref, zs_ref, gq_ref, gk_ref, gkv_ref, bf_ref, wukv_ref,
                 grp_ref, eq_ref, ek_ref, oneq_ref, onek_ref,
                 qn_ref, kn_ref, aq_ref, ak_ref, dk_ref, fvt_ref, svt_ref, dvt_ref, carry_ref, *, ts):
    def value_tile(v):
        return jnp.transpose(v).reshape(N_PAIRS, LANES, ts).astype(BF16)

    fvt_ref[0, :, 0] = value_tile(zfv_ref[0].astype(F32))
    svt_ref[0, :, 0] = value_tile(zsv_ref[0].astype(F32))

    @pl.when(pl.program_id(1) == 0)
    def _():
        carry_ref[...] = jnp.zeros_like(carry_ref)

    def head_norm(z_ref, g_ref):
        x = z_ref[0].astype(F32)
        sq = x * x
        hi = sq.astype(BF16)
        lo = (sq - hi.astype(F32)).astype(BF16)
        ms = (jnp.dot(hi, grp_ref[...], preferred_element_type=F32)
              + jnp.dot(lo, grp_ref[...], preferred_element_type=F32)) * (1.0 / HEAD_DIM)
        return x * lax.rsqrt(ms + EPS) * g_ref[...]

    qn_ref[0] = head_norm(zq_ref, gq_ref).astype(BF16)
    kn_ref[0] = head_norm(zk_ref, gk_ref).astype(BF16)

    lane = lax.broadcasted_iota(jnp.int32, (ts, LANES), 1)
    lf = jnp.where(lane < N_HEADS, _log_sigmoid(zs_ref[0] + bf_ref[...]), 0.0)
    r = lax.broadcasted_iota(jnp.int32, (ts, ts), 0)
    c_ = lax.broadcasted_iota(jnp.int32, (ts, ts), 1)
    tri = (c_ <= r).astype(F32)
    c = jnp.dot(tri, lf, precision=lax.Precision.HIGHEST, preferred_element_type=F32) + carry_ref[...]
    carry_ref[...] = c[ts - 1:ts, :]
    c0 = c.astype(BF16)
    r1 = c - c0.astype(F32)
    c1 = r1.astype(BF16)
    c2 = (r1 - c1.astype(F32)).astype(BF16)
    pieces = (c0, c1, c2)
    aq = oneq_ref[...]
    ak = onek_ref[...]
    for k in range(3):
        aq = aq + jnp.dot(pieces[k], eq_ref[k], preferred_element_type=F32)
        ak = ak - jnp.dot(pieces[k], ek_ref[k], preferred_element_type=F32)
    aq_ref[0] = aq.astype(BF16)
    ak_ref[0] = ak.astype(BF16)

    lat = zl_ref[0, :, :KV_RANK].astype(F32)
    msl = jnp.mean(lat * lat, axis=-1, keepdims=True)
    latn = (lat * lax.rsqrt(msl + EPS) * gkv_ref[...]).astype(BF16)
    kv = jnp.dot(latn, wukv_ref[...], preferred_element_type=F32)
    dk_ref[0] = kv[:, :W_ATT].astype(BF16)
    dvt_ref[0, :, 0] = value_tile(kv[:, W_ATT:].astype(BF16).astype(F32))


def _aug_constants():
    eq = np.zeros((3, LANES, LANES), np.float32)
    ek = np.zeros((3, LANES, LANES), np.float32)
    oneq = np.zeros((1, LANES), np.float32)
    onek = np.zeros((1, LANES), np.float32)
    for h in range(N_HEADS):
        for k in range(3):
            eq[k, h, 8 * h + k] = 1.0
            ek[k, h, 8 * h + 3 + k] = 1.0
            oneq[0, 8 * h + 3 + k] = 1.0
            onek[0, 8 * h + k] = 1.0
    grp = np.kron(np.eye(N_HEADS, dtype=np.float32), np.ones((HEAD_DIM, HEAD_DIM), np.float32))
    return (jnp.asarray(grp, BF16), jnp.asarray(eq, BF16), jnp.asarray(ek, BF16),
            jnp.asarray(oneq), jnp.asarray(onek))


def _prep(z3, zs3, q_norm, k_norm, kv_norm, b_forget, w_ukv, ts):
    b, s, _ = z3.shape
    grp, eq, ek, oneq, onek = _aug_constants()
    gq = (jnp.tile(q_norm, N_HEADS) * HEAD_DIM ** -0.5).reshape(1, W_ATT)
    gk = jnp.tile(k_norm, N_HEADS).reshape(1, W_ATT)
    bf = jnp.pad(b_forget, (SM_FF, LANES - N_HEADS - SM_FF)).reshape(1, LANES)
    const = lambda shape: pl.BlockSpec(shape, lambda bi, si: (0,) * len(shape))
    zblk = lambda unit: pl.BlockSpec((1, ts, W_ATT), lambda bi, si: (bi, si, unit // 4))
    seq_out = lambda w: pl.BlockSpec((1, ts, w), lambda bi, si: (bi, si, 0))
    vt_out = pl.BlockSpec((1, N_PAIRS, 1, LANES, ts), lambda bi, si: (bi, 0, si, 0, 0))
    vt_shape = jax.ShapeDtypeStruct((b, N_PAIRS, s // ts, LANES, ts), BF16)
    return pl.pallas_call(
        functools.partial(_prep_kernel, ts=ts),
        grid=(b, s // ts),
        in_specs=[zblk(U_FQ), zblk(U_FK), zblk(U_DLAT), zblk(U_FV), zblk(U_SV),
                  pl.BlockSpec((1, ts, LANES), lambda bi, si: (bi, si, 0)),
                  const((1, W_ATT)), const((1, W_ATT)), const((1, KV_RANK)), const((1, LANES)),
                  const((KV_RANK, 2 * W_ATT)), const((W_ATT, W_ATT)),
                  const((3, LANES, LANES)), const((3, LANES, LANES)), const((1, LANES)), const((1, LANES))],
        out_specs=[seq_out(W_ATT), seq_out(W_ATT), seq_out(LANES), seq_out(LANES), seq_out(W_ATT),
                   vt_out, vt_out, vt_out],
        out_shape=[jax.ShapeDtypeStruct((b, s, W_ATT), BF16), jax.ShapeDtypeStruct((b, s, W_ATT), BF16),
                   jax.ShapeDtypeStruct((b, s, LANES), BF16), jax.ShapeDtypeStruct((b, s, LANES), BF16),
                   jax.ShapeDtypeStruct((b, s, W_ATT), BF16), vt_shape, vt_shape, vt_shape],
        scratch_shapes=[pltpu.VMEM((1, LANES), F32)],
        compiler_params=_cparams(("parallel", "arbitrary")),
        name="mixer_prep",
    )(z3, z3, z3, z3, z3, zs3, gq, gk, kv_norm.reshape(1, KV_RANK), bf, w_ukv.astype(BF16), grp, eq, ek, oneq, onek)


def _nt_dot(a, b):
    return lax.dot_general(a, b, (((1,), (1,)), ((), ())), preferred_element_type=F32)


def _half_mask(shape, half):
    lane = lax.broadcasted_iota(jnp.int32, shape, 1)
    return (lane >= HEAD_DIM * half) & (lane < HEAD_DIM * (half + 1))


def _pair_out(acc0, acc1):
    return jnp.transpose(jnp.concatenate([acc0, acc1], axis=0))


def _online_softmax_pair(last, scores, weighted_values, mask_last, t):
    def soft(s, m, l):
        m_new = jnp.maximum(m, jnp.max(s, axis=0, keepdims=True))
        alpha = jnp.exp(m - m_new)
        p = jnp.exp(s - m_new)
        return m_new, alpha * l + jnp.sum(p, axis=0, keepdims=True), alpha, p.astype(BF16)

    def drain(j, p, alpha, acc):
        return tuple(alpha[h] * acc[h] + weighted_values(j, h, p[h]) for h in range(2))

    def body(n, c):
        s, p_prev, a_prev, m, l, acc = c
        s_next = scores(n + 1)
        acc = drain(jnp.maximum(n - 1, 0), p_prev, a_prev, acc)
        r = [soft(s[h], m[h], l[h]) for h in range(2)]
        return (s_next, (r[0][3], r[1][3]), (r[0][2], r[1][2]), (r[0][0], r[1][0]), (r[0][1], r[1][1]), acc)

    two = lambda x: (x, x)
    init = (scores(0), two(jnp.zeros((t, t), BF16)), two(jnp.ones((1, t), F32)),
            two(jnp.full((1, t), NEG, F32)), two(jnp.zeros((1, t), F32)), two(jnp.zeros((HEAD_DIM, t), F32)))
    s, p_prev, a_prev, m, l, acc = lax.fori_loop(0, last, body, init)
    acc = drain(jnp.maximum(last - 1, 0), p_prev, a_prev, acc)
    if mask_last is not None:
        s = tuple(mask_last(x) for x in s)
    r = [soft(s[h], m[h], l[h]) for h in range(2)]
    acc = drain(last, (r[0][3], r[1][3]), (r[0][2], r[1][2]), acc)
    return acc[0] / r[0][1], acc[1] / r[1][1]


def _fox_kernel(q_ref, k_ref, vt_ref, aq_ref, ak_ref, fo_ref, o_ref, *, t):
    pair = pl.program_id(1)
    i = pl.program_id(2)
    q = q_ref[0].astype(F32)
    aq = aq_ref[0].astype(F32)
    lane = lax.broadcasted_iota(jnp.int32, (t, LANES), 1)
    krow = lax.broadcasted_iota(jnp.int32, (t, t), 0)
    qcol = lax.broadcasted_iota(jnp.int32, (t, t), 1)
    causal = krow <= qcol
    qcs = []
    for half in range(2):
        head = 2 * pair + half
        qm = jnp.where(_half_mask((t, LANES), half), q, 0.0).astype(BF16)
        am = jnp.where((lane >= 8 * head) & (lane < 8 * head + 6), aq, 0.0).astype(BF16)
        qcs.append(jnp.concatenate([qm, am], axis=1))

    def scores(j):
        ks = pl.multiple_of(j * t, t)
        kc = jnp.concatenate([k_ref[0, pl.ds(ks, t), :], ak_ref[0, pl.ds(ks, t), :]], axis=1)
        return tuple(_nt_dot(kc, qcs[half]) for half in range(2))

    def weighted_values(j, half, p):
        return jnp.dot(vt_ref[0, 0, j, HEAD_DIM * half:HEAD_DIM * (half + 1), :], p, preferred_element_type=F32)

    o0, o1 = _online_softmax_pair(i, scores, weighted_values, lambda s: jnp.where(causal, s, NEG), t)
    o = _pair_out(o0, o1)
    o_ref[0] = (o * jax.nn.sigmoid(fo_ref[0].astype(F32))).astype(o_ref.dtype)


def _fox(qn, kn, vt, aq, ak, z3, t):
    b, s, _ = qn.shape
    qblk = lambda unit: pl.BlockSpec((1, t, LANES), lambda bi, p, i: (bi, i, unit + p))
    return pl.pallas_call(
        functools.partial(_fox_kernel, t=t),
        grid=(b, N_PAIRS, s // t),
        in_specs=[qblk(0),
                  pl.BlockSpec((1, s, LANES), lambda bi, p, i: (bi, 0, p)),
                  pl.BlockSpec((1, 1, s // t, LANES, t), lambda bi, p, i: (bi, p, 0, 0, 0)),
                  pl.BlockSpec((1, t, LANES), lambda bi, p, i: (bi, i, 0)),
                  pl.BlockSpec((1, s, LANES), lambda bi, p, i: (bi, 0, 0)),
                  qblk(U_FO)],
        out_specs=qblk(0),
        out_shape=jax.ShapeDtypeStruct((b, s, W_ATT), BF16),
        compiler_params=_cparams(("parallel", "parallel", "arbitrary")),
        name="fox_attention",
    )(qn, kn, vt, aq, ak, z3)


def _sb_kernel(q_ref, k_ref, vt_ref, o_ref, *, t):
    i = pl.program_id(2)
    q = q_ref[0].astype(F32) * (HEAD_DIM ** -0.5)
    krow = lax.broadcasted_iota(jnp.int32, (t, t), 0)
    qcol = lax.broadcasted_iota(jnp.int32, (t, t), 1)
    strict = krow < qcol
    after = (qcol > krow).astype(BF16)
    qms = [jnp.where(_half_mask((t, LANES), half), q, 0.0).astype(BF16) for half in range(2)]

    def local(j, diag):
        ks = pl.multiple_of(j * t, t)
        k = k_ref[0, pl.ds(ks, t), :]
        out = []
        for half in range(2):
            z = _nt_dot(k, qms[half])
            lz = _log_sigmoid(z)
            l1m = lz - z
            if diag:
                l1m = jnp.where(strict, l1m, 0.0)
            hi = l1m.astype(BF16)
            lo = (l1m - hi.astype(F32)).astype(BF16)
            suffix = (jnp.dot(after, hi, preferred_element_type=F32)
                      + jnp.dot(after, lo, preferred_element_type=F32))
            out.append((lz + suffix, jnp.sum(l1m, axis=0, keepdims=True)))
        return out

    def finish(j, loc, carry, keep):
        new = []
        for half in range(2):
            logw, colsum = loc[half]
            rsum, acc = carry[half]
            a = jnp.exp(logw + rsum)
            if keep is not None:
                a = jnp.where(keep, a, 0.0)
            pv = jnp.dot(vt_ref[0, 0, j, HEAD_DIM * half:HEAD_DIM * (half + 1), :], a.astype(BF16),
                         preferred_element_type=F32)
            new.append((rsum + colsum, acc + pv))
        return tuple(new)

    prev = jnp.maximum(i - 1, 0)
    loc_diag = local(i, True)
    loc_prev = local(prev, False)
    zero = (jnp.zeros((1, t), F32), jnp.zeros((HEAD_DIM, t), F32))
    carry = finish(i, loc_diag, (zero, zero), strict)
    carry = finish(prev, loc_prev, carry, i > 0)

    def more(c):
        n, ((r0, _), (r1, _)) = c
        return (n < i) & (jnp.maximum(jnp.max(r0), jnp.max(r1)) > SB_UNDERFLOW)

    def body(c):
        n, carry = c
        j = i - 1 - n
        return n + 1, finish(j, local(j, False), carry, None)

    _, ((_, acc0), (_, acc1)) = lax.while_loop(more, body, (jnp.int32(1), carry))
    o_ref[0] = _pair_out(acc0, acc1).astype(o_ref.dtype)


def _sb(z3, vt, t):
    b, s, _ = z3.shape
    return pl.pallas_call(
        functools.partial(_sb_kernel, t=t),
        grid=(b, N_PAIRS, s // t),
        in_specs=[pl.BlockSpec((1, t, LANES), lambda bi, p, i: (bi, i, U_SQ + p)),
                  pl.BlockSpec((1, s, LANES), lambda bi, p, i: (bi, 0, U_SK + p)),
                  pl.BlockSpec((1, 1, s // t, LANES, t), lambda bi, p, i: (bi, p, 0, 0, 0))],
        out_specs=pl.BlockSpec((1, t, LANES), lambda bi, p, i: (bi, i, p)),
        out_shape=jax.ShapeDtypeStruct((b, s, W_ATT), BF16),
        compiler_params=_cparams(("parallel", "parallel", "arbitrary")),
        name="stickbreak_attention",
    )(z3, z3, vt)


def _t5_bucket(n):
    max_exact = N_BUCKETS // 2
    nf = jnp.maximum(n, 1).astype(F32)
    large = max_exact + (jnp.log(nf / max_exact) / math.log(MAX_DISTANCE / max_exact)
                         * (N_BUCKETS - max_exact)).astype(jnp.int32)
    large = jnp.minimum(large, N_BUCKETS - 1)
    return jnp.where(n < max_exact, n, large)


def _bias_tiles_kernel(relb_ref, o_ref, *, t):
    h = pl.program_id(0)
    krow = lax.broadcasted_iota(jnp.int32, (t, t), 0)
    qcol = lax.broadcasted_iota(jnp.int32, (t, t), 1)
    o_ref[0, 0] = jnp.full((t, t), relb_ref[N_BUCKETS - 1, h], F32)
    for slot, shift in ((1, t), (2, 0)):
        bucket = _t5_bucket(jnp.maximum(qcol - krow + shift, 0))
        val = jnp.full((t, t), relb_ref[0, h], F32)
        for k in range(1, N_BUCKETS):
            val = jnp.where(bucket == k, relb_ref[k, h], val)
        o_ref[0, slot] = val


def _bias_tiles(rel_bias, t):
    assert t >= MAX_DISTANCE
    return pl.pallas_call(
        functools.partial(_bias_tiles_kernel, t=t),
        grid=(N_HEADS,),
        in_specs=[pl.BlockSpec(memory_space=pltpu.SMEM)],
        out_specs=pl.BlockSpec((1, 3, t, t), lambda h: (h, 0, 0, 0)),
        out_shape=jax.ShapeDtypeStruct((N_HEADS, 3, t, t), F32),
        compiler_params=_cparams(("arbitrary",)),
        name="t5_bias_tiles",
    )(rel_bias)


def _dsa_kernel(dq_ref, qi_ref, zs_ref, kidx_ref, dk_ref, dvt_ref, bias_ref, o_ref, key_ref, hi_ref, lo_ref,
                madd_ref, *, t, n_sel):
    i = pl.program_id(1)
    nch = i + 1
    krow = lax.broadcasted_iota(jnp.int32, (t, t), 0)
    qcol = lax.broadcasted_iota(jnp.int32, (t, t), 1)
    idx_scale = (IDX_DIM ** -0.5) * (IDX_HEADS ** -0.5)

    zst = jnp.transpose(zs_ref[0])
    qi = qi_ref[0].astype(F32)
    qih, wih = [], []
    for h in range(IDX_HEADS):
        blk = qi[:, (h // 2) * LANES:(h // 2 + 1) * LANES]
        qih.append(jnp.where(_half_mask((t, LANES), h % 2), blk, 0.0).astype(BF16))
        wih.append(zst[SM_DIW + h:SM_DIW + h + 1, :] * idx_scale)

    def score_chunk(j, _):
        ks = pl.multiple_of(j * t, t)
        kc = kidx_ref[0, pl.ds(ks, t), :]
        sc = jnp.zeros((t, t), F32)
        for h in range(IDX_HEADS):
            sc = sc + jnp.maximum(_nt_dot(kc, qih[h]), 0.0) * wih[h]
        sc = jnp.where(sc == 0.0, 0.0, sc)
        sc = jnp.where(j * t + krow <= i * t + qcol, sc, -jnp.inf)
        bits = pltpu.bitcast(sc, jnp.int32)
        key = jnp.where(bits < 0, bits ^ jnp.int32(0x7FFFFFFF), bits)
        key_ref[j] = key
        hi_ref[j] = (key >> 16).astype(jnp.int16)
        return 0

    lax.fori_loop(0, nch, score_chunk, 0)

    i16 = jnp.int16
    rows16 = 16
    lowest = jnp.full((t, t), -32768, i16)
    npairs = (nch + 1) // 2

    @pl.when(nch % 2 == 1)
    def _():
        hi_ref[nch] = lowest
        lo_ref[nch] = lowest

    def count16(ref, pred):
        def body(j2, c):
            pieces = []
            for j in (2 * j2, 2 * j2 + 1):
                ind = jnp.where(pred(ref[j]), i16(1), i16(0))
                pieces += [ind[r * rows16:(r + 1) * rows16] for r in range(t // rows16)]
            while len(pieces) > 1:
                pieces = [a + b for a, b in zip(pieces[0::2], pieces[1::2])]
            return c + pieces[0]
        per_lane = lax.fori_loop(0, npairs, body, jnp.zeros((rows16, t), i16))
        return jnp.sum(per_lane.astype(F32), axis=0, keepdims=True)

    def search16(ref, base):
        def bit_step(n, thr):
            cand = thr + lax.shift_left(jnp.int32(1), 15 - n)
            c16 = cand.astype(i16)
            cnt = base + count16(ref, lambda v: v >= c16)
            return jnp.where(cnt >= float(n_sel), cand, thr)
        return lax.fori_loop(0, 16, bit_step, jnp.full((1, t), -32768, jnp.int32))

    thr_hi = search16(hi_ref, 0.0)
    thr_hi16 = thr_hi.astype(i16)
    above = count16(hi_ref, lambda v: v > thr_hi16)

    def low_halves(j, _):
        low = ((key_ref[j] & 0xFFFF) - 32768).astype(i16)
        lo_ref[j] = jnp.where(hi_ref[j] == thr_hi16, low, i16(-32768))
        return 0

    lax.fori_loop(0, nch, low_halves, 0)
    thr_lo = search16(lo_ref, above)
    thr = lax.shift_left(thr_hi, 16) | (thr_lo + 32768)

    def count_keys(pred):
        def body(j, c):
            return c + jnp.sum(jnp.where(pred(key_ref[j]), 1.0, 0.0), axis=0, keepdims=True)
        return lax.fori_loop(0, nch, body, jnp.zeros((1, t), F32))

    need = float(n_sel) - count_keys(lambda k: k > thr)

    upto = (qcol <= krow).astype(BF16)

    def mask_chunk(j, seen):
        key = key_ref[j]
        eq = key == thr
        rank = jnp.dot(upto, jnp.where(eq, 1.0, 0.0).astype(BF16), preferred_element_type=F32) + seen
        sel = (key > thr) | (eq & (rank <= need))
        sel = sel & (j * t + krow <= i * t + qcol)
        madd_ref[j] = jnp.where(sel, 0.0, NEG)
        return rank[t - 1:t, :]

    lax.fori_loop(0, nch, mask_chunk, jnp.zeros((1, t), F32))

    for pair in range(N_PAIRS):
        lo, hi_ = pair * LANES, (pair + 1) * LANES
        qp = dq_ref[0, :, lo:hi_].astype(F32) * (HEAD_DIM ** -0.5)
        qms = [jnp.where(_half_mask((t, LANES), half), qp, 0.0).astype(BF16) for half in range(2)]

        def scores(j, pair=pair, lo=lo, hi_=hi_, qms=qms):
            ks = pl.multiple_of(j * t, t)
            slot = jnp.clip(j - i + 2, 0, 2)
            k = dk_ref[0, pl.ds(ks, t), lo:hi_]
            madd = madd_ref[j]
            return tuple(_nt_dot(k, qms[half]) + bias_ref[2 * pair + half, slot] + madd for half in range(2))

        def weighted_values(j, half, p, pair=pair):
            return jnp.dot(dvt_ref[0, pair, j, HEAD_DIM * half:HEAD_DIM * (half + 1), :], p,
                           preferred_element_type=F32)

        o0, o1 = _online_softmax_pair(i, scores, weighted_values, None, t)
        o_ref[0, :, lo:hi_] = _pair_out(o0, o1).astype(o_ref.dtype)


def _dsa(z3, zs3, dk, dvt, bias, t):
    b, s, _ = z3.shape
    n_sel = min(TOPK_MAX, s // 4)
    nt = s // t
    return pl.pallas_call(
        functools.partial(_dsa_kernel, t=t, n_sel=n_sel),
        grid=(b, s // t),
        in_specs=[pl.BlockSpec((1, t, W_ATT), lambda bi, i: (bi, i, U_DQ // 4)),
                  pl.BlockSpec((1, t, 2 * LANES), lambda bi, i: (bi, i, U_DIQ // 2)),
                  pl.BlockSpec((1, t, LANES), lambda bi, i: (bi, i, 0)),
                  pl.BlockSpec((1, s, LANES), lambda bi, i: (bi, 0, U_DIK)),
                  pl.BlockSpec((1, s, W_ATT), lambda bi, i: (bi, 0, 0)),
                  pl.BlockSpec((1, N_PAIRS, s // t, LANES, t), lambda bi, i: (bi, 0, 0, 0, 0)),
                  pl.BlockSpec((N_HEADS, 3, t, t), lambda bi, i: (0, 0, 0, 0))],
        out_specs=pl.BlockSpec((1, t, W_ATT), lambda bi, i: (bi, i, 0)),
        out_shape=jax.ShapeDtypeStruct((b, s, W_ATT), BF16),
        scratch_shapes=[pltpu.VMEM((nt, t, t), jnp.int32), pltpu.VMEM((nt + nt % 2, t, t), jnp.int16),
                        pltpu.VMEM((nt + nt % 2, t, t), jnp.int16), pltpu.VMEM((nt, t, t), F32)],
        compiler_params=_cparams(("parallel", "arbitrary")),
        name="dsa_attention",
    )(z3, z3, zs3, z3, dk, dvt, bias)


def _merge_kernel(yf_ref, ys_ref, yd_ref, wf_ref, ws_ref, wd_ref, g0_ref, g1_ref, g2_ref, o_ref):
    def branch(y_ref, w_ref, g_ref):
        proj = jnp.dot(y_ref[...], w_ref[...].astype(BF16), preferred_element_type=F32)
        return jax.nn.sigmoid(g_ref[...].astype(F32)) * proj

    o_ref[...] = (branch(yf_ref, wf_ref, g0_ref) + branch(ys_ref, ws_ref, g1_ref)
                  + branch(yd_ref, wd_ref, g2_ref)).astype(o_ref.dtype)


def _merge(yf, ys, yd, wf, ws, wd, z, d, tm, tn):
    m = yf.shape[0]
    yblk = pl.BlockSpec((tm, W_ATT), lambda i, j: (i, 0))
    wblk = pl.BlockSpec((W_ATT, tn), lambda i, j: (0, j))
    gblk = lambda g: pl.BlockSpec((tm, tn), lambda i, j: (i, (U_GATES * LANES + g * d) // tn + j))
    return pl.pallas_call(
        _merge_kernel,
        grid=(m // tm, d // tn),
        in_specs=[yblk, yblk, yblk, wblk, wblk, wblk, gblk(0), gblk(1), gblk(2)],
        out_specs=pl.BlockSpec((tm, tn), lambda i, j: (i, j)),
        out_shape=jax.ShapeDtypeStruct((m, d), BF16),
        compiler_params=_cparams(("parallel", "arbitrary")),
        name="branch_merge",
    )(yf, ys, yd, wf, ws, wd, z, z, z)


def _matmul_res_kernel(a_ref, w_ref, r_ref, o_ref):
    o_ref[...] = r_ref[...] + jnp.dot(a_ref[...], w_ref[...].astype(BF16), preferred_element_type=F32)


def _matmul_res(a, w, res, tm, tn):
    m, k = a.shape
    n = w.shape[1]
    return pl.pallas_call(
        _matmul_res_kernel,
        grid=(m // tm, n // tn),
        in_specs=[pl.BlockSpec((tm, k), lambda i, j: (i, 0)), pl.BlockSpec((k, tn), lambda i, j: (0, j)),
                  pl.BlockSpec((tm, tn), lambda i, j: (i, j))],
        out_specs=pl.BlockSpec((tm, tn), lambda i, j: (i, j)),
        out_shape=jax.ShapeDtypeStruct((m, n), F32),
        compiler_params=_cparams(("parallel", "arbitrary")),
        name="out_proj_residual",
    )(a, w, res)


def _row_copy(src_hbm, row, dst_vmem, slot, sem):
    return pltpu.make_async_copy(src_hbm.at[pl.ds(row, 1)], dst_vmem.at[pl.ds(slot, 1)], sem)


def _ffn_kernel(te_ref, nv_ref, src_ref, rows_ref, x_ref, wg_ref, wu_ref, wd_ref, *rest, grouped, tm, nf):
    ti = pl.program_id(0)
    f = pl.program_id(1)
    if grouped:
        o_ref, gbuf, xs, wgb, wub, wdb, sem = rest
    else:
        (o_ref,) = rest

    @pl.when(f == 0)
    def _():
        o_ref[...] = jnp.zeros_like(o_ref)

    if grouped:
        steps = max(k for k in range(1, nf) if tm % k == 0)
        per = tm // steps

        def issue_rows(tile, lo, n):
            def issue(r, _):
                _row_copy(x_ref, src_ref[tile * tm + lo + r], gbuf, lo + r, sem).start()
                return 0
            lax.fori_loop(0, n, issue, 0, unroll=ISSUE_UNROLL)

        @pl.when((ti == 0) & (f == 0))
        def _():
            issue_rows(0, 0, tm)

        @pl.when((f == 0) & (ti < nv_ref[0]))
        def _():
            def wait(r, _):
                _row_copy(x_ref, 0, gbuf, r, sem).wait()
                return 0
            lax.fori_loop(0, tm, wait, 0, unroll=ISSUE_UNROLL)
            xs[...] = gbuf[...].astype(BF16)

        @pl.when((f > 0) & (f <= steps) & (ti + 1 < nv_ref[0]))
        def _():
            issue_rows(ti + 1, (f - 1) * per, per)

    def swiglu(x, wgate, wup, wdown):
        g = jnp.dot(x, wgate, preferred_element_type=F32)
        u = jnp.dot(x, wup, preferred_element_type=F32)
        a = (g * jax.nn.sigmoid(g) * u).astype(BF16)
        return jnp.dot(a, wdown, preferred_element_type=F32)

    def whole_tile(x):
        o_ref[...] += swiglu(x, wg_ref[0].astype(BF16), wu_ref[0].astype(BF16), wd_ref[0].astype(BF16))

    @pl.when(ti < nv_ref[0])
    def _():
        if not grouped:
            whole_tile(x_ref[...])
            return
        rows = rows_ref[ti]

        @pl.when(rows > tm - SUB_ROWS)
        def _():
            whole_tile(xs[...])

        @pl.when(rows <= tm - SUB_ROWS)
        def _():
            wgb[...] = wg_ref[0].astype(BF16)
            wub[...] = wu_ref[0].astype(BF16)
            wdb[...] = wd_ref[0].astype(BF16)
            for lo in range(0, tm, SUB_ROWS):
                @pl.when(lo < rows)
                def _(lo=lo):
                    o_ref[lo:lo + SUB_ROWS, :] += swiglu(xs[lo:lo + SUB_ROWS, :], wgb[...], wub[...], wdb[...])


def _ffn(x, wg, wu, wd, tile_expert, n_valid, tm, fc, src=None, tile_rows=None):
    grouped = src is not None
    d = x.shape[1]
    p = src.shape[0] if grouped else x.shape[0]
    nf = wg.shape[2] // fc

    def chunk(ti, f, nv):
        return jnp.where(ti < nv[0], f, nf - 1)

    wspecs = [pl.BlockSpec((1, d, fc), lambda ti, f, te, nv, sr, rw: (te[ti], 0, chunk(ti, f, nv))),
              pl.BlockSpec((1, d, fc), lambda ti, f, te, nv, sr, rw: (te[ti], 0, chunk(ti, f, nv))),
              pl.BlockSpec((1, fc, d), lambda ti, f, te, nv, sr, rw: (te[ti], chunk(ti, f, nv), 0))]
    scratch = []
    if grouped:
        in_specs = [pl.BlockSpec(memory_space=pl.ANY)] + wspecs
        args = [x, wg, wu, wd]
        scratch += [pltpu.VMEM((tm, d), F32), pltpu.VMEM((tm, d), BF16), pltpu.VMEM((d, fc), BF16),
                    pltpu.VMEM((d, fc), BF16), pltpu.VMEM((fc, d), BF16), pltpu.SemaphoreType.DMA(())]
    else:
        in_specs = [pl.BlockSpec((tm, d), lambda ti, f, te, nv, sr, rw: (ti, 0))] + wspecs
        args = [x, wg, wu, wd]
        src = tile_rows = jnp.zeros((1,), jnp.int32)
    grid_spec = pltpu.PrefetchScalarGridSpec(
        num_scalar_prefetch=4,
        grid=(p // tm, nf),
        in_specs=in_specs,
        out_specs=pl.BlockSpec((tm, d), lambda ti, f, te, nv, sr, rw: (ti, 0)),
        scratch_shapes=scratch,
    )
    return pl.pallas_call(
        functools.partial(_ffn_kernel, grouped=grouped, tm=tm, nf=nf),
        grid_spec=grid_spec,
        out_shape=jax.ShapeDtypeStruct((p, d), F32),
        compiler_params=_cparams(("arbitrary", "arbitrary")),
        name="grouped_swiglu" if grouped else "dense_swiglu",
    )(tile_expert, n_valid, src, tile_rows, *args)


def _combine_kernel(pos_ref, x_ref, info_ref, y_ref, g_ref, o_ref, buf, sem, *, tt, n, norm):
    i = pl.program_id(0)

    def issue_tile(tile, slot):
        def issue(r, _):
            for k in range(2):
                _row_copy(y_ref, pos_ref[k * n + tile * tt + r], buf.at[slot, k], r, sem.at[slot]).start()
            return 0
        lax.fori_loop(0, tt, issue, 0, unroll=ISSUE_UNROLL)

    @pl.when(i == 0)
    def _():
        issue_tile(0, 0)

    @pl.when(i + 1 < pl.num_programs(0))
    def _():
        issue_tile(i + 1, (i + 1) % 2)

    slot = i % 2

    def wait(r, _):
        for k in range(2):
            _row_copy(y_ref, 0, buf.at[slot, k], r, sem.at[slot]).wait()
        return 0

    lax.fori_loop(0, tt, wait, 0)
    info = info_ref[...]
    o = x_ref[...] + (info[:, 0:1] * buf[slot, 0] + info[:, 1:2] * buf[slot, 1])
    if norm:
        o = o * lax.rsqrt(jnp.mean(o * o, axis=-1, keepdims=True) + EPS) * g_ref[...]
    o_ref[...] = o


def _combine(x2, info, y, pos, tt, gain=None):
    n, d = x2.shape
    norm = gain is not None
    grid_spec = pltpu.PrefetchScalarGridSpec(
        num_scalar_prefetch=1,
        grid=(n // tt,),
        in_specs=[pl.BlockSpec((tt, d), lambda i, ps: (i, 0)), pl.BlockSpec((tt, LANES), lambda i, ps: (i, 0)),
                  pl.BlockSpec(memory_space=pl.ANY), pl.BlockSpec((1, d), lambda i, ps: (0, 0))],
        out_specs=pl.BlockSpec((tt, d), lambda i, ps: (i, 0)),
        scratch_shapes=[pltpu.VMEM((2, 2, tt, d), F32), pltpu.SemaphoreType.DMA((2,))],
    )
    return pl.pallas_call(
        functools.partial(_combine_kernel, tt=tt, n=n, norm=norm),
        grid_spec=grid_spec,
        out_shape=jax.ShapeDtypeStruct((n, d), F32),
        compiler_params=_cparams(("arbitrary",)),
        name="moe_combine",
    )(pos, x2, info, y, (gain if norm else jnp.ones((d,), F32)).reshape(1, d))


def _moe(x2, h, info, wg, wu, wd, tm, fc, out_gain=None):
    n = h.shape[0]
    e = wg.shape[0]
    eid = jnp.concatenate([info[:, 2], info[:, 3]]).astype(jnp.int32)
    tok = jnp.tile(jnp.arange(n, dtype=jnp.int32), 2)
    onehot = (eid[:, None] == jnp.arange(e, dtype=jnp.int32)[None, :]).astype(jnp.int32)
    rank = jnp.sum((jnp.cumsum(onehot, axis=0) - 1) * onehot, axis=1)
    counts = jnp.sum(onehot, axis=0)
    padded = ((counts + tm - 1) // tm) * tm
    ends = jnp.cumsum(padded)
    starts = ends - padded
    pos = starts[eid] + rank
    n_tiles = (2 * n) // tm + e
    p = n_tiles * tm
    src = jnp.zeros((p,), jnp.int32).at[pos].set(tok)
    n_valid = (ends[e - 1] // tm).astype(jnp.int32)
    tile_start = jnp.arange(n_tiles, dtype=jnp.int32) * tm
    tile_start = jnp.minimum(tile_start, (n_valid - 1) * tm)
    tile_expert = jnp.sum((tile_start[:, None] >= ends[None, :]).astype(jnp.int32), axis=1)
    tile_rows = jnp.clip((starts + counts)[tile_expert] - tile_start, 0, tm).astype(jnp.int32)
    y = _ffn(h, wg, wu, wd, tile_expert, n_valid.reshape(1), tm, fc, src=src, tile_rows=tile_rows)
    return _combine(x2, info, y, pos, _tile(n, 256), out_gain)


def _tile(total, want):
    t = min(total, want)
    assert total % t == 0
    return t


def kernel(x, w_in, b_forget, q_norm, k_norm, kv_norm, w_ukv, w_br_fox, w_br_sb, w_br_dsa, w_out, rel_bias, norm_mix, norm_ffn, w_ffn_gate, w_ffn_up, w_ffn_down, w_router, w_moe_gate, w_moe_up, w_moe_down, norm_final):
    b, s, d = x.shape
    m = b * s
    depth = w_in.shape[0]
    ta = _tile(s, 256)
    tm_norm = _tile(m, 512)
    tm = _tile(m, 1024)
    tn = 512
    fc = 256
    tm_moe = _tile(2 * m, 1024)

    bias = _bias_tiles(rel_bias, ta)
    x2 = x.reshape(m, d)
    pending = None
    for l in range(depth):
        if pending is None:
            h = _rmsnorm(x2, norm_mix[l], BF16, tm_norm)
        else:
            x2, h = _add_rmsnorm(x2, pending, norm_mix[l], tm_norm)
            pending = None
        z, zs = _inproj(h, _relayout_w_in(w_in, l), tm, 2 * tn)
        z3 = z.reshape(b, s, Z_COLS)
        zs3 = zs.reshape(b, s, LANES)
        qn, kn, aq, ak, dk, fvt, svt, dvt = _prep(z3, zs3, q_norm[l], k_norm[l], kv_norm[l], b_forget[l],
                                                  w_ukv[l], ta)
        y_fox = _fox(qn, kn, fvt, aq, ak, z3, ta).reshape(m, W_ATT)
        y_sb = _sb(z3, svt, ta).reshape(m, W_ATT)
        y_dsa = _dsa(z3, zs3, dk, dvt, bias, ta).reshape(m, W_ATT)
        mix = _merge(y_fox, y_sb, y_dsa, w_br_fox[l], w_br_sb[l], w_br_dsa[l], z, d, tm, tn)
        x2 = _matmul_res(mix, w_out[l], x2, tm, 2 * tn)
        j = l // 2
        if l % 2 == 0:
            h = _rmsnorm(x2, norm_ffn[l], BF16, tm_norm)
            n_tiles = m // tm
            pending = _ffn(h, w_ffn_gate[j][None], w_ffn_up[j][None], w_ffn_down[j][None],
                           jnp.zeros((n_tiles,), jnp.int32), jnp.full((1,), n_tiles, jnp.int32), tm, fc)
        else:
            h, info = _rmsnorm_router(x2, norm_ffn[l], w_router[j], tm_norm)
            last = l == depth - 1
            x2 = _moe(x2, h, info, w_moe_gate[j], w_moe_up[j], w_moe_down[j], tm_moe, fc,
                      norm_final if last else None)
    if pending is not None:
        x2 = _rmsnorm(x2 + pending, norm_final, F32, tm_norm)
    return x2.reshape(b, s, d)
```

```python
import functools
import math

import jax
import jax.numpy as jnp
import numpy as np
from jax import lax
from jax.experimental import pallas as pl
from jax.experimental.pallas import tpu as pltpu

F32 = jnp.float32
BF16 = jnp.bfloat16

HEAD_DIM = 64
N_HEADS = 8
N_PAIRS = N_HEADS // 2
W_ATT = N_HEADS * HEAD_DIM
KV_RANK = 256
IDX_HEADS = 4
IDX_DIM = 64
TOPK_MAX = 256
N_BUCKETS = 32
MAX_DISTANCE = 128
N_EXPERTS = 8
N_BRANCH = 3
EPS = 1e-6
LANES = 128
NEG = -1e30
SB_UNDERFLOW = -104.0
ISSUE_UNROLL = 8
SUB_ROWS = 256

U_FQ, U_FK, U_FV, U_FO, U_SQ, U_SK, U_SV, U_DQ = 0, 4, 8, 12, 16, 20, 24, 28
U_DLAT, U_DIQ, U_DIK, U_SMALL, U_GATES = 32, 34, 36, 37, 40
N_UNITS = 88
Z_COLS = N_UNITS * LANES
SM_FF, SM_DIW = 0, 8

VMEM_LIMIT = 56 * 1024 * 1024


def _cparams(sem):
    return pltpu.CompilerParams(dimension_semantics=sem, vmem_limit_bytes=VMEM_LIMIT)


def _log_sigmoid(x):
    return jnp.minimum(x, 0.0) - jnp.log1p(jnp.exp(-jnp.abs(x)))


def _rmsnorm_kernel(x_ref, g_ref, o_ref):
    x = x_ref[...]
    ms = jnp.mean(x * x, axis=-1, keepdims=True)
    o_ref[...] = (x * lax.rsqrt(ms + EPS) * g_ref[...]).astype(o_ref.dtype)


def _rmsnorm(x, g, out_dtype, tm):
    m, d = x.shape
    return pl.pallas_call(
        _rmsnorm_kernel,
        grid=(m // tm,),
        in_specs=[pl.BlockSpec((tm, d), lambda i: (i, 0)), pl.BlockSpec((1, d), lambda i: (0, 0))],
        out_specs=pl.BlockSpec((tm, d), lambda i: (i, 0)),
        out_shape=jax.ShapeDtypeStruct((m, d), out_dtype),
        compiler_params=_cparams(("parallel",)),
        name="rmsnorm",
    )(x, g.reshape(1, d))


def _add_rmsnorm_kernel(x_ref, y_ref, g_ref, xo_ref, h_ref):
    x = x_ref[...] + y_ref[...]
    xo_ref[...] = x
    ms = jnp.mean(x * x, axis=-1, keepdims=True)
    h_ref[...] = (x * lax.rsqrt(ms + EPS) * g_ref[...]).astype(h_ref.dtype)


def _add_rmsnorm(x, y, g, tm):
    m, d = x.shape
    blk = pl.BlockSpec((tm, d), lambda i: (i, 0))
    return pl.pallas_call(
        _add_rmsnorm_kernel,
        grid=(m // tm,),
        in_specs=[blk, blk, pl.BlockSpec((1, d), lambda i: (0, 0))],
        out_specs=[blk, blk],
        out_shape=[jax.ShapeDtypeStruct((m, d), F32), jax.ShapeDtypeStruct((m, d), BF16)],
        compiler_params=_cparams(("parallel",)),
        name="add_rmsnorm",
    )(x, y, g.reshape(1, d))


def _rmsnorm_router_kernel(x_ref, g_ref, wr_ref, o_ref, info_ref):
    x = x_ref[...]
    ms = jnp.mean(x * x, axis=-1, keepdims=True)
    h = x * lax.rsqrt(ms + EPS) * g_ref[...]
    o_ref[...] = h.astype(o_ref.dtype)
    logits = jnp.dot(h, wr_ref[...], precision=lax.Precision.HIGHEST, preferred_element_type=F32)
    lane = lax.broadcasted_iota(jnp.int32, logits.shape, 1).astype(F32)
    lg = jnp.where(lane < N_EXPERTS, logits, -jnp.inf)
    v1 = jnp.max(lg, axis=-1, keepdims=True)
    i1 = jnp.min(jnp.where(lg == v1, lane, float(LANES)), axis=-1, keepdims=True)
    lg2 = jnp.where(lane == i1, -jnp.inf, lg)
    v2 = jnp.max(lg2, axis=-1, keepdims=True)
    i2 = jnp.min(jnp.where(lg2 == v2, lane, float(LANES)), axis=-1, keepdims=True)
    e2 = jnp.exp(v2 - v1)
    w1 = 1.0 / (1.0 + e2)
    w2 = e2 / (1.0 + e2)
    info = jnp.where(lane == 0.0, w1, jnp.where(lane == 1.0, w2, jnp.where(
        lane == 2.0, i1, jnp.where(lane == 3.0, i2, 0.0))))
    info_ref[...] = info


def _rmsnorm_router(x, g, w_router, tm):
    m, d = x.shape
    wr = jnp.pad(w_router, ((0, 0), (0, LANES - w_router.shape[1])))
    return pl.pallas_call(
        _rmsnorm_router_kernel,
        grid=(m // tm,),
        in_specs=[pl.BlockSpec((tm, d), lambda i: (i, 0)), pl.BlockSpec((1, d), lambda i: (0, 0)),
                  pl.BlockSpec((d, LANES), lambda i: (0, 0))],
        out_specs=[pl.BlockSpec((tm, d), lambda i: (i, 0)), pl.BlockSpec((tm, LANES), lambda i: (i, 0))],
        out_shape=[jax.ShapeDtypeStruct((m, d), F32), jax.ShapeDtypeStruct((m, LANES), F32)],
        compiler_params=_cparams(("parallel",)),
        name="rmsnorm_router",
    )(x, g.reshape(1, d), wr)


def _inproj_kernel(a_ref, w_ref, z_ref, zs_ref, *, small_tile):
    acc = jnp.dot(a_ref[...], w_ref[...], preferred_element_type=F32)
    z_ref[...] = acc.astype(z_ref.dtype)

    @pl.when(pl.program_id(1) == small_tile)
    def _():
        off = (U_SMALL * LANES) % acc.shape[1]
        zs_ref[...] = acc[:, off:off + LANES]


def _inproj(h, w, tm, tn):
    m, d = h.shape
    return pl.pallas_call(
        functools.partial(_inproj_kernel, small_tile=(U_SMALL * LANES) // tn),
        grid=(m // tm, Z_COLS // tn),
        in_specs=[pl.BlockSpec((tm, d), lambda i, j: (i, 0)), pl.BlockSpec((d, tn), lambda i, j: (0, j))],
        out_specs=[pl.BlockSpec((tm, tn), lambda i, j: (i, j)), pl.BlockSpec((tm, LANES), lambda i, j: (i, 0))],
        out_shape=[jax.ShapeDtypeStruct((m, Z_COLS), BF16), jax.ShapeDtypeStruct((m, LANES), F32)],
        compiler_params=_cparams(("parallel", "arbitrary")),
        name="inproj",
    )(h, w)


W_IN_ORDER = (("fq", W_ATT), ("fk", W_ATT), ("fv", W_ATT), ("ff", N_HEADS), ("fo", W_ATT),
              ("sq", W_ATT), ("sk", W_ATT), ("sv", W_ATT), ("dq", W_ATT), ("dlat", KV_RANK),
              ("diq", IDX_HEADS * IDX_DIM), ("dik", IDX_DIM), ("diw", IDX_HEADS))
RELAYOUT_COLS = 4 * LANES


def _relayout_kernel(tbl_ref, w_ref, ff_ref, o_ref, *, layer, special):
    j = pl.program_id(0)

    @pl.when(j != special)
    def _():
        o_ref[...] = jnp.transpose(w_ref[:, layer, :]).astype(o_ref.dtype)

    @pl.when(j == special)
    def _():
        xt = jnp.transpose(w_ref[0:LANES, layer, :])
        fft = jnp.transpose(ff_ref[:, layer, :])
        lane = lax.broadcasted_iota(jnp.int32, xt.shape, 1)
        dik2 = jnp.where(lane < IDX_DIM, xt, pltpu.roll(xt, IDX_DIM, axis=1))
        diw = pltpu.roll(xt, LANES - IDX_DIM + SM_DIW, axis=1)
        small = jnp.where(lane < N_HEADS, fft,
                          jnp.where((lane >= SM_DIW) & (lane < SM_DIW + IDX_HEADS), diw, 0.0))
        o_ref[:, 0:LANES] = dik2.astype(o_ref.dtype)
        o_ref[:, LANES:2 * LANES] = small.astype(o_ref.dtype)
        o_ref[:, 2 * LANES:] = jnp.zeros((o_ref.shape[0], o_ref.shape[1] - 2 * LANES), o_ref.dtype)


def _relayout_w_in(w_in, layer):
    _, d, n_in = w_in.shape
    src, o = {}, 0
    for name, width in W_IN_ORDER + (("gates", N_BRANCH * d),):
        src[name] = o
        o += width
    assert o == n_in and U_GATES * LANES + N_BRANCH * d == Z_COLS and SM_FF == 0
    assert src["diq"] == src["dlat"] + KV_RANK and src["diw"] == src["dik"] + IDX_DIM
    units = {U_FQ: "fq", U_FK: "fk", U_FV: "fv", U_FO: "fo", U_SQ: "sq", U_SK: "sk", U_SV: "sv",
             U_DQ: "dq", U_DLAT: "dlat", U_DIK: "dik"}
    per = RELAYOUT_COLS // LANES
    tbl = [src[units[u]] for u in range(0, U_GATES, per)]
    tbl += [src["gates"] + k * RELAYOUT_COLS for k in range(N_BRANCH * d // RELAYOUT_COLS)]
    elem = lambda rows: (pl.Element(rows), pl.Element(w_in.shape[0]), pl.Element(d))
    grid_spec = pltpu.PrefetchScalarGridSpec(
        num_scalar_prefetch=1,
        grid=(Z_COLS // RELAYOUT_COLS,),
        in_specs=[pl.BlockSpec(elem(RELAYOUT_COLS), lambda j, tbl: (tbl[j], 0, 0)),
                  pl.BlockSpec(elem(LANES), lambda j, tbl: (src["ff"], 0, 0))],
        out_specs=pl.BlockSpec((d, RELAYOUT_COLS), lambda j, tbl: (0, j)),
    )
    wt = jnp.transpose(w_in, (2, 0, 1))
    return pl.pallas_call(
        functools.partial(_relayout_kernel, layer=layer, special=U_DIK // per),
        grid_spec=grid_spec,
        out_shape=jax.ShapeDtypeStruct((d, Z_COLS), BF16),
        compiler_params=_cparams(("arbitrary",)),
        name="w_in_relayout",
    )(jnp.asarray(tbl, jnp.int32), wt, wt)


def _prep_kernel(zq_ref, zk_ref, zl_ref, zfv_ref, zsv_ref, zs_ref, gq_ref, gk_ref, gkv_ref, bf_ref, wukv_ref,
                 grp_ref, eq_ref, ek_ref, oneq_ref, onek_ref,
                 qn_ref, kn_ref, aq_ref, ak_ref, dk_ref, fvt_ref, svt_ref, dvt_ref, carry_ref, *, ts):
    def value_tile(v):
        return jnp.transpose(v).reshape(N_PAIRS, LANES, ts).astype(BF16)

    fvt_ref[0, :, 0] = value_tile(zfv_ref[0].astype(F32))
    svt_ref[0, :, 0] = value_tile(zsv_ref[0].astype(F32))

    @pl.when(pl.program_id(1) == 0)
    def _():
        carry_ref[...] = jnp.zeros_like(carry_ref)

    def head_norm(z_ref, g_ref):
        x = z_ref[0].astype(F32)
        sq = x * x
        hi = sq.astype(BF16)
        lo = (sq - hi.astype(F32)).astype(BF16)
        ms = (jnp.dot(hi, grp_ref[...], preferred_element_type=F32)
              + jnp.dot(lo, grp_ref[...], preferred_element_type=F32)) * (1.0 / HEAD_DIM)
        return x * lax.rsqrt(ms + EPS) * g_ref[...]

    qn_ref[0] = head_norm(zq_ref, gq_ref).astype(BF16)
    kn_ref[0] = head_norm(zk_ref, gk_ref).astype(BF16)

    lane = lax.broadcasted_iota(jnp.int32, (ts, LANES), 1)
    lf = jnp.where(lane < N_HEADS, _log_sigmoid(zs_ref[0] + bf_ref[...]), 0.0)
    r = lax.broadcasted_iota(jnp.int32, (ts, ts), 0)
    c_ = lax.broadcasted_iota(jnp.int32, (ts, ts), 1)
    tri = (c_ <= r).astype(F32)
    c = jnp.dot(tri, lf, precision=lax.Precision.HIGHEST, preferred_element_type=F32) + carry_ref[...]
    carry_ref[...] = c[ts - 1:ts, :]
    c0 = c.astype(BF16)
    r1 = c - c0.astype(F32)
    c1 = r1.astype(BF16)
    c2 = (r1 - c1.astype(F32)).astype(BF16)
    pieces = (c0, c1, c2)
    aq = oneq_ref[...]
    ak = onek_ref[...]
    for k in range(3):
        aq = aq + jnp.dot(pieces[k], eq_ref[k], preferred_element_type=F32)
        ak = ak - jnp.dot(pieces[k], ek_ref[k], preferred_element_type=F32)
    aq_ref[0] = aq.astype(BF16)
    ak_ref[0] = ak.astype(BF16)

    lat = zl_ref[0, :, :KV_RANK].astype(F32)
    msl = jnp.mean(lat * lat, axis=-1, keepdims=True)
    latn = (lat * lax.rsqrt(msl + EPS) * gkv_ref[...]).astype(BF16)
    kv = jnp.dot(latn, wukv_ref[...], preferred_element_type=F32)
    dk_ref[0] = kv[:, :W_ATT].astype(BF16)
    dvt_ref[0, :, 0] = value_tile(kv[:, W_ATT:].astype(BF16).astype(F32))


def _aug_constants():
    eq = np.zeros((3, LANES, LANES), np.float32)
    ek = np.zeros((3, LANES, LANES), np.float32)
    oneq = np.zeros((1, LANES), np.float32)
    onek = np.zeros((1, LANES), np.float32)
    for h in range(N_HEADS):
        for k in range(3):
            eq[k, h, 8 * h + k] = 1.0
            ek[k, h, 8 * h + 3 + k] = 1.0
            oneq[0, 8 * h + 3 + k] = 1.0
            onek[0, 8 * h + k] = 1.0
    grp = np.kron(np.eye(N_HEADS, dtype=np.float32), np.ones((HEAD_DIM, HEAD_DIM), np.float32))
    return (jnp.asarray(grp, BF16), jnp.asarray(eq, BF16), jnp.asarray(ek, BF16),
            jnp.asarray(oneq), jnp.asarray(onek))


def _prep(z3, zs3, q_norm, k_norm, kv_norm, b_forget, w_ukv, ts):
    b, s, _ = z3.shape
    grp, eq, ek, oneq, onek = _aug_constants()
    gq = (jnp.tile(q_norm, N_HEADS) * HEAD_DIM ** -0.5).reshape(1, W_ATT)
    gk = jnp.tile(k_norm, N_HEADS).reshape(1, W_ATT)
    bf = jnp.pad(b_forget, (SM_FF, LANES - N_HEADS - SM_FF)).reshape(1, LANES)
    const = lambda shape: pl.BlockSpec(shape, lambda bi, si: (0,) * len(shape))
    zblk = lambda unit: pl.BlockSpec((1, ts, W_ATT), lambda bi, si: (bi, si, unit // 4))
    seq_out = lambda w: pl.BlockSpec((1, ts, w), lambda bi, si: (bi, si, 0))
    vt_out = pl.BlockSpec((1, N_PAIRS, 1, LANES, ts), lambda bi, si: (bi, 0, si, 0, 0))
    vt_shape = jax.ShapeDtypeStruct((b, N_PAIRS, s // ts, LANES, ts), BF16)
    return pl.pallas_call(
        functools.partial(_prep_kernel, ts=ts),
        grid=(b, s // ts),
        in_specs=[zblk(U_FQ), zblk(U_FK), zblk(U_DLAT), zblk(U_FV), zblk(U_SV),
                  pl.BlockSpec((1, ts, LANES), lambda bi, si: (bi, si, 0)),
                  const((1, W_ATT)), const((1, W_ATT)), const((1, KV_RANK)), const((1, LANES)),
                  const((KV_RANK, 2 * W_ATT)), const((W_ATT, W_ATT)),
                  const((3, LANES, LANES)), const((3, LANES, LANES)), const((1, LANES)), const((1, LANES))],
        out_specs=[seq_out(W_ATT), seq_out(W_ATT), seq_out(LANES), seq_out(LANES), seq_out(W_ATT),
                   vt_out, vt_out, vt_out],
        out_shape=[jax.ShapeDtypeStruct((b, s, W_ATT), BF16), jax.ShapeDtypeStruct((b, s, W_ATT), BF16),
                   jax.ShapeDtypeStruct((b, s, LANES), BF16), jax.ShapeDtypeStruct((b, s, LANES), BF16),
                   jax.ShapeDtypeStruct((b, s, W_ATT), BF16), vt_shape, vt_shape, vt_shape],
        scratch_shapes=[pltpu.VMEM((1, LANES), F32)],
        compiler_params=_cparams(("parallel", "arbitrary")),
        name="mixer_prep",
    )(z3, z3, z3, z3, z3, zs3, gq, gk, kv_norm.reshape(1, KV_RANK), bf, w_ukv.astype(BF16), grp, eq, ek, oneq, onek)


def _nt_dot(a, b):
    return lax.dot_general(a, b, (((1,), (1,)), ((), ())), preferred_element_type=F32)


def _half_mask(shape, half):
    lane = lax.broadcasted_iota(jnp.int32, shape, 1)
    return (lane >= HEAD_DIM * half) & (lane < HEAD_DIM * (half + 1))


def _pair_out(acc0, acc1):
    return jnp.transpose(jnp.concatenate([acc0, acc1], axis=0))


def _online_softmax_pair(last, scores, weighted_values, mask_last, t):
    def soft(s, m, l):
        m_new = jnp.maximum(m, jnp.max(s, axis=0, keepdims=True))
        alpha = jnp.exp(m - m_new)
        p = jnp.exp(s - m_new)
        return m_new, alpha * l + jnp.sum(p, axis=0, keepdims=True), alpha, p.astype(BF16)

    def drain(j, p, alpha, acc):
        return tuple(alpha[h] * acc[h] + weighted_values(j, h, p[h]) for h in range(2))

    def body(n, c):
        s, p_prev, a_prev, m, l, acc = c
        s_next = scores(n + 1)
        acc = drain(jnp.maximum(n - 1, 0), p_prev, a_prev, acc)
        r = [soft(s[h], m[h], l[h]) for h in range(2)]
        return (s_next, (r[0][3], r[1][3]), (r[0][2], r[1][2]), (r[0][0], r[1][0]), (r[0][1], r[1][1]), acc)

    two = lambda x: (x, x)
    init = (scores(0), two(jnp.zeros((t, t), BF16)), two(jnp.ones((1, t), F32)),
            two(jnp.full((1, t), NEG, F32)), two(jnp.zeros((1, t), F32)), two(jnp.zeros((HEAD_DIM, t), F32)))
    s, p_prev, a_prev, m, l, acc = lax.fori_loop(0, last, body, init)
    acc = drain(jnp.maximum(last - 1, 0), p_prev, a_prev, acc)
    if mask_last is not None:
        s = tuple(mask_last(x) for x in s)
    r = [soft(s[h], m[h], l[h]) for h in range(2)]
    acc = drain(last, (r[0][3], r[1][3]), (r[0][2], r[1][2]), acc)
    return acc[0] / r[0][1], acc[1] / r[1][1]


def _fox_kernel(q_ref, k_ref, vt_ref, aq_ref, ak_ref, fo_ref, o_ref, *, t):
    pair = pl.program_id(1)
    i = pl.program_id(2)
    q = q_ref[0].astype(F32)
    aq = aq_ref[0].astype(F32)
    lane = lax.broadcasted_iota(jnp.int32, (t, LANES), 1)
    krow = lax.broadcasted_iota(jnp.int32, (t, t), 0)
    qcol = lax.broadcasted_iota(jnp.int32, (t, t), 1)
    causal = krow <= qcol
    qcs = []
    for half in range(2):
        head = 2 * pair + half
        qm = jnp.where(_half_mask((t, LANES), half), q, 0.0).astype(BF16)
        am = jnp.where((lane >= 8 * head) & (lane < 8 * head + 6), aq, 0.0).astype(BF16)
        qcs.append(jnp.concatenate([qm, am], axis=1))

    def scores(j):
        ks = pl.multiple_of(j * t, t)
        kc = jnp.concatenate([k_ref[0, pl.ds(ks, t), :], ak_ref[0, pl.ds(ks, t), :]], axis=1)
        return tuple(_nt_dot(kc, qcs[half]) for half in range(2))

    def weighted_values(j, half, p):
        return jnp.dot(vt_ref[0, 0, j, HEAD_DIM * half:HEAD_DIM * (half + 1), :], p, preferred_element_type=F32)

    o0, o1 = _online_softmax_pair(i, scores, weighted_values, lambda s: jnp.where(causal, s, NEG), t)
    o = _pair_out(o0, o1)
    o_ref[0] = (o * jax.nn.sigmoid(fo_ref[0].astype(F32))).astype(o_ref.dtype)


def _fox(qn, kn, vt, aq, ak, z3, t):
    b, s, _ = qn.shape
    qblk = lambda unit: pl.BlockSpec((1, t, LANES), lambda bi, p, i: (bi, i, unit + p))
    return pl.pallas_call(
        functools.partial(_fox_kernel, t=t),
        grid=(b, N_PAIRS, s // t),
        in_specs=[qblk(0),
                  pl.BlockSpec((1, s, LANES), lambda bi, p, i: (bi, 0, p)),
                  pl.BlockSpec((1, 1, s // t, LANES, t), lambda bi, p, i: (bi, p, 0, 0, 0)),
                  pl.BlockSpec((1, t, LANES), lambda bi, p, i: (bi, i, 0)),
                  pl.BlockSpec((1, s, LANES), lambda bi, p, i: (bi, 0, 0)),
                  qblk(U_FO)],
        out_specs=qblk(0),
        out_shape=jax.ShapeDtypeStruct((b, s, W_ATT), BF16),
        compiler_params=_cparams(("parallel", "parallel", "arbitrary")),
        name="fox_attention",
    )(qn, kn, vt, aq, ak, z3)


def _sb_kernel(q_ref, k_ref, vt_ref, o_ref, *, t):
    i = pl.program_id(2)
    q = q_ref[0].astype(F32) * (HEAD_DIM ** -0.5)
    krow = lax.broadcasted_iota(jnp.int32, (t, t), 0)
    qcol = lax.broadcasted_iota(jnp.int32, (t, t), 1)
    strict = krow < qcol
    after = (qcol > krow).astype(BF16)
    qms = [jnp.where(_half_mask((t, LANES), half), q, 0.0).astype(BF16) for half in range(2)]

    def local(j, diag):
        ks = pl.multiple_of(j * t, t)
        k = k_ref[0, pl.ds(ks, t), :]
        out = []
        for half in range(2):
            z = _nt_dot(k, qms[half])
            lz = _log_sigmoid(z)
            l1m = lz - z
            if diag:
                l1m = jnp.where(strict, l1m, 0.0)
            hi = l1m.astype(BF16)
            lo = (l1m - hi.astype(F32)).astype(BF16)
            suffix = (jnp.dot(after, hi, preferred_element_type=F32)
                      + jnp.dot(after, lo, preferred_element_type=F32))
            out.append((lz + suffix, jnp.sum(l1m, axis=0, keepdims=True)))
        return out

    def finish(j, loc, carry, keep):
        new = []
        for half in range(2):
            logw, colsum = loc[half]
            rsum, acc = carry[half]
            a = jnp.exp(logw + rsum)
            if keep is not None:
                a = jnp.where(keep, a, 0.0)
            pv = jnp.dot(vt_ref[0, 0, j, HEAD_DIM * half:HEAD_DIM * (half + 1), :], a.astype(BF16),
                         preferred_element_type=F32)
            new.append((rsum + colsum, acc + pv))
        return tuple(new)

    prev = jnp.maximum(i - 1, 0)
    loc_diag = local(i, True)
    loc_prev = local(prev, False)
    zero = (jnp.zeros((1, t), F32), jnp.zeros((HEAD_DIM, t), F32))
    carry = finish(i, loc_diag, (zero, zero), strict)
    carry = finish(prev, loc_prev, carry, i > 0)

    def more(c):
        n, ((r0, _), (r1, _)) = c
        return (n < i) & (jnp.maximum(jnp.max(r0), jnp.max(r1)) > SB_UNDERFLOW)

    def body(c):
        n, carry = c
        j = i - 1 - n
        return n + 1, finish(j, local(j, False), carry, None)

    _, ((_, acc0), (_, acc1)) = lax.while_loop(more, body, (jnp.int32(1), carry))
    o_ref[0] = _pair_out(acc0, acc1).astype(o_ref.dtype)


def _sb(z3, vt, t):
    b, s, _ = z3.shape
    return pl.pallas_call(
        functools.partial(_sb_kernel, t=t),
        grid=(b, N_PAIRS, s // t),
        in_specs=[pl.BlockSpec((1, t, LANES), lambda bi, p, i: (bi, i, U_SQ + p)),
                  pl.BlockSpec((1, s, LANES), lambda bi, p, i: (bi, 0, U_SK + p)),
                  pl.BlockSpec((1, 1, s // t, LANES, t), lambda bi, p, i: (bi, p, 0, 0, 0))],
        out_specs=pl.BlockSpec((1, t, LANES), lambda bi, p, i: (bi, i, p)),
        out_shape=jax.ShapeDtypeStruct((b, s, W_ATT), BF16),
        compiler_params=_cparams(("parallel", "parallel", "arbitrary")),
        name="stickbreak_attention",
    )(z3, z3, vt)


def _t5_bucket(n):
    max_exact = N_BUCKETS // 2
    nf = jnp.maximum(n, 1).astype(F32)
    large = max_exact + (jnp.log(nf / max_exact) / math.log(MAX_DISTANCE / max_exact)
                         * (N_BUCKETS - max_exact)).astype(jnp.int32)
    large = jnp.minimum(large, N_BUCKETS - 1)
    return jnp.where(n < max_exact, n, large)


def _bias_tiles_kernel(relb_ref, o_ref, *, t):
    h = pl.program_id(0)
    krow = lax.broadcasted_iota(jnp.int32, (t, t), 0)
    qcol = lax.broadcasted_iota(jnp.int32, (t, t), 1)
    o_ref[0, 0] = jnp.full((t, t), relb_ref[N_BUCKETS - 1, h], F32)
    for slot, shift in ((1, t), (2, 0)):
        bucket = _t5_bucket(jnp.maximum(qcol - krow + shift, 0))
        val = jnp.full((t, t), relb_ref[0, h], F32)
        for k in range(1, N_BUCKETS):
            val = jnp.where(bucket == k, relb_ref[k, h], val)
        o_ref[0, slot] = val


def _bias_tiles(rel_bias, t):
    assert t >= MAX_DISTANCE
    return pl.pallas_call(
        functools.partial(_bias_tiles_kernel, t=t),
        grid=(N_HEADS,),
        in_specs=[pl.BlockSpec(memory_space=pltpu.SMEM)],
        out_specs=pl.BlockSpec((1, 3, t, t), lambda h: (h, 0, 0, 0)),
        out_shape=jax.ShapeDtypeStruct((N_HEADS, 3, t, t), F32),
        compiler_params=_cparams(("arbitrary",)),
        name="t5_bias_tiles",
    )(rel_bias)


def _dsa_kernel(dq_ref, qi_ref, zs_ref, kidx_ref, dk_ref, dvt_ref, bias_ref, o_ref, key_ref, hi_ref, lo_ref,
                madd_ref, *, t, n_sel):
    i = pl.program_id(1)
    nch = i + 1
    krow = lax.broadcasted_iota(jnp.int32, (t, t), 0)
    qcol = lax.broadcasted_iota(jnp.int32, (t, t), 1)
    idx_scale = (IDX_DIM ** -0.5) * (IDX_HEADS ** -0.5)

    zst = jnp.transpose(zs_ref[0])
    qi = qi_ref[0].astype(F32)
    qih, wih = [], []
    for h in range(IDX_HEADS):
        blk = qi[:, (h // 2) * LANES:(h // 2 + 1) * LANES]
        qih.append(jnp.where(_half_mask((t, LANES), h % 2), blk, 0.0).astype(BF16))
        wih.append(zst[SM_DIW + h:SM_DIW + h + 1, :] * idx_scale)

    def score_chunk(j, _):
        ks = pl.multiple_of(j * t, t)
        kc = kidx_ref[0, pl.ds(ks, t), :]
        sc = jnp.zeros((t, t), F32)
        for h in range(IDX_HEADS):
            sc = sc + jnp.maximum(_nt_dot(kc, qih[h]), 0.0) * wih[h]
        sc = jnp.where(sc == 0.0, 0.0, sc)
        sc = jnp.where(j * t + krow <= i * t + qcol, sc, -jnp.inf)
        bits = pltpu.bitcast(sc, jnp.int32)
        key = jnp.where(bits < 0, bits ^ jnp.int32(0x7FFFFFFF), bits)
        key_ref[j] = key
        hi_ref[j] = (key >> 16).astype(jnp.int16)
        return 0

    lax.fori_loop(0, nch, score_chunk, 0)

    i16 = jnp.int16
    rows16 = 16
    lowest = jnp.full((t, t), -32768, i16)
    npairs = (nch + 1) // 2

    @pl.when(nch % 2 == 1)
    def _():
        hi_ref[nch] = lowest
        lo_ref[nch] = lowest

    def count16(ref, pred):
        def body(j2, c):
            pieces = []
            for j in (2 * j2, 2 * j2 + 1):
                ind = jnp.where(pred(ref[j]), i16(1), i16(0))
                pieces += [ind[r * rows16:(r + 1) * rows16] for r in range(t // rows16)]
            while len(pieces) > 1:
                pieces = [a + b for a, b in zip(pieces[0::2], pieces[1::2])]
            return c + pieces[0]
        per_lane = lax.fori_loop(0, npairs, body, jnp.zeros((rows16, t), i16))
        return jnp.sum(per_lane.astype(F32), axis=0, keepdims=True)

    def search16(ref, base):
        def bit_step(n, thr):
            cand = thr + lax.shift_left(jnp.int32(1), 15 - n)
            c16 = cand.astype(i16)
            cnt = base + count16(ref, lambda v: v >= c16)
            return jnp.where(cnt >= float(n_sel), cand, thr)
        return lax.fori_loop(0, 16, bit_step, jnp.full((1, t), -32768, jnp.int32))

    thr_hi = search16(hi_ref, 0.0)
    thr_hi16 = thr_hi.astype(i16)
    above = count16(hi_ref, lambda v: v > thr_hi16)

    def low_halves(j, _):
        low = ((key_ref[j] & 0xFFFF) - 32768).astype(i16)
        lo_ref[j] = jnp.where(hi_ref[j] == thr_hi16, low, i16(-32768))
        return 0

    lax.fori_loop(0, nch, low_halves, 0)
    thr_lo = search16(lo_ref, above)
    thr = lax.shift_left(thr_hi, 16) | (thr_lo + 32768)

    def count_keys(pred):
        def body(j, c):
            return c + jnp.sum(jnp.where(pred(key_ref[j]), 1.0, 0.0), axis=0, keepdims=True)
        return lax.fori_loop(0, nch, body, jnp.zeros((1, t), F32))

    need = float(n_sel) - count_keys(lambda k: k > thr)

    upto = (qcol <= krow).astype(BF16)

    def mask_chunk(j, seen):
        key = key_ref[j]
        eq = key == thr
        rank = jnp.dot(upto, jnp.where(eq, 1.0, 0.0).astype(BF16), preferred_element_type=F32) + seen
        sel = (key > thr) | (eq & (rank <= need))
        sel = sel & (j * t + krow <= i * t + qcol)
        madd_ref[j] = jnp.where(sel, 0.0, NEG)
        return rank[t - 1:t, :]

    lax.fori_loop(0, nch, mask_chunk, jnp.zeros((1, t), F32))

    for pair in range(N_PAIRS):
        lo, hi_ = pair * LANES, (pair + 1) * LANES
        qp = dq_ref[0, :, lo:hi_].astype(F32) * (HEAD_DIM ** -0.5)
        qms = [jnp.where(_half_mask((t, LANES), half), qp, 0.0).astype(BF16) for half in range(2)]

        def scores(j, pair=pair, lo=lo, hi_=hi_, qms=qms):
            ks = pl.multiple_of(j * t, t)
            slot = jnp.clip(j - i + 2, 0, 2)
            k = dk_ref[0, pl.ds(ks, t), lo:hi_]
            madd = madd_ref[j]
            return tuple(_nt_dot(k, qms[half]) + bias_ref[2 * pair + half, slot] + madd for half in range(2))

        def weighted_values(j, half, p, pair=pair):
            return jnp.dot(dvt_ref[0, pair, j, HEAD_DIM * half:HEAD_DIM * (half + 1), :], p,
                           preferred_element_type=F32)

        o0, o1 = _online_softmax_pair(i, scores, weighted_values, None, t)
        o_ref[0, :, lo:hi_] = _pair_out(o0, o1).astype(o_ref.dtype)


def _dsa(z3, zs3, dk, dvt, bias, t):
    b, s, _ = z3.shape
    n_sel = min(TOPK_MAX, s // 4)
    nt = s // t
    return pl.pallas_call(
        functools.partial(_dsa_kernel, t=t, n_sel=n_sel),
        grid=(b, s // t),
        in_specs=[pl.BlockSpec((1, t, W_ATT), lambda bi, i: (bi, i, U_DQ // 4)),
                  pl.BlockSpec((1, t, 2 * LANES), lambda bi, i: (bi, i, U_DIQ // 2)),
                  pl.BlockSpec((1, t, LANES), lambda bi, i: (bi, i, 0)),
                  pl.BlockSpec((1, s, LANES), lambda bi, i: (bi, 0, U_DIK)),
                  pl.BlockSpec((1, s, W_ATT), lambda bi, i: (bi, 0, 0)),
                  pl.BlockSpec((1, N_PAIRS, s // t, LANES, t), lambda bi, i: (bi, 0, 0, 0, 0)),
                  pl.BlockSpec((N_HEADS, 3, t, t), lambda bi, i: (0, 0, 0, 0))],
        out_specs=pl.BlockSpec((1, t, W_ATT), lambda bi, i: (bi, i, 0)),
        out_shape=jax.ShapeDtypeStruct((b, s, W_ATT), BF16),
        scratch_shapes=[pltpu.VMEM((nt, t, t), jnp.int32), pltpu.VMEM((nt + nt % 2, t, t), jnp.int16),
                        pltpu.VMEM((nt + nt % 2, t, t), jnp.int16), pltpu.VMEM((nt, t, t), F32)],
        compiler_params=_cparams(("parallel", "arbitrary")),
        name="dsa_attention",
    )(z3, z3, zs3, z3, dk, dvt, bias)


def _mix_out_kernel(yf_ref, ys_ref, yd_ref, wf_ref, ws_ref, wd_ref, *rest, gw):
    gate_refs, (wo_ref, r_ref, o_ref, mix_ref) = rest[:-4], rest[-4:]
    per_branch = len(gate_refs) // N_BRANCH

    @pl.when(pl.program_id(1) == 0)
    def _():
        for c in range(per_branch):
            cols = slice(c * gw, (c + 1) * gw)
            acc = None
            for b, (y_ref, w_ref) in enumerate(((yf_ref, wf_ref), (ys_ref, ws_ref), (yd_ref, wd_ref))):
                proj = jnp.dot(y_ref[...], w_ref[:, cols], preferred_element_type=F32)
                term = jax.nn.sigmoid(gate_refs[b * per_branch + c][...].astype(F32)) * proj
                acc = term if acc is None else acc + term
            mix_ref[:, cols] = acc.astype(mix_ref.dtype)

    o_ref[...] = r_ref[...] + jnp.dot(mix_ref[...], wo_ref[...], preferred_element_type=F32)


def _mix_out(yf, ys, yd, wf, ws, wd, z, w_out, res, tm, tn, gw):
    m, d = res.shape
    assert (U_GATES * LANES) % gw == 0 and d % gw == 0
    yblk = pl.BlockSpec((tm, W_ATT), lambda i, j: (i, 0))
    wblk = pl.BlockSpec((W_ATT, d), lambda i, j: (0, 0))
    gate_specs = [pl.BlockSpec((tm, gw), lambda i, j, k=k: (i, (U_GATES * LANES) // gw + k))
                  for k in range(N_BRANCH * d // gw)]
    return pl.pallas_call(
        functools.partial(_mix_out_kernel, gw=gw),
        grid=(m // tm, d // tn),
        in_specs=[yblk, yblk, yblk, wblk, wblk, wblk] + gate_specs + [
            pl.BlockSpec((d, tn), lambda i, j: (0, j)), pl.BlockSpec((tm, tn), lambda i, j: (i, j))],
        out_specs=pl.BlockSpec((tm, tn), lambda i, j: (i, j)),
        out_shape=jax.ShapeDtypeStruct((m, d), F32),
        scratch_shapes=[pltpu.VMEM((tm, d), BF16)],
        compiler_params=_cparams(("parallel", "arbitrary")),
        name="merge_out_proj",
    )(yf, ys, yd, wf, ws, wd, *([z] * len(gate_specs)), w_out, res)


def _row_copy(src_hbm, row, dst_vmem, slot, sem):
    return pltpu.make_async_copy(src_hbm.at[pl.ds(row, 1)], dst_vmem.at[pl.ds(slot, 1)], sem)


def _ffn_kernel(te_ref, nv_ref, src_ref, rows_ref, x_ref, wg_ref, wu_ref, wd_ref, *rest, grouped, tm, nf):
    ti = pl.program_id(0)
    f = pl.program_id(1)
    if grouped:
        o_ref, gbuf, xs, wgb, wub, wdb, sem = rest
    else:
        (o_ref,) = rest

    @pl.when(f == 0)
    def _():
        o_ref[...] = jnp.zeros_like(o_ref)

    if grouped:
        steps = max(k for k in range(1, nf) if tm % k == 0)
        per = tm // steps

        def issue_rows(tile, lo, n):
            def issue(r, _):
                _row_copy(x_ref, src_ref[tile * tm + lo + r], gbuf, lo + r, sem).start()
                return 0
            lax.fori_loop(0, n, issue, 0, unroll=ISSUE_UNROLL)

        @pl.when((ti == 0) & (f == 0))
        def _():
            issue_rows(0, 0, tm)

        @pl.when((f == 0) & (ti < nv_ref[0]))
        def _():
            def wait(r, _):
                _row_copy(x_ref, 0, gbuf, r, sem).wait()
                return 0
            lax.fori_loop(0, tm, wait, 0, unroll=ISSUE_UNROLL)
            xs[...] = gbuf[...].astype(BF16)

        @pl.when((f > 0) & (f <= steps) & (ti + 1 < nv_ref[0]))
        def _():
            issue_rows(ti + 1, (f - 1) * per, per)

    def swiglu(x, wgate, wup, wdown):
        g = jnp.dot(x, wgate, preferred_element_type=F32)
        u = jnp.dot(x, wup, preferred_element_type=F32)
        a = (g * jax.nn.sigmoid(g) * u).astype(BF16)
        return jnp.dot(a, wdown, preferred_element_type=F32)

    def whole_tile(x):
        o_ref[...] += swiglu(x, wg_ref[0].astype(BF16), wu_ref[0].astype(BF16), wd_ref[0].astype(BF16))

    @pl.when(ti < nv_ref[0])
    def _():
        if not grouped:
            whole_tile(x_ref[...])
            return
        rows = rows_ref[ti]

        @pl.when(rows > tm - SUB_ROWS)
        def _():
            whole_tile(xs[...])

        @pl.when(rows <= tm - SUB_ROWS)
        def _():
            wgb[...] = wg_ref[0].astype(BF16)
            wub[...] = wu_ref[0].astype(BF16)
            wdb[...] = wd_ref[0].astype(BF16)
            for lo in range(0, tm, SUB_ROWS):
                @pl.when(lo < rows)
                def _(lo=lo):
                    o_ref[lo:lo + SUB_ROWS, :] += swiglu(xs[lo:lo + SUB_ROWS, :], wgb[...], wub[...], wdb[...])


def _ffn(x, wg, wu, wd, tile_expert, n_valid, tm, fc, src=None, tile_rows=None):
    grouped = src is not None
    d = x.shape[1]
    p = src.shape[0] if grouped else x.shape[0]
    nf = wg.shape[2] // fc

    def chunk(ti, f, nv):
        return jnp.where(ti < nv[0], f, nf - 1)

    wspecs = [pl.BlockSpec((1, d, fc), lambda ti, f, te, nv, sr, rw: (te[ti], 0, chunk(ti, f, nv))),
              pl.BlockSpec((1, d, fc), lambda ti, f, te, nv, sr, rw: (te[ti], 0, chunk(ti, f, nv))),
              pl.BlockSpec((1, fc, d), lambda ti, f, te, nv, sr, rw: (te[ti], chunk(ti, f, nv), 0))]
    scratch = []
    if grouped:
        in_specs = [pl.BlockSpec(memory_space=pl.ANY)] + wspecs
        args = [x, wg, wu, wd]
        scratch += [pltpu.VMEM((tm, d), F32), pltpu.VMEM((tm, d), BF16), pltpu.VMEM((d, fc), BF16),
                    pltpu.VMEM((d, fc), BF16), pltpu.VMEM((fc, d), BF16), pltpu.SemaphoreType.DMA(())]
    else:
        in_specs = [pl.BlockSpec((tm, d), lambda ti, f, te, nv, sr, rw: (ti, 0))] + wspecs
        args = [x, wg, wu, wd]
        src = tile_rows = jnp.zeros((1,), jnp.int32)
    grid_spec = pltpu.PrefetchScalarGridSpec(
        num_scalar_prefetch=4,
        grid=(p // tm, nf),
        in_specs=in_specs,
        out_specs=pl.BlockSpec((tm, d), lambda ti, f, te, nv, sr, rw: (ti, 0)),
        scratch_shapes=scratch,
    )
    return pl.pallas_call(
        functools.partial(_ffn_kernel, grouped=grouped, tm=tm, nf=nf),
        grid_spec=grid_spec,
        out_shape=jax.ShapeDtypeStruct((p, d), F32),
        compiler_params=_cparams(("arbitrary", "arbitrary")),
        name="grouped_swiglu" if grouped else "dense_swiglu",
    )(tile_expert, n_valid, src, tile_rows, *args)


def _combine_kernel(pos_ref, x_ref, info_ref, y_ref, g_ref, o_ref, buf, sem, *, tt, n, norm):
    i = pl.program_id(0)

    def issue_tile(tile, slot):
        def issue(r, _):
            for k in range(2):
                _row_copy(y_ref, pos_ref[k * n + tile * tt + r], buf.at[slot, k], r, sem.at[slot]).start()
            return 0
        lax.fori_loop(0, tt, issue, 0, unroll=ISSUE_UNROLL)

    @pl.when(i == 0)
    def _():
        issue_tile(0, 0)

    @pl.when(i + 1 < pl.num_programs(0))
    def _():
        issue_tile(i + 1, (i + 1) % 2)

    slot = i % 2

    def wait(r, _):
        for k in range(2):
            _row_copy(y_ref, 0, buf.at[slot, k], r, sem.at[slot]).wait()
        return 0

    lax.fori_loop(0, tt, wait, 0)
    info = info_ref[...]
    o = x_ref[...] + (info[:, 0:1] * buf[slot, 0] + info[:, 1:2] * buf[slot, 1])
    if norm:
        o = o * lax.rsqrt(jnp.mean(o * o, axis=-1, keepdims=True) + EPS) * g_ref[...]
    o_ref[...] = o


def _combine(x2, info, y, pos, tt, gain=None):
    n, d = x2.shape
    norm = gain is not None
    grid_spec = pltpu.PrefetchScalarGridSpec(
        num_scalar_prefetch=1,
        grid=(n // tt,),
        in_specs=[pl.BlockSpec((tt, d), lambda i, ps: (i, 0)), pl.BlockSpec((tt, LANES), lambda i, ps: (i, 0)),
                  pl.BlockSpec(memory_space=pl.ANY), pl.BlockSpec((1, d), lambda i, ps: (0, 0))],
        out_specs=pl.BlockSpec((tt, d), lambda i, ps: (i, 0)),
        scratch_shapes=[pltpu.VMEM((2, 2, tt, d), F32), pltpu.SemaphoreType.DMA((2,))],
    )
    return pl.pallas_call(
        functools.partial(_combine_kernel, tt=tt, n=n, norm=norm),
        grid_spec=grid_spec,
        out_shape=jax.ShapeDtypeStruct((n, d), F32),
        compiler_params=_cparams(("arbitrary",)),
        name="moe_combine",
    )(pos, x2, info, y, (gain if norm else jnp.ones((d,), F32)).reshape(1, d))


def _moe(x2, h, info, wg, wu, wd, tm, fc, out_gain=None):
    n = h.shape[0]
    e = wg.shape[0]
    eid = jnp.concatenate([info[:, 2], info[:, 3]]).astype(jnp.int32)
    tok = jnp.tile(jnp.arange(n, dtype=jnp.int32), 2)
    onehot = (eid[:, None] == jnp.arange(e, dtype=jnp.int32)[None, :]).astype(jnp.int32)
    rank = jnp.sum((jnp.cumsum(onehot, axis=0) - 1) * onehot, axis=1)
    counts = jnp.sum(onehot, axis=0)
    padded = ((counts + tm - 1) // tm) * tm
    ends = jnp.cumsum(padded)
    starts = ends - padded
    pos = starts[eid] + rank
    n_tiles = (2 * n) // tm + e
    p = n_tiles * tm
    src = jnp.zeros((p,), jnp.int32).at[pos].set(tok)
    n_valid = (ends[e - 1] // tm).astype(jnp.int32)
    tile_start = jnp.arange(n_tiles, dtype=jnp.int32) * tm
    tile_start = jnp.minimum(tile_start, (n_valid - 1) * tm)
    tile_expert = jnp.sum((tile_start[:, None] >= ends[None, :]).astype(jnp.int32), axis=1)
    tile_rows = jnp.clip((starts + counts)[tile_expert] - tile_start, 0, tm).astype(jnp.int32)
    y = _ffn(h, wg, wu, wd, tile_expert, n_valid.reshape(1), tm, fc, src=src, tile_rows=tile_rows)
    return _combine(x2, info, y, pos, _tile(n, 256), out_gain)


def _tile(total, want):
    t = min(total, want)
    assert total % t == 0
    return t


def kernel(x, w_in, b_forget, q_norm, k_norm, kv_norm, w_ukv, w_br_fox, w_br_sb, w_br_dsa, w_out, rel_bias, norm_mix, norm_ffn, w_ffn_gate, w_ffn_up, w_ffn_down, w_router, w_moe_gate, w_moe_up, w_moe_down, norm_final):
    b, s, d = x.shape
    m = b * s
    depth = w_in.shape[0]
    ta = _tile(s, 256)
    tm_norm = _tile(m, 512)
    tm = _tile(m, 1024)
    tn = 512
    fc = 256
    tm_moe = _tile(2 * m, 1024)

    bias = _bias_tiles(rel_bias, ta)
    x2 = x.reshape(m, d)
    pending = None
    for l in range(depth):
        if pending is None:
            h = _rmsnorm(x2, norm_mix[l], BF16, tm_norm)
        else:
            x2, h = _add_rmsnorm(x2, pending, norm_mix[l], tm_norm)
            pending = None
        z, zs = _inproj(h, _relayout_w_in(w_in, l), tm, 2 * tn)
        z3 = z.reshape(b, s, Z_COLS)
        zs3 = zs.reshape(b, s, LANES)
        qn, kn, aq, ak, dk, fvt, svt, dvt = _prep(z3, zs3, q_norm[l], k_norm[l], kv_norm[l], b_forget[l],
                                                  w_ukv[l], ta)
        y_fox = _fox(qn, kn, fvt, aq, ak, z3, ta).reshape(m, W_ATT)
        y_sb = _sb(z3, svt, ta).reshape(m, W_ATT)
        y_dsa = _dsa(z3, zs3, dk, dvt, bias, ta).reshape(m, W_ATT)
        x2 = _mix_out(y_fox, y_sb, y_dsa, w_br_fox[l].astype(BF16), w_br_sb[l].astype(BF16),
                      w_br_dsa[l].astype(BF16), z, w_out[l].astype(BF16), x2, tm_norm, 2 * tn, 2 * tn)
        j = l // 2
        if l % 2 == 0:
            h = _rmsnorm(x2, norm_ffn[l], BF16, tm_norm)
            n_tiles = m // tm
            pending = _ffn(h, w_ffn_gate[j][None], w_ffn_up[j][None], w_ffn_down[j][None],
                           jnp.zeros((n_tiles,), jnp.int32), jnp.full((1,), n_tiles, jnp.int32), tm, fc)
        else:
            h, info = _rmsnorm_router(x2, norm_ffn[l], w_router[j], tm_norm)
            last = l == depth - 1
            x2 = _moe(x2, h, info, w_moe_gate[j], w_moe_up[j], w_moe_down[j], tm_moe, fc,
                      norm_final if last else None)
    if pending is not None:
        x2 = _rmsnorm(x2 + pending, norm_final, F32, tm_norm)
    return x2.reshape(b, s, d)
```

```python
import functools
import math

import jax
import jax.numpy as jnp
import numpy as np
from jax import lax
from jax.experimental import pallas as pl
from jax.experimental.pallas import tpu as pltpu

F32 = jnp.float32
BF16 = jnp.bfloat16

HEAD_DIM = 64
N_HEADS = 8
N_PAIRS = N_HEADS // 2
W_ATT = N_HEADS * HEAD_DIM
KV_RANK = 256
IDX_HEADS = 4
IDX_DIM = 64
TOPK_MAX = 256
N_BUCKETS = 32
MAX_DISTANCE = 128
N_EXPERTS = 8
N_BRANCH = 3
EPS = 1e-6
LANES = 128
NEG = -1e30
SB_UNDERFLOW = -104.0
ISSUE_UNROLL = 8
SUB_ROWS = 256

U_FQ, U_FK, U_FV, U_FO, U_SQ, U_SK, U_SV, U_DQ = 0, 4, 8, 12, 16, 20, 24, 28
U_DLAT, U_DIQ, U_DIK, U_SMALL, U_GATES = 32, 34, 36, 37, 40
N_UNITS = 88
Z_COLS = N_UNITS * LANES
SM_FF, SM_DIW = 0, 8

VMEM_LIMIT = 56 * 1024 * 1024


def _cparams(sem):
    return pltpu.CompilerParams(dimension_semantics=sem, vmem_limit_bytes=VMEM_LIMIT)


def _log_sigmoid(x):
    return jnp.minimum(x, 0.0) - jnp.log1p(jnp.exp(-jnp.abs(x)))


def _rmsnorm_kernel(x_ref, g_ref, o_ref):
    x = x_ref[...]
    ms = jnp.mean(x * x, axis=-1, keepdims=True)
    o_ref[...] = (x * lax.rsqrt(ms + EPS) * g_ref[...]).astype(o_ref.dtype)


def _rmsnorm(x, g, out_dtype, tm):
    m, d = x.shape
    return pl.pallas_call(
        _rmsnorm_kernel,
        grid=(m // tm,),
        in_specs=[pl.BlockSpec((tm, d), lambda i: (i, 0)), pl.BlockSpec((1, d), lambda i: (0, 0))],
        out_specs=pl.BlockSpec((tm, d), lambda i: (i, 0)),
        out_shape=jax.ShapeDtypeStruct((m, d), out_dtype),
        compiler_params=_cparams(("parallel",)),
        name="rmsnorm",
    )(x, g.reshape(1, d))


def _add_rmsnorm_kernel(x_ref, y_ref, g_ref, xo_ref, h_ref):
    x = x_ref[...] + y_ref[...]
    xo_ref[...] = x
    ms = jnp.mean(x * x, axis=-1, keepdims=True)
    h_ref[...] = (x * lax.rsqrt(ms + EPS) * g_ref[...]).astype(h_ref.dtype)


def _add_rmsnorm(x, y, g, tm):
    m, d = x.shape
    blk = pl.BlockSpec((tm, d), lambda i: (i, 0))
    return pl.pallas_call(
        _add_rmsnorm_kernel,
        grid=(m // tm,),
        in_specs=[blk, blk, pl.BlockSpec((1, d), lambda i: (0, 0))],
        out_specs=[blk, blk],
        out_shape=[jax.ShapeDtypeStruct((m, d), F32), jax.ShapeDtypeStruct((m, d), BF16)],
        compiler_params=_cparams(("parallel",)),
        name="add_rmsnorm",
    )(x, y, g.reshape(1, d))


def _rmsnorm_router_kernel(x_ref, g_ref, wr_ref, o_ref, info_ref):
    x = x_ref[...]
    ms = jnp.mean(x * x, axis=-1, keepdims=True)
    h = x * lax.rsqrt(ms + EPS) * g_ref[...]
    o_ref[...] = h.astype(o_ref.dtype)
    logits = jnp.dot(h, wr_ref[...], precision=lax.Precision.HIGHEST, preferred_element_type=F32)
    lane = lax.broadcasted_iota(jnp.int32, logits.shape, 1).astype(F32)
    lg = jnp.where(lane < N_EXPERTS, logits, -jnp.inf)
    v1 = jnp.max(lg, axis=-1, keepdims=True)
    i1 = jnp.min(jnp.where(lg == v1, lane, float(LANES)), axis=-1, keepdims=True)
    lg2 = jnp.where(lane == i1, -jnp.inf, lg)
    v2 = jnp.max(lg2, axis=-1, keepdims=True)
    i2 = jnp.min(jnp.where(lg2 == v2, lane, float(LANES)), axis=-1, keepdims=True)
    e2 = jnp.exp(v2 - v1)
    w1 = 1.0 / (1.0 + e2)
    w2 = e2 / (1.0 + e2)
    info = jnp.where(lane == 0.0, w1, jnp.where(lane == 1.0, w2, jnp.where(
        lane == 2.0, i1, jnp.where(lane == 3.0, i2, 0.0))))
    info_ref[...] = info


def _rmsnorm_router(x, g, w_router, tm):
    m, d = x.shape
    wr = jnp.pad(w_router, ((0, 0), (0, LANES - w_router.shape[1])))
    return pl.pallas_call(
        _rmsnorm_router_kernel,
        grid=(m // tm,),
        in_specs=[pl.BlockSpec((tm, d), lambda i: (i, 0)), pl.BlockSpec((1, d), lambda i: (0, 0)),
                  pl.BlockSpec((d, LANES), lambda i: (0, 0))],
        out_specs=[pl.BlockSpec((tm, d), lambda i: (i, 0)), pl.BlockSpec((tm, LANES), lambda i: (i, 0))],
        out_shape=[jax.ShapeDtypeStruct((m, d), F32), jax.ShapeDtypeStruct((m, LANES), F32)],
        compiler_params=_cparams(("parallel",)),
        name="rmsnorm_router",
    )(x, g.reshape(1, d), wr)


def _inproj_kernel(a_ref, w_ref, z_ref, zs_ref, *, small_tile):
    acc = jnp.dot(a_ref[...], w_ref[...], preferred_element_type=F32)
    z_ref[...] = acc.astype(z_ref.dtype)

    @pl.when(pl.program_id(1) == small_tile)
    def _():
        off = (U_SMALL * LANES) % acc.shape[1]
        zs_ref[...] = acc[:, off:off + LANES]


def _inproj(h, w, tm, tn):
    m, d = h.shape
    return pl.pallas_call(
        functools.partial(_inproj_kernel, small_tile=(U_SMALL * LANES) // tn),
        grid=(m // tm, Z_COLS // tn),
        in_specs=[pl.BlockSpec((tm, d), lambda i, j: (i, 0)), pl.BlockSpec((d, tn), lambda i, j: (0, j))],
        out_specs=[pl.BlockSpec((tm, tn), lambda i, j: (i, j)), pl.BlockSpec((tm, LANES), lambda i, j: (i, 0))],
        out_shape=[jax.ShapeDtypeStruct((m, Z_COLS), BF16), jax.ShapeDtypeStruct((m, LANES), F32)],
        compiler_params=_cparams(("parallel", "arbitrary")),
        name="inproj",
    )(h, w)


W_IN_ORDER = (("fq", W_ATT), ("fk", W_ATT), ("fv", W_ATT), ("ff", N_HEADS), ("fo", W_ATT),
              ("sq", W_ATT), ("sk", W_ATT), ("sv", W_ATT), ("dq", W_ATT), ("dlat", KV_RANK),
              ("diq", IDX_HEADS * IDX_DIM), ("dik", IDX_DIM), ("diw", IDX_HEADS))
RELAYOUT_COLS = 4 * LANES


def _relayout_kernel(tbl_ref, w_ref, ff_ref, o_ref, *, layer, special):
    j = pl.program_id(0)

    @pl.when(j != special)
    def _():
        o_ref[...] = jnp.transpose(w_ref[:, layer, :]).astype(o_ref.dtype)

    @pl.when(j == special)
    def _():
        xt = jnp.transpose(w_ref[0:LANES, layer, :])
        fft = jnp.transpose(ff_ref[:, layer, :])
        lane = lax.broadcasted_iota(jnp.int32, xt.shape, 1)
        dik2 = jnp.where(lane < IDX_DIM, xt, pltpu.roll(xt, IDX_DIM, axis=1))
        diw = pltpu.roll(xt, LANES - IDX_DIM + SM_DIW, axis=1)
        small = jnp.where(lane < N_HEADS, fft,
                          jnp.where((lane >= SM_DIW) & (lane < SM_DIW + IDX_HEADS), diw, 0.0))
        o_ref[:, 0:LANES] = dik2.astype(o_ref.dtype)
        o_ref[:, LANES:2 * LANES] = small.astype(o_ref.dtype)
        o_ref[:, 2 * LANES:] = jnp.zeros((o_ref.shape[0], o_ref.shape[1] - 2 * LANES), o_ref.dtype)


def _relayout_w_in(w_in, layer):
    _, d, n_in = w_in.shape
    src, o = {}, 0
    for name, width in W_IN_ORDER + (("gates", N_BRANCH * d),):
        src[name] = o
        o += width
    assert o == n_in and U_GATES * LANES + N_BRANCH * d == Z_COLS and SM_FF == 0
    assert src["diq"] == src["dlat"] + KV_RANK and src["diw"] == src["dik"] + IDX_DIM
    units = {U_FQ: "fq", U_FK: "fk", U_FV: "fv", U_FO: "fo", U_SQ: "sq", U_SK: "sk", U_SV: "sv",
             U_DQ: "dq", U_DLAT: "dlat", U_DIK: "dik"}
    per = RELAYOUT_COLS // LANES
    tbl = [src[units[u]] for u in range(0, U_GATES, per)]
    tbl += [src["gates"] + k * RELAYOUT_COLS for k in range(N_BRANCH * d // RELAYOUT_COLS)]
    elem = lambda rows: (pl.Element(rows), pl.Element(w_in.shape[0]), pl.Element(d))
    grid_spec = pltpu.PrefetchScalarGridSpec(
        num_scalar_prefetch=1,
        grid=(Z_COLS // RELAYOUT_COLS,),
        in_specs=[pl.BlockSpec(elem(RELAYOUT_COLS), lambda j, tbl: (tbl[j], 0, 0)),
                  pl.BlockSpec(elem(LANES), lambda j, tbl: (src["ff"], 0, 0))],
        out_specs=pl.BlockSpec((d, RELAYOUT_COLS), lambda j, tbl: (0, j)),
    )
    wt = jnp.transpose(w_in, (2, 0, 1))
    return pl.pallas_call(
        functools.partial(_relayout_kernel, layer=layer, special=U_DIK // per),
        grid_spec=grid_spec,
        out_shape=jax.ShapeDtypeStruct((d, Z_COLS), BF16),
        compiler_params=_cparams(("arbitrary",)),
        name="w_in_relayout",
    )(jnp.asarray(tbl, jnp.int32), wt, wt)


def _prep_kernel(zq_ref, zk_ref, zl_ref, zfv_ref, zsv_ref, zs_ref, gq_ref, gk_ref, gkv_ref, bf_ref, wukv_ref,
                 grp_ref, eq_ref, ek_ref, oneq_ref, onek_ref,
                 qn_ref, kn_ref, aq_ref, ak_ref, dk_ref, fvt_ref, svt_ref, dvt_ref, carry_ref, *, ts):
    def value_tile(v):
        return jnp.transpose(v).reshape(N_PAIRS, LANES, ts).astype(BF16)

    fvt_ref[0, :, 0] = value_tile(zfv_ref[0].astype(F32))
    svt_ref[0, :, 0] = value_tile(zsv_ref[0].astype(F32))

    @pl.when(pl.program_id(1) == 0)
    def _():
        carry_ref[...] = jnp.zeros_like(carry_ref)

    def head_norm(z_ref, g_ref):
        x = z_ref[0].astype(F32)
        sq = x * x
        hi = sq.astype(BF16)
        lo = (sq - hi.astype(F32)).astype(BF16)
        ms = (jnp.dot(hi, grp_ref[...], preferred_element_type=F32)
              + jnp.dot(lo, grp_ref[...], preferred_element_type=F32)) * (1.0 / HEAD_DIM)
        return x * lax.rsqrt(ms + EPS) * g_ref[...]

    qn_ref[0] = head_norm(zq_ref, gq_ref).astype(BF16)
    kn_ref[0] = head_norm(zk_ref, gk_ref).astype(BF16)

    lane = lax.broadcasted_iota(jnp.int32, (ts, LANES), 1)
    lf = jnp.where(lane < N_HEADS, _log_sigmoid(zs_ref[0] + bf_ref[...]), 0.0)
    r = lax.broadcasted_iota(jnp.int32, (ts, ts), 0)
    c_ = lax.broadcasted_iota(jnp.int32, (ts, ts), 1)
    tri = (c_ <= r).astype(F32)
    c = jnp.dot(tri, lf, precision=lax.Precision.HIGHEST, preferred_element_type=F32) + carry_ref[...]
    carry_ref[...] = c[ts - 1:ts, :]
    c0 = c.astype(BF16)
    r1 = c - c0.astype(F32)
    c1 = r1.astype(BF16)
    c2 = (r1 - c1.astype(F32)).astype(BF16)
    pieces = (c0, c1, c2)
    aq = oneq_ref[...]
    ak = onek_ref[...]
    for k in range(3):
        aq = aq + jnp.dot(pieces[k], eq_ref[k], preferred_element_type=F32)
        ak = ak - jnp.dot(pieces[k], ek_ref[k], preferred_element_type=F32)
    aq_ref[0] = aq.astype(BF16)
    ak_ref[0] = ak.astype(BF16)

    lat = zl_ref[0, :, :KV_RANK].astype(F32)
    msl = jnp.mean(lat * lat, axis=-1, keepdims=True)
    latn = (lat * lax.rsqrt(msl + EPS) * gkv_ref[...]).astype(BF16)
    kv = jnp.dot(latn, wukv_ref[...], preferred_element_type=F32)
    dk_ref[0] = kv[:, :W_ATT].astype(BF16)
    dvt_ref[0, :, 0] = value_tile(kv[:, W_ATT:].astype(BF16).astype(F32))


def _aug_constants():
    eq = np.zeros((3, LANES, LANES), np.float32)
    ek = np.zeros((3, LANES, LANES), np.float32)
    oneq = np.zeros((1, LANES), np.float32)
    onek = np.zeros((1, LANES), np.float32)
    for h in range(N_HEADS):
        for k in range(3):
            eq[k, h, 8 * h + k] = 1.0
            ek[k, h, 8 * h + 3 + k] = 1.0
            oneq[0, 8 * h + 3 + k] = 1.0
            onek[0, 8 * h + k] = 1.0
    grp = np.kron(np.eye(N_HEADS, dtype=np.float32), np.ones((HEAD_DIM, HEAD_DIM), np.float32))
    return (jnp.asarray(grp, BF16), jnp.asarray(eq, BF16), jnp.asarray(ek, BF16),
            jnp.asarray(oneq), jnp.asarray(onek))


def _prep(z3, zs3, q_norm, k_norm, kv_norm, b_forget, w_ukv, ts):
    b, s, _ = z3.shape
    grp, eq, ek, oneq, onek = _aug_constants()
    gq = (jnp.tile(q_norm, N_HEADS) * HEAD_DIM ** -0.5).reshape(1, W_ATT)
    gk = jnp.tile(k_norm, N_HEADS).reshape(1, W_ATT)
    bf = jnp.pad(b_forget, (SM_FF, LANES - N_HEADS - SM_FF)).reshape(1, LANES)
    const = lambda shape: pl.BlockSpec(shape, lambda bi, si: (0,) * len(shape))
    zblk = lambda unit: pl.BlockSpec((1, ts, W_ATT), lambda bi, si: (bi, si, unit // 4))
    seq_out = lambda w: pl.BlockSpec((1, ts, w), lambda bi, si: (bi, si, 0))
    vt_out = pl.BlockSpec((1, N_PAIRS, 1, LANES, ts), lambda bi, si: (bi, 0, si, 0, 0))
    vt_shape = jax.ShapeDtypeStruct((b, N_PAIRS, s // ts, LANES, ts), BF16)
    return pl.pallas_call(
        functools.partial(_prep_kernel, ts=ts),
        grid=(b, s // ts),
        in_specs=[zblk(U_FQ), zblk(U_FK), zblk(U_DLAT), zblk(U_FV), zblk(U_SV),
                  pl.BlockSpec((1, ts, LANES), lambda bi, si: (bi, si, 0)),
                  const((1, W_ATT)), const((1, W_ATT)), const((1, KV_RANK)), const((1, LANES)),
                  const((KV_RANK, 2 * W_ATT)), const((W_ATT, W_ATT)),
                  const((3, LANES, LANES)), const((3, LANES, LANES)), const((1, LANES)), const((1, LANES))],
        out_specs=[seq_out(W_ATT), seq_out(W_ATT), seq_out(LANES), seq_out(LANES), seq_out(W_ATT),
                   vt_out, vt_out, vt_out],
        out_shape=[jax.ShapeDtypeStruct((b, s, W_ATT), BF16), jax.ShapeDtypeStruct((b, s, W_ATT), BF16),
                   jax.ShapeDtypeStruct((b, s, LANES), BF16), jax.ShapeDtypeStruct((b, s, LANES), BF16),
                   jax.ShapeDtypeStruct((b, s, W_ATT), BF16), vt_shape, vt_shape, vt_shape],
        scratch_shapes=[pltpu.VMEM((1, LANES), F32)],
        compiler_params=_cparams(("parallel", "arbitrary")),
        name="mixer_prep",
    )(z3, z3, z3, z3, z3, zs3, gq, gk, kv_norm.reshape(1, KV_RANK), bf, w_ukv.astype(BF16), grp, eq, ek, oneq, onek)


def _nt_dot(a, b):
    return lax.dot_general(a, b, (((1,), (1,)), ((), ())), preferred_element_type=F32)


def _half_mask(shape, half):
    lane = lax.broadcasted_iota(jnp.int32, shape, 1)
    return (lane >= HEAD_DIM * half) & (lane < HEAD_DIM * (half + 1))


def _pair_out(acc0, acc1):
    return jnp.transpose(jnp.concatenate([acc0, acc1], axis=0))


def _online_softmax_pair(last, scores, weighted_values, mask_last, t):
    def soft(s, m, l):
        m_new = jnp.maximum(m, jnp.max(s, axis=0, keepdims=True))
        alpha = jnp.exp(m - m_new)
        p = jnp.exp(s - m_new)
        return m_new, alpha * l + jnp.sum(p, axis=0, keepdims=True), alpha, p.astype(BF16)

    def drain(j, p, alpha, acc):
        return tuple(alpha[h] * acc[h] + weighted_values(j, h, p[h]) for h in range(2))

    def body(n, c):
        s, p_prev, a_prev, m, l, acc = c
        s_next = scores(n + 1)
        acc = drain(jnp.maximum(n - 1, 0), p_prev, a_prev, acc)
        r = [soft(s[h], m[h], l[h]) for h in range(2)]
        return (s_next, (r[0][3], r[1][3]), (r[0][2], r[1][2]), (r[0][0], r[1][0]), (r[0][1], r[1][1]), acc)

    two = lambda x: (x, x)
    init = (scores(0), two(jnp.zeros((t, t), BF16)), two(jnp.ones((1, t), F32)),
            two(jnp.full((1, t), NEG, F32)), two(jnp.zeros((1, t), F32)), two(jnp.zeros((HEAD_DIM, t), F32)))
    s, p_prev, a_prev, m, l, acc = lax.fori_loop(0, last, body, init)
    acc = drain(jnp.maximum(last - 1, 0), p_prev, a_prev, acc)
    if mask_last is not None:
        s = tuple(mask_last(x) for x in s)
    r = [soft(s[h], m[h], l[h]) for h in range(2)]
    acc = drain(last, (r[0][3], r[1][3]), (r[0][2], r[1][2]), acc)
    return acc[0] / r[0][1], acc[1] / r[1][1]


def _fox_kernel(q_ref, k_ref, vt_ref, aq_ref, ak_ref, fo_ref, o_ref, *, t):
    pair = pl.program_id(1)
    i = pl.program_id(2)
    q = q_ref[0].astype(F32)
    aq = aq_ref[0].astype(F32)
    lane = lax.broadcasted_iota(jnp.int32, (t, LANES), 1)
    krow = lax.broadcasted_iota(jnp.int32, (t, t), 0)
    qcol = lax.broadcasted_iota(jnp.int32, (t, t), 1)
    causal = krow <= qcol
    qcs = []
    for half in range(2):
        head = 2 * pair + half
        qm = jnp.where(_half_mask((t, LANES), half), q, 0.0).astype(BF16)
        am = jnp.where((lane >= 8 * head) & (lane < 8 * head + 6), aq, 0.0).astype(BF16)
        qcs.append(jnp.concatenate([qm, am], axis=1))

    def scores(j):
        ks = pl.multiple_of(j * t, t)
        kc = jnp.concatenate([k_ref[0, pl.ds(ks, t), :], ak_ref[0, pl.ds(ks, t), :]], axis=1)
        return tuple(_nt_dot(kc, qcs[half]) for half in range(2))

    def weighted_values(j, half, p):
        return jnp.dot(vt_ref[0, 0, j, HEAD_DIM * half:HEAD_DIM * (half + 1), :], p, preferred_element_type=F32)

    o0, o1 = _online_softmax_pair(i, scores, weighted_values, lambda s: jnp.where(causal, s, NEG), t)
    o = _pair_out(o0, o1)
    o_ref[0] = (o * jax.nn.sigmoid(fo_ref[0].astype(F32))).astype(o_ref.dtype)


def _fox(qn, kn, vt, aq, ak, z3, t):
    b, s, _ = qn.shape
    qblk = lambda unit: pl.BlockSpec((1, t, LANES), lambda bi, p, i: (bi, i, unit + p))
    return pl.pallas_call(
        functools.partial(_fox_kernel, t=t),
        grid=(b, N_PAIRS, s // t),
        in_specs=[qblk(0),
                  pl.BlockSpec((1, s, LANES), lambda bi, p, i: (bi, 0, p)),
                  pl.BlockSpec((1, 1, s // t, LANES, t), lambda bi, p, i: (bi, p, 0, 0, 0)),
                  pl.BlockSpec((1, t, LANES), lambda bi, p, i: (bi, i, 0)),
                  pl.BlockSpec((1, s, LANES), lambda bi, p, i: (bi, 0, 0)),
                  qblk(U_FO)],
        out_specs=qblk(0),
        out_shape=jax.ShapeDtypeStruct((b, s, W_ATT), BF16),
        compiler_params=_cparams(("parallel", "parallel", "arbitrary")),
        name="fox_attention",
    )(qn, kn, vt, aq, ak, z3)


def _sb_kernel(q_ref, k_ref, vt_ref, o_ref, *, t):
    i = pl.program_id(2)
    q = q_ref[0].astype(F32) * (HEAD_DIM ** -0.5)
    krow = lax.broadcasted_iota(jnp.int32, (t, t), 0)
    qcol = lax.broadcasted_iota(jnp.int32, (t, t), 1)
    strict = krow < qcol
    after = (qcol > krow).astype(BF16)
    qms = [jnp.where(_half_mask((t, LANES), half), q, 0.0).astype(BF16) for half in range(2)]

    def local(j, diag):
        ks = pl.multiple_of(j * t, t)
        k = k_ref[0, pl.ds(ks, t), :]
        out = []
        for half in range(2):
            z = _nt_dot(k, qms[half])
            lz = _log_sigmoid(z)
            l1m = lz - z
            if diag:
                l1m = jnp.where(strict, l1m, 0.0)
            hi = l1m.astype(BF16)
            lo = (l1m - hi.astype(F32)).astype(BF16)
            suffix = (jnp.dot(after, hi, preferred_element_type=F32)
                      + jnp.dot(after, lo, preferred_element_type=F32))
            out.append((lz + suffix, jnp.sum(l1m, axis=0, keepdims=True)))
        return out

    def finish(j, loc, carry, keep):
        new = []
        for half in range(2):
            logw, colsum = loc[half]
            rsum, acc = carry[half]
            a = jnp.exp(logw + rsum)
            if keep is not None:
                a = jnp.where(keep, a, 0.0)
            pv = jnp.dot(vt_ref[0, 0, j, HEAD_DIM * half:HEAD_DIM * (half + 1), :], a.astype(BF16),
                         preferred_element_type=F32)
            new.append((rsum + colsum, acc + pv))
        return tuple(new)

    prev = jnp.maximum(i - 1, 0)
    loc_diag = local(i, True)
    loc_prev = local(prev, False)
    zero = (jnp.zeros((1, t), F32), jnp.zeros((HEAD_DIM, t), F32))
    carry = finish(i, loc_diag, (zero, zero), strict)
    carry = finish(prev, loc_prev, carry, i > 0)

    def more(c):
        n, ((r0, _), (r1, _)) = c
        return (n < i) & (jnp.maximum(jnp.max(r0), jnp.max(r1)) > SB_UNDERFLOW)

    def body(c):
        n, carry = c
        j = i - 1 - n
        return n + 1, finish(j, local(j, False), carry, None)

    _, ((_, acc0), (_, acc1)) = lax.while_loop(more, body, (jnp.int32(1), carry))
    o_ref[0] = _pair_out(acc0, acc1).astype(o_ref.dtype)


def _sb(z3, vt, t):
    b, s, _ = z3.shape
    return pl.pallas_call(
        functools.partial(_sb_kernel, t=t),
        grid=(b, N_PAIRS, s // t),
        in_specs=[pl.BlockSpec((1, t, LANES), lambda bi, p, i: (bi, i, U_SQ + p)),
                  pl.BlockSpec((1, s, LANES), lambda bi, p, i: (bi, 0, U_SK + p)),
                  pl.BlockSpec((1, 1, s // t, LANES, t), lambda bi, p, i: (bi, p, 0, 0, 0))],
        out_specs=pl.BlockSpec((1, t, LANES), lambda bi, p, i: (bi, i, p)),
        out_shape=jax.ShapeDtypeStruct((b, s, W_ATT), BF16),
        compiler_params=_cparams(("parallel", "parallel", "arbitrary")),
        name="stickbreak_attention",
    )(z3, z3, vt)


def _t5_bucket(n):
    max_exact = N_BUCKETS // 2
    nf = jnp.maximum(n, 1).astype(F32)
    large = max_exact + (jnp.log(nf / max_exact) / math.log(MAX_DISTANCE / max_exact)
                         * (N_BUCKETS - max_exact)).astype(jnp.int32)
    large = jnp.minimum(large, N_BUCKETS - 1)
    return jnp.where(n < max_exact, n, large)


def _bias_tiles_kernel(relb_ref, o_ref, *, t):
    h = pl.program_id(0)
    krow = lax.broadcasted_iota(jnp.int32, (t, t), 0)
    qcol = lax.broadcasted_iota(jnp.int32, (t, t), 1)
    o_ref[0, 0] = jnp.full((t, t), relb_ref[N_BUCKETS - 1, h], F32)
    for slot, shift in ((1, t), (2, 0)):
        bucket = _t5_bucket(jnp.maximum(qcol - krow + shift, 0))
        val = jnp.full((t, t), relb_ref[0, h], F32)
        for k in range(1, N_BUCKETS):
            val = jnp.where(bucket == k, relb_ref[k, h], val)
        o_ref[0, slot] = val


def _bias_tiles(rel_bias, t):
    assert t >= MAX_DISTANCE
    return pl.pallas_call(
        functools.partial(_bias_tiles_kernel, t=t),
        grid=(N_HEADS,),
        in_specs=[pl.BlockSpec(memory_space=pltpu.SMEM)],
        out_specs=pl.BlockSpec((1, 3, t, t), lambda h: (h, 0, 0, 0)),
        out_shape=jax.ShapeDtypeStruct((N_HEADS, 3, t, t), F32),
        compiler_params=_cparams(("arbitrary",)),
        name="t5_bias_tiles",
    )(rel_bias)


def _dsa_kernel(dq_ref, qi_ref, zs_ref, kidx_ref, dk_ref, dvt_ref, bias_ref, o_ref, key_ref, hi_ref, lo_ref,
                madd_ref, *, t, n_sel):
    i = pl.program_id(1)
    nch = i + 1
    krow = lax.broadcasted_iota(jnp.int32, (t, t), 0)
    qcol = lax.broadcasted_iota(jnp.int32, (t, t), 1)
    idx_scale = (IDX_DIM ** -0.5) * (IDX_HEADS ** -0.5)

    zst = jnp.transpose(zs_ref[0])
    qi = qi_ref[0].astype(F32)
    qih, wih = [], []
    for h in range(IDX_HEADS):
        blk = qi[:, (h // 2) * LANES:(h // 2 + 1) * LANES]
        qih.append(jnp.where(_half_mask((t, LANES), h % 2), blk, 0.0).astype(BF16))
        wih.append(zst[SM_DIW + h:SM_DIW + h + 1, :] * idx_scale)

    def score_chunk(j, _):
        ks = pl.multiple_of(j * t, t)
        kc = kidx_ref[0, pl.ds(ks, t), :]
        sc = jnp.zeros((t, t), F32)
        for h in range(IDX_HEADS):
            sc = sc + jnp.maximum(_nt_dot(kc, qih[h]), 0.0) * wih[h]
        sc = jnp.where(sc == 0.0, 0.0, sc)
        sc = jnp.where(j * t + krow <= i * t + qcol, sc, -jnp.inf)
        bits = pltpu.bitcast(sc, jnp.int32)
        key = jnp.where(bits < 0, bits ^ jnp.int32(0x7FFFFFFF), bits)
        key_ref[j] = key
        hi_ref[j] = (key >> 16).astype(jnp.int16)
        return 0

    lax.fori_loop(0, nch, score_chunk, 0)

    i16 = jnp.int16
    rows16 = 16
    lowest = jnp.full((t, t), -32768, i16)
    npairs = (nch + 1) // 2

    @pl.when(nch % 2 == 1)
    def _():
        hi_ref[nch] = lowest
        lo_ref[nch] = lowest

    def count16(ref, pred):
        def body(j2, c):
            pieces = []
            for j in (2 * j2, 2 * j2 + 1):
                ind = jnp.where(pred(ref[j]), i16(1), i16(0))
                pieces += [ind[r * rows16:(r + 1) * rows16] for r in range(t // rows16)]
            while len(pieces) > 1:
                pieces = [a + b for a, b in zip(pieces[0::2], pieces[1::2])]
            return c + pieces[0]
        per_lane = lax.fori_loop(0, npairs, body, jnp.zeros((rows16, t), i16))
        return jnp.sum(per_lane.astype(F32), axis=0, keepdims=True)

    def search16(ref, base):
        def bit_step(n, thr):
            cand = thr + lax.shift_left(jnp.int32(1), 15 - n)
            c16 = cand.astype(i16)
            cnt = base + count16(ref, lambda v: v >= c16)
            return jnp.where(cnt >= float(n_sel), cand, thr)
        return lax.fori_loop(0, 16, bit_step, jnp.full((1, t), -32768, jnp.int32))

    thr_hi = search16(hi_ref, 0.0)
    thr_hi16 = thr_hi.astype(i16)
    above = count16(hi_ref, lambda v: v > thr_hi16)

    def low_halves(j, _):
        low = ((key_ref[j] & 0xFFFF) - 32768).astype(i16)
        lo_ref[j] = jnp.where(hi_ref[j] == thr_hi16, low, i16(-32768))
        return 0

    lax.fori_loop(0, nch, low_halves, 0)
    thr_lo = search16(lo_ref, above)
    thr = lax.shift_left(thr_hi, 16) | (thr_lo + 32768)

    def count_keys(pred):
        def body(j, c):
            return c + jnp.sum(jnp.where(pred(key_ref[j]), 1.0, 0.0), axis=0, keepdims=True)
        return lax.fori_loop(0, nch, body, jnp.zeros((1, t), F32))

    need = float(n_sel) - count_keys(lambda k: k > thr)

    upto = (qcol <= krow).astype(BF16)

    def mask_chunk(j, seen):
        key = key_ref[j]
        eq = key == thr
        rank = jnp.dot(upto, jnp.where(eq, 1.0, 0.0).astype(BF16), preferred_element_type=F32) + seen
        sel = (key > thr) | (eq & (rank <= need))
        sel = sel & (j * t + krow <= i * t + qcol)
        madd_ref[j] = jnp.where(sel, 0.0, NEG)
        return rank[t - 1:t, :]

    lax.fori_loop(0, nch, mask_chunk, jnp.zeros((1, t), F32))

    for pair in range(N_PAIRS):
        lo, hi_ = pair * LANES, (pair + 1) * LANES
        qp = dq_ref[0, :, lo:hi_].astype(F32) * (HEAD_DIM ** -0.5)
        qms = [jnp.where(_half_mask((t, LANES), half), qp, 0.0).astype(BF16) for half in range(2)]

        def scores(j, pair=pair, lo=lo, hi_=hi_, qms=qms):
            ks = pl.multiple_of(j * t, t)
            slot = jnp.clip(j - i + 2, 0, 2)
            k = dk_ref[0, pl.ds(ks, t), lo:hi_]
            madd = madd_ref[j]
            return tuple(_nt_dot(k, qms[half]) + bias_ref[2 * pair + half, slot] + madd for half in range(2))

        def weighted_values(j, half, p, pair=pair):
            return jnp.dot(dvt_ref[0, pair, j, HEAD_DIM * half:HEAD_DIM * (half + 1), :], p,
                           preferred_element_type=F32)

        o0, o1 = _online_softmax_pair(i, scores, weighted_values, None, t)
        o_ref[0, :, lo:hi_] = _pair_out(o0, o1).astype(o_ref.dtype)


def _dsa(z3, zs3, dk, dvt, bias, t):
    b, s, _ = z3.shape
    n_sel = min(TOPK_MAX, s // 4)
    nt = s // t
    return pl.pallas_call(
        functools.partial(_dsa_kernel, t=t, n_sel=n_sel),
        grid=(b, s // t),
        in_specs=[pl.BlockSpec((1, t, W_ATT), lambda bi, i: (bi, i, U_DQ // 4)),
                  pl.BlockSpec((1, t, 2 * LANES), lambda bi, i: (bi, i, U_DIQ // 2)),
                  pl.BlockSpec((1, t, LANES), lambda bi, i: (bi, i, 0)),
                  pl.BlockSpec((1, s, LANES), lambda bi, i: (bi, 0, U_DIK)),
                  pl.BlockSpec((1, s, W_ATT), lambda bi, i: (bi, 0, 0)),
                  pl.BlockSpec((1, N_PAIRS, s // t, LANES, t), lambda bi, i: (bi, 0, 0, 0, 0)),
                  pl.BlockSpec((N_HEADS, 3, t, t), lambda bi, i: (0, 0, 0, 0))],
        out_specs=pl.BlockSpec((1, t, W_ATT), lambda bi, i: (bi, i, 0)),
        out_shape=jax.ShapeDtypeStruct((b, s, W_ATT), BF16),
        scratch_shapes=[pltpu.VMEM((nt, t, t), jnp.int32), pltpu.VMEM((nt + nt % 2, t, t), jnp.int16),
                        pltpu.VMEM((nt + nt % 2, t, t), jnp.int16), pltpu.VMEM((nt, t, t), F32)],
        compiler_params=_cparams(("parallel", "arbitrary")),
        name="dsa_attention",
    )(z3, z3, zs3, z3, dk, dvt, bias)


def _merge_kernel(yf_ref, ys_ref, yd_ref, wf_ref, ws_ref, wd_ref, g0_ref, g1_ref, g2_ref, o_ref):
    def branch(y_ref, w_ref, g_ref):
        proj = jnp.dot(y_ref[...], w_ref[...].astype(BF16), preferred_element_type=F32)
        gate = 0.5 * jnp.tanh(0.5 * g_ref[...].astype(F32)) + 0.5
        return gate * proj

    o_ref[...] = (branch(yf_ref, wf_ref, g0_ref) + branch(ys_ref, ws_ref, g1_ref)
                  + branch(yd_ref, wd_ref, g2_ref)).astype(o_ref.dtype)


def _merge(yf, ys, yd, wf, ws, wd, z, d, tm, tn):
    m = yf.shape[0]
    yblk = pl.BlockSpec((tm, W_ATT), lambda i, j: (i, 0))
    wblk = pl.BlockSpec((W_ATT, tn), lambda i, j: (0, j))
    gblk = lambda g: pl.BlockSpec((tm, tn), lambda i, j: (i, (U_GATES * LANES + g * d) // tn + j))
    return pl.pallas_call(
        _merge_kernel,
        grid=(m // tm, d // tn),
        in_specs=[yblk, yblk, yblk, wblk, wblk, wblk, gblk(0), gblk(1), gblk(2)],
        out_specs=pl.BlockSpec((tm, tn), lambda i, j: (i, j)),
        out_shape=jax.ShapeDtypeStruct((m, d), BF16),
        compiler_params=_cparams(("parallel", "arbitrary")),
        name="branch_merge",
    )(yf, ys, yd, wf, ws, wd, z, z, z)


def _matmul_res_kernel(a_ref, w_ref, r_ref, o_ref):
    o_ref[...] = r_ref[...] + jnp.dot(a_ref[...], w_ref[...].astype(BF16), preferred_element_type=F32)


def _matmul_res(a, w, res, tm, tn):
    m, k = a.shape
    n = w.shape[1]
    return pl.pallas_call(
        _matmul_res_kernel,
        grid=(m // tm, n // tn),
        in_specs=[pl.BlockSpec((tm, k), lambda i, j: (i, 0)), pl.BlockSpec((k, tn), lambda i, j: (0, j)),
                  pl.BlockSpec((tm, tn), lambda i, j: (i, j))],
        out_specs=pl.BlockSpec((tm, tn), lambda i, j: (i, j)),
        out_shape=jax.ShapeDtypeStruct((m, n), F32),
        compiler_params=_cparams(("parallel", "arbitrary")),
        name="out_proj_residual",
    )(a, w, res)


def _row_copy(src_hbm, row, dst_vmem, slot, sem):
    return pltpu.make_async_copy(src_hbm.at[pl.ds(row, 1)], dst_vmem.at[pl.ds(slot, 1)], sem)


def _ffn_kernel(te_ref, nv_ref, src_ref, rows_ref, x_ref, wg_ref, wu_ref, wd_ref, *rest, grouped, tm, nf):
    ti = pl.program_id(0)
    f = pl.program_id(1)
    if grouped:
        o_ref, gbuf, xs, wgb, wub, wdb, sem = rest
    else:
        (o_ref,) = rest

    @pl.when(f == 0)
    def _():
        o_ref[...] = jnp.zeros_like(o_ref)

    if grouped:
        steps = max(k for k in range(1, nf) if tm % k == 0)
        per = tm // steps

        def issue_rows(tile, lo, n):
            def issue(r, _):
                _row_copy(x_ref, src_ref[tile * tm + lo + r], gbuf, lo + r, sem).start()
                return 0
            lax.fori_loop(0, n, issue, 0, unroll=ISSUE_UNROLL)

        @pl.when((ti == 0) & (f == 0))
        def _():
            issue_rows(0, 0, tm)

        @pl.when((f == 0) & (ti < nv_ref[0]))
        def _():
            def wait(r, _):
                _row_copy(x_ref, 0, gbuf, r, sem).wait()
                return 0
            lax.fori_loop(0, tm, wait, 0, unroll=ISSUE_UNROLL)
            xs[...] = gbuf[...].astype(BF16)

        @pl.when((f > 0) & (f <= steps) & (ti + 1 < nv_ref[0]))
        def _():
            issue_rows(ti + 1, (f - 1) * per, per)

    def swiglu(x, wgate, wup, wdown):
        g = jnp.dot(x, wgate, preferred_element_type=F32)
        u = jnp.dot(x, wup, preferred_element_type=F32)
        a = (g * jax.nn.sigmoid(g) * u).astype(BF16)
        return jnp.dot(a, wdown, preferred_element_type=F32)

    def whole_tile(x):
        o_ref[...] += swiglu(x, wg_ref[0].astype(BF16), wu_ref[0].astype(BF16), wd_ref[0].astype(BF16))

    @pl.when(ti < nv_ref[0])
    def _():
        if not grouped:
            whole_tile(x_ref[...])
            return
        rows = rows_ref[ti]

        @pl.when(rows > tm - SUB_ROWS)
        def _():
            whole_tile(xs[...])

        @pl.when(rows <= tm - SUB_ROWS)
        def _():
            wgb[...] = wg_ref[0].astype(BF16)
            wub[...] = wu_ref[0].astype(BF16)
            wdb[...] = wd_ref[0].astype(BF16)
            for lo in range(0, tm, SUB_ROWS):
                @pl.when(lo < rows)
                def _(lo=lo):
                    o_ref[lo:lo + SUB_ROWS, :] += swiglu(xs[lo:lo + SUB_ROWS, :], wgb[...], wub[...], wdb[...])


def _ffn(x, wg, wu, wd, tile_expert, n_valid, tm, fc, src=None, tile_rows=None):
    grouped = src is not None
    d = x.shape[1]
    p = src.shape[0] if grouped else x.shape[0]
    nf = wg.shape[2] // fc

    def chunk(ti, f, nv):
        return jnp.where(ti < nv[0], f, nf - 1)

    wspecs = [pl.BlockSpec((1, d, fc), lambda ti, f, te, nv, sr, rw: (te[ti], 0, chunk(ti, f, nv))),
              pl.BlockSpec((1, d, fc), lambda ti, f, te, nv, sr, rw: (te[ti], 0, chunk(ti, f, nv))),
              pl.BlockSpec((1, fc, d), lambda ti, f, te, nv, sr, rw: (te[ti], chunk(ti, f, nv), 0))]
    scratch = []
    if grouped:
        in_specs = [pl.BlockSpec(memory_space=pl.ANY)] + wspecs
        args = [x, wg, wu, wd]
        scratch += [pltpu.VMEM((tm, d), F32), pltpu.VMEM((tm, d), BF16), pltpu.VMEM((d, fc), BF16),
                    pltpu.VMEM((d, fc), BF16), pltpu.VMEM((fc, d), BF16), pltpu.SemaphoreType.DMA(())]
    else:
        in_specs = [pl.BlockSpec((tm, d), lambda ti, f, te, nv, sr, rw: (ti, 0))] + wspecs
        args = [x, wg, wu, wd]
        src = tile_rows = jnp.zeros((1,), jnp.int32)
    grid_spec = pltpu.PrefetchScalarGridSpec(
        num_scalar_prefetch=4,
        grid=(p // tm, nf),
        in_specs=in_specs,
        out_specs=pl.BlockSpec((tm, d), lambda ti, f, te, nv, sr, rw: (ti, 0)),
        scratch_shapes=scratch,
    )
    return pl.pallas_call(
        functools.partial(_ffn_kernel, grouped=grouped, tm=tm, nf=nf),
        grid_spec=grid_spec,
        out_shape=jax.ShapeDtypeStruct((p, d), F32),
        compiler_params=_cparams(("arbitrary", "arbitrary")),
        name="grouped_swiglu" if grouped else "dense_swiglu",
    )(tile_expert, n_valid, src, tile_rows, *args)


def _combine_kernel(pos_ref, x_ref, info_ref, y_ref, g_ref, o_ref, buf, sem, *, tt, n, norm):
    i = pl.program_id(0)

    def issue_tile(tile, slot):
        def issue(r, _):
            for k in range(2):
                _row_copy(y_ref, pos_ref[k * n + tile * tt + r], buf.at[slot, k], r, sem.at[slot]).start()
            return 0
        lax.fori_loop(0, tt, issue, 0, unroll=ISSUE_UNROLL)

    @pl.when(i == 0)
    def _():
        issue_tile(0, 0)

    @pl.when(i + 1 < pl.num_programs(0))
    def _():
        issue_tile(i + 1, (i + 1) % 2)

    slot = i % 2

    def wait(r, _):
        for k in range(2):
            _row_copy(y_ref, 0, buf.at[slot, k], r, sem.at[slot]).wait()
        return 0

    lax.fori_loop(0, tt, wait, 0)
    info = info_ref[...]
    o = x_ref[...] + (info[:, 0:1] * buf[slot, 0] + info[:, 1:2] * buf[slot, 1])
    if norm:
        o = o * lax.rsqrt(jnp.mean(o * o, axis=-1, keepdims=True) + EPS) * g_ref[...]
    o_ref[...] = o


def _combine(x2, info, y, pos, tt, gain=None):
    n, d = x2.shape
    norm = gain is not None
    grid_spec = pltpu.PrefetchScalarGridSpec(
        num_scalar_prefetch=1,
        grid=(n // tt,),
        in_specs=[pl.BlockSpec((tt, d), lambda i, ps: (i, 0)), pl.BlockSpec((tt, LANES), lambda i, ps: (i, 0)),
                  pl.BlockSpec(memory_space=pl.ANY), pl.BlockSpec((1, d), lambda i, ps: (0, 0))],
        out_specs=pl.BlockSpec((tt, d), lambda i, ps: (i, 0)),
        scratch_shapes=[pltpu.VMEM((2, 2, tt, d), F32), pltpu.SemaphoreType.DMA((2,))],
    )
    return pl.pallas_call(
        functools.partial(_combine_kernel, tt=tt, n=n, norm=norm),
        grid_spec=grid_spec,
        out_shape=jax.ShapeDtypeStruct((n, d), F32),
        compiler_params=_cparams(("arbitrary",)),
        name="moe_combine",
    )(pos, x2, info, y, (gain if norm else jnp.ones((d,), F32)).reshape(1, d))


def _moe(x2, h, info, wg, wu, wd, tm, fc, out_gain=None):
    n = h.shape[0]
    e = wg.shape[0]
    eid = jnp.concatenate([info[:, 2], info[:, 3]]).astype(jnp.int32)
    tok = jnp.tile(jnp.arange(n, dtype=jnp.int32), 2)
    onehot = (eid[:, None] == jnp.arange(e, dtype=jnp.int32)[None, :]).astype(jnp.int32)
    rank = jnp.sum((jnp.cumsum(onehot, axis=0) - 1) * onehot, axis=1)
    counts = jnp.sum(onehot, axis=0)
    padded = ((counts + tm - 1) // tm) * tm
    ends = jnp.cumsum(padded)
    starts = ends - padded
    pos = starts[eid] + rank
    n_tiles = (2 * n) // tm + e
    p = n_tiles * tm
    src = jnp.zeros((p,), jnp.int32).at[pos].set(tok)
    n_valid = (ends[e - 1] // tm).astype(jnp.int32)
    tile_start = jnp.arange(n_tiles, dtype=jnp.int32) * tm
    tile_start = jnp.minimum(tile_start, (n_valid - 1) * tm)
    tile_expert = jnp.sum((tile_start[:, None] >= ends[None, :]).astype(jnp.int32), axis=1)
    tile_rows = jnp.clip((starts + counts)[tile_expert] - tile_start, 0, tm).astype(jnp.int32)
    y = _ffn(h, wg, wu, wd, tile_expert, n_valid.reshape(1), tm, fc, src=src, tile_rows=tile_rows)
    return _combine(x2, info, y, pos, _tile(n, 256), out_gain)


def _tile(total, want):
    t = min(total, want)
    assert total % t == 0
    return t


def kernel(x, w_in, b_forget, q_norm, k_norm, kv_norm, w_ukv, w_br_fox, w_br_sb, w_br_dsa, w_out, rel_bias, norm_mix, norm_ffn, w_ffn_gate, w_ffn_up, w_ffn_down, w_router, w_moe_gate, w_moe_up, w_moe_down, norm_final):
    b, s, d = x.shape
    m = b * s
    depth = w_in.shape[0]
    ta = _tile(s, 256)
    tm_norm = _tile(m, 512)
    tm = _tile(m, 1024)
    tn = 512
    fc = 256
    tm_moe = _tile(2 * m, 1024)

    bias = _bias_tiles(rel_bias, ta)
    x2 = x.reshape(m, d)
    pending = None
    for l in range(depth):
        if pending is None:
            h = _rmsnorm(x2, norm_mix[l], BF16, tm_norm)
        else:
            x2, h = _add_rmsnorm(x2, pending, norm_mix[l], tm_norm)
            pending = None
        z, zs = _inproj(h, _relayout_w_in(w_in, l), tm, 2 * tn)
        z3 = z.reshape(b, s, Z_COLS)
        zs3 = zs.reshape(b, s, LANES)
        qn, kn, aq, ak, dk, fvt, svt, dvt = _prep(z3, zs3, q_norm[l], k_norm[l], kv_norm[l], b_forget[l],
                                                  w_ukv[l], ta)
        y_fox = _fox(qn, kn, fvt, aq, ak, z3, ta).reshape(m, W_ATT)
        y_sb = _sb(z3, svt, ta).reshape(m, W_ATT)
        y_dsa = _dsa(z3, zs3, dk, dvt, bias, ta).reshape(m, W_ATT)
        mix = _merge(y_fox, y_sb, y_dsa, w_br_fox[l], w_br_sb[l], w_br_dsa[l], z, d, tm, tn)
        x2 = _matmul_res(mix, w_out[l], x2, tm, 2 * tn)
        j = l // 2
        if l % 2 == 0:
            h = _rmsnorm(x2, norm_ffn[l], BF16, tm_norm)
            n_tiles = m // tm
            pending = _ffn(h, w_ffn_gate[j][None], w_ffn_up[j][None], w_ffn_down[j][None],
                           jnp.zeros((n_tiles,), jnp.int32), jnp.full((1,), n_tiles, jnp.int32), tm, 2 * fc)
        else:
            h, info = _rmsnorm_router(x2, norm_ffn[l], w_router[j], tm_norm)
            last = l == depth - 1
            x2 = _moe(x2, h, info, w_moe_gate[j], w_moe_up[j], w_moe_down[j], tm_moe, fc,
                      norm_final if last else None)
    if pending is not None:
        x2 = _rmsnorm(x2 + pending, norm_final, F32, tm_norm)
    return x2.reshape(b, s, d)
```

```python
import functools
import math

import jax
import jax.numpy as jnp
import numpy as np
from jax import lax
from jax.experimental import pallas as pl
from jax.experimental.pallas import tpu as pltpu

F32 = jnp.float32
BF16 = jnp.bfloat16

HEAD_DIM = 64
N_HEADS = 8
N_PAIRS = N_HEADS // 2
W_ATT = N_HEADS * HEAD_DIM
KV_RANK = 256
IDX_HEADS = 4
IDX_DIM = 64
TOPK_MAX = 256
N_BUCKETS = 32
MAX_DISTANCE = 128
N_EXPERTS = 8
N_BRANCH = 3
EPS = 1e-6
LANES = 128
NEG = -1e30
SB_UNDERFLOW = -104.0
ISSUE_UNROLL = 8
SUB_ROWS = 256

U_FQ, U_FK, U_FV, U_FO, U_SQ, U_SK, U_SV, U_DQ = 0, 4, 8, 12, 16, 20, 24, 28
U_DLAT, U_DIQ, U_DIK, U_SMALL, U_GATES = 32, 34, 36, 37, 40
N_UNITS = 88
Z_COLS = N_UNITS * LANES
SM_FF, SM_DIW = 0, 8

VMEM_LIMIT = 56 * 1024 * 1024


def _cparams(sem):
    return pltpu.CompilerParams(dimension_semantics=sem, vmem_limit_bytes=VMEM_LIMIT)


def _log_sigmoid(x):
    return jnp.minimum(x, 0.0) - jnp.log1p(jnp.exp(-jnp.abs(x)))


def _rmsnorm_kernel(x_ref, g_ref, o_ref):
    x = x_ref[...]
    ms = jnp.mean(x * x, axis=-1, keepdims=True)
    o_ref[...] = (x * lax.rsqrt(ms + EPS) * g_ref[...]).astype(o_ref.dtype)


def _rmsnorm(x, g, out_dtype, tm):
    m, d = x.shape
    return pl.pallas_call(
        _rmsnorm_kernel,
        grid=(m // tm,),
        in_specs=[pl.BlockSpec((tm, d), lambda i: (i, 0)), pl.BlockSpec((1, d), lambda i: (0, 0))],
        out_specs=pl.BlockSpec((tm, d), lambda i: (i, 0)),
        out_shape=jax.ShapeDtypeStruct((m, d), out_dtype),
        compiler_params=_cparams(("parallel",)),
        name="rmsnorm",
    )(x, g.reshape(1, d))


def _add_rmsnorm_kernel(x_ref, y_ref, g_ref, xo_ref, h_ref):
    x = x_ref[...] + y_ref[...]
    xo_ref[...] = x
    ms = jnp.mean(x * x, axis=-1, keepdims=True)
    h_ref[...] = (x * lax.rsqrt(ms + EPS) * g_ref[...]).astype(h_ref.dtype)


def _add_rmsnorm(x, y, g, tm):
    m, d = x.shape
    blk = pl.BlockSpec((tm, d), lambda i: (i, 0))
    return pl.pallas_call(
        _add_rmsnorm_kernel,
        grid=(m // tm,),
        in_specs=[blk, blk, pl.BlockSpec((1, d), lambda i: (0, 0))],
        out_specs=[blk, blk],
        out_shape=[jax.ShapeDtypeStruct((m, d), F32), jax.ShapeDtypeStruct((m, d), BF16)],
        compiler_params=_cparams(("parallel",)),
        name="add_rmsnorm",
    )(x, y, g.reshape(1, d))


def _rmsnorm_router_kernel(x_ref, g_ref, wr_ref, o_ref, info_ref):
    x = x_ref[...]
    ms = jnp.mean(x * x, axis=-1, keepdims=True)
    h = x * lax.rsqrt(ms + EPS) * g_ref[...]
    o_ref[...] = h.astype(o_ref.dtype)
    logits = jnp.dot(h, wr_ref[...], precision=lax.Precision.HIGHEST, preferred_element_type=F32)
    lane = lax.broadcasted_iota(jnp.int32, logits.shape, 1).astype(F32)
    lg = jnp.where(lane < N_EXPERTS, logits, -jnp.inf)
    v1 = jnp.max(lg, axis=-1, keepdims=True)
    i1 = jnp.min(jnp.where(lg == v1, lane, float(LANES)), axis=-1, keepdims=True)
    lg2 = jnp.where(lane == i1, -jnp.inf, lg)
    v2 = jnp.max(lg2, axis=-1, keepdims=True)
    i2 = jnp.min(jnp.where(lg2 == v2, lane, float(LANES)), axis=-1, keepdims=True)
    e2 = jnp.exp(v2 - v1)
    w1 = 1.0 / (1.0 + e2)
    w2 = e2 / (1.0 + e2)
    info = jnp.where(lane == 0.0, w1, jnp.where(lane == 1.0, w2, jnp.where(
        lane == 2.0, i1, jnp.where(lane == 3.0, i2, 0.0))))
    info_ref[...] = info


def _rmsnorm_router(x, g, w_router, tm):
    m, d = x.shape
    wr = jnp.pad(w_router, ((0, 0), (0, LANES - w_router.shape[1])))
    return pl.pallas_call(
        _rmsnorm_router_kernel,
        grid=(m // tm,),
        in_specs=[pl.BlockSpec((tm, d), lambda i: (i, 0)), pl.BlockSpec((1, d), lambda i: (0, 0)),
                  pl.BlockSpec((d, LANES), lambda i: (0, 0))],
        out_specs=[pl.BlockSpec((tm, d), lambda i: (i, 0)), pl.BlockSpec((tm, LANES), lambda i: (i, 0))],
        out_shape=[jax.ShapeDtypeStruct((m, d), F32), jax.ShapeDtypeStruct((m, LANES), F32)],
        compiler_params=_cparams(("parallel",)),
        name="rmsnorm_router",
    )(x, g.reshape(1, d), wr)


def _inproj_kernel(a_ref, w_ref, z_ref, zs_ref, *, small_tile):
    acc = jnp.dot(a_ref[...], w_ref[...], preferred_element_type=F32)
    z_ref[...] = acc.astype(z_ref.dtype)

    @pl.when(pl.program_id(1) == small_tile)
    def _():
        off = (U_SMALL * LANES) % acc.shape[1]
        zs_ref[...] = acc[:, off:off + LANES]


def _inproj(h, w, tm, tn):
    m, d = h.shape
    return pl.pallas_call(
        functools.partial(_inproj_kernel, small_tile=(U_SMALL * LANES) // tn),
        grid=(m // tm, Z_COLS // tn),
        in_specs=[pl.BlockSpec((tm, d), lambda i, j: (i, 0)), pl.BlockSpec((d, tn), lambda i, j: (0, j))],
        out_specs=[pl.BlockSpec((tm, tn), lambda i, j: (i, j)), pl.BlockSpec((tm, LANES), lambda i, j: (i, 0))],
        out_shape=[jax.ShapeDtypeStruct((m, Z_COLS), BF16), jax.ShapeDtypeStruct((m, LANES), F32)],
        compiler_params=_cparams(("parallel", "arbitrary")),
        name="inproj",
    )(h, w)


W_IN_ORDER = (("fq", W_ATT), ("fk", W_ATT), ("fv", W_ATT), ("ff", N_HEADS), ("fo", W_ATT),
              ("sq", W_ATT), ("sk", W_ATT), ("sv", W_ATT), ("dq", W_ATT), ("dlat", KV_RANK),
              ("diq", IDX_HEADS * IDX_DIM), ("dik", IDX_DIM), ("diw", IDX_HEADS))
RELAYOUT_COLS = 4 * LANES


def _relayout_kernel(tbl_ref, w_ref, ff_ref, o_ref, *, layer, special):
    j = pl.program_id(0)

    @pl.when(j != special)
    def _():
        o_ref[...] = jnp.transpose(w_ref[:, layer, :]).astype(o_ref.dtype)

    @pl.when(j == special)
    def _():
        xt = jnp.transpose(w_ref[0:LANES, layer, :])
        fft = jnp.transpose(ff_ref[:, layer, :])
        lane = lax.broadcasted_iota(jnp.int32, xt.shape, 1)
        dik2 = jnp.where(lane < IDX_DIM, xt, pltpu.roll(xt, IDX_DIM, axis=1))
        diw = pltpu.roll(xt, LANES - IDX_DIM + SM_DIW, axis=1)
        small = jnp.where(lane < N_HEADS, fft,
                          jnp.where((lane >= SM_DIW) & (lane < SM_DIW + IDX_HEADS), diw, 0.0))
        o_ref[:, 0:LANES] = dik2.astype(o_ref.dtype)
        o_ref[:, LANES:2 * LANES] = small.astype(o_ref.dtype)
        o_ref[:, 2 * LANES:] = jnp.zeros((o_ref.shape[0], o_ref.shape[1] - 2 * LANES), o_ref.dtype)


def _relayout_w_in(w_in, layer):
    _, d, n_in = w_in.shape
    src, o = {}, 0
    for name, width in W_IN_ORDER + (("gates", N_BRANCH * d),):
        src[name] = o
        o += width
    assert o == n_in and U_GATES * LANES + N_BRANCH * d == Z_COLS and SM_FF == 0
    assert src["diq"] == src["dlat"] + KV_RANK and src["diw"] == src["dik"] + IDX_DIM
    units = {U_FQ: "fq", U_FK: "fk", U_FV: "fv", U_FO: "fo", U_SQ: "sq", U_SK: "sk", U_SV: "sv",
             U_DQ: "dq", U_DLAT: "dlat", U_DIK: "dik"}
    per = RELAYOUT_COLS // LANES
    tbl = [src[units[u]] for u in range(0, U_GATES, per)]
    tbl += [src["gates"] + k * RELAYOUT_COLS for k in range(N_BRANCH * d // RELAYOUT_COLS)]
    elem = lambda rows: (pl.Element(rows), pl.Element(w_in.shape[0]), pl.Element(d))
    grid_spec = pltpu.PrefetchScalarGridSpec(
        num_scalar_prefetch=1,
        grid=(Z_COLS // RELAYOUT_COLS,),
        in_specs=[pl.BlockSpec(elem(RELAYOUT_COLS), lambda j, tbl: (tbl[j], 0, 0)),
                  pl.BlockSpec(elem(LANES), lambda j, tbl: (src["ff"], 0, 0))],
        out_specs=pl.BlockSpec((d, RELAYOUT_COLS), lambda j, tbl: (0, j)),
    )
    wt = jnp.transpose(w_in, (2, 0, 1))
    return pl.pallas_call(
        functools.partial(_relayout_kernel, layer=layer, special=U_DIK // per),
        grid_spec=grid_spec,
        out_shape=jax.ShapeDtypeStruct((d, Z_COLS), BF16),
        compiler_params=_cparams(("arbitrary",)),
        name="w_in_relayout",
    )(jnp.asarray(tbl, jnp.int32), wt, wt)


def _prep_kernel(zq_ref, zk_ref, zl_ref, zfv_ref, zsv_ref, zs_ref, gq_ref, gk_ref, gkv_ref, bf_ref, wukv_ref,
                 grp_ref, eq_ref, ek_ref, oneq_ref, onek_ref,
                 qn_ref, kn_ref, aq_ref, ak_ref, dk_ref, fvt_ref, svt_ref, dvt_ref, carry_ref, *, ts):
    def value_tile(v):
        return jnp.transpose(v).reshape(N_PAIRS, LANES, ts).astype(BF16)

    fvt_ref[0, :, 0] = value_tile(zfv_ref[0].astype(F32))
    svt_ref[0, :, 0] = value_tile(zsv_ref[0].astype(F32))

    @pl.when(pl.program_id(1) == 0)
    def _():
        carry_ref[...] = jnp.zeros_like(carry_ref)

    def head_norm(z_ref, g_ref):
        x = z_ref[0].astype(F32)
        sq = x * x
        hi = sq.astype(BF16)
        lo = (sq - hi.astype(F32)).astype(BF16)
        ms = (jnp.dot(hi, grp_ref[...], preferred_element_type=F32)
              + jnp.dot(lo, grp_ref[...], preferred_element_type=F32)) * (1.0 / HEAD_DIM)
        return x * lax.rsqrt(ms + EPS) * g_ref[...]

    qn_ref[0] = head_norm(zq_ref, gq_ref).astype(BF16)
    kn_ref[0] = head_norm(zk_ref, gk_ref).astype(BF16)

    lane = lax.broadcasted_iota(jnp.int32, (ts, LANES), 1)
    lf = jnp.where(lane < N_HEADS, _log_sigmoid(zs_ref[0] + bf_ref[...]), 0.0)
    r = lax.broadcasted_iota(jnp.int32, (ts, ts), 0)
    c_ = lax.broadcasted_iota(jnp.int32, (ts, ts), 1)
    tri = (c_ <= r).astype(F32)
    c = jnp.dot(tri, lf, precision=lax.Precision.HIGHEST, preferred_element_type=F32) + carry_ref[...]
    carry_ref[...] = c[ts - 1:ts, :]
    c0 = c.astype(BF16)
    r1 = c - c0.astype(F32)
    c1 = r1.astype(BF16)
    c2 = (r1 - c1.astype(F32)).astype(BF16)
    pieces = (c0, c1, c2)
    aq = oneq_ref[...]
    ak = onek_ref[...]
    for k in range(3):
        aq = aq + jnp.dot(pieces[k], eq_ref[k], preferred_element_type=F32)
        ak = ak - jnp.dot(pieces[k], ek_ref[k], preferred_element_type=F32)
    aq_ref[0] = aq.astype(BF16)
    ak_ref[0] = ak.astype(BF16)

    lat = zl_ref[0, :, :KV_RANK].astype(F32)
    msl = jnp.mean(lat * lat, axis=-1, keepdims=True)
    latn = (lat * lax.rsqrt(msl + EPS) * gkv_ref[...]).astype(BF16)
    kv = jnp.dot(latn, wukv_ref[...], preferred_element_type=F32)
    dk_ref[0] = kv[:, :W_ATT].astype(BF16)
    dvt_ref[0, :, 0] = value_tile(kv[:, W_ATT:].astype(BF16).astype(F32))


def _aug_constants():
    eq = np.zeros((3, LANES, LANES), np.float32)
    ek = np.zeros((3, LANES, LANES), np.float32)
    oneq = np.zeros((1, LANES), np.float32)
    onek = np.zeros((1, LANES), np.float32)
    for h in range(N_HEADS):
        for k in range(3):
            eq[k, h, 8 * h + k] = 1.0
            ek[k, h, 8 * h + 3 + k] = 1.0
            oneq[0, 8 * h + 3 + k] = 1.0
            onek[0, 8 * h + k] = 1.0
    grp = np.kron(np.eye(N_HEADS, dtype=np.float32), np.ones((HEAD_DIM, HEAD_DIM), np.float32))
    return (jnp.asarray(grp, BF16), jnp.asarray(eq, BF16), jnp.asarray(ek, BF16),
            jnp.asarray(oneq), jnp.asarray(onek))


def _prep(z3, zs3, q_norm, k_norm, kv_norm, b_forget, w_ukv, ts):
    b, s, _ = z3.shape
    grp, eq, ek, oneq, onek = _aug_constants()
    gq = (jnp.tile(q_norm, N_HEADS) * HEAD_DIM ** -0.5).reshape(1, W_ATT)
    gk = jnp.tile(k_norm, N_HEADS).reshape(1, W_ATT)
    bf = jnp.pad(b_forget, (SM_FF, LANES - N_HEADS - SM_FF)).reshape(1, LANES)
    const = lambda shape: pl.BlockSpec(shape, lambda bi, si: (0,) * len(shape))
    zblk = lambda unit: pl.BlockSpec((1, ts, W_ATT), lambda bi, si: (bi, si, unit // 4))
    seq_out = lambda w: pl.BlockSpec((1, ts, w), lambda bi, si: (bi, si, 0))
    vt_out = pl.BlockSpec((1, N_PAIRS, 1, LANES, ts), lambda bi, si: (bi, 0, si, 0, 0))
    vt_shape = jax.ShapeDtypeStruct((b, N_PAIRS, s // ts, LANES, ts), BF16)
    return pl.pallas_call(
        functools.partial(_prep_kernel, ts=ts),
        grid=(b, s // ts),
        in_specs=[zblk(U_FQ), zblk(U_FK), zblk(U_DLAT), zblk(U_FV), zblk(U_SV),
                  pl.BlockSpec((1, ts, LANES), lambda bi, si: (bi, si, 0)),
                  const((1, W_ATT)), const((1, W_ATT)), const((1, KV_RANK)), const((1, LANES)),
                  const((KV_RANK, 2 * W_ATT)), const((W_ATT, W_ATT)),
                  const((3, LANES, LANES)), const((3, LANES, LANES)), const((1, LANES)), const((1, LANES))],
        out_specs=[seq_out(W_ATT), seq_out(W_ATT), seq_out(LANES), seq_out(LANES), seq_out(W_ATT),
                   vt_out, vt_out, vt_out],
        out_shape=[jax.ShapeDtypeStruct((b, s, W_ATT), BF16), jax.ShapeDtypeStruct((b, s, W_ATT), BF16),
                   jax.ShapeDtypeStruct((b, s, LANES), BF16), jax.ShapeDtypeStruct((b, s, LANES), BF16),
                   jax.ShapeDtypeStruct((b, s, W_ATT), BF16), vt_shape, vt_shape, vt_shape],
        scratch_shapes=[pltpu.VMEM((1, LANES), F32)],
        compiler_params=_cparams(("parallel", "arbitrary")),
        name="mixer_prep",
    )(z3, z3, z3, z3, z3, zs3, gq, gk, kv_norm.reshape(1, KV_RANK), bf, w_ukv.astype(BF16), grp, eq, ek, oneq, onek)


def _nt_dot(a, b):
    return lax.dot_general(a, b, (((1,), (1,)), ((), ())), preferred_element_type=F32)


def _half_mask(shape, half):
    lane = lax.broadcasted_iota(jnp.int32, shape, 1)
    return (lane >= HEAD_DIM * half) & (lane < HEAD_DIM * (half + 1))


def _pair_out(acc0, acc1):
    return jnp.transpose(jnp.concatenate([acc0, acc1], axis=0))


def _online_softmax_pair(last, scores, weighted_values, mask_last, t):
    def soft(s, m, l):
        m_new = jnp.maximum(m, jnp.max(s, axis=0, keepdims=True))
        alpha = jnp.exp(m - m_new)
        p = jnp.exp(s - m_new)
        return m_new, alpha * l + jnp.sum(p, axis=0, keepdims=True), alpha, p.astype(BF16)

    def drain(j, p, alpha, acc):
        return tuple(alpha[h] * acc[h] + weighted_values(j, h, p[h]) for h in range(2))

    def body(n, c):
        s, p_prev, a_prev, m, l, acc = c
        s_next = scores(n + 1)
        acc = drain(jnp.maximum(n - 1, 0), p_prev, a_prev, acc)
        r = [soft(s[h], m[h], l[h]) for h in range(2)]
        return (s_next, (r[0][3], r[1][3]), (r[0][2], r[1][2]), (r[0][0], r[1][0]), (r[0][1], r[1][1]), acc)

    two = lambda x: (x, x)
    init = (scores(0), two(jnp.zeros((t, t), BF16)), two(jnp.ones((1, t), F32)),
            two(jnp.full((1, t), NEG, F32)), two(jnp.zeros((1, t), F32)), two(jnp.zeros((HEAD_DIM, t), F32)))
    s, p_prev, a_prev, m, l, acc = lax.fori_loop(0, last, body, init)
    acc = drain(jnp.maximum(last - 1, 0), p_prev, a_prev, acc)
    if mask_last is not None:
        s = tuple(mask_last(x) for x in s)
    r = [soft(s[h], m[h], l[h]) for h in range(2)]
    acc = drain(last, (r[0][3], r[1][3]), (r[0][2], r[1][2]), acc)
    return acc[0] / r[0][1], acc[1] / r[1][1]


def _fox_kernel(q_ref, k_ref, vt_ref, aq_ref, ak_ref, fo_ref, o_ref, *, t):
    pair = pl.program_id(1)
    i = pl.program_id(2)
    q = q_ref[0].astype(F32)
    aq = aq_ref[0].astype(F32)
    lane = lax.broadcasted_iota(jnp.int32, (t, LANES), 1)
    krow = lax.broadcasted_iota(jnp.int32, (t, t), 0)
    qcol = lax.broadcasted_iota(jnp.int32, (t, t), 1)
    causal = krow <= qcol
    qcs = []
    for half in range(2):
        head = 2 * pair + half
        qm = jnp.where(_half_mask((t, LANES), half), q, 0.0).astype(BF16)
        am = jnp.where((lane >= 8 * head) & (lane < 8 * head + 6), aq, 0.0).astype(BF16)
        qcs.append(jnp.concatenate([qm, am], axis=1))

    def scores(j):
        ks = pl.multiple_of(j * t, t)
        kc = jnp.concatenate([k_ref[0, pl.ds(ks, t), :], ak_ref[0, pl.ds(ks, t), :]], axis=1)
        return tuple(_nt_dot(kc, qcs[half]) for half in range(2))

    def weighted_values(j, half, p):
        return jnp.dot(vt_ref[0, 0, j, HEAD_DIM * half:HEAD_DIM * (half + 1), :], p, preferred_element_type=F32)

    o0, o1 = _online_softmax_pair(i, scores, weighted_values, lambda s: jnp.where(causal, s, NEG), t)
    o = _pair_out(o0, o1)
    o_ref[0] = (o * jax.nn.sigmoid(fo_ref[0].astype(F32))).astype(o_ref.dtype)


def _fox(qn, kn, vt, aq, ak, z3, t):
    b, s, _ = qn.shape
    qblk = lambda unit: pl.BlockSpec((1, t, LANES), lambda bi, p, i: (bi, i, unit + p))
    return pl.pallas_call(
        functools.partial(_fox_kernel, t=t),
        grid=(b, N_PAIRS, s // t),
        in_specs=[qblk(0),
                  pl.BlockSpec((1, s, LANES), lambda bi, p, i: (bi, 0, p)),
                  pl.BlockSpec((1, 1, s // t, LANES, t), lambda bi, p, i: (bi, p, 0, 0, 0)),
                  pl.BlockSpec((1, t, LANES), lambda bi, p, i: (bi, i, 0)),
                  pl.BlockSpec((1, s, LANES), lambda bi, p, i: (bi, 0, 0)),
                  qblk(U_FO)],
        out_specs=qblk(0),
        out_shape=jax.ShapeDtypeStruct((b, s, W_ATT), BF16),
        compiler_params=_cparams(("parallel", "parallel", "arbitrary")),
        name="fox_attention",
    )(qn, kn, vt, aq, ak, z3)


def _sb_kernel(q_ref, k_ref, vt_ref, o_ref, *, t):
    i = pl.program_id(2)
    q = q_ref[0].astype(F32) * (HEAD_DIM ** -0.5)
    krow = lax.broadcasted_iota(jnp.int32, (t, t), 0)
    qcol = lax.broadcasted_iota(jnp.int32, (t, t), 1)
    strict = krow < qcol
    after = (qcol > krow).astype(BF16)
    qms = [jnp.where(_half_mask((t, LANES), half), q, 0.0).astype(BF16) for half in range(2)]

    def local(j, diag):
        ks = pl.multiple_of(j * t, t)
        k = k_ref[0, pl.ds(ks, t), :]
        out = []
        for half in range(2):
            z = _nt_dot(k, qms[half])
            lz = _log_sigmoid(z)
            l1m = lz - z
            if diag:
                l1m = jnp.where(strict, l1m, 0.0)
            hi = l1m.astype(BF16)
            lo = (l1m - hi.astype(F32)).astype(BF16)
            suffix = (jnp.dot(after, hi, preferred_element_type=F32)
                      + jnp.dot(after, lo, preferred_element_type=F32))
            out.append((lz + suffix, jnp.sum(l1m, axis=0, keepdims=True)))
        return out

    def finish(j, loc, carry, keep):
        new = []
        for half in range(2):
            logw, colsum = loc[half]
            rsum, acc = carry[half]
            a = jnp.exp(logw + rsum)
            if keep is not None:
                a = jnp.where(keep, a, 0.0)
            pv = jnp.dot(vt_ref[0, 0, j, HEAD_DIM * half:HEAD_DIM * (half + 1), :], a.astype(BF16),
                         preferred_element_type=F32)
            new.append((rsum + colsum, acc + pv))
        return tuple(new)

    prev = jnp.maximum(i - 1, 0)
    loc_diag = local(i, True)
    loc_prev = local(prev, False)
    zero = (jnp.zeros((1, t), F32), jnp.zeros((HEAD_DIM, t), F32))
    carry = finish(i, loc_diag, (zero, zero), strict)
    carry = finish(prev, loc_prev, carry, i > 0)

    def more(c):
        n, ((r0, _), (r1, _)) = c
        return (n < i) & (jnp.maximum(jnp.max(r0), jnp.max(r1)) > SB_UNDERFLOW)

    def body(c):
        n, carry = c
        j = i - 1 - n
        return n + 1, finish(j, local(j, False), carry, None)

    _, ((_, acc0), (_, acc1)) = lax.while_loop(more, body, (jnp.int32(1), carry))
    o_ref[0] = _pair_out(acc0, acc1).astype(o_ref.dtype)


def _sb(z3, vt, t):
    b, s, _ = z3.shape
    return pl.pallas_call(
        functools.partial(_sb_kernel, t=t),
        grid=(b, N_PAIRS, s // t),
        in_specs=[pl.BlockSpec((1, t, LANES), lambda bi, p, i: (bi, i, U_SQ + p)),
                  pl.BlockSpec((1, s, LANES), lambda bi, p, i: (bi, 0, U_SK + p)),
                  pl.BlockSpec((1, 1, s // t, LANES, t), lambda bi, p, i: (bi, p, 0, 0, 0))],
        out_specs=pl.BlockSpec((1, t, LANES), lambda bi, p, i: (bi, i, p)),
        out_shape=jax.ShapeDtypeStruct((b, s, W_ATT), BF16),
        compiler_params=_cparams(("parallel", "parallel", "arbitrary")),
        name="stickbreak_attention",
    )(z3, z3, vt)


def _t5_bucket(n):
    max_exact = N_BUCKETS // 2
    nf = jnp.maximum(n, 1).astype(F32)
    large = max_exact + (jnp.log(nf / max_exact) / math.log(MAX_DISTANCE / max_exact)
                         * (N_BUCKETS - max_exact)).astype(jnp.int32)
    large = jnp.minimum(large, N_BUCKETS - 1)
    return jnp.where(n < max_exact, n, large)


def _bias_tiles_kernel(relb_ref, o_ref, *, t):
    h = pl.program_id(0)
    krow = lax.broadcasted_iota(jnp.int32, (t, t), 0)
    qcol = lax.broadcasted_iota(jnp.int32, (t, t), 1)
    o_ref[0, 0] = jnp.full((t, t), relb_ref[N_BUCKETS - 1, h], F32)
    for slot, shift in ((1, t), (2, 0)):
        bucket = _t5_bucket(jnp.maximum(qcol - krow + shift, 0))
        val = jnp.full((t, t), relb_ref[0, h], F32)
        for k in range(1, N_BUCKETS):
            val = jnp.where(bucket == k, relb_ref[k, h], val)
        o_ref[0, slot] = val


def _bias_tiles(rel_bias, t):
    assert t >= MAX_DISTANCE
    return pl.pallas_call(
        functools.partial(_bias_tiles_kernel, t=t),
        grid=(N_HEADS,),
        in_specs=[pl.BlockSpec(memory_space=pltpu.SMEM)],
        out_specs=pl.BlockSpec((1, 3, t, t), lambda h: (h, 0, 0, 0)),
        out_shape=jax.ShapeDtypeStruct((N_HEADS, 3, t, t), F32),
        compiler_params=_cparams(("arbitrary",)),
        name="t5_bias_tiles",
    )(rel_bias)


def _dsa_kernel(dq_ref, qi_ref, zs_ref, kidx_ref, dk_ref, dvt_ref, bias_ref, o_ref, key_ref, hi_ref, lo_ref,
                madd_ref, *, t, n_sel):
    i = pl.program_id(1)
    nch = i + 1
    krow = lax.broadcasted_iota(jnp.int32, (t, t), 0)
    qcol = lax.broadcasted_iota(jnp.int32, (t, t), 1)
    idx_scale = (IDX_DIM ** -0.5) * (IDX_HEADS ** -0.5)

    zst = jnp.transpose(zs_ref[0])
    qi = qi_ref[0].astype(F32)
    qih, wih = [], []
    for h in range(IDX_HEADS):
        blk = qi[:, (h // 2) * LANES:(h // 2 + 1) * LANES]
        qih.append(jnp.where(_half_mask((t, LANES), h % 2), blk, 0.0).astype(BF16))
        wih.append(zst[SM_DIW + h:SM_DIW + h + 1, :] * idx_scale)

    def score_chunk(j, _):
        ks = pl.multiple_of(j * t, t)
        kc = kidx_ref[0, pl.ds(ks, t), :]
        sc = jnp.zeros((t, t), F32)
        for h in range(IDX_HEADS):
            sc = sc + jnp.maximum(_nt_dot(kc, qih[h]), 0.0) * wih[h]
        sc = jnp.where(sc == 0.0, 0.0, sc)
        sc = jnp.where(j * t + krow <= i * t + qcol, sc, -jnp.inf)
        bits = pltpu.bitcast(sc, jnp.int32)
        key = jnp.where(bits < 0, bits ^ jnp.int32(0x7FFFFFFF), bits)
        key_ref[j] = key
        hi_ref[j] = (key >> 16).astype(jnp.int16)
        return 0

    lax.fori_loop(0, nch, score_chunk, 0)

    i16 = jnp.int16
    rows16 = 16
    lowest = jnp.full((t, t), -32768, i16)
    npairs = (nch + 1) // 2

    @pl.when(nch % 2 == 1)
    def _():
        hi_ref[nch] = lowest
        lo_ref[nch] = lowest

    def count16(ref, pred):
        def body(j2, c):
            pieces = []
            for j in (2 * j2, 2 * j2 + 1):
                ind = jnp.where(pred(ref[j]), i16(1), i16(0))
                pieces += [ind[r * rows16:(r + 1) * rows16] for r in range(t // rows16)]
            while len(pieces) > 1:
                pieces = [a + b for a, b in zip(pieces[0::2], pieces[1::2])]
            return c + pieces[0]
        per_lane = lax.fori_loop(0, npairs, body, jnp.zeros((rows16, t), i16))
        return jnp.sum(per_lane.astype(F32), axis=0, keepdims=True)

    def search16(ref, base):
        def bit_step(n, thr):
            cand = thr + lax.shift_left(jnp.int32(1), 15 - n)
            c16 = cand.astype(i16)
            cnt = base + count16(ref, lambda v: v >= c16)
            return jnp.where(cnt >= float(n_sel), cand, thr)
        return lax.fori_loop(0, 16, bit_step, jnp.full((1, t), -32768, jnp.int32))

    thr_hi = search16(hi_ref, 0.0)
    thr_hi16 = thr_hi.astype(i16)
    above = count16(hi_ref, lambda v: v > thr_hi16)

    def low_halves(j, _):
        low = ((key_ref[j] & 0xFFFF) - 32768).astype(i16)
        lo_ref[j] = jnp.where(hi_ref[j] == thr_hi16, low, i16(-32768))
        return 0

    lax.fori_loop(0, nch, low_halves, 0)
    thr_lo = search16(lo_ref, above)
    thr = lax.shift_left(thr_hi, 16) | (thr_lo + 32768)

    def count_keys(pred):
        def body(j, c):
            return c + jnp.sum(jnp.where(pred(key_ref[j]), 1.0, 0.0), axis=0, keepdims=True)
        return lax.fori_loop(0, nch, body, jnp.zeros((1, t), F32))

    need = float(n_sel) - count_keys(lambda k: k > thr)

    upto = (qcol <= krow).astype(BF16)

    def mask_chunk(j, seen):
        key = key_ref[j]
        eq = key == thr
        rank = jnp.dot(upto, jnp.where(eq, 1.0, 0.0).astype(BF16), preferred_element_type=F32) + seen
        sel = (key > thr) | (eq & (rank <= need))
        sel = sel & (j * t + krow <= i * t + qcol)
        madd_ref[j] = jnp.where(sel, 0.0, NEG)
        return rank[t - 1:t, :]

    lax.fori_loop(0, nch, mask_chunk, jnp.zeros((1, t), F32))

    for pair in range(N_PAIRS):
        lo, hi_ = pair * LANES, (pair + 1) * LANES
        qp = dq_ref[0, :, lo:hi_].astype(F32) * (HEAD_DIM ** -0.5)
        qms = [jnp.where(_half_mask((t, LANES), half), qp, 0.0).astype(BF16) for half in range(2)]

        def scores(j, pair=pair, lo=lo, hi_=hi_, qms=qms):
            ks = pl.multiple_of(j * t, t)
            slot = jnp.clip(j - i + 2, 0, 2)
            k = dk_ref[0, pl.ds(ks, t), lo:hi_]
            madd = madd_ref[j]
            return tuple(_nt_dot(k, qms[half]) + bias_ref[2 * pair + half, slot] + madd for half in range(2))

        def weighted_values(j, half, p, pair=pair):
            return jnp.dot(dvt_ref[0, pair, j, HEAD_DIM * half:HEAD_DIM * (half + 1), :], p,
                           preferred_element_type=F32)

        o0, o1 = _online_softmax_pair(i, scores, weighted_values, None, t)
        o_ref[0, :, lo:hi_] = _pair_out(o0, o1).astype(o_ref.dtype)


def _dsa(z3, zs3, dk, dvt, bias, t):
    b, s, _ = z3.shape
    n_sel = min(TOPK_MAX, s // 4)
    nt = s // t
    return pl.pallas_call(
        functools.partial(_dsa_kernel, t=t, n_sel=n_sel),
        grid=(b, s // t),
        in_specs=[pl.BlockSpec((1, t, W_ATT), lambda bi, i: (bi, i, U_DQ // 4)),
                  pl.BlockSpec((1, t, 2 * LANES), lambda bi, i: (bi, i, U_DIQ // 2)),
                  pl.BlockSpec((1, t, LANES), lambda bi, i: (bi, i, 0)),
                  pl.BlockSpec((1, s, LANES), lambda bi, i: (bi, 0, U_DIK)),
                  pl.BlockSpec((1, s, W_ATT), lambda bi, i: (bi, 0, 0)),
                  pl.BlockSpec((1, N_PAIRS, s // t, LANES, t), lambda bi, i: (bi, 0, 0, 0, 0)),
                  pl.BlockSpec((N_HEADS, 3, t, t), lambda bi, i: (0, 0, 0, 0))],
        out_specs=pl.BlockSpec((1, t, W_ATT), lambda bi, i: (bi, i, 0)),
        out_shape=jax.ShapeDtypeStruct((b, s, W_ATT), BF16),
        scratch_shapes=[pltpu.VMEM((nt, t, t), jnp.int32), pltpu.VMEM((nt + nt % 2, t, t), jnp.int16),
                        pltpu.VMEM((nt + nt % 2, t, t), jnp.int16), pltpu.VMEM((nt, t, t), F32)],
        compiler_params=_cparams(("parallel", "arbitrary")),
        name="dsa_attention",
    )(z3, z3, zs3, z3, dk, dvt, bias)


def _merge_kernel(yf_ref, ys_ref, yd_ref, wf_ref, ws_ref, wd_ref, g0_ref, g1_ref, g2_ref, o_ref):
    def branch(y_ref, w_ref, g_ref):
        proj = jnp.dot(y_ref[...], w_ref[...].astype(BF16), preferred_element_type=F32)
        gate = 0.5 * jnp.tanh(0.5 * g_ref[...].astype(F32)) + 0.5
        return gate * proj

    o_ref[...] = (branch(yf_ref, wf_ref, g0_ref) + branch(ys_ref, ws_ref, g1_ref)
                  + branch(yd_ref, wd_ref, g2_ref)).astype(o_ref.dtype)


def _merge(yf, ys, yd, wf, ws, wd, z, d, tm, tn):
    m = yf.shape[0]
    yblk = pl.BlockSpec((tm, W_ATT), lambda i, j: (i, 0))
    wblk = pl.BlockSpec((W_ATT, tn), lambda i, j: (0, j))
    gblk = lambda g: pl.BlockSpec((tm, tn), lambda i, j: (i, (U_GATES * LANES + g * d) // tn + j))
    return pl.pallas_call(
        _merge_kernel,
        grid=(m // tm, d // tn),
        in_specs=[yblk, yblk, yblk, wblk, wblk, wblk, gblk(0), gblk(1), gblk(2)],
        out_specs=pl.BlockSpec((tm, tn), lambda i, j: (i, j)),
        out_shape=jax.ShapeDtypeStruct((m, d), BF16),
        compiler_params=_cparams(("parallel", "arbitrary")),
        name="branch_merge",
    )(yf, ys, yd, wf, ws, wd, z, z, z)


def _matmul_res_kernel(a_ref, w_ref, r_ref, o_ref):
    o_ref[...] = r_ref[...] + jnp.dot(a_ref[...], w_ref[...].astype(BF16), preferred_element_type=F32)


def _matmul_res(a, w, res, tm, tn):
    m, k = a.shape
    n = w.shape[1]
    return pl.pallas_call(
        _matmul_res_kernel,
        grid=(m // tm, n // tn),
        in_specs=[pl.BlockSpec((tm, k), lambda i, j: (i, 0)), pl.BlockSpec((k, tn), lambda i, j: (0, j)),
                  pl.BlockSpec((tm, tn), lambda i, j: (i, j))],
        out_specs=pl.BlockSpec((tm, tn), lambda i, j: (i, j)),
        out_shape=jax.ShapeDtypeStruct((m, n), F32),
        compiler_params=_cparams(("parallel", "arbitrary")),
        name="out_proj_residual",
    )(a, w, res)


def _row_copy(src_hbm, row, dst_vmem, slot, sem):
    return pltpu.make_async_copy(src_hbm.at[pl.ds(row, 1)], dst_vmem.at[pl.ds(slot, 1)], sem)


def _ffn_kernel(te_ref, nv_ref, src_ref, rows_ref, x_ref, wg_ref, wu_ref, wd_ref, *rest, grouped, tm, nf):
    ti = pl.program_id(0)
    f = pl.program_id(1)
    if grouped:
        o_ref, gbuf, xs, sem = rest
    else:
        (o_ref,) = rest

    @pl.when(f == 0)
    def _():
        o_ref[...] = jnp.zeros_like(o_ref)

    if grouped:
        steps = max(k for k in range(1, nf) if tm % k == 0)
        per = tm // steps

        def issue_rows(tile, lo, n):
            def issue(r, _):
                _row_copy(x_ref, src_ref[tile * tm + lo + r], gbuf, lo + r, sem).start()
                return 0
            lax.fori_loop(0, n, issue, 0, unroll=ISSUE_UNROLL)

        @pl.when((ti == 0) & (f == 0))
        def _():
            issue_rows(0, 0, tm)

        @pl.when((f == 0) & (ti < nv_ref[0]))
        def _():
            def wait(r, _):
                _row_copy(x_ref, 0, gbuf, r, sem).wait()
                return 0
            lax.fori_loop(0, tm, wait, 0, unroll=ISSUE_UNROLL)
            xs[...] = gbuf[...].astype(BF16)

        @pl.when((f > 0) & (f <= steps) & (ti + 1 < nv_ref[0]))
        def _():
            issue_rows(ti + 1, (f - 1) * per, per)

    def swiglu(x, wgate, wup, wdown):
        g = jnp.dot(x, wgate, preferred_element_type=F32)
        u = jnp.dot(x, wup, preferred_element_type=F32)
        a = (g * jax.nn.sigmoid(g) * u).astype(BF16)
        return jnp.dot(a, wdown, preferred_element_type=F32)

    def whole_tile(x):
        o_ref[...] += swiglu(x, wg_ref[0].astype(BF16), wu_ref[0].astype(BF16), wd_ref[0].astype(BF16))

    @pl.when(ti < nv_ref[0])
    def _():
        if not grouped:
            whole_tile(x_ref[...])
            return
        rows = rows_ref[ti]

        @pl.when(rows > tm - SUB_ROWS)
        def _():
            whole_tile(xs[...])

        @pl.when(rows <= tm - SUB_ROWS)
        def _():
            for lo in range(0, tm, SUB_ROWS):
                @pl.when(lo < rows)
                def _(lo=lo):
                    o_ref[lo:lo + SUB_ROWS, :] += swiglu(
                        xs[lo:lo + SUB_ROWS, :], wg_ref[0].astype(BF16), wu_ref[0].astype(BF16),
                        wd_ref[0].astype(BF16))


def _ffn(x, wg, wu, wd, tile_expert, n_valid, tm, fc, src=None, tile_rows=None):
    grouped = src is not None
    d = x.shape[1]
    p = src.shape[0] if grouped else x.shape[0]
    nf = wg.shape[2] // fc

    def chunk(ti, f, nv):
        return jnp.where(ti < nv[0], f, nf - 1)

    wspecs = [pl.BlockSpec((1, d, fc), lambda ti, f, te, nv, sr, rw: (te[ti], 0, chunk(ti, f, nv))),
              pl.BlockSpec((1, d, fc), lambda ti, f, te, nv, sr, rw: (te[ti], 0, chunk(ti, f, nv))),
              pl.BlockSpec((1, fc, d), lambda ti, f, te, nv, sr, rw: (te[ti], chunk(ti, f, nv), 0))]
    scratch = []
    if grouped:
        in_specs = [pl.BlockSpec(memory_space=pl.ANY)] + wspecs
        args = [x, wg, wu, wd]
        scratch += [pltpu.VMEM((tm, d), F32), pltpu.VMEM((tm, d), BF16), pltpu.SemaphoreType.DMA(())]
    else:
        in_specs = [pl.BlockSpec((tm, d), lambda ti, f, te, nv, sr, rw: (ti, 0))] + wspecs
        args = [x, wg, wu, wd]
        src = tile_rows = jnp.zeros((1,), jnp.int32)
    grid_spec = pltpu.PrefetchScalarGridSpec(
        num_scalar_prefetch=4,
        grid=(p // tm, nf),
        in_specs=in_specs,
        out_specs=pl.BlockSpec((tm, d), lambda ti, f, te, nv, sr, rw: (ti, 0),
                               **({"pipeline_mode": pl.Buffered(1)} if grouped else {})),
        scratch_shapes=scratch,
    )
    return pl.pallas_call(
        functools.partial(_ffn_kernel, grouped=grouped, tm=tm, nf=nf),
        grid_spec=grid_spec,
        out_shape=jax.ShapeDtypeStruct((p, d), F32),
        compiler_params=_cparams(("arbitrary", "arbitrary")),
        name="grouped_swiglu" if grouped else "dense_swiglu",
    )(tile_expert, n_valid, src, tile_rows, *args)


def _combine_kernel(pos_ref, x_ref, info_ref, y_ref, g_ref, o_ref, buf, sem, *, tt, n, norm):
    i = pl.program_id(0)

    def issue_tile(tile, slot):
        def issue(r, _):
            for k in range(2):
                _row_copy(y_ref, pos_ref[k * n + tile * tt + r], buf.at[slot, k], r, sem.at[slot]).start()
            return 0
        lax.fori_loop(0, tt, issue, 0, unroll=ISSUE_UNROLL)

    @pl.when(i == 0)
    def _():
        issue_tile(0, 0)

    @pl.when(i + 1 < pl.num_programs(0))
    def _():
        issue_tile(i + 1, (i + 1) % 2)

    slot = i % 2

    def wait(r, _):
        for k in range(2):
            _row_copy(y_ref, 0, buf.at[slot, k], r, sem.at[slot]).wait()
        return 0

    lax.fori_loop(0, tt, wait, 0)
    info = info_ref[...]
    o = x_ref[...] + (info[:, 0:1] * buf[slot, 0] + info[:, 1:2] * buf[slot, 1])
    if norm:
        o = o * lax.rsqrt(jnp.mean(o * o, axis=-1, keepdims=True) + EPS) * g_ref[...]
    o_ref[...] = o


def _combine(x2, info, y, pos, tt, gain=None):
    n, d = x2.shape
    norm = gain is not None
    grid_spec = pltpu.PrefetchScalarGridSpec(
        num_scalar_prefetch=1,
        grid=(n // tt,),
        in_specs=[pl.BlockSpec((tt, d), lambda i, ps: (i, 0)), pl.BlockSpec((tt, LANES), lambda i, ps: (i, 0)),
                  pl.BlockSpec(memory_space=pl.ANY), pl.BlockSpec((1, d), lambda i, ps: (0, 0))],
        out_specs=pl.BlockSpec((tt, d), lambda i, ps: (i, 0)),
        scratch_shapes=[pltpu.VMEM((2, 2, tt, d), F32), pltpu.SemaphoreType.DMA((2,))],
    )
    return pl.pallas_call(
        functools.partial(_combine_kernel, tt=tt, n=n, norm=norm),
        grid_spec=grid_spec,
        out_shape=jax.ShapeDtypeStruct((n, d), F32),
        compiler_params=_cparams(("arbitrary",)),
        name="moe_combine",
    )(pos, x2, info, y, (gain if norm else jnp.ones((d,), F32)).reshape(1, d))


def _moe(x2, h, info, wg, wu, wd, tm, fc, out_gain=None):
    n = h.shape[0]
    e = wg.shape[0]
    eid = jnp.concatenate([info[:, 2], info[:, 3]]).astype(jnp.int32)
    tok = jnp.tile(jnp.arange(n, dtype=jnp.int32), 2)
    onehot = (eid[:, None] == jnp.arange(e, dtype=jnp.int32)[None, :]).astype(jnp.int32)
    rank = jnp.sum((jnp.cumsum(onehot, axis=0) - 1) * onehot, axis=1)
    counts = jnp.sum(onehot, axis=0)
    padded = ((counts + tm - 1) // tm) * tm
    ends = jnp.cumsum(padded)
    starts = ends - padded
    pos = starts[eid] + rank
    n_tiles = (2 * n) // tm + e
    p = n_tiles * tm
    src = jnp.zeros((p,), jnp.int32).at[pos].set(tok)
    n_valid = (ends[e - 1] // tm).astype(jnp.int32)
    tile_start = jnp.arange(n_tiles, dtype=jnp.int32) * tm
    tile_start = jnp.minimum(tile_start, (n_valid - 1) * tm)
    tile_expert = jnp.sum((tile_start[:, None] >= ends[None, :]).astype(jnp.int32), axis=1)
    tile_rows = jnp.clip((starts + counts)[tile_expert] - tile_start, 0, tm).astype(jnp.int32)
    y = _ffn(h, wg, wu, wd, tile_expert, n_valid.reshape(1), tm, fc, src=src, tile_rows=tile_rows)
    return _combine(x2, info, y, pos, _tile(n, 256), out_gain)


def _tile(total, want):
    t = min(total, want)
    assert total % t == 0
    return t


def kernel(x, w_in, b_forget, q_norm, k_norm, kv_norm, w_ukv, w_br_fox, w_br_sb, w_br_dsa, w_out, rel_bias, norm_mix, norm_ffn, w_ffn_gate, w_ffn_up, w_ffn_down, w_router, w_moe_gate, w_moe_up, w_moe_down, norm_final):
    b, s, d = x.shape
    m = b * s
    depth = w_in.shape[0]
    ta = _tile(s, 256)
    tm_norm = _tile(m, 512)
    tm = _tile(m, 1024)
    tn = 512
    fc = 256
    tm_moe = _tile(2 * m, 1024)

    bias = _bias_tiles(rel_bias, ta)
    x2 = x.reshape(m, d)
    pending = None
    for l in range(depth):
        if pending is None:
            h = _rmsnorm(x2, norm_mix[l], BF16, tm_norm)
        else:
            x2, h = _add_rmsnorm(x2, pending, norm_mix[l], tm_norm)
            pending = None
        z, zs = _inproj(h, _relayout_w_in(w_in, l), tm, 2 * tn)
        z3 = z.reshape(b, s, Z_COLS)
        zs3 = zs.reshape(b, s, LANES)
        qn, kn, aq, ak, dk, fvt, svt, dvt = _prep(z3, zs3, q_norm[l], k_norm[l], kv_norm[l], b_forget[l],
                                                  w_ukv[l], ta)
        y_fox = _fox(qn, kn, fvt, aq, ak, z3, ta).reshape(m, W_ATT)
        y_sb = _sb(z3, svt, ta).reshape(m, W_ATT)
        y_dsa = _dsa(z3, zs3, dk, dvt, bias, ta).reshape(m, W_ATT)
        mix = _merge(y_fox, y_sb, y_dsa, w_br_fox[l], w_br_sb[l], w_br_dsa[l], z, d, tm, tn)
        x2 = _matmul_res(mix, w_out[l], x2, tm, 2 * tn)
        j = l // 2
        if l % 2 == 0:
            h = _rmsnorm(x2, norm_ffn[l], BF16, tm_norm)
            n_tiles = m // tm
            pending = _ffn(h, w_ffn_gate[j][None], w_ffn_up[j][None], w_ffn_down[j][None],
                           jnp.zeros((n_tiles,), jnp.int32), jnp.full((1,), n_tiles, jnp.int32), tm, 2 * fc)
        else:
            h, info = _rmsnorm_router(x2, norm_ffn[l], w_router[j], tm_norm)
            last = l == depth - 1
            x2 = _moe(x2, h, info, w_moe_gate[j], w_moe_up[j], w_moe_down[j], tm_moe, 2 * fc,
                      norm_final if last else None)
    if pending is not None:
        x2 = _rmsnorm(x2 + pending, norm_final, F32, tm_norm)
    return x2.reshape(b, s, d)
```

```python
import functools
import math

import jax
import jax.numpy as jnp
import numpy as np
from jax import lax
from jax.experimental import pallas as pl
from jax.experimental.pallas import tpu as pltpu

F32 = jnp.float32
BF16 = jnp.bfloat16

HEAD_DIM = 64
N_HEADS = 8
N_PAIRS = N_HEADS // 2
W_ATT = N_HEADS * HEAD_DIM
KV_RANK = 256
IDX_HEADS = 4
IDX_DIM = 64
TOPK_MAX = 256
N_BUCKETS = 32
MAX_DISTANCE = 128
N_EXPERTS = 8
N_BRANCH = 3
EPS = 1e-6
LANES = 128
NEG = -1e30
SB_UNDERFLOW = -104.0
ISSUE_UNROLL = 16
SUB_ROWS = 256

U_FQ, U_FK, U_FV, U_FO, U_SQ, U_SK, U_SV, U_DQ = 0, 4, 8, 12, 16, 20, 24, 28
U_DLAT, U_DIQ, U_DIK, U_SMALL, U_GATES = 32, 34, 36, 37, 40
N_UNITS = 88
Z_COLS = N_UNITS * LANES
SM_FF, SM_DIW = 0, 8

VMEM_LIMIT = 56 * 1024 * 1024


def _cparams(sem):
    return pltpu.CompilerParams(dimension_semantics=sem, vmem_limit_bytes=VMEM_LIMIT)


def _log_sigmoid(x):
    return jnp.minimum(x, 0.0) - jnp.log1p(jnp.exp(-jnp.abs(x)))


def _rmsnorm_kernel(x_ref, g_ref, o_ref):
    x = x_ref[...]
    ms = jnp.mean(x * x, axis=-1, keepdims=True)
    o_ref[...] = (x * lax.rsqrt(ms + EPS) * g_ref[...]).astype(o_ref.dtype)


def _rmsnorm(x, g, out_dtype, tm):
    m, d = x.shape
    return pl.pallas_call(
        _rmsnorm_kernel,
        grid=(m // tm,),
        in_specs=[pl.BlockSpec((tm, d), lambda i: (i, 0)), pl.BlockSpec((1, d), lambda i: (0, 0))],
        out_specs=pl.BlockSpec((tm, d), lambda i: (i, 0)),
        out_shape=jax.ShapeDtypeStruct((m, d), out_dtype),
        compiler_params=_cparams(("parallel",)),
        name="rmsnorm",
    )(x, g.reshape(1, d))


def _add_rmsnorm_kernel(x_ref, y_ref, g_ref, xo_ref, h_ref):
    x = x_ref[...] + y_ref[...]
    xo_ref[...] = x
    ms = jnp.mean(x * x, axis=-1, keepdims=True)
    h_ref[...] = (x * lax.rsqrt(ms + EPS) * g_ref[...]).astype(h_ref.dtype)


def _add_rmsnorm(x, y, g, tm):
    m, d = x.shape
    blk = pl.BlockSpec((tm, d), lambda i: (i, 0))
    return pl.pallas_call(
        _add_rmsnorm_kernel,
        grid=(m // tm,),
        in_specs=[blk, blk, pl.BlockSpec((1, d), lambda i: (0, 0))],
        out_specs=[blk, blk],
        out_shape=[jax.ShapeDtypeStruct((m, d), F32), jax.ShapeDtypeStruct((m, d), BF16)],
        compiler_params=_cparams(("parallel",)),
        name="add_rmsnorm",
    )(x, y, g.reshape(1, d))


def _rmsnorm_router_kernel(x_ref, g_ref, wr_ref, o_ref, info_ref):
    x = x_ref[...]
    ms = jnp.mean(x * x, axis=-1, keepdims=True)
    h = x * lax.rsqrt(ms + EPS) * g_ref[...]
    o_ref[...] = h.astype(o_ref.dtype)
    logits = jnp.dot(h, wr_ref[...], precision=lax.Precision.HIGHEST, preferred_element_type=F32)
    lane = lax.broadcasted_iota(jnp.int32, logits.shape, 1).astype(F32)
    lg = jnp.where(lane < N_EXPERTS, logits, -jnp.inf)
    v1 = jnp.max(lg, axis=-1, keepdims=True)
    i1 = jnp.min(jnp.where(lg == v1, lane, float(LANES)), axis=-1, keepdims=True)
    lg2 = jnp.where(lane == i1, -jnp.inf, lg)
    v2 = jnp.max(lg2, axis=-1, keepdims=True)
    i2 = jnp.min(jnp.where(lg2 == v2, lane, float(LANES)), axis=-1, keepdims=True)
    e2 = jnp.exp(v2 - v1)
    w1 = 1.0 / (1.0 + e2)
    w2 = e2 / (1.0 + e2)
    info = jnp.where(lane == 0.0, w1, jnp.where(lane == 1.0, w2, jnp.where(
        lane == 2.0, i1, jnp.where(lane == 3.0, i2, 0.0))))
    info_ref[...] = info


def _rmsnorm_router(x, g, w_router, tm):
    m, d = x.shape
    wr = jnp.pad(w_router, ((0, 0), (0, LANES - w_router.shape[1])))
    return pl.pallas_call(
        _rmsnorm_router_kernel,
        grid=(m // tm,),
        in_specs=[pl.BlockSpec((tm, d), lambda i: (i, 0)), pl.BlockSpec((1, d), lambda i: (0, 0)),
                  pl.BlockSpec((d, LANES), lambda i: (0, 0))],
        out_specs=[pl.BlockSpec((tm, d), lambda i: (i, 0)), pl.BlockSpec((tm, LANES), lambda i: (i, 0))],
        out_shape=[jax.ShapeDtypeStruct((m, d), F32), jax.ShapeDtypeStruct((m, LANES), F32)],
        compiler_params=_cparams(("parallel",)),
        name="rmsnorm_router",
    )(x, g.reshape(1, d), wr)


def _inproj_kernel(a_ref, w_ref, z_ref, zs_ref, *, small_tile):
    acc = jnp.dot(a_ref[...], w_ref[...], preferred_element_type=F32)
    z_ref[...] = acc.astype(z_ref.dtype)

    @pl.when(pl.program_id(1) == small_tile)
    def _():
        off = (U_SMALL * LANES) % acc.shape[1]
        zs_ref[...] = acc[:, off:off + LANES]


def _inproj(h, w, tm, tn):
    m, d = h.shape
    return pl.pallas_call(
        functools.partial(_inproj_kernel, small_tile=(U_SMALL * LANES) // tn),
        grid=(m // tm, Z_COLS // tn),
        in_specs=[pl.BlockSpec((tm, d), lambda i, j: (i, 0)), pl.BlockSpec((d, tn), lambda i, j: (0, j))],
        out_specs=[pl.BlockSpec((tm, tn), lambda i, j: (i, j)), pl.BlockSpec((tm, LANES), lambda i, j: (i, 0))],
        out_shape=[jax.ShapeDtypeStruct((m, Z_COLS), BF16), jax.ShapeDtypeStruct((m, LANES), F32)],
        compiler_params=_cparams(("parallel", "arbitrary")),
        name="inproj",
    )(h, w)


W_IN_ORDER = (("fq", W_ATT), ("fk", W_ATT), ("fv", W_ATT), ("ff", N_HEADS), ("fo", W_ATT),
              ("sq", W_ATT), ("sk", W_ATT), ("sv", W_ATT), ("dq", W_ATT), ("dlat", KV_RANK),
              ("diq", IDX_HEADS * IDX_DIM), ("dik", IDX_DIM), ("diw", IDX_HEADS))
RELAYOUT_COLS = 4 * LANES


def _relayout_kernel(tbl_ref, w_ref, ff_ref, o_ref, *, layer, special):
    j = pl.program_id(0)

    @pl.when(j != special)
    def _():
        o_ref[...] = jnp.transpose(w_ref[:, layer, :]).astype(o_ref.dtype)

    @pl.when(j == special)
    def _():
        xt = jnp.transpose(w_ref[0:LANES, layer, :])
        fft = jnp.transpose(ff_ref[:, layer, :])
        lane = lax.broadcasted_iota(jnp.int32, xt.shape, 1)
        dik2 = jnp.where(lane < IDX_DIM, xt, pltpu.roll(xt, IDX_DIM, axis=1))
        diw = pltpu.roll(xt, LANES - IDX_DIM + SM_DIW, axis=1)
        small = jnp.where(lane < N_HEADS, fft,
                          jnp.where((lane >= SM_DIW) & (lane < SM_DIW + IDX_HEADS), diw, 0.0))
        o_ref[:, 0:LANES] = dik2.astype(o_ref.dtype)
        o_ref[:, LANES:2 * LANES] = small.astype(o_ref.dtype)
        o_ref[:, 2 * LANES:] = jnp.zeros((o_ref.shape[0], o_ref.shape[1] - 2 * LANES), o_ref.dtype)


def _relayout_w_in(w_in, layer):
    _, d, n_in = w_in.shape
    src, o = {}, 0
    for name, width in W_IN_ORDER + (("gates", N_BRANCH * d),):
        src[name] = o
        o += width
    assert o == n_in and U_GATES * LANES + N_BRANCH * d == Z_COLS and SM_FF == 0
    assert src["diq"] == src["dlat"] + KV_RANK and src["diw"] == src["dik"] + IDX_DIM
    units = {U_FQ: "fq", U_FK: "fk", U_FV: "fv", U_FO: "fo", U_SQ: "sq", U_SK: "sk", U_SV: "sv",
             U_DQ: "dq", U_DLAT: "dlat", U_DIK: "dik"}
    per = RELAYOUT_COLS // LANES
    tbl = [src[units[u]] for u in range(0, U_GATES, per)]
    tbl += [src["gates"] + k * RELAYOUT_COLS for k in range(N_BRANCH * d // RELAYOUT_COLS)]
    elem = lambda rows: (pl.Element(rows), pl.Element(w_in.shape[0]), pl.Element(d))
    grid_spec = pltpu.PrefetchScalarGridSpec(
        num_scalar_prefetch=1,
        grid=(Z_COLS // RELAYOUT_COLS,),
        in_specs=[pl.BlockSpec(elem(RELAYOUT_COLS), lambda j, tbl: (tbl[j], 0, 0)),
                  pl.BlockSpec(elem(LANES), lambda j, tbl: (src["ff"], 0, 0))],
        out_specs=pl.BlockSpec((d, RELAYOUT_COLS), lambda j, tbl: (0, j)),
    )
    wt = jnp.transpose(w_in, (2, 0, 1))
    return pl.pallas_call(
        functools.partial(_relayout_kernel, layer=layer, special=U_DIK // per),
        grid_spec=grid_spec,
        out_shape=jax.ShapeDtypeStruct((d, Z_COLS), BF16),
        compiler_params=_cparams(("arbitrary",)),
        name="w_in_relayout",
    )(jnp.asarray(tbl, jnp.int32), wt, wt)


def _prep_kernel(zq_ref, zk_ref, zl_ref, zfv_ref, zsv_ref, zs_ref, gq_ref, gk_ref, gkv_ref, bf_ref, wukv_ref,
                 grp_ref, eq_ref, ek_ref, oneq_ref, onek_ref,
                 qn_ref, kn_ref, aq_ref, ak_ref, dk_ref, fvt_ref, svt_ref, dvt_ref, carry_ref, *, ts):
    def value_tile(v):
        return jnp.transpose(v).reshape(N_PAIRS, LANES, ts).astype(BF16)

    fvt_ref[0, :, 0] = value_tile(zfv_ref[0].astype(F32))
    svt_ref[0, :, 0] = value_tile(zsv_ref[0].astype(F32))

    @pl.when(pl.program_id(1) == 0)
    def _():
        carry_ref[...] = jnp.zeros_like(carry_ref)

    def head_norm(z_ref, g_ref):
        x = z_ref[0].astype(F32)
        sq = x * x
        hi = sq.astype(BF16)
        lo = (sq - hi.astype(F32)).astype(BF16)
        ms = (jnp.dot(hi, grp_ref[...], preferred_element_type=F32)
              + jnp.dot(lo, grp_ref[...], preferred_element_type=F32)) * (1.0 / HEAD_DIM)
        return x * lax.rsqrt(ms + EPS) * g_ref[...]

    qn_ref[0] = head_norm(zq_ref, gq_ref).astype(BF16)
    kn_ref[0] = head_norm(zk_ref, gk_ref).astype(BF16)

    lane = lax.broadcasted_iota(jnp.int32, (ts, LANES), 1)
    lf = jnp.where(lane < N_HEADS, _log_sigmoid(zs_ref[0] + bf_ref[...]), 0.0)
    r = lax.broadcasted_iota(jnp.int32, (ts, ts), 0)
    c_ = lax.broadcasted_iota(jnp.int32, (ts, ts), 1)
    tri = (c_ <= r).astype(F32)
    c = jnp.dot(tri, lf, precision=lax.Precision.HIGHEST, preferred_element_type=F32) + carry_ref[...]
    carry_ref[...] = c[ts - 1:ts, :]
    c0 = c.astype(BF16)
    r1 = c - c0.astype(F32)
    c1 = r1.astype(BF16)
    c2 = (r1 - c1.astype(F32)).astype(BF16)
    pieces = (c0, c1, c2)
    aq = oneq_ref[...]
    ak = onek_ref[...]
    for k in range(3):
        aq = aq + jnp.dot(pieces[k], eq_ref[k], preferred_element_type=F32)
        ak = ak - jnp.dot(pieces[k], ek_ref[k], preferred_element_type=F32)
    aq_ref[0] = aq.astype(BF16)
    ak_ref[0] = ak.astype(BF16)

    lat = zl_ref[0, :, :KV_RANK].astype(F32)
    msl = jnp.mean(lat * lat, axis=-1, keepdims=True)
    latn = (lat * lax.rsqrt(msl + EPS) * gkv_ref[...]).astype(BF16)
    kv = jnp.dot(latn, wukv_ref[...], preferred_element_type=F32)
    dk_ref[0] = kv[:, :W_ATT].astype(BF16)
    dvt_ref[0, :, 0] = value_tile(kv[:, W_ATT:].astype(BF16).astype(F32))


def _aug_constants():
    eq = np.zeros((3, LANES, LANES), np.float32)
    ek = np.zeros((3, LANES, LANES), np.float32)
    oneq = np.zeros((1, LANES), np.float32)
    onek = np.zeros((1, LANES), np.float32)
    for h in range(N_HEADS):
        for k in range(3):
            eq[k, h, 8 * h + k] = 1.0
            ek[k, h, 8 * h + 3 + k] = 1.0
            oneq[0, 8 * h + 3 + k] = 1.0
            onek[0, 8 * h + k] = 1.0
    grp = np.kron(np.eye(N_HEADS, dtype=np.float32), np.ones((HEAD_DIM, HEAD_DIM), np.float32))
    return (jnp.asarray(grp, BF16), jnp.asarray(eq, BF16), jnp.asarray(ek, BF16),
            jnp.asarray(oneq), jnp.asarray(onek))


def _prep(z3, zs3, q_norm, k_norm, kv_norm, b_forget, w_ukv, ts):
    b, s, _ = z3.shape
    grp, eq, ek, oneq, onek = _aug_constants()
    gq = (jnp.tile(q_norm, N_HEADS) * HEAD_DIM ** -0.5).reshape(1, W_ATT)
    gk = jnp.tile(k_norm, N_HEADS).reshape(1, W_ATT)
    bf = jnp.pad(b_forget, (SM_FF, LANES - N_HEADS - SM_FF)).reshape(1, LANES)
    const = lambda shape: pl.BlockSpec(shape, lambda bi, si: (0,) * len(shape))
    zblk = lambda unit: pl.BlockSpec((1, ts, W_ATT), lambda bi, si: (bi, si, unit // 4))
    seq_out = lambda w: pl.BlockSpec((1, ts, w), lambda bi, si: (bi, si, 0))
    vt_out = pl.BlockSpec((1, N_PAIRS, 1, LANES, ts), lambda bi, si: (bi, 0, si, 0, 0))
    vt_shape = jax.ShapeDtypeStruct((b, N_PAIRS, s // ts, LANES, ts), BF16)
    return pl.pallas_call(
        functools.partial(_prep_kernel, ts=ts),
        grid=(b, s // ts),
        in_specs=[zblk(U_FQ), zblk(U_FK), zblk(U_DLAT), zblk(U_FV), zblk(U_SV),
                  pl.BlockSpec((1, ts, LANES), lambda bi, si: (bi, si, 0)),
                  const((1, W_ATT)), const((1, W_ATT)), const((1, KV_RANK)), const((1, LANES)),
                  const((KV_RANK, 2 * W_ATT)), const((W_ATT, W_ATT)),
                  const((3, LANES, LANES)), const((3, LANES, LANES)), const((1, LANES)), const((1, LANES))],
        out_specs=[seq_out(W_ATT), seq_out(W_ATT), seq_out(LANES), seq_out(LANES), seq_out(W_ATT),
                   vt_out, vt_out, vt_out],
        out_shape=[jax.ShapeDtypeStruct((b, s, W_ATT), BF16), jax.ShapeDtypeStruct((b, s, W_ATT), BF16),
                   jax.ShapeDtypeStruct((b, s, LANES), BF16), jax.ShapeDtypeStruct((b, s, LANES), BF16),
                   jax.ShapeDtypeStruct((b, s, W_ATT), BF16), vt_shape, vt_shape, vt_shape],
        scratch_shapes=[pltpu.VMEM((1, LANES), F32)],
        compiler_params=_cparams(("parallel", "arbitrary")),
        name="mixer_prep",
    )(z3, z3, z3, z3, z3, zs3, gq, gk, kv_norm.reshape(1, KV_RANK), bf, w_ukv.astype(BF16), grp, eq, ek, oneq, onek)


def _nt_dot(a, b):
    return lax.dot_general(a, b, (((1,), (1,)), ((), ())), preferred_element_type=F32)


def _half_mask(shape, half):
    lane = lax.broadcasted_iota(jnp.int32, shape, 1)
    return (lane >= HEAD_DIM * half) & (lane < HEAD_DIM * (half + 1))


def _pair_out(acc0, acc1):
    return jnp.transpose(jnp.concatenate([acc0, acc1], axis=0))


def _online_softmax_pair(last, scores, weighted_values, mask_last, t):
    def soft(s, m, l):
        m_new = jnp.maximum(m, jnp.max(s, axis=0, keepdims=True))
        alpha = jnp.exp(m - m_new)
        p = jnp.exp(s - m_new)
        return m_new, alpha * l + jnp.sum(p, axis=0, keepdims=True), alpha, p.astype(BF16)

    def drain(j, p, alpha, acc):
        return tuple(alpha[h] * acc[h] + weighted_values(j, h, p[h]) for h in range(2))

    def body(n, c):
        s, p_prev, a_prev, m, l, acc = c
        s_next = scores(n + 1)
        acc = drain(jnp.maximum(n - 1, 0), p_prev, a_prev, acc)
        r = [soft(s[h], m[h], l[h]) for h in range(2)]
        return (s_next, (r[0][3], r[1][3]), (r[0][2], r[1][2]), (r[0][0], r[1][0]), (r[0][1], r[1][1]), acc)

    two = lambda x: (x, x)
    init = (scores(0), two(jnp.zeros((t, t), BF16)), two(jnp.ones((1, t), F32)),
            two(jnp.full((1, t), NEG, F32)), two(jnp.zeros((1, t), F32)), two(jnp.zeros((HEAD_DIM, t), F32)))
    s, p_prev, a_prev, m, l, acc = lax.fori_loop(0, last, body, init)
    acc = drain(jnp.maximum(last - 1, 0), p_prev, a_prev, acc)
    if mask_last is not None:
        s = tuple(mask_last(x) for x in s)
    r = [soft(s[h], m[h], l[h]) for h in range(2)]
    acc = drain(last, (r[0][3], r[1][3]), (r[0][2], r[1][2]), acc)
    return acc[0] / r[0][1], acc[1] / r[1][1]


def _fox_kernel(q_ref, k_ref, vt_ref, aq_ref, ak_ref, fo_ref, o_ref, *, t):
    pair = pl.program_id(1)
    i = pl.program_id(2)
    q = q_ref[0].astype(F32)
    aq = aq_ref[0].astype(F32)
    lane = lax.broadcasted_iota(jnp.int32, (t, LANES), 1)
    krow = lax.broadcasted_iota(jnp.int32, (t, t), 0)
    qcol = lax.broadcasted_iota(jnp.int32, (t, t), 1)
    causal = krow <= qcol
    qcs = []
    for half in range(2):
        head = 2 * pair + half
        qm = jnp.where(_half_mask((t, LANES), half), q, 0.0).astype(BF16)
        am = jnp.where((lane >= 8 * head) & (lane < 8 * head + 6), aq, 0.0).astype(BF16)
        qcs.append(jnp.concatenate([qm, am], axis=1))

    def scores(j):
        ks = pl.multiple_of(j * t, t)
        kc = jnp.concatenate([k_ref[0, pl.ds(ks, t), :], ak_ref[0, pl.ds(ks, t), :]], axis=1)
        return tuple(_nt_dot(kc, qcs[half]) for half in range(2))

    def weighted_values(j, half, p):
        return jnp.dot(vt_ref[0, 0, j, HEAD_DIM * half:HEAD_DIM * (half + 1), :], p, preferred_element_type=F32)

    o0, o1 = _online_softmax_pair(i, scores, weighted_values, lambda s: jnp.where(causal, s, NEG), t)
    o = _pair_out(o0, o1)
    o_ref[0] = (o * jax.nn.sigmoid(fo_ref[0].astype(F32))).astype(o_ref.dtype)


def _fox(qn, kn, vt, aq, ak, z3, t):
    b, s, _ = qn.shape
    qblk = lambda unit: pl.BlockSpec((1, t, LANES), lambda bi, p, i: (bi, i, unit + p))
    return pl.pallas_call(
        functools.partial(_fox_kernel, t=t),
        grid=(b, N_PAIRS, s // t),
        in_specs=[qblk(0),
                  pl.BlockSpec((1, s, LANES), lambda bi, p, i: (bi, 0, p)),
                  pl.BlockSpec((1, 1, s // t, LANES, t), lambda bi, p, i: (bi, p, 0, 0, 0)),
                  pl.BlockSpec((1, t, LANES), lambda bi, p, i: (bi, i, 0)),
                  pl.BlockSpec((1, s, LANES), lambda bi, p, i: (bi, 0, 0)),
                  qblk(U_FO)],
        out_specs=qblk(0),
        out_shape=jax.ShapeDtypeStruct((b, s, W_ATT), BF16),
        compiler_params=_cparams(("parallel", "parallel", "arbitrary")),
        name="fox_attention",
    )(qn, kn, vt, aq, ak, z3)


def _sb_kernel(q_ref, k_ref, vt_ref, o_ref, *, t):
    i = pl.program_id(2)
    q = q_ref[0].astype(F32) * (HEAD_DIM ** -0.5)
    krow = lax.broadcasted_iota(jnp.int32, (t, t), 0)
    qcol = lax.broadcasted_iota(jnp.int32, (t, t), 1)
    strict = krow < qcol
    after = (qcol > krow).astype(BF16)
    qms = [jnp.where(_half_mask((t, LANES), half), q, 0.0).astype(BF16) for half in range(2)]

    def local(j, diag):
        ks = pl.multiple_of(j * t, t)
        k = k_ref[0, pl.ds(ks, t), :]
        out = []
        for half in range(2):
            z = _nt_dot(k, qms[half])
            lz = _log_sigmoid(z)
            l1m = lz - z
            if diag:
                l1m = jnp.where(strict, l1m, 0.0)
            hi = l1m.astype(BF16)
            lo = (l1m - hi.astype(F32)).astype(BF16)
            suffix = (jnp.dot(after, hi, preferred_element_type=F32)
                      + jnp.dot(after, lo, preferred_element_type=F32))
            out.append((lz + suffix, jnp.sum(l1m, axis=0, keepdims=True)))
        return out

    def finish(j, loc, carry, keep):
        new = []
        for half in range(2):
            logw, colsum = loc[half]
            rsum, acc = carry[half]
            a = jnp.exp(logw + rsum)
            if keep is not None:
                a = jnp.where(keep, a, 0.0)
            pv = jnp.dot(vt_ref[0, 0, j, HEAD_DIM * half:HEAD_DIM * (half + 1), :], a.astype(BF16),
                         preferred_element_type=F32)
            new.append((rsum + colsum, acc + pv))
        return tuple(new)

    prev = jnp.maximum(i - 1, 0)
    loc_diag = local(i, True)
    loc_prev = local(prev, False)
    zero = (jnp.zeros((1, t), F32), jnp.zeros((HEAD_DIM, t), F32))
    carry = finish(i, loc_diag, (zero, zero), strict)
    carry = finish(prev, loc_prev, carry, i > 0)

    def more(c):
        n, ((r0, _), (r1, _)) = c
        return (n < i) & (jnp.maximum(jnp.max(r0), jnp.max(r1)) > SB_UNDERFLOW)

    def body(c):
        n, carry = c
        j = i - 1 - n
        return n + 1, finish(j, local(j, False), carry, None)

    _, ((_, acc0), (_, acc1)) = lax.while_loop(more, body, (jnp.int32(1), carry))
    o_ref[0] = _pair_out(acc0, acc1).astype(o_ref.dtype)


def _sb(z3, vt, t):
    b, s, _ = z3.shape
    return pl.pallas_call(
        functools.partial(_sb_kernel, t=t),
        grid=(b, N_PAIRS, s // t),
        in_specs=[pl.BlockSpec((1, t, LANES), lambda bi, p, i: (bi, i, U_SQ + p)),
                  pl.BlockSpec((1, s, LANES), lambda bi, p, i: (bi, 0, U_SK + p)),
                  pl.BlockSpec((1, 1, s // t, LANES, t), lambda bi, p, i: (bi, p, 0, 0, 0))],
        out_specs=pl.BlockSpec((1, t, LANES), lambda bi, p, i: (bi, i, p)),
        out_shape=jax.ShapeDtypeStruct((b, s, W_ATT), BF16),
        compiler_params=_cparams(("parallel", "parallel", "arbitrary")),
        name="stickbreak_attention",
    )(z3, z3, vt)


def _t5_bucket(n):
    max_exact = N_BUCKETS // 2
    nf = jnp.maximum(n, 1).astype(F32)
    large = max_exact + (jnp.log(nf / max_exact) / math.log(MAX_DISTANCE / max_exact)
                         * (N_BUCKETS - max_exact)).astype(jnp.int32)
    large = jnp.minimum(large, N_BUCKETS - 1)
    return jnp.where(n < max_exact, n, large)


def _bias_tiles_kernel(relb_ref, o_ref, *, t):
    h = pl.program_id(0)
    krow = lax.broadcasted_iota(jnp.int32, (t, t), 0)
    qcol = lax.broadcasted_iota(jnp.int32, (t, t), 1)
    o_ref[0, 0] = jnp.full((t, t), relb_ref[N_BUCKETS - 1, h], F32)
    for slot, shift in ((1, t), (2, 0)):
        bucket = _t5_bucket(jnp.maximum(qcol - krow + shift, 0))
        val = jnp.full((t, t), relb_ref[0, h], F32)
        for k in range(1, N_BUCKETS):
            val = jnp.where(bucket == k, relb_ref[k, h], val)
        o_ref[0, slot] = val


def _bias_tiles(rel_bias, t):
    assert t >= MAX_DISTANCE
    return pl.pallas_call(
        functools.partial(_bias_tiles_kernel, t=t),
        grid=(N_HEADS,),
        in_specs=[pl.BlockSpec(memory_space=pltpu.SMEM)],
        out_specs=pl.BlockSpec((1, 3, t, t), lambda h: (h, 0, 0, 0)),
        out_shape=jax.ShapeDtypeStruct((N_HEADS, 3, t, t), F32),
        compiler_params=_cparams(("arbitrary",)),
        name="t5_bias_tiles",
    )(rel_bias)


def _dsa_kernel(dq_ref, qi_ref, zs_ref, kidx_ref, dk_ref, dvt_ref, bias_ref, o_ref, key_ref, hi_ref, lo_ref,
                madd_ref, *, t, n_sel):
    i = pl.program_id(1)
    nch = i + 1
    krow = lax.broadcasted_iota(jnp.int32, (t, t), 0)
    qcol = lax.broadcasted_iota(jnp.int32, (t, t), 1)
    idx_scale = (IDX_DIM ** -0.5) * (IDX_HEADS ** -0.5)

    zst = jnp.transpose(zs_ref[0])
    qi = qi_ref[0].astype(F32)
    qih, wih = [], []
    for h in range(IDX_HEADS):
        blk = qi[:, (h // 2) * LANES:(h // 2 + 1) * LANES]
        qih.append(jnp.where(_half_mask((t, LANES), h % 2), blk, 0.0).astype(BF16))
        wih.append(zst[SM_DIW + h:SM_DIW + h + 1, :] * idx_scale)

    def score_chunk(j, _):
        ks = pl.multiple_of(j * t, t)
        kc = kidx_ref[0, pl.ds(ks, t), :]
        sc = jnp.zeros((t, t), F32)
        for h in range(IDX_HEADS):
            sc = sc + jnp.maximum(_nt_dot(kc, qih[h]), 0.0) * wih[h]
        sc = jnp.where(sc == 0.0, 0.0, sc)
        sc = jnp.where(j * t + krow <= i * t + qcol, sc, -jnp.inf)
        bits = pltpu.bitcast(sc, jnp.int32)
        key = jnp.where(bits < 0, bits ^ jnp.int32(0x7FFFFFFF), bits)
        key_ref[j] = key
        hi_ref[j] = (key >> 16).astype(jnp.int16)
        return 0

    lax.fori_loop(0, nch, score_chunk, 0)

    i16 = jnp.int16
    rows16 = 16
    lowest = jnp.full((t, t), -32768, i16)
    npairs = (nch + 1) // 2

    @pl.when(nch % 2 == 1)
    def _():
        hi_ref[nch] = lowest
        lo_ref[nch] = lowest

    def count16(ref, pred):
        def body(j2, c):
            pieces = []
            for j in (2 * j2, 2 * j2 + 1):
                ind = jnp.where(pred(ref[j]), i16(1), i16(0))
                pieces += [ind[r * rows16:(r + 1) * rows16] for r in range(t // rows16)]
            while len(pieces) > 1:
                pieces = [a + b for a, b in zip(pieces[0::2], pieces[1::2])]
            return c + pieces[0]
        per_lane = lax.fori_loop(0, npairs, body, jnp.zeros((rows16, t), i16))
        return jnp.sum(per_lane.astype(F32), axis=0, keepdims=True)

    def search16(ref, base):
        def bit_step(n, thr):
            cand = thr + lax.shift_left(jnp.int32(1), 15 - n)
            c16 = cand.astype(i16)
            cnt = base + count16(ref, lambda v: v >= c16)
            return jnp.where(cnt >= float(n_sel), cand, thr)
        return lax.fori_loop(0, 16, bit_step, jnp.full((1, t), -32768, jnp.int32))

    thr_hi = search16(hi_ref, 0.0)
    thr_hi16 = thr_hi.astype(i16)
    above = count16(hi_ref, lambda v: v > thr_hi16)

    def low_halves(j, _):
        low = ((key_ref[j] & 0xFFFF) - 32768).astype(i16)
        lo_ref[j] = jnp.where(hi_ref[j] == thr_hi16, low, i16(-32768))
        return 0

    lax.fori_loop(0, nch, low_halves, 0)
    thr_lo = search16(lo_ref, above)
    thr = lax.shift_left(thr_hi, 16) | (thr_lo + 32768)

    def count_keys(pred):
        def body(j, c):
            return c + jnp.sum(jnp.where(pred(key_ref[j]), 1.0, 0.0), axis=0, keepdims=True)
        return lax.fori_loop(0, nch, body, jnp.zeros((1, t), F32))

    need = float(n_sel) - count_keys(lambda k: k > thr)

    upto = (qcol <= krow).astype(BF16)

    def mask_chunk(j, seen):
        key = key_ref[j]
        eq = key == thr
        rank = jnp.dot(upto, jnp.where(eq, 1.0, 0.0).astype(BF16), preferred_element_type=F32) + seen
        sel = (key > thr) | (eq & (rank <= need))
        sel = sel & (j * t + krow <= i * t + qcol)
        madd_ref[j] = jnp.where(sel, 0.0, NEG)
        return rank[t - 1:t, :]

    lax.fori_loop(0, nch, mask_chunk, jnp.zeros((1, t), F32))

    for pair in range(N_PAIRS):
        lo, hi_ = pair * LANES, (pair + 1) * LANES
        qp = dq_ref[0, :, lo:hi_].astype(F32) * (HEAD_DIM ** -0.5)
        qms = [jnp.where(_half_mask((t, LANES), half), qp, 0.0).astype(BF16) for half in range(2)]

        def scores(j, pair=pair, lo=lo, hi_=hi_, qms=qms):
            ks = pl.multiple_of(j * t, t)
            slot = jnp.clip(j - i + 2, 0, 2)
            k = dk_ref[0, pl.ds(ks, t), lo:hi_]
            madd = madd_ref[j]
            return tuple(_nt_dot(k, qms[half]) + bias_ref[2 * pair + half, slot] + madd for half in range(2))

        def weighted_values(j, half, p, pair=pair):
            return jnp.dot(dvt_ref[0, pair, j, HEAD_DIM * half:HEAD_DIM * (half + 1), :], p,
                           preferred_element_type=F32)

        o0, o1 = _online_softmax_pair(i, scores, weighted_values, None, t)
        o_ref[0, :, lo:hi_] = _pair_out(o0, o1).astype(o_ref.dtype)


def _dsa(z3, zs3, dk, dvt, bias, t):
    b, s, _ = z3.shape
    n_sel = min(TOPK_MAX, s // 4)
    nt = s // t
    return pl.pallas_call(
        functools.partial(_dsa_kernel, t=t, n_sel=n_sel),
        grid=(b, s // t),
        in_specs=[pl.BlockSpec((1, t, W_ATT), lambda bi, i: (bi, i, U_DQ // 4)),
                  pl.BlockSpec((1, t, 2 * LANES), lambda bi, i: (bi, i, U_DIQ // 2)),
                  pl.BlockSpec((1, t, LANES), lambda bi, i: (bi, i, 0)),
                  pl.BlockSpec((1, s, LANES), lambda bi, i: (bi, 0, U_DIK)),
                  pl.BlockSpec((1, s, W_ATT), lambda bi, i: (bi, 0, 0)),
                  pl.BlockSpec((1, N_PAIRS, s // t, LANES, t), lambda bi, i: (bi, 0, 0, 0, 0)),
                  pl.BlockSpec((N_HEADS, 3, t, t), lambda bi, i: (0, 0, 0, 0))],
        out_specs=pl.BlockSpec((1, t, W_ATT), lambda bi, i: (bi, i, 0)),
        out_shape=jax.ShapeDtypeStruct((b, s, W_ATT), BF16),
        scratch_shapes=[pltpu.VMEM((nt, t, t), jnp.int32), pltpu.VMEM((nt + nt % 2, t, t), jnp.int16),
                        pltpu.VMEM((nt + nt % 2, t, t), jnp.int16), pltpu.VMEM((nt, t, t), F32)],
        compiler_params=_cparams(("parallel", "arbitrary")),
        name="dsa_attention",
    )(z3, z3, zs3, z3, dk, dvt, bias)


def _merge_kernel(yf_ref, ys_ref, yd_ref, wf_ref, ws_ref, wd_ref, g0_ref, g1_ref, g2_ref, o_ref):
    def branch(y_ref, w_ref, g_ref):
        proj = jnp.dot(y_ref[...], w_ref[...].astype(BF16), preferred_element_type=F32)
        gate = 0.5 * jnp.tanh(0.5 * g_ref[...].astype(F32)) + 0.5
        return gate * proj

    o_ref[...] = (branch(yf_ref, wf_ref, g0_ref) + branch(ys_ref, ws_ref, g1_ref)
                  + branch(yd_ref, wd_ref, g2_ref)).astype(o_ref.dtype)


def _merge(yf, ys, yd, wf, ws, wd, z, d, tm, tn):
    m = yf.shape[0]
    yblk = pl.BlockSpec((tm, W_ATT), lambda i, j: (i, 0))
    wblk = pl.BlockSpec((W_ATT, tn), lambda i, j: (0, j))
    gblk = lambda g: pl.BlockSpec((tm, tn), lambda i, j: (i, (U_GATES * LANES + g * d) // tn + j))
    return pl.pallas_call(
        _merge_kernel,
        grid=(m // tm, d // tn),
        in_specs=[yblk, yblk, yblk, wblk, wblk, wblk, gblk(0), gblk(1), gblk(2)],
        out_specs=pl.BlockSpec((tm, tn), lambda i, j: (i, j)),
        out_shape=jax.ShapeDtypeStruct((m, d), BF16),
        compiler_params=_cparams(("parallel", "arbitrary")),
        name="branch_merge",
    )(yf, ys, yd, wf, ws, wd, z, z, z)


def _matmul_res_kernel(a_ref, w_ref, r_ref, o_ref):
    o_ref[...] = r_ref[...] + jnp.dot(a_ref[...], w_ref[...].astype(BF16), preferred_element_type=F32)


def _matmul_res(a, w, res, tm, tn):
    m, k = a.shape
    n = w.shape[1]
    return pl.pallas_call(
        _matmul_res_kernel,
        grid=(m // tm, n // tn),
        in_specs=[pl.BlockSpec((tm, k), lambda i, j: (i, 0)), pl.BlockSpec((k, tn), lambda i, j: (0, j)),
                  pl.BlockSpec((tm, tn), lambda i, j: (i, j))],
        out_specs=pl.BlockSpec((tm, tn), lambda i, j: (i, j)),
        out_shape=jax.ShapeDtypeStruct((m, n), F32),
        compiler_params=_cparams(("parallel", "arbitrary")),
        name="out_proj_residual",
    )(a, w, res)


def _row_copy(src_hbm, row, dst_vmem, slot, sem):
    return pltpu.make_async_copy(src_hbm.at[pl.ds(row, 1)], dst_vmem.at[pl.ds(slot, 1)], sem)


def _ffn_kernel(te_ref, nv_ref, src_ref, rows_ref, x_ref, wg_ref, wu_ref, wd_ref, *rest, grouped, tm, nf):
    ti = pl.program_id(0)
    f = pl.program_id(1)
    if grouped:
        o_ref, gbuf, xs, sem = rest
    else:
        (o_ref,) = rest

    @pl.when(f == 0)
    def _():
        o_ref[...] = jnp.zeros_like(o_ref)

    if grouped:
        steps = max(k for k in range(1, nf) if tm % k == 0)
        per = tm // steps

        def issue_rows(tile, lo, n):
            def issue(r, _):
                _row_copy(x_ref, src_ref[tile * tm + lo + r], gbuf, lo + r, sem).start()
                return 0
            lax.fori_loop(0, n, issue, 0, unroll=ISSUE_UNROLL)

        @pl.when((ti == 0) & (f == 0))
        def _():
            issue_rows(0, 0, tm)

        @pl.when((f == 0) & (ti < nv_ref[0]))
        def _():
            def wait(r, _):
                _row_copy(x_ref, 0, gbuf, r, sem).wait()
                return 0
            lax.fori_loop(0, tm, wait, 0, unroll=ISSUE_UNROLL)
            xs[...] = gbuf[...].astype(BF16)

        @pl.when((f > 0) & (f <= steps) & (ti + 1 < nv_ref[0]))
        def _():
            issue_rows(ti + 1, (f - 1) * per, per)

    def swiglu(x, wgate, wup, wdown):
        g = jnp.dot(x, wgate, preferred_element_type=F32)
        u = jnp.dot(x, wup, preferred_element_type=F32)
        a = (g * jax.nn.sigmoid(g) * u).astype(BF16)
        return jnp.dot(a, wdown, preferred_element_type=F32)

    def whole_tile(x):
        o_ref[...] += swiglu(x, wg_ref[0].astype(BF16), wu_ref[0].astype(BF16), wd_ref[0].astype(BF16))

    @pl.when(ti < nv_ref[0])
    def _():
        if not grouped:
            whole_tile(x_ref[...])
            return
        rows = rows_ref[ti]

        @pl.when(rows > tm - SUB_ROWS)
        def _():
            whole_tile(xs[...])

        @pl.when(rows <= tm - SUB_ROWS)
        def _():
            for lo in range(0, tm, SUB_ROWS):
                @pl.when(lo < rows)
                def _(lo=lo):
                    o_ref[lo:lo + SUB_ROWS, :] += swiglu(
                        xs[lo:lo + SUB_ROWS, :], wg_ref[0].astype(BF16), wu_ref[0].astype(BF16),
                        wd_ref[0].astype(BF16))


def _ffn(x, wg, wu, wd, tile_expert, n_valid, tm, fc, src=None, tile_rows=None):
    grouped = src is not None
    d = x.shape[1]
    p = src.shape[0] if grouped else x.shape[0]
    nf = wg.shape[2] // fc

    def chunk(ti, f, nv):
        return jnp.where(ti < nv[0], f, nf - 1)

    wspecs = [pl.BlockSpec((1, d, fc), lambda ti, f, te, nv, sr, rw: (te[ti], 0, chunk(ti, f, nv))),
              pl.BlockSpec((1, d, fc), lambda ti, f, te, nv, sr, rw: (te[ti], 0, chunk(ti, f, nv))),
              pl.BlockSpec((1, fc, d), lambda ti, f, te, nv, sr, rw: (te[ti], chunk(ti, f, nv), 0))]
    scratch = []
    if grouped:
        in_specs = [pl.BlockSpec(memory_space=pl.ANY)] + wspecs
        args = [x, wg, wu, wd]
        scratch += [pltpu.VMEM((tm, d), F32), pltpu.VMEM((tm, d), BF16), pltpu.SemaphoreType.DMA(())]
    else:
        in_specs = [pl.BlockSpec((tm, d), lambda ti, f, te, nv, sr, rw: (ti, 0))] + wspecs
        args = [x, wg, wu, wd]
        src = tile_rows = jnp.zeros((1,), jnp.int32)
    grid_spec = pltpu.PrefetchScalarGridSpec(
        num_scalar_prefetch=4,
        grid=(p // tm, nf),
        in_specs=in_specs,
        out_specs=pl.BlockSpec((tm, d), lambda ti, f, te, nv, sr, rw: (ti, 0),
                               **({"pipeline_mode": pl.Buffered(1)} if grouped else {})),
        scratch_shapes=scratch,
    )
    return pl.pallas_call(
        functools.partial(_ffn_kernel, grouped=grouped, tm=tm, nf=nf),
        grid_spec=grid_spec,
        out_shape=jax.ShapeDtypeStruct((p, d), F32),
        compiler_params=_cparams(("arbitrary", "arbitrary")),
        name="grouped_swiglu" if grouped else "dense_swiglu",
    )(tile_expert, n_valid, src, tile_rows, *args)


def _combine_kernel(pos_ref, x_ref, info_ref, y_ref, g_ref, o_ref, buf, sem, *, tt, n, norm):
    i = pl.program_id(0)

    def issue_tile(tile, slot):
        def issue(r, _):
            for k in range(2):
                _row_copy(y_ref, pos_ref[k * n + tile * tt + r], buf.at[slot, k], r, sem.at[slot]).start()
            return 0
        lax.fori_loop(0, tt, issue, 0, unroll=ISSUE_UNROLL)

    @pl.when(i == 0)
    def _():
        issue_tile(0, 0)

    @pl.when(i + 1 < pl.num_programs(0))
    def _():
        issue_tile(i + 1, (i + 1) % 2)

    slot = i % 2

    def wait(r, _):
        for k in range(2):
            _row_copy(y_ref, 0, buf.at[slot, k], r, sem.at[slot]).wait()
        return 0

    lax.fori_loop(0, tt, wait, 0)
    info = info_ref[...]
    o = x_ref[...] + (info[:, 0:1] * buf[slot, 0] + info[:, 1:2] * buf[slot, 1])
    if norm:
        o = o * lax.rsqrt(jnp.mean(o * o, axis=-1, keepdims=True) + EPS) * g_ref[...]
    o_ref[...] = o


def _combine(x2, info, y, pos, tt, gain=None):
    n, d = x2.shape
    norm = gain is not None
    grid_spec = pltpu.PrefetchScalarGridSpec(
        num_scalar_prefetch=1,
        grid=(n // tt,),
        in_specs=[pl.BlockSpec((tt, d), lambda i, ps: (i, 0)), pl.BlockSpec((tt, LANES), lambda i, ps: (i, 0)),
                  pl.BlockSpec(memory_space=pl.ANY), pl.BlockSpec((1, d), lambda i, ps: (0, 0))],
        out_specs=pl.BlockSpec((tt, d), lambda i, ps: (i, 0)),
        scratch_shapes=[pltpu.VMEM((2, 2, tt, d), F32), pltpu.SemaphoreType.DMA((2,))],
    )
    return pl.pallas_call(
        functools.partial(_combine_kernel, tt=tt, n=n, norm=norm),
        grid_spec=grid_spec,
        out_shape=jax.ShapeDtypeStruct((n, d), F32),
        compiler_params=_cparams(("arbitrary",)),
        name="moe_combine",
    )(pos, x2, info, y, (gain if norm else jnp.ones((d,), F32)).reshape(1, d))


def _moe(x2, h, info, wg, wu, wd, tm, fc, out_gain=None):
    n = h.shape[0]
    e = wg.shape[0]
    eid = jnp.concatenate([info[:, 2], info[:, 3]]).astype(jnp.int32)
    tok = jnp.tile(jnp.arange(n, dtype=jnp.int32), 2)
    onehot = (eid[:, None] == jnp.arange(e, dtype=jnp.int32)[None, :]).astype(jnp.int32)
    rank = jnp.sum((jnp.cumsum(onehot, axis=0) - 1) * onehot, axis=1)
    counts = jnp.sum(onehot, axis=0)
    padded = ((counts + tm - 1) // tm) * tm
    ends = jnp.cumsum(padded)
    starts = ends - padded
    pos = starts[eid] + rank
    n_tiles = (2 * n) // tm + e
    p = n_tiles * tm
    src = jnp.zeros((p,), jnp.int32).at[pos].set(tok)
    n_valid = (ends[e - 1] // tm).astype(jnp.int32)
    tile_start = jnp.arange(n_tiles, dtype=jnp.int32) * tm
    tile_start = jnp.minimum(tile_start, (n_valid - 1) * tm)
    tile_expert = jnp.sum((tile_start[:, None] >= ends[None, :]).astype(jnp.int32), axis=1)
    tile_rows = jnp.clip((starts + counts)[tile_expert] - tile_start, 0, tm).astype(jnp.int32)
    y = _ffn(h, wg, wu, wd, tile_expert, n_valid.reshape(1), tm, fc, src=src, tile_rows=tile_rows)
    return _combine(x2, info, y, pos, _tile(n, 256), out_gain)


def _tile(total, want):
    t = min(total, want)
    assert total % t == 0
    return t


def kernel(x, w_in, b_forget, q_norm, k_norm, kv_norm, w_ukv, w_br_fox, w_br_sb, w_br_dsa, w_out, rel_bias, norm_mix, norm_ffn, w_ffn_gate, w_ffn_up, w_ffn_down, w_router, w_moe_gate, w_moe_up, w_moe_down, norm_final):
    b, s, d = x.shape
    m = b * s
    depth = w_in.shape[0]
    ta = _tile(s, 256)
    tm_norm = _tile(m, 512)
    tm = _tile(m, 1024)
    tn = 512
    fc = 256
    tm_moe = _tile(2 * m, 1024)

    bias = _bias_tiles(rel_bias, ta)
    x2 = x.reshape(m, d)
    pending = None
    for l in range(depth):
        if pending is None:
            h = _rmsnorm(x2, norm_mix[l], BF16, tm)
        else:
            x2, h = _add_rmsnorm(x2, pending, norm_mix[l], tm_norm)
            pending = None
        z, zs = _inproj(h, _relayout_w_in(w_in, l), tm, 2 * tn)
        z3 = z.reshape(b, s, Z_COLS)
        zs3 = zs.reshape(b, s, LANES)
        qn, kn, aq, ak, dk, fvt, svt, dvt = _prep(z3, zs3, q_norm[l], k_norm[l], kv_norm[l], b_forget[l],
                                                  w_ukv[l], ta)
        y_fox = _fox(qn, kn, fvt, aq, ak, z3, ta).reshape(m, W_ATT)
        y_sb = _sb(z3, svt, ta).reshape(m, W_ATT)
        y_dsa = _dsa(z3, zs3, dk, dvt, bias, ta).reshape(m, W_ATT)
        mix = _merge(y_fox, y_sb, y_dsa, w_br_fox[l], w_br_sb[l], w_br_dsa[l], z, d, tm, tn)
        x2 = _matmul_res(mix, w_out[l], x2, tm, 2 * tn)
        j = l // 2
        if l % 2 == 0:
            h = _rmsnorm(x2, norm_ffn[l], BF16, tm)
            n_tiles = m // tm
            pending = _ffn(h, w_ffn_gate[j][None], w_ffn_up[j][None], w_ffn_down[j][None],
                           jnp.zeros((n_tiles,), jnp.int32), jnp.full((1,), n_tiles, jnp.int32), tm, 2 * fc)
        else:
            h, info = _rmsnorm_router(x2, norm_ffn[l], w_router[j], tm)
            last = l == depth - 1
            x2 = _moe(x2, h, info, w_moe_gate[j], w_moe_up[j], w_moe_down[j], tm_moe, 2 * fc,
                      norm_final if last else None)
    if pending is not None:
        x2 = _rmsnorm(x2 + pending, norm_final, F32, tm_norm)
    return x2.reshape(b, s, d)
```

```python
import functools
import math

import jax
import jax.numpy as jnp
import numpy as np
from jax import lax
from jax.experimental import pallas as pl
from jax.experimental.pallas import tpu as pltpu

F32 = jnp.float32
BF16 = jnp.bfloat16

HEAD_DIM = 64
N_HEADS = 8
N_PAIRS = N_HEADS // 2
W_ATT = N_HEADS * HEAD_DIM
KV_RANK = 256
IDX_HEADS = 4
IDX_DIM = 64
TOPK_MAX = 256
N_BUCKETS = 32
MAX_DISTANCE = 128
N_EXPERTS = 8
N_BRANCH = 3
EPS = 1e-6
LANES = 128
NEG = -1e30
SB_UNDERFLOW = -104.0
ISSUE_UNROLL = 16
SUB_ROWS = 256

U_FQ, U_FK, U_FV, U_FO, U_SQ, U_SK, U_SV, U_DQ = 0, 4, 8, 12, 16, 20, 24, 28
U_DLAT, U_DIQ, U_DIK, U_SMALL, U_GATES = 32, 34, 36, 37, 40
N_UNITS = 88
Z_COLS = N_UNITS * LANES
SM_FF, SM_DIW = 0, 8

VMEM_LIMIT = 56 * 1024 * 1024


def _cparams(sem):
    return pltpu.CompilerParams(dimension_semantics=sem, vmem_limit_bytes=VMEM_LIMIT)


def _log_sigmoid(x):
    return jnp.minimum(x, 0.0) - jnp.log(1.0 + jnp.exp(-jnp.abs(x)))


def _rmsnorm_kernel(x_ref, g_ref, o_ref):
    x = x_ref[...]
    ms = jnp.mean(x * x, axis=-1, keepdims=True)
    o_ref[...] = (x * lax.rsqrt(ms + EPS) * g_ref[...]).astype(o_ref.dtype)


def _rmsnorm(x, g, out_dtype, tm):
    m, d = x.shape
    return pl.pallas_call(
        _rmsnorm_kernel,
        grid=(m // tm,),
        in_specs=[pl.BlockSpec((tm, d), lambda i: (i, 0)), pl.BlockSpec((1, d), lambda i: (0, 0))],
        out_specs=pl.BlockSpec((tm, d), lambda i: (i, 0)),
        out_shape=jax.ShapeDtypeStruct((m, d), out_dtype),
        compiler_params=_cparams(("parallel",)),
        name="rmsnorm",
    )(x, g.reshape(1, d))


def _add_rmsnorm_kernel(x_ref, y_ref, g_ref, xo_ref, h_ref):
    x = x_ref[...] + y_ref[...]
    xo_ref[...] = x
    ms = jnp.mean(x * x, axis=-1, keepdims=True)
    h_ref[...] = (x * lax.rsqrt(ms + EPS) * g_ref[...]).astype(h_ref.dtype)


def _add_rmsnorm(x, y, g, tm):
    m, d = x.shape
    blk = pl.BlockSpec((tm, d), lambda i: (i, 0))
    return pl.pallas_call(
        _add_rmsnorm_kernel,
        grid=(m // tm,),
        in_specs=[blk, blk, pl.BlockSpec((1, d), lambda i: (0, 0))],
        out_specs=[blk, blk],
        out_shape=[jax.ShapeDtypeStruct((m, d), F32), jax.ShapeDtypeStruct((m, d), BF16)],
        compiler_params=_cparams(("parallel",)),
        name="add_rmsnorm",
    )(x, y, g.reshape(1, d))


def _rmsnorm_router_kernel(x_ref, g_ref, wr_ref, o_ref, info_ref):
    x = x_ref[...]
    ms = jnp.mean(x * x, axis=-1, keepdims=True)
    h = x * lax.rsqrt(ms + EPS) * g_ref[...]
    o_ref[...] = h.astype(o_ref.dtype)
    logits = jnp.dot(h, wr_ref[...], precision=lax.Precision.HIGHEST, preferred_element_type=F32)
    lane = lax.broadcasted_iota(jnp.int32, logits.shape, 1).astype(F32)
    lg = jnp.where(lane < N_EXPERTS, logits, -jnp.inf)
    v1 = jnp.max(lg, axis=-1, keepdims=True)
    i1 = jnp.min(jnp.where(lg == v1, lane, float(LANES)), axis=-1, keepdims=True)
    lg2 = jnp.where(lane == i1, -jnp.inf, lg)
    v2 = jnp.max(lg2, axis=-1, keepdims=True)
    i2 = jnp.min(jnp.where(lg2 == v2, lane, float(LANES)), axis=-1, keepdims=True)
    e2 = jnp.exp(v2 - v1)
    w1 = 1.0 / (1.0 + e2)
    w2 = e2 / (1.0 + e2)
    info = jnp.where(lane == 0.0, w1, jnp.where(lane == 1.0, w2, jnp.where(
        lane == 2.0, i1, jnp.where(lane == 3.0, i2, 0.0))))
    info_ref[...] = info


def _rmsnorm_router(x, g, w_router, tm):
    m, d = x.shape
    wr = jnp.pad(w_router, ((0, 0), (0, LANES - w_router.shape[1])))
    return pl.pallas_call(
        _rmsnorm_router_kernel,
        grid=(m // tm,),
        in_specs=[pl.BlockSpec((tm, d), lambda i: (i, 0)), pl.BlockSpec((1, d), lambda i: (0, 0)),
                  pl.BlockSpec((d, LANES), lambda i: (0, 0))],
        out_specs=[pl.BlockSpec((tm, d), lambda i: (i, 0)), pl.BlockSpec((tm, LANES), lambda i: (i, 0))],
        out_shape=[jax.ShapeDtypeStruct((m, d), F32), jax.ShapeDtypeStruct((m, LANES), F32)],
        compiler_params=_cparams(("parallel",)),
        name="rmsnorm_router",
    )(x, g.reshape(1, d), wr)


def _inproj_kernel(a_ref, w_ref, z_ref, zs_ref, *, small_tile):
    acc = jnp.dot(a_ref[...], w_ref[...], preferred_element_type=F32)
    z_ref[...] = acc.astype(z_ref.dtype)

    @pl.when(pl.program_id(1) == small_tile)
    def _():
        off = (U_SMALL * LANES) % acc.shape[1]
        zs_ref[...] = acc[:, off:off + LANES]


def _inproj(h, w, tm, tn):
    m, d = h.shape
    return pl.pallas_call(
        functools.partial(_inproj_kernel, small_tile=(U_SMALL * LANES) // tn),
        grid=(m // tm, Z_COLS // tn),
        in_specs=[pl.BlockSpec((tm, d), lambda i, j: (i, 0)), pl.BlockSpec((d, tn), lambda i, j: (0, j))],
        out_specs=[pl.BlockSpec((tm, tn), lambda i, j: (i, j)), pl.BlockSpec((tm, LANES), lambda i, j: (i, 0))],
        out_shape=[jax.ShapeDtypeStruct((m, Z_COLS), BF16), jax.ShapeDtypeStruct((m, LANES), F32)],
        compiler_params=_cparams(("parallel", "arbitrary")),
        name="inproj",
    )(h, w)


W_IN_ORDER = (("fq", W_ATT), ("fk", W_ATT), ("fv", W_ATT), ("ff", N_HEADS), ("fo", W_ATT),
              ("sq", W_ATT), ("sk", W_ATT), ("sv", W_ATT), ("dq", W_ATT), ("dlat", KV_RANK),
              ("diq", IDX_HEADS * IDX_DIM), ("dik", IDX_DIM), ("diw", IDX_HEADS))
RELAYOUT_COLS = 4 * LANES


def _relayout_kernel(tbl_ref, w_ref, ff_ref, o_ref, *, layer, special):
    j = pl.program_id(0)

    @pl.when(j != special)
    def _():
        o_ref[...] = jnp.transpose(w_ref[:, layer, :]).astype(o_ref.dtype)

    @pl.when(j == special)
    def _():
        xt = jnp.transpose(w_ref[0:LANES, layer, :])
        fft = jnp.transpose(ff_ref[:, layer, :])
        lane = lax.broadcasted_iota(jnp.int32, xt.shape, 1)
        dik2 = jnp.where(lane < IDX_DIM, xt, pltpu.roll(xt, IDX_DIM, axis=1))
        diw = pltpu.roll(xt, LANES - IDX_DIM + SM_DIW, axis=1)
        small = jnp.where(lane < N_HEADS, fft,
                          jnp.where((lane >= SM_DIW) & (lane < SM_DIW + IDX_HEADS), diw, 0.0))
        o_ref[:, 0:LANES] = dik2.astype(o_ref.dtype)
        o_ref[:, LANES:2 * LANES] = small.astype(o_ref.dtype)
        o_ref[:, 2 * LANES:] = jnp.zeros((o_ref.shape[0], o_ref.shape[1] - 2 * LANES), o_ref.dtype)


def _relayout_w_in(w_in, layer):
    _, d, n_in = w_in.shape
    src, o = {}, 0
    for name, width in W_IN_ORDER + (("gates", N_BRANCH * d),):
        src[name] = o
        o += width
    assert o == n_in and U_GATES * LANES + N_BRANCH * d == Z_COLS and SM_FF == 0
    assert src["diq"] == src["dlat"] + KV_RANK and src["diw"] == src["dik"] + IDX_DIM
    units = {U_FQ: "fq", U_FK: "fk", U_FV: "fv", U_FO: "fo", U_SQ: "sq", U_SK: "sk", U_SV: "sv",
             U_DQ: "dq", U_DLAT: "dlat", U_DIK: "dik"}
    per = RELAYOUT_COLS // LANES
    tbl = [src[units[u]] for u in range(0, U_GATES, per)]
    tbl += [src["gates"] + k * RELAYOUT_COLS for k in range(N_BRANCH * d // RELAYOUT_COLS)]
    elem = lambda rows: (pl.Element(rows), pl.Element(w_in.shape[0]), pl.Element(d))
    grid_spec = pltpu.PrefetchScalarGridSpec(
        num_scalar_prefetch=1,
        grid=(Z_COLS // RELAYOUT_COLS,),
        in_specs=[pl.BlockSpec(elem(RELAYOUT_COLS), lambda j, tbl: (tbl[j], 0, 0)),
                  pl.BlockSpec(elem(LANES), lambda j, tbl: (src["ff"], 0, 0))],
        out_specs=pl.BlockSpec((d, RELAYOUT_COLS), lambda j, tbl: (0, j)),
    )
    wt = jnp.transpose(w_in, (2, 0, 1))
    return pl.pallas_call(
        functools.partial(_relayout_kernel, layer=layer, special=U_DIK // per),
        grid_spec=grid_spec,
        out_shape=jax.ShapeDtypeStruct((d, Z_COLS), BF16),
        compiler_params=_cparams(("arbitrary",)),
        name="w_in_relayout",
    )(jnp.asarray(tbl, jnp.int32), wt, wt)


def _prep_kernel(zq_ref, zk_ref, zl_ref, zfv_ref, zsv_ref, zs_ref, gq_ref, gk_ref, gkv_ref, bf_ref, wukv_ref,
                 grp_ref, eq_ref, ek_ref, oneq_ref, onek_ref,
                 qn_ref, kn_ref, aq_ref, ak_ref, dk_ref, fvt_ref, svt_ref, dvt_ref, carry_ref, *, ts):
    def value_tile(v):
        return jnp.transpose(v).reshape(N_PAIRS, LANES, ts).astype(BF16)

    fvt_ref[0, :, 0] = value_tile(zfv_ref[0].astype(F32))
    svt_ref[0, :, 0] = value_tile(zsv_ref[0].astype(F32))

    @pl.when(pl.program_id(1) == 0)
    def _():
        carry_ref[...] = jnp.zeros_like(carry_ref)

    def head_norm(z_ref, g_ref):
        x = z_ref[0].astype(F32)
        sq = x * x
        hi = sq.astype(BF16)
        lo = (sq - hi.astype(F32)).astype(BF16)
        ms = (jnp.dot(hi, grp_ref[...], preferred_element_type=F32)
              + jnp.dot(lo, grp_ref[...], preferred_element_type=F32)) * (1.0 / HEAD_DIM)
        return x * lax.rsqrt(ms + EPS) * g_ref[...]

    qn_ref[0] = head_norm(zq_ref, gq_ref).astype(BF16)
    kn_ref[0] = head_norm(zk_ref, gk_ref).astype(BF16)

    lane = lax.broadcasted_iota(jnp.int32, (ts, LANES), 1)
    lf = jnp.where(lane < N_HEADS, _log_sigmoid(zs_ref[0] + bf_ref[...]), 0.0)
    r = lax.broadcasted_iota(jnp.int32, (ts, ts), 0)
    c_ = lax.broadcasted_iota(jnp.int32, (ts, ts), 1)
    tri = (c_ <= r).astype(F32)
    c = jnp.dot(tri, lf, precision=lax.Precision.HIGHEST, preferred_element_type=F32) + carry_ref[...]
    carry_ref[...] = c[ts - 1:ts, :]
    c0 = c.astype(BF16)
    r1 = c - c0.astype(F32)
    c1 = r1.astype(BF16)
    c2 = (r1 - c1.astype(F32)).astype(BF16)
    pieces = (c0, c1, c2)
    aq = oneq_ref[...]
    ak = onek_ref[...]
    for k in range(3):
        aq = aq + jnp.dot(pieces[k], eq_ref[k], preferred_element_type=F32)
        ak = ak - jnp.dot(pieces[k], ek_ref[k], preferred_element_type=F32)
    aq_ref[0] = aq.astype(BF16)
    ak_ref[0] = ak.astype(BF16)

    lat = zl_ref[0, :, :KV_RANK].astype(F32)
    msl = jnp.mean(lat * lat, axis=-1, keepdims=True)
    latn = (lat * lax.rsqrt(msl + EPS) * gkv_ref[...]).astype(BF16)
    kv = jnp.dot(latn, wukv_ref[...], preferred_element_type=F32)
    dk_ref[0] = kv[:, :W_ATT].astype(BF16)
    dvt_ref[0, :, 0] = value_tile(kv[:, W_ATT:].astype(BF16).astype(F32))


def _aug_constants():
    eq = np.zeros((3, LANES, LANES), np.float32)
    ek = np.zeros((3, LANES, LANES), np.float32)
    oneq = np.zeros((1, LANES), np.float32)
    onek = np.zeros((1, LANES), np.float32)
    for h in range(N_HEADS):
        for k in range(3):
            eq[k, h, 8 * h + k] = 1.0
            ek[k, h, 8 * h + 3 + k] = 1.0
            oneq[0, 8 * h + 3 + k] = 1.0
            onek[0, 8 * h + k] = 1.0
    grp = np.kron(np.eye(N_HEADS, dtype=np.float32), np.ones((HEAD_DIM, HEAD_DIM), np.float32))
    return (jnp.asarray(grp, BF16), jnp.asarray(eq, BF16), jnp.asarray(ek, BF16),
            jnp.asarray(oneq), jnp.asarray(onek))


def _prep(z3, zs3, q_norm, k_norm, kv_norm, b_forget, w_ukv, ts):
    b, s, _ = z3.shape
    grp, eq, ek, oneq, onek = _aug_constants()
    gq = (jnp.tile(q_norm, N_HEADS) * HEAD_DIM ** -0.5).reshape(1, W_ATT)
    gk = jnp.tile(k_norm, N_HEADS).reshape(1, W_ATT)
    bf = jnp.pad(b_forget, (SM_FF, LANES - N_HEADS - SM_FF)).reshape(1, LANES)
    const = lambda shape: pl.BlockSpec(shape, lambda bi, si: (0,) * len(shape))
    zblk = lambda unit: pl.BlockSpec((1, ts, W_ATT), lambda bi, si: (bi, si, unit // 4))
    seq_out = lambda w: pl.BlockSpec((1, ts, w), lambda bi, si: (bi, si, 0))
    vt_out = pl.BlockSpec((1, N_PAIRS, 1, LANES, ts), lambda bi, si: (bi, 0, si, 0, 0))
    vt_shape = jax.ShapeDtypeStruct((b, N_PAIRS, s // ts, LANES, ts), BF16)
    return pl.pallas_call(
        functools.partial(_prep_kernel, ts=ts),
        grid=(b, s // ts),
        in_specs=[zblk(U_FQ), zblk(U_FK), zblk(U_DLAT), zblk(U_FV), zblk(U_SV),
                  pl.BlockSpec((1, ts, LANES), lambda bi, si: (bi, si, 0)),
                  const((1, W_ATT)), const((1, W_ATT)), const((1, KV_RANK)), const((1, LANES)),
                  const((KV_RANK, 2 * W_ATT)), const((W_ATT, W_ATT)),
                  const((3, LANES, LANES)), const((3, LANES, LANES)), const((1, LANES)), const((1, LANES))],
        out_specs=[seq_out(W_ATT), seq_out(W_ATT), seq_out(LANES), seq_out(LANES), seq_out(W_ATT),
                   vt_out, vt_out, vt_out],
        out_shape=[jax.ShapeDtypeStruct((b, s, W_ATT), BF16), jax.ShapeDtypeStruct((b, s, W_ATT), BF16),
                   jax.ShapeDtypeStruct((b, s, LANES), BF16), jax.ShapeDtypeStruct((b, s, LANES), BF16),
                   jax.ShapeDtypeStruct((b, s, W_ATT), BF16), vt_shape, vt_shape, vt_shape],
        scratch_shapes=[pltpu.VMEM((1, LANES), F32)],
        compiler_params=_cparams(("parallel", "arbitrary")),
        name="mixer_prep",
    )(z3, z3, z3, z3, z3, zs3, gq, gk, kv_norm.reshape(1, KV_RANK), bf, w_ukv.astype(BF16), grp, eq, ek, oneq, onek)


def _nt_dot(a, b):
    return lax.dot_general(a, b, (((1,), (1,)), ((), ())), preferred_element_type=F32)


def _half_mask(shape, half):
    lane = lax.broadcasted_iota(jnp.int32, shape, 1)
    return (lane >= HEAD_DIM * half) & (lane < HEAD_DIM * (half + 1))


def _pair_out(acc0, acc1):
    return jnp.transpose(jnp.concatenate([acc0, acc1], axis=0))


def _online_softmax_pair(last, scores, weighted_values, mask_last, t):
    def soft(s, m, l):
        m_new = jnp.maximum(m, jnp.max(s, axis=0, keepdims=True))
        alpha = jnp.exp(m - m_new)
        p = jnp.exp(s - m_new)
        return m_new, alpha * l + jnp.sum(p, axis=0, keepdims=True), alpha, p.astype(BF16)

    def drain(j, p, alpha, acc):
        return tuple(alpha[h] * acc[h] + weighted_values(j, h, p[h]) for h in range(2))

    def body(n, c):
        s, p_prev, a_prev, m, l, acc = c
        s_next = scores(n + 1)
        acc = drain(jnp.maximum(n - 1, 0), p_prev, a_prev, acc)
        r = [soft(s[h], m[h], l[h]) for h in range(2)]
        return (s_next, (r[0][3], r[1][3]), (r[0][2], r[1][2]), (r[0][0], r[1][0]), (r[0][1], r[1][1]), acc)

    two = lambda x: (x, x)
    init = (scores(0), two(jnp.zeros((t, t), BF16)), two(jnp.ones((1, t), F32)),
            two(jnp.full((1, t), NEG, F32)), two(jnp.zeros((1, t), F32)), two(jnp.zeros((HEAD_DIM, t), F32)))
    s, p_prev, a_prev, m, l, acc = lax.fori_loop(0, last, body, init)
    acc = drain(jnp.maximum(last - 1, 0), p_prev, a_prev, acc)
    if mask_last is not None:
        s = tuple(mask_last(x) for x in s)
    r = [soft(s[h], m[h], l[h]) for h in range(2)]
    acc = drain(last, (r[0][3], r[1][3]), (r[0][2], r[1][2]), acc)
    return acc[0] / r[0][1], acc[1] / r[1][1]


def _fox_kernel(q_ref, k_ref, vt_ref, aq_ref, ak_ref, fo_ref, o_ref, *, t):
    pair = pl.program_id(1)
    i = pl.program_id(2)
    q = q_ref[0].astype(F32)
    aq = aq_ref[0].astype(F32)
    lane = lax.broadcasted_iota(jnp.int32, (t, LANES), 1)
    krow = lax.broadcasted_iota(jnp.int32, (t, t), 0)
    qcol = lax.broadcasted_iota(jnp.int32, (t, t), 1)
    causal = krow <= qcol
    qcs = []
    for half in range(2):
        head = 2 * pair + half
        qm = jnp.where(_half_mask((t, LANES), half), q, 0.0).astype(BF16)
        am = jnp.where((lane >= 8 * head) & (lane < 8 * head + 6), aq, 0.0).astype(BF16)
        qcs.append(jnp.concatenate([qm, am], axis=1))

    def scores(j):
        ks = pl.multiple_of(j * t, t)
        kc = jnp.concatenate([k_ref[0, pl.ds(ks, t), :], ak_ref[0, pl.ds(ks, t), :]], axis=1)
        return tuple(_nt_dot(kc, qcs[half]) for half in range(2))

    def weighted_values(j, half, p):
        return jnp.dot(vt_ref[0, 0, j, HEAD_DIM * half:HEAD_DIM * (half + 1), :], p, preferred_element_type=F32)

    o0, o1 = _online_softmax_pair(i, scores, weighted_values, lambda s: jnp.where(causal, s, NEG), t)
    o = _pair_out(o0, o1)
    o_ref[0] = (o * jax.nn.sigmoid(fo_ref[0].astype(F32))).astype(o_ref.dtype)


def _fox(qn, kn, vt, aq, ak, z3, t):
    b, s, _ = qn.shape
    qblk = lambda unit: pl.BlockSpec((1, t, LANES), lambda bi, p, i: (bi, i, unit + p))
    return pl.pallas_call(
        functools.partial(_fox_kernel, t=t),
        grid=(b, N_PAIRS, s // t),
        in_specs=[qblk(0),
                  pl.BlockSpec((1, s, LANES), lambda bi, p, i: (bi, 0, p)),
                  pl.BlockSpec((1, 1, s // t, LANES, t), lambda bi, p, i: (bi, p, 0, 0, 0)),
                  pl.BlockSpec((1, t, LANES), lambda bi, p, i: (bi, i, 0)),
                  pl.BlockSpec((1, s, LANES), lambda bi, p, i: (bi, 0, 0)),
                  qblk(U_FO)],
        out_specs=qblk(0),
        out_shape=jax.ShapeDtypeStruct((b, s, W_ATT), BF16),
        compiler_params=_cparams(("parallel", "parallel", "arbitrary")),
        name="fox_attention",
    )(qn, kn, vt, aq, ak, z3)


def _sb_kernel(q_ref, k_ref, vt_ref, o_ref, *, t):
    i = pl.program_id(2)
    q = q_ref[0].astype(F32) * (HEAD_DIM ** -0.5)
    krow = lax.broadcasted_iota(jnp.int32, (t, t), 0)
    qcol = lax.broadcasted_iota(jnp.int32, (t, t), 1)
    strict = krow < qcol
    after = (qcol > krow).astype(BF16)
    qms = [jnp.where(_half_mask((t, LANES), half), q, 0.0).astype(BF16) for half in range(2)]

    def local(j, diag):
        ks = pl.multiple_of(j * t, t)
        k = k_ref[0, pl.ds(ks, t), :]
        out = []
        for half in range(2):
            z = _nt_dot(k, qms[half])
            lz = _log_sigmoid(z)
            l1m = lz - z
            if diag:
                l1m = jnp.where(strict, l1m, 0.0)
            hi = l1m.astype(BF16)
            lo = (l1m - hi.astype(F32)).astype(BF16)
            suffix = (jnp.dot(after, hi, preferred_element_type=F32)
                      + jnp.dot(after, lo, preferred_element_type=F32))
            out.append((lz + suffix, jnp.sum(l1m, axis=0, keepdims=True)))
        return out

    def finish(j, loc, carry, keep):
        new = []
        for half in range(2):
            logw, colsum = loc[half]
            rsum, acc = carry[half]
            a = jnp.exp(logw + rsum)
            if keep is not None:
                a = jnp.where(keep, a, 0.0)
            pv = jnp.dot(vt_ref[0, 0, j, HEAD_DIM * half:HEAD_DIM * (half + 1), :], a.astype(BF16),
                         preferred_element_type=F32)
            new.append((rsum + colsum, acc + pv))
        return tuple(new)

    prev = jnp.maximum(i - 1, 0)
    loc_diag = local(i, True)
    loc_prev = local(prev, False)
    zero = (jnp.zeros((1, t), F32), jnp.zeros((HEAD_DIM, t), F32))
    carry = finish(i, loc_diag, (zero, zero), strict)
    carry = finish(prev, loc_prev, carry, i > 0)

    def more(c):
        n, ((r0, _), (r1, _)) = c
        return (n < i) & (jnp.maximum(jnp.max(r0), jnp.max(r1)) > SB_UNDERFLOW)

    def body(c):
        n, carry = c
        j = i - 1 - n
        return n + 1, finish(j, local(j, False), carry, None)

    _, ((_, acc0), (_, acc1)) = lax.while_loop(more, body, (jnp.int32(1), carry))
    o_ref[0] = _pair_out(acc0, acc1).astype(o_ref.dtype)


def _sb(z3, vt, t):
    b, s, _ = z3.shape
    return pl.pallas_call(
        functools.partial(_sb_kernel, t=t),
        grid=(b, N_PAIRS, s // t),
        in_specs=[pl.BlockSpec((1, t, LANES), lambda bi, p, i: (bi, i, U_SQ + p)),
                  pl.BlockSpec((1, s, LANES), lambda bi, p, i: (bi, 0, U_SK + p)),
                  pl.BlockSpec((1, 1, s // t, LANES, t), lambda bi, p, i: (bi, p, 0, 0, 0))],
        out_specs=pl.BlockSpec((1, t, LANES), lambda bi, p, i: (bi, i, p)),
        out_shape=jax.ShapeDtypeStruct((b, s, W_ATT), BF16),
        compiler_params=_cparams(("parallel", "parallel", "arbitrary")),
        name="stickbreak_attention",
    )(z3, z3, vt)


def _t5_bucket(n):
    max_exact = N_BUCKETS // 2
    nf = jnp.maximum(n, 1).astype(F32)
    large = max_exact + (jnp.log(nf / max_exact) / math.log(MAX_DISTANCE / max_exact)
                         * (N_BUCKETS - max_exact)).astype(jnp.int32)
    large = jnp.minimum(large, N_BUCKETS - 1)
    return jnp.where(n < max_exact, n, large)


def _bias_tiles_kernel(relb_ref, o_ref, *, t):
    h = pl.program_id(0)
    krow = lax.broadcasted_iota(jnp.int32, (t, t), 0)
    qcol = lax.broadcasted_iota(jnp.int32, (t, t), 1)
    o_ref[0, 0] = jnp.full((t, t), relb_ref[N_BUCKETS - 1, h], F32)
    for slot, shift in ((1, t), (2, 0)):
        bucket = _t5_bucket(jnp.maximum(qcol - krow + shift, 0))
        val = jnp.full((t, t), relb_ref[0, h], F32)
        for k in range(1, N_BUCKETS):
            val = jnp.where(bucket == k, relb_ref[k, h], val)
        o_ref[0, slot] = val


def _bias_tiles(rel_bias, t):
    assert t >= MAX_DISTANCE
    return pl.pallas_call(
        functools.partial(_bias_tiles_kernel, t=t),
        grid=(N_HEADS,),
        in_specs=[pl.BlockSpec(memory_space=pltpu.SMEM)],
        out_specs=pl.BlockSpec((1, 3, t, t), lambda h: (h, 0, 0, 0)),
        out_shape=jax.ShapeDtypeStruct((N_HEADS, 3, t, t), F32),
        compiler_params=_cparams(("arbitrary",)),
        name="t5_bias_tiles",
    )(rel_bias)


def _dsa_kernel(dq_ref, qi_ref, zs_ref, kidx_ref, dk_ref, dvt_ref, bias_ref, o_ref, key_ref, hi_ref, lo_ref,
                madd_ref, *, t, n_sel):
    i = pl.program_id(1)
    nch = i + 1
    krow = lax.broadcasted_iota(jnp.int32, (t, t), 0)
    qcol = lax.broadcasted_iota(jnp.int32, (t, t), 1)
    idx_scale = (IDX_DIM ** -0.5) * (IDX_HEADS ** -0.5)

    zst = jnp.transpose(zs_ref[0])
    qi = qi_ref[0].astype(F32)
    qih, wih = [], []
    for h in range(IDX_HEADS):
        blk = qi[:, (h // 2) * LANES:(h // 2 + 1) * LANES]
        qih.append(jnp.where(_half_mask((t, LANES), h % 2), blk, 0.0).astype(BF16))
        wih.append(zst[SM_DIW + h:SM_DIW + h + 1, :] * idx_scale)

    def score_chunk(j, _):
        ks = pl.multiple_of(j * t, t)
        kc = kidx_ref[0, pl.ds(ks, t), :]
        sc = jnp.zeros((t, t), F32)
        for h in range(IDX_HEADS):
            sc = sc + jnp.maximum(_nt_dot(kc, qih[h]), 0.0) * wih[h]
        sc = jnp.where(sc == 0.0, 0.0, sc)
        sc = jnp.where(j * t + krow <= i * t + qcol, sc, -jnp.inf)
        bits = pltpu.bitcast(sc, jnp.int32)
        key = jnp.where(bits < 0, bits ^ jnp.int32(0x7FFFFFFF), bits)
        key_ref[j] = key
        hi_ref[j] = (key >> 16).astype(jnp.int16)
        return 0

    lax.fori_loop(0, nch, score_chunk, 0)

    i16 = jnp.int16
    rows16 = 16
    lowest = jnp.full((t, t), -32768, i16)
    npairs = (nch + 1) // 2

    @pl.when(nch % 2 == 1)
    def _():
        hi_ref[nch] = lowest
        lo_ref[nch] = lowest

    def count16(ref, pred):
        def body(j2, c):
            pieces = []
            for j in (2 * j2, 2 * j2 + 1):
                ind = jnp.where(pred(ref[j]), i16(1), i16(0))
                pieces += [ind[r * rows16:(r + 1) * rows16] for r in range(t // rows16)]
            while len(pieces) > 1:
                pieces = [a + b for a, b in zip(pieces[0::2], pieces[1::2])]
            return c + pieces[0]
        per_lane = lax.fori_loop(0, npairs, body, jnp.zeros((rows16, t), i16))
        return jnp.sum(per_lane.astype(F32), axis=0, keepdims=True)

    def search16(ref, base):
        def bit_step(n, thr):
            cand = thr + lax.shift_left(jnp.int32(1), 15 - n)
            c16 = cand.astype(i16)
            cnt = base + count16(ref, lambda v: v >= c16)
            return jnp.where(cnt >= float(n_sel), cand, thr)
        return lax.fori_loop(0, 16, bit_step, jnp.full((1, t), -32768, jnp.int32))

    thr_hi = search16(hi_ref, 0.0)
    thr_hi16 = thr_hi.astype(i16)
    above = count16(hi_ref, lambda v: v > thr_hi16)

    def low_halves(j, _):
        low = ((key_ref[j] & 0xFFFF) - 32768).astype(i16)
        lo_ref[j] = jnp.where(hi_ref[j] == thr_hi16, low, i16(-32768))
        return 0

    lax.fori_loop(0, nch, low_halves, 0)
    thr_lo = search16(lo_ref, above)
    thr = lax.shift_left(thr_hi, 16) | (thr_lo + 32768)

    def count_keys(pred):
        def body(j, c):
            return c + jnp.sum(jnp.where(pred(key_ref[j]), 1.0, 0.0), axis=0, keepdims=True)
        return lax.fori_loop(0, nch, body, jnp.zeros((1, t), F32))

    need = float(n_sel) - count_keys(lambda k: k > thr)

    upto = (qcol <= krow).astype(BF16)

    def mask_chunk(j, seen):
        key = key_ref[j]
        eq = key == thr
        rank = jnp.dot(upto, jnp.where(eq, 1.0, 0.0).astype(BF16), preferred_element_type=F32) + seen
        sel = (key > thr) | (eq & (rank <= need))
        sel = sel & (j * t + krow <= i * t + qcol)
        madd_ref[j] = jnp.where(sel, 0.0, NEG)
        return rank[t - 1:t, :]

    lax.fori_loop(0, nch, mask_chunk, jnp.zeros((1, t), F32))

    for pair in range(N_PAIRS):
        lo, hi_ = pair * LANES, (pair + 1) * LANES
        qp = dq_ref[0, :, lo:hi_].astype(F32) * (HEAD_DIM ** -0.5)
        qms = [jnp.where(_half_mask((t, LANES), half), qp, 0.0).astype(BF16) for half in range(2)]

        def scores(j, pair=pair, lo=lo, hi_=hi_, qms=qms):
            ks = pl.multiple_of(j * t, t)
            slot = jnp.clip(j - i + 2, 0, 2)
            k = dk_ref[0, pl.ds(ks, t), lo:hi_]
            madd = madd_ref[j]
            return tuple(_nt_dot(k, qms[half]) + bias_ref[2 * pair + half, slot] + madd for half in range(2))

        def weighted_values(j, half, p, pair=pair):
            return jnp.dot(dvt_ref[0, pair, j, HEAD_DIM * half:HEAD_DIM * (half + 1), :], p,
                           preferred_element_type=F32)

        o0, o1 = _online_softmax_pair(i, scores, weighted_values, None, t)
        o_ref[0, :, lo:hi_] = _pair_out(o0, o1).astype(o_ref.dtype)


def _dsa(z3, zs3, dk, dvt, bias, t):
    b, s, _ = z3.shape
    n_sel = min(TOPK_MAX, s // 4)
    nt = s // t
    return pl.pallas_call(
        functools.partial(_dsa_kernel, t=t, n_sel=n_sel),
        grid=(b, s // t),
        in_specs=[pl.BlockSpec((1, t, W_ATT), lambda bi, i: (bi, i, U_DQ // 4)),
                  pl.BlockSpec((1, t, 2 * LANES), lambda bi, i: (bi, i, U_DIQ // 2)),
                  pl.BlockSpec((1, t, LANES), lambda bi, i: (bi, i, 0)),
                  pl.BlockSpec((1, s, LANES), lambda bi, i: (bi, 0, U_DIK)),
                  pl.BlockSpec((1, s, W_ATT), lambda bi, i: (bi, 0, 0)),
                  pl.BlockSpec((1, N_PAIRS, s // t, LANES, t), lambda bi, i: (bi, 0, 0, 0, 0)),
                  pl.BlockSpec((N_HEADS, 3, t, t), lambda bi, i: (0, 0, 0, 0))],
        out_specs=pl.BlockSpec((1, t, W_ATT), lambda bi, i: (bi, i, 0)),
        out_shape=jax.ShapeDtypeStruct((b, s, W_ATT), BF16),
        scratch_shapes=[pltpu.VMEM((nt, t, t), jnp.int32), pltpu.VMEM((nt + nt % 2, t, t), jnp.int16),
                        pltpu.VMEM((nt + nt % 2, t, t), jnp.int16), pltpu.VMEM((nt, t, t), F32)],
        compiler_params=_cparams(("parallel", "arbitrary")),
        name="dsa_attention",
    )(z3, z3, zs3, z3, dk, dvt, bias)


def _merge_kernel(yf_ref, ys_ref, yd_ref, wf_ref, ws_ref, wd_ref, g0_ref, g1_ref, g2_ref, o_ref):
    def branch(y_ref, w_ref, g_ref):
        proj = jnp.dot(y_ref[...], w_ref[...].astype(BF16), preferred_element_type=F32)
        gate = 0.5 * jnp.tanh(0.5 * g_ref[...].astype(F32)) + 0.5
        return gate * proj

    o_ref[...] = (branch(yf_ref, wf_ref, g0_ref) + branch(ys_ref, ws_ref, g1_ref)
                  + branch(yd_ref, wd_ref, g2_ref)).astype(o_ref.dtype)


def _merge(yf, ys, yd, wf, ws, wd, z, d, tm, tn):
    m = yf.shape[0]
    yblk = pl.BlockSpec((tm, W_ATT), lambda i, j: (i, 0))
    wblk = pl.BlockSpec((W_ATT, tn), lambda i, j: (0, j))
    gblk = lambda g: pl.BlockSpec((tm, tn), lambda i, j: (i, (U_GATES * LANES + g * d) // tn + j))
    return pl.pallas_call(
        _merge_kernel,
        grid=(m // tm, d // tn),
        in_specs=[yblk, yblk, yblk, wblk, wblk, wblk, gblk(0), gblk(1), gblk(2)],
        out_specs=pl.BlockSpec((tm, tn), lambda i, j: (i, j)),
        out_shape=jax.ShapeDtypeStruct((m, d), BF16),
        compiler_params=_cparams(("parallel", "arbitrary")),
        name="branch_merge",
    )(yf, ys, yd, wf, ws, wd, z, z, z)


def _matmul_res_kernel(a_ref, w_ref, r_ref, o_ref):
    o_ref[...] = r_ref[...] + jnp.dot(a_ref[...], w_ref[...].astype(BF16), preferred_element_type=F32)


def _matmul_res(a, w, res, tm, tn):
    m, k = a.shape
    n = w.shape[1]
    return pl.pallas_call(
        _matmul_res_kernel,
        grid=(m // tm, n // tn),
        in_specs=[pl.BlockSpec((tm, k), lambda i, j: (i, 0)), pl.BlockSpec((k, tn), lambda i, j: (0, j)),
                  pl.BlockSpec((tm, tn), lambda i, j: (i, j))],
        out_specs=pl.BlockSpec((tm, tn), lambda i, j: (i, j)),
        out_shape=jax.ShapeDtypeStruct((m, n), F32),
        compiler_params=_cparams(("parallel", "arbitrary")),
        name="out_proj_residual",
    )(a, w, res)


def _row_copy(src_hbm, row, dst_vmem, slot, sem):
    return pltpu.make_async_copy(src_hbm.at[pl.ds(row, 1)], dst_vmem.at[pl.ds(slot, 1)], sem)


def _ffn_kernel(te_ref, nv_ref, src_ref, rows_ref, x_ref, wg_ref, wu_ref, wd_ref, *rest, grouped, tm, nf):
    ti = pl.program_id(0)
    f = pl.program_id(1)
    if grouped:
        o_ref, gbuf, xs, sem = rest
    else:
        (o_ref,) = rest

    @pl.when(f == 0)
    def _():
        o_ref[...] = jnp.zeros_like(o_ref)

    if grouped:
        steps = max(k for k in range(1, nf) if tm % k == 0)
        per = tm // steps

        def issue_rows(tile, lo, n):
            def issue(r, _):
                _row_copy(x_ref, src_ref[tile * tm + lo + r], gbuf, lo + r, sem).start()
                return 0
            lax.fori_loop(0, n, issue, 0, unroll=ISSUE_UNROLL)

        @pl.when((ti == 0) & (f == 0))
        def _():
            issue_rows(0, 0, tm)

        @pl.when((f == 0) & (ti < nv_ref[0]))
        def _():
            def wait(r, _):
                _row_copy(x_ref, 0, gbuf, r, sem).wait()
                return 0
            lax.fori_loop(0, tm, wait, 0, unroll=ISSUE_UNROLL)
            xs[...] = gbuf[...].astype(BF16)

        @pl.when((f > 0) & (f <= steps) & (ti + 1 < nv_ref[0]))
        def _():
            issue_rows(ti + 1, (f - 1) * per, per)

    def swiglu(x, wgate, wup, wdown):
        g = jnp.dot(x, wgate, preferred_element_type=F32)
        u = jnp.dot(x, wup, preferred_element_type=F32)
        a = (g * jax.nn.sigmoid(g) * u).astype(BF16)
        return jnp.dot(a, wdown, preferred_element_type=F32)

    def whole_tile(x):
        o_ref[...] += swiglu(x, wg_ref[0].astype(BF16), wu_ref[0].astype(BF16), wd_ref[0].astype(BF16))

    @pl.when(ti < nv_ref[0])
    def _():
        if not grouped:
            whole_tile(x_ref[...])
            return
        rows = rows_ref[ti]

        @pl.when(rows > tm - SUB_ROWS)
        def _():
            whole_tile(xs[...])

        @pl.when(rows <= tm - SUB_ROWS)
        def _():
            for lo in range(0, tm, SUB_ROWS):
                @pl.when(lo < rows)
                def _(lo=lo):
                    o_ref[lo:lo + SUB_ROWS, :] += swiglu(
                        xs[lo:lo + SUB_ROWS, :], wg_ref[0].astype(BF16), wu_ref[0].astype(BF16),
                        wd_ref[0].astype(BF16))


def _ffn(x, wg, wu, wd, tile_expert, n_valid, tm, fc, src=None, tile_rows=None):
    grouped = src is not None
    d = x.shape[1]
    p = src.shape[0] if grouped else x.shape[0]
    nf = wg.shape[2] // fc

    def chunk(ti, f, nv):
        return jnp.where(ti < nv[0], f, nf - 1)

    wspecs = [pl.BlockSpec((1, d, fc), lambda ti, f, te, nv, sr, rw: (te[ti], 0, chunk(ti, f, nv))),
              pl.BlockSpec((1, d, fc), lambda ti, f, te, nv, sr, rw: (te[ti], 0, chunk(ti, f, nv))),
              pl.BlockSpec((1, fc, d), lambda ti, f, te, nv, sr, rw: (te[ti], chunk(ti, f, nv), 0))]
    scratch = []
    if grouped:
        in_specs = [pl.BlockSpec(memory_space=pl.ANY)] + wspecs
        args = [x, wg, wu, wd]
        scratch += [pltpu.VMEM((tm, d), F32), pltpu.VMEM((tm, d), BF16), pltpu.SemaphoreType.DMA(())]
    else:
        in_specs = [pl.BlockSpec((tm, d), lambda ti, f, te, nv, sr, rw: (ti, 0))] + wspecs
        args = [x, wg, wu, wd]
        src = tile_rows = jnp.zeros((1,), jnp.int32)
    grid_spec = pltpu.PrefetchScalarGridSpec(
        num_scalar_prefetch=4,
        grid=(p // tm, nf),
        in_specs=in_specs,
        out_specs=pl.BlockSpec((tm, d), lambda ti, f, te, nv, sr, rw: (ti, 0),
                               **({"pipeline_mode": pl.Buffered(1)} if grouped else {})),
        scratch_shapes=scratch,
    )
    return pl.pallas_call(
        functools.partial(_ffn_kernel, grouped=grouped, tm=tm, nf=nf),
        grid_spec=grid_spec,
        out_shape=jax.ShapeDtypeStruct((p, d), F32),
        compiler_params=_cparams(("arbitrary", "arbitrary")),
        name="grouped_swiglu" if grouped else "dense_swiglu",
    )(tile_expert, n_valid, src, tile_rows, *args)


def _combine_kernel(pos_ref, x_ref, info_ref, y_ref, g_ref, o_ref, buf, sem, *, tt, n, norm):
    i = pl.program_id(0)

    def issue_tile(tile, slot):
        def issue(r, _):
            for k in range(2):
                _row_copy(y_ref, pos_ref[k * n + tile * tt + r], buf.at[slot, k], r, sem.at[slot]).start()
            return 0
        lax.fori_loop(0, tt, issue, 0, unroll=ISSUE_UNROLL)

    @pl.when(i == 0)
    def _():
        issue_tile(0, 0)

    @pl.when(i + 1 < pl.num_programs(0))
    def _():
        issue_tile(i + 1, (i + 1) % 2)

    slot = i % 2

    def wait(r, _):
        for k in range(2):
            _row_copy(y_ref, 0, buf.at[slot, k], r, sem.at[slot]).wait()
        return 0

    lax.fori_loop(0, tt, wait, 0)
    info = info_ref[...]
    o = x_ref[...] + (info[:, 0:1] * buf[slot, 0] + info[:, 1:2] * buf[slot, 1])
    if norm:
        o = o * lax.rsqrt(jnp.mean(o * o, axis=-1, keepdims=True) + EPS) * g_ref[...]
    o_ref[...] = o


def _combine(x2, info, y, pos, tt, gain=None):
    n, d = x2.shape
    norm = gain is not None
    grid_spec = pltpu.PrefetchScalarGridSpec(
        num_scalar_prefetch=1,
        grid=(n // tt,),
        in_specs=[pl.BlockSpec((tt, d), lambda i, ps: (i, 0)), pl.BlockSpec((tt, LANES), lambda i, ps: (i, 0)),
                  pl.BlockSpec(memory_space=pl.ANY), pl.BlockSpec((1, d), lambda i, ps: (0, 0))],
        out_specs=pl.BlockSpec((tt, d), lambda i, ps: (i, 0)),
        scratch_shapes=[pltpu.VMEM((2, 2, tt, d), F32), pltpu.SemaphoreType.DMA((2,))],
    )
    return pl.pallas_call(
        functools.partial(_combine_kernel, tt=tt, n=n, norm=norm),
        grid_spec=grid_spec,
        out_shape=jax.ShapeDtypeStruct((n, d), F32),
        compiler_params=_cparams(("arbitrary",)),
        name="moe_combine",
    )(pos, x2, info, y, (gain if norm else jnp.ones((d,), F32)).reshape(1, d))


def _moe(x2, h, info, wg, wu, wd, tm, fc, out_gain=None):
    n = h.shape[0]
    e = wg.shape[0]
    eid = jnp.concatenate([info[:, 2], info[:, 3]]).astype(jnp.int32)
    tok = jnp.tile(jnp.arange(n, dtype=jnp.int32), 2)
    onehot = (eid[:, None] == jnp.arange(e, dtype=jnp.int32)[None, :]).astype(jnp.int32)
    rank = jnp.sum((jnp.cumsum(onehot, axis=0) - 1) * onehot, axis=1)
    counts = jnp.sum(onehot, axis=0)
    padded = ((counts + tm - 1) // tm) * tm
    ends = jnp.cumsum(padded)
    starts = ends - padded
    pos = starts[eid] + rank
    n_tiles = (2 * n) // tm + e
    p = n_tiles * tm
    src = jnp.zeros((p,), jnp.int32).at[pos].set(tok)
    n_valid = (ends[e - 1] // tm).astype(jnp.int32)
    tile_start = jnp.arange(n_tiles, dtype=jnp.int32) * tm
    tile_start = jnp.minimum(tile_start, (n_valid - 1) * tm)
    tile_expert = jnp.sum((tile_start[:, None] >= ends[None, :]).astype(jnp.int32), axis=1)
    tile_rows = jnp.clip((starts + counts)[tile_expert] - tile_start, 0, tm).astype(jnp.int32)
    y = _ffn(h, wg, wu, wd, tile_expert, n_valid.reshape(1), tm, fc, src=src, tile_rows=tile_rows)
    return _combine(x2, info, y, pos, _tile(n, 256), out_gain)


def _tile(total, want):
    t = min(total, want)
    assert total % t == 0
    return t


def kernel(x, w_in, b_forget, q_norm, k_norm, kv_norm, w_ukv, w_br_fox, w_br_sb, w_br_dsa, w_out, rel_bias, norm_mix, norm_ffn, w_ffn_gate, w_ffn_up, w_ffn_down, w_router, w_moe_gate, w_moe_up, w_moe_down, norm_final):
    b, s, d = x.shape
    m = b * s
    depth = w_in.shape[0]
    ta = _tile(s, 256)
    tm_norm = _tile(m, 512)
    tm = _tile(m, 1024)
    tn = 512
    fc = 256
    tm_moe = _tile(2 * m, 1024)

    bias = _bias_tiles(rel_bias, ta)
    x2 = x.reshape(m, d)
    pending = None
    for l in range(depth):
        if pending is None:
            h = _rmsnorm(x2, norm_mix[l], BF16, tm)
        else:
            x2, h = _add_rmsnorm(x2, pending, norm_mix[l], tm_norm)
            pending = None
        z, zs = _inproj(h, _relayout_w_in(w_in, l), tm, 2 * tn)
        z3 = z.reshape(b, s, Z_COLS)
        zs3 = zs.reshape(b, s, LANES)
        qn, kn, aq, ak, dk, fvt, svt, dvt = _prep(z3, zs3, q_norm[l], k_norm[l], kv_norm[l], b_forget[l],
                                                  w_ukv[l], ta)
        y_fox = _fox(qn, kn, fvt, aq, ak, z3, ta).reshape(m, W_ATT)
        y_sb = _sb(z3, svt, ta).reshape(m, W_ATT)
        y_dsa = _dsa(z3, zs3, dk, dvt, bias, ta).reshape(m, W_ATT)
        mix = _merge(y_fox, y_sb, y_dsa, w_br_fox[l], w_br_sb[l], w_br_dsa[l], z, d, tm, tn)
        x2 = _matmul_res(mix, w_out[l], x2, tm, 2 * tn)
        j = l // 2
        if l % 2 == 0:
            h = _rmsnorm(x2, norm_ffn[l], BF16, tm)
            n_tiles = m // tm
            pending = _ffn(h, w_ffn_gate[j][None], w_ffn_up[j][None], w_ffn_down[j][None],
                           jnp.zeros((n_tiles,), jnp.int32), jnp.full((1,), n_tiles, jnp.int32), tm, 2 * fc)
        else:
            h, info = _rmsnorm_router(x2, norm_ffn[l], w_router[j], tm)
            last = l == depth - 1
            x2 = _moe(x2, h, info, w_moe_gate[j], w_moe_up[j], w_moe_down[j], tm_moe, 2 * fc,
                      norm_final if last else None)
    if pending is not None:
        x2 = _rmsnorm(x2 + pending, norm_final, F32, tm_norm)
    return x2.reshape(b, s, d)
```

```python
import functools
import math

import jax
import jax.numpy as jnp
import numpy as np
from jax import lax
from jax.experimental import pallas as pl
from jax.experimental.pallas import tpu as pltpu

F32 = jnp.float32
BF16 = jnp.bfloat16

HEAD_DIM = 64
N_HEADS = 8
N_PAIRS = N_HEADS // 2
W_ATT = N_HEADS * HEAD_DIM
KV_RANK = 256
IDX_HEADS = 4
IDX_DIM = 64
TOPK_MAX = 256
N_BUCKETS = 32
MAX_DISTANCE = 128
N_EXPERTS = 8
N_BRANCH = 3
EPS = 1e-6
LOG2E = 1.4426950408889634
LANES = 128
NEG = -1e30
SB_UNDERFLOW = -104.0
ISSUE_UNROLL = 16
SUB_ROWS = 256

U_FQ, U_FK, U_FV, U_FO, U_SQ, U_SK, U_SV, U_DQ = 0, 4, 8, 12, 16, 20, 24, 28
U_DLAT, U_DIQ, U_DIK, U_SMALL, U_GATES = 32, 34, 36, 37, 40
N_UNITS = 88
Z_COLS = N_UNITS * LANES
SM_FF, SM_DIW = 0, 8

VMEM_LIMIT = 56 * 1024 * 1024


def _cparams(sem):
    return pltpu.CompilerParams(dimension_semantics=sem, vmem_limit_bytes=VMEM_LIMIT)


def _log_sigmoid(x):
    return jnp.minimum(x, 0.0) - jnp.log(1.0 + jnp.exp(-jnp.abs(x)))


def _rmsnorm_kernel(x_ref, g_ref, o_ref):
    x = x_ref[...]
    ms = jnp.mean(x * x, axis=-1, keepdims=True)
    o_ref[...] = (x * lax.rsqrt(ms + EPS) * g_ref[...]).astype(o_ref.dtype)


def _rmsnorm(x, g, out_dtype, tm):
    m, d = x.shape
    return pl.pallas_call(
        _rmsnorm_kernel,
        grid=(m // tm,),
        in_specs=[pl.BlockSpec((tm, d), lambda i: (i, 0)), pl.BlockSpec((1, d), lambda i: (0, 0))],
        out_specs=pl.BlockSpec((tm, d), lambda i: (i, 0)),
        out_shape=jax.ShapeDtypeStruct((m, d), out_dtype),
        compiler_params=_cparams(("parallel",)),
        name="rmsnorm",
    )(x, g.reshape(1, d))


def _add_rmsnorm_kernel(x_ref, y_ref, g_ref, xo_ref, h_ref):
    x = x_ref[...] + y_ref[...]
    xo_ref[...] = x
    ms = jnp.mean(x * x, axis=-1, keepdims=True)
    h_ref[...] = (x * lax.rsqrt(ms + EPS) * g_ref[...]).astype(h_ref.dtype)


def _add_rmsnorm(x, y, g, tm):
    m, d = x.shape
    blk = pl.BlockSpec((tm, d), lambda i: (i, 0))
    return pl.pallas_call(
        _add_rmsnorm_kernel,
        grid=(m // tm,),
        in_specs=[blk, blk, pl.BlockSpec((1, d), lambda i: (0, 0))],
        out_specs=[blk, blk],
        out_shape=[jax.ShapeDtypeStruct((m, d), F32), jax.ShapeDtypeStruct((m, d), BF16)],
        compiler_params=_cparams(("parallel",)),
        name="add_rmsnorm",
    )(x, y, g.reshape(1, d))


def _rmsnorm_router_kernel(x_ref, g_ref, wr_ref, o_ref, info_ref):
    x = x_ref[...]
    ms = jnp.mean(x * x, axis=-1, keepdims=True)
    h = x * lax.rsqrt(ms + EPS) * g_ref[...]
    o_ref[...] = h.astype(o_ref.dtype)
    logits = jnp.dot(h, wr_ref[...], precision=lax.Precision.HIGHEST, preferred_element_type=F32)
    lane = lax.broadcasted_iota(jnp.int32, logits.shape, 1).astype(F32)
    lg = jnp.where(lane < N_EXPERTS, logits, -jnp.inf)
    v1 = jnp.max(lg, axis=-1, keepdims=True)
    i1 = jnp.min(jnp.where(lg == v1, lane, float(LANES)), axis=-1, keepdims=True)
    lg2 = jnp.where(lane == i1, -jnp.inf, lg)
    v2 = jnp.max(lg2, axis=-1, keepdims=True)
    i2 = jnp.min(jnp.where(lg2 == v2, lane, float(LANES)), axis=-1, keepdims=True)
    e2 = jnp.exp(v2 - v1)
    w1 = 1.0 / (1.0 + e2)
    w2 = e2 / (1.0 + e2)
    info = jnp.where(lane == 0.0, w1, jnp.where(lane == 1.0, w2, jnp.where(
        lane == 2.0, i1, jnp.where(lane == 3.0, i2, 0.0))))
    info_ref[...] = info


def _rmsnorm_router(x, g, w_router, tm):
    m, d = x.shape
    wr = jnp.pad(w_router, ((0, 0), (0, LANES - w_router.shape[1])))
    return pl.pallas_call(
        _rmsnorm_router_kernel,
        grid=(m // tm,),
        in_specs=[pl.BlockSpec((tm, d), lambda i: (i, 0)), pl.BlockSpec((1, d), lambda i: (0, 0)),
                  pl.BlockSpec((d, LANES), lambda i: (0, 0))],
        out_specs=[pl.BlockSpec((tm, d), lambda i: (i, 0)), pl.BlockSpec((tm, LANES), lambda i: (i, 0))],
        out_shape=[jax.ShapeDtypeStruct((m, d), F32), jax.ShapeDtypeStruct((m, LANES), F32)],
        compiler_params=_cparams(("parallel",)),
        name="rmsnorm_router",
    )(x, g.reshape(1, d), wr)


def _inproj_kernel(a_ref, w_ref, z_ref, zs_ref, *, small_tile):
    acc = jnp.dot(a_ref[...], w_ref[...], preferred_element_type=F32)
    z_ref[...] = acc.astype(z_ref.dtype)

    @pl.when(pl.program_id(1) == small_tile)
    def _():
        off = (U_SMALL * LANES) % acc.shape[1]
        zs_ref[...] = acc[:, off:off + LANES]


def _inproj(h, w, tm, tn):
    m, d = h.shape
    return pl.pallas_call(
        functools.partial(_inproj_kernel, small_tile=(U_SMALL * LANES) // tn),
        grid=(m // tm, Z_COLS // tn),
        in_specs=[pl.BlockSpec((tm, d), lambda i, j: (i, 0)), pl.BlockSpec((d, tn), lambda i, j: (0, j))],
        out_specs=[pl.BlockSpec((tm, tn), lambda i, j: (i, j)), pl.BlockSpec((tm, LANES), lambda i, j: (i, 0))],
        out_shape=[jax.ShapeDtypeStruct((m, Z_COLS), BF16), jax.ShapeDtypeStruct((m, LANES), F32)],
        compiler_params=_cparams(("parallel", "arbitrary")),
        name="inproj",
    )(h, w)


W_IN_ORDER = (("fq", W_ATT), ("fk", W_ATT), ("fv", W_ATT), ("ff", N_HEADS), ("fo", W_ATT),
              ("sq", W_ATT), ("sk", W_ATT), ("sv", W_ATT), ("dq", W_ATT), ("dlat", KV_RANK),
              ("diq", IDX_HEADS * IDX_DIM), ("dik", IDX_DIM), ("diw", IDX_HEADS))
RELAYOUT_COLS = 4 * LANES


def _relayout_kernel(tbl_ref, w_ref, ff_ref, o_ref, *, layer, special):
    j = pl.program_id(0)

    @pl.when(j != special)
    def _():
        o_ref[...] = jnp.transpose(w_ref[:, layer, :]).astype(o_ref.dtype)

    @pl.when(j == special)
    def _():
        xt = jnp.transpose(w_ref[0:LANES, layer, :])
        fft = jnp.transpose(ff_ref[:, layer, :])
        lane = lax.broadcasted_iota(jnp.int32, xt.shape, 1)
        dik2 = jnp.where(lane < IDX_DIM, xt, pltpu.roll(xt, IDX_DIM, axis=1))
        diw = pltpu.roll(xt, LANES - IDX_DIM + SM_DIW, axis=1)
        small = jnp.where(lane < N_HEADS, fft,
                          jnp.where((lane >= SM_DIW) & (lane < SM_DIW + IDX_HEADS), diw, 0.0))
        o_ref[:, 0:LANES] = dik2.astype(o_ref.dtype)
        o_ref[:, LANES:2 * LANES] = small.astype(o_ref.dtype)
        o_ref[:, 2 * LANES:] = jnp.zeros((o_ref.shape[0], o_ref.shape[1] - 2 * LANES), o_ref.dtype)


def _relayout_w_in(w_in, layer):
    _, d, n_in = w_in.shape
    src, o = {}, 0
    for name, width in W_IN_ORDER + (("gates", N_BRANCH * d),):
        src[name] = o
        o += width
    assert o == n_in and U_GATES * LANES + N_BRANCH * d == Z_COLS and SM_FF == 0
    assert src["diq"] == src["dlat"] + KV_RANK and src["diw"] == src["dik"] + IDX_DIM
    units = {U_FQ: "fq", U_FK: "fk", U_FV: "fv", U_FO: "fo", U_SQ: "sq", U_SK: "sk", U_SV: "sv",
             U_DQ: "dq", U_DLAT: "dlat", U_DIK: "dik"}
    per = RELAYOUT_COLS // LANES
    tbl = [src[units[u]] for u in range(0, U_GATES, per)]
    tbl += [src["gates"] + k * RELAYOUT_COLS for k in range(N_BRANCH * d // RELAYOUT_COLS)]
    elem = lambda rows: (pl.Element(rows), pl.Element(w_in.shape[0]), pl.Element(d))
    grid_spec = pltpu.PrefetchScalarGridSpec(
        num_scalar_prefetch=1,
        grid=(Z_COLS // RELAYOUT_COLS,),
        in_specs=[pl.BlockSpec(elem(RELAYOUT_COLS), lambda j, tbl: (tbl[j], 0, 0)),
                  pl.BlockSpec(elem(LANES), lambda j, tbl: (src["ff"], 0, 0))],
        out_specs=pl.BlockSpec((d, RELAYOUT_COLS), lambda j, tbl: (0, j)),
    )
    wt = jnp.transpose(w_in, (2, 0, 1))
    return pl.pallas_call(
        functools.partial(_relayout_kernel, layer=layer, special=U_DIK // per),
        grid_spec=grid_spec,
        out_shape=jax.ShapeDtypeStruct((d, Z_COLS), BF16),
        compiler_params=_cparams(("arbitrary",)),
        name="w_in_relayout",
    )(jnp.asarray(tbl, jnp.int32), wt, wt)


def _prep_kernel(zq_ref, zk_ref, zl_ref, zfv_ref, zsv_ref, zs_ref, gq_ref, gk_ref, gkv_ref, bf_ref, wukv_ref,
                 grp_ref, eq_ref, ek_ref, oneq_ref, onek_ref,
                 qn_ref, kn_ref, aq_ref, ak_ref, dk_ref, fvt_ref, svt_ref, dvt_ref, carry_ref, *, ts):
    def value_tile(v):
        return jnp.transpose(v).reshape(N_PAIRS, LANES, ts).astype(BF16)

    fvt_ref[0, :, 0] = value_tile(zfv_ref[0].astype(F32))
    svt_ref[0, :, 0] = value_tile(zsv_ref[0].astype(F32))

    @pl.when(pl.program_id(1) == 0)
    def _():
        carry_ref[...] = jnp.zeros_like(carry_ref)

    def head_norm(z_ref, g_ref):
        x = z_ref[0].astype(F32)
        sq = x * x
        hi = sq.astype(BF16)
        lo = (sq - hi.astype(F32)).astype(BF16)
        ms = (jnp.dot(hi, grp_ref[...], preferred_element_type=F32)
              + jnp.dot(lo, grp_ref[...], preferred_element_type=F32)) * (1.0 / HEAD_DIM)
        return x * lax.rsqrt(ms + EPS) * g_ref[...]

    qn_ref[0] = head_norm(zq_ref, gq_ref).astype(BF16)
    kn_ref[0] = head_norm(zk_ref, gk_ref).astype(BF16)

    lane = lax.broadcasted_iota(jnp.int32, (ts, LANES), 1)
    lf = jnp.where(lane < N_HEADS, _log_sigmoid(zs_ref[0] + bf_ref[...]), 0.0)
    r = lax.broadcasted_iota(jnp.int32, (ts, ts), 0)
    c_ = lax.broadcasted_iota(jnp.int32, (ts, ts), 1)
    tri = (c_ <= r).astype(F32)
    c = jnp.dot(tri, lf, precision=lax.Precision.HIGHEST, preferred_element_type=F32) + carry_ref[...]
    carry_ref[...] = c[ts - 1:ts, :]
    cs = c * LOG2E
    c0 = cs.astype(BF16)
    r1 = cs - c0.astype(F32)
    c1 = r1.astype(BF16)
    c2 = (r1 - c1.astype(F32)).astype(BF16)
    pieces = (c0, c1, c2)
    aq = oneq_ref[...]
    ak = onek_ref[...]
    for k in range(3):
        aq = aq + jnp.dot(pieces[k], eq_ref[k], preferred_element_type=F32)
        ak = ak - jnp.dot(pieces[k], ek_ref[k], preferred_element_type=F32)
    aq_ref[0] = aq.astype(BF16)
    ak_ref[0] = ak.astype(BF16)

    lat = zl_ref[0, :, :KV_RANK].astype(F32)
    msl = jnp.mean(lat * lat, axis=-1, keepdims=True)
    latn = (lat * lax.rsqrt(msl + EPS) * gkv_ref[...]).astype(BF16)
    kv = jnp.dot(latn, wukv_ref[...], preferred_element_type=F32)
    dk_ref[0] = kv[:, :W_ATT].astype(BF16)
    dvt_ref[0, :, 0] = value_tile(kv[:, W_ATT:].astype(BF16).astype(F32))


def _aug_constants():
    eq = np.zeros((3, LANES, LANES), np.float32)
    ek = np.zeros((3, LANES, LANES), np.float32)
    oneq = np.zeros((1, LANES), np.float32)
    onek = np.zeros((1, LANES), np.float32)
    for h in range(N_HEADS):
        for k in range(3):
            eq[k, h, 8 * h + k] = 1.0
            ek[k, h, 8 * h + 3 + k] = 1.0
            oneq[0, 8 * h + 3 + k] = 1.0
            onek[0, 8 * h + k] = 1.0
    grp = np.kron(np.eye(N_HEADS, dtype=np.float32), np.ones((HEAD_DIM, HEAD_DIM), np.float32))
    return (jnp.asarray(grp, BF16), jnp.asarray(eq, BF16), jnp.asarray(ek, BF16),
            jnp.asarray(oneq), jnp.asarray(onek))


def _prep(z3, zs3, q_norm, k_norm, kv_norm, b_forget, w_ukv, ts):
    b, s, _ = z3.shape
    grp, eq, ek, oneq, onek = _aug_constants()
    gq = (jnp.tile(q_norm, N_HEADS) * (HEAD_DIM ** -0.5 * LOG2E)).reshape(1, W_ATT)
    gk = jnp.tile(k_norm, N_HEADS).reshape(1, W_ATT)
    bf = jnp.pad(b_forget, (SM_FF, LANES - N_HEADS - SM_FF)).reshape(1, LANES)
    const = lambda shape: pl.BlockSpec(shape, lambda bi, si: (0,) * len(shape))
    zblk = lambda unit: pl.BlockSpec((1, ts, W_ATT), lambda bi, si: (bi, si, unit // 4))
    seq_out = lambda w: pl.BlockSpec((1, ts, w), lambda bi, si: (bi, si, 0))
    vt_out = pl.BlockSpec((1, N_PAIRS, 1, LANES, ts), lambda bi, si: (bi, 0, si, 0, 0))
    vt_shape = jax.ShapeDtypeStruct((b, N_PAIRS, s // ts, LANES, ts), BF16)
    return pl.pallas_call(
        functools.partial(_prep_kernel, ts=ts),
        grid=(b, s // ts),
        in_specs=[zblk(U_FQ), zblk(U_FK), zblk(U_DLAT), zblk(U_FV), zblk(U_SV),
                  pl.BlockSpec((1, ts, LANES), lambda bi, si: (bi, si, 0)),
                  const((1, W_ATT)), const((1, W_ATT)), const((1, KV_RANK)), const((1, LANES)),
                  const((KV_RANK, 2 * W_ATT)), const((W_ATT, W_ATT)),
                  const((3, LANES, LANES)), const((3, LANES, LANES)), const((1, LANES)), const((1, LANES))],
        out_specs=[seq_out(W_ATT), seq_out(W_ATT), seq_out(LANES), seq_out(LANES), seq_out(W_ATT),
                   vt_out, vt_out, vt_out],
        out_shape=[jax.ShapeDtypeStruct((b, s, W_ATT), BF16), jax.ShapeDtypeStruct((b, s, W_ATT), BF16),
                   jax.ShapeDtypeStruct((b, s, LANES), BF16), jax.ShapeDtypeStruct((b, s, LANES), BF16),
                   jax.ShapeDtypeStruct((b, s, W_ATT), BF16), vt_shape, vt_shape, vt_shape],
        scratch_shapes=[pltpu.VMEM((1, LANES), F32)],
        compiler_params=_cparams(("parallel", "arbitrary")),
        name="mixer_prep",
    )(z3, z3, z3, z3, z3, zs3, gq, gk, kv_norm.reshape(1, KV_RANK), bf, w_ukv.astype(BF16), grp, eq, ek, oneq, onek)


def _nt_dot(a, b):
    return lax.dot_general(a, b, (((1,), (1,)), ((), ())), preferred_element_type=F32)


def _half_mask(shape, half):
    lane = lax.broadcasted_iota(jnp.int32, shape, 1)
    return (lane >= HEAD_DIM * half) & (lane < HEAD_DIM * (half + 1))


def _pair_out(acc0, acc1):
    return jnp.transpose(jnp.concatenate([acc0, acc1], axis=0))


def _online_softmax_pair(last, scores, weighted_values, mask_last, t):
    def soft(s, m, l):
        m_new = jnp.maximum(m, jnp.max(s, axis=0, keepdims=True))
        alpha = jnp.exp2(m - m_new)
        p = jnp.exp2(s - m_new)
        return m_new, alpha * l + jnp.sum(p, axis=0, keepdims=True), alpha, p.astype(BF16)

    def drain(j, p, alpha, acc):
        return tuple(alpha[h] * acc[h] + weighted_values(j, h, p[h]) for h in range(2))

    def body(n, c):
        s, p_prev, a_prev, m, l, acc = c
        s_next = scores(n + 1)
        acc = drain(jnp.maximum(n - 1, 0), p_prev, a_prev, acc)
        r = [soft(s[h], m[h], l[h]) for h in range(2)]
        return (s_next, (r[0][3], r[1][3]), (r[0][2], r[1][2]), (r[0][0], r[1][0]), (r[0][1], r[1][1]), acc)

    two = lambda x: (x, x)
    init = (scores(0), two(jnp.zeros((t, t), BF16)), two(jnp.ones((1, t), F32)),
            two(jnp.full((1, t), NEG, F32)), two(jnp.zeros((1, t), F32)), two(jnp.zeros((HEAD_DIM, t), F32)))
    s, p_prev, a_prev, m, l, acc = lax.fori_loop(0, last, body, init)
    acc = drain(jnp.maximum(last - 1, 0), p_prev, a_prev, acc)
    if mask_last is not None:
        s = tuple(mask_last(x) for x in s)
    r = [soft(s[h], m[h], l[h]) for h in range(2)]
    acc = drain(last, (r[0][3], r[1][3]), (r[0][2], r[1][2]), acc)
    return acc[0] / r[0][1], acc[1] / r[1][1]


def _fox_kernel(q_ref, k_ref, vt_ref, aq_ref, ak_ref, fo_ref, o_ref, *, t):
    pair = pl.program_id(1)
    i = pl.program_id(2)
    q = q_ref[0].astype(F32)
    aq = aq_ref[0].astype(F32)
    lane = lax.broadcasted_iota(jnp.int32, (t, LANES), 1)
    krow = lax.broadcasted_iota(jnp.int32, (t, t), 0)
    qcol = lax.broadcasted_iota(jnp.int32, (t, t), 1)
    causal = krow <= qcol
    qcs = []
    for half in range(2):
        head = 2 * pair + half
        qm = jnp.where(_half_mask((t, LANES), half), q, 0.0).astype(BF16)
        am = jnp.where((lane >= 8 * head) & (lane < 8 * head + 6), aq, 0.0).astype(BF16)
        qcs.append(jnp.concatenate([qm, am], axis=1))

    def scores(j):
        ks = pl.multiple_of(j * t, t)
        kc = jnp.concatenate([k_ref[0, pl.ds(ks, t), :], ak_ref[0, pl.ds(ks, t), :]], axis=1)
        return tuple(_nt_dot(kc, qcs[half]) for half in range(2))

    def weighted_values(j, half, p):
        return jnp.dot(vt_ref[0, 0, j, HEAD_DIM * half:HEAD_DIM * (half + 1), :], p, preferred_element_type=F32)

    o0, o1 = _online_softmax_pair(i, scores, weighted_values, lambda s: jnp.where(causal, s, NEG), t)
    o = _pair_out(o0, o1)
    o_ref[0] = (o * jax.nn.sigmoid(fo_ref[0].astype(F32))).astype(o_ref.dtype)


def _fox(qn, kn, vt, aq, ak, z3, t):
    b, s, _ = qn.shape
    qblk = lambda unit: pl.BlockSpec((1, t, LANES), lambda bi, p, i: (bi, i, unit + p))
    return pl.pallas_call(
        functools.partial(_fox_kernel, t=t),
        grid=(b, N_PAIRS, s // t),
        in_specs=[qblk(0),
                  pl.BlockSpec((1, s, LANES), lambda bi, p, i: (bi, 0, p)),
                  pl.BlockSpec((1, 1, s // t, LANES, t), lambda bi, p, i: (bi, p, 0, 0, 0)),
                  pl.BlockSpec((1, t, LANES), lambda bi, p, i: (bi, i, 0)),
                  pl.BlockSpec((1, s, LANES), lambda bi, p, i: (bi, 0, 0)),
                  qblk(U_FO)],
        out_specs=qblk(0),
        out_shape=jax.ShapeDtypeStruct((b, s, W_ATT), BF16),
        compiler_params=_cparams(("parallel", "parallel", "arbitrary")),
        name="fox_attention",
    )(qn, kn, vt, aq, ak, z3)


def _sb_kernel(q_ref, k_ref, vt_ref, o_ref, *, t):
    i = pl.program_id(2)
    q = q_ref[0].astype(F32) * (HEAD_DIM ** -0.5)
    krow = lax.broadcasted_iota(jnp.int32, (t, t), 0)
    qcol = lax.broadcasted_iota(jnp.int32, (t, t), 1)
    strict = krow < qcol
    after = (qcol > krow).astype(BF16)
    qms = [jnp.where(_half_mask((t, LANES), half), q, 0.0).astype(BF16) for half in range(2)]

    def local(j, diag):
        ks = pl.multiple_of(j * t, t)
        k = k_ref[0, pl.ds(ks, t), :]
        out = []
        for half in range(2):
            z = _nt_dot(k, qms[half])
            lz = _log_sigmoid(z)
            l1m = lz - z
            if diag:
                l1m = jnp.where(strict, l1m, 0.0)
            hi = l1m.astype(BF16)
            lo = (l1m - hi.astype(F32)).astype(BF16)
            suffix = (jnp.dot(after, hi, preferred_element_type=F32)
                      + jnp.dot(after, lo, preferred_element_type=F32))
            out.append((lz + suffix, jnp.sum(l1m, axis=0, keepdims=True)))
        return out

    def finish(j, loc, carry, keep):
        new = []
        for half in range(2):
            logw, colsum = loc[half]
            rsum, acc = carry[half]
            a = jnp.exp(logw + rsum)
            if keep is not None:
                a = jnp.where(keep, a, 0.0)
            pv = jnp.dot(vt_ref[0, 0, j, HEAD_DIM * half:HEAD_DIM * (half + 1), :], a.astype(BF16),
                         preferred_element_type=F32)
            new.append((rsum + colsum, acc + pv))
        return tuple(new)

    prev = jnp.maximum(i - 1, 0)
    loc_diag = local(i, True)
    loc_prev = local(prev, False)
    zero = (jnp.zeros((1, t), F32), jnp.zeros((HEAD_DIM, t), F32))
    carry = finish(i, loc_diag, (zero, zero), strict)
    carry = finish(prev, loc_prev, carry, i > 0)

    def more(c):
        n, ((r0, _), (r1, _)) = c
        return (n < i) & (jnp.maximum(jnp.max(r0), jnp.max(r1)) > SB_UNDERFLOW)

    def body(c):
        n, carry = c
        j = i - 1 - n
        return n + 1, finish(j, local(j, False), carry, None)

    _, ((_, acc0), (_, acc1)) = lax.while_loop(more, body, (jnp.int32(1), carry))
    o_ref[0] = _pair_out(acc0, acc1).astype(o_ref.dtype)


def _sb(z3, vt, t):
    b, s, _ = z3.shape
    return pl.pallas_call(
        functools.partial(_sb_kernel, t=t),
        grid=(b, N_PAIRS, s // t),
        in_specs=[pl.BlockSpec((1, t, LANES), lambda bi, p, i: (bi, i, U_SQ + p)),
                  pl.BlockSpec((1, s, LANES), lambda bi, p, i: (bi, 0, U_SK + p)),
                  pl.BlockSpec((1, 1, s // t, LANES, t), lambda bi, p, i: (bi, p, 0, 0, 0))],
        out_specs=pl.BlockSpec((1, t, LANES), lambda bi, p, i: (bi, i, p)),
        out_shape=jax.ShapeDtypeStruct((b, s, W_ATT), BF16),
        compiler_params=_cparams(("parallel", "parallel", "arbitrary")),
        name="stickbreak_attention",
    )(z3, z3, vt)


def _t5_bucket(n):
    max_exact = N_BUCKETS // 2
    nf = jnp.maximum(n, 1).astype(F32)
    large = max_exact + (jnp.log(nf / max_exact) / math.log(MAX_DISTANCE / max_exact)
                         * (N_BUCKETS - max_exact)).astype(jnp.int32)
    large = jnp.minimum(large, N_BUCKETS - 1)
    return jnp.where(n < max_exact, n, large)


def _bias_tiles_kernel(relb_ref, o_ref, *, t):
    h = pl.program_id(0)
    krow = lax.broadcasted_iota(jnp.int32, (t, t), 0)
    qcol = lax.broadcasted_iota(jnp.int32, (t, t), 1)
    o_ref[0, 0] = jnp.full((t, t), relb_ref[N_BUCKETS - 1, h] * LOG2E, F32)
    for slot, shift in ((1, t), (2, 0)):
        bucket = _t5_bucket(jnp.maximum(qcol - krow + shift, 0))
        val = jnp.full((t, t), relb_ref[0, h], F32)
        for k in range(1, N_BUCKETS):
            val = jnp.where(bucket == k, relb_ref[k, h], val)
        o_ref[0, slot] = val * LOG2E


def _bias_tiles(rel_bias, t):
    assert t >= MAX_DISTANCE
    return pl.pallas_call(
        functools.partial(_bias_tiles_kernel, t=t),
        grid=(N_HEADS,),
        in_specs=[pl.BlockSpec(memory_space=pltpu.SMEM)],
        out_specs=pl.BlockSpec((1, 3, t, t), lambda h: (h, 0, 0, 0)),
        out_shape=jax.ShapeDtypeStruct((N_HEADS, 3, t, t), F32),
        compiler_params=_cparams(("arbitrary",)),
        name="t5_bias_tiles",
    )(rel_bias)


def _dsa_kernel(dq_ref, qi_ref, zs_ref, kidx_ref, dk_ref, dvt_ref, bias_ref, o_ref, key_ref, hi_ref, lo_ref,
                madd_ref, *, t, n_sel):
    i = pl.program_id(1)
    nch = i + 1
    krow = lax.broadcasted_iota(jnp.int32, (t, t), 0)
    qcol = lax.broadcasted_iota(jnp.int32, (t, t), 1)
    idx_scale = (IDX_DIM ** -0.5) * (IDX_HEADS ** -0.5)

    zst = jnp.transpose(zs_ref[0])
    qi = qi_ref[0].astype(F32)
    qih, wih = [], []
    for h in range(IDX_HEADS):
        blk = qi[:, (h // 2) * LANES:(h // 2 + 1) * LANES]
        qih.append(jnp.where(_half_mask((t, LANES), h % 2), blk, 0.0).astype(BF16))
        wih.append(zst[SM_DIW + h:SM_DIW + h + 1, :] * idx_scale)

    def score_chunk(j, _):
        ks = pl.multiple_of(j * t, t)
        kc = kidx_ref[0, pl.ds(ks, t), :]
        sc = jnp.zeros((t, t), F32)
        for h in range(IDX_HEADS):
            sc = sc + jnp.maximum(_nt_dot(kc, qih[h]), 0.0) * wih[h]
        sc = jnp.where(sc == 0.0, 0.0, sc)
        sc = jnp.where(j * t + krow <= i * t + qcol, sc, -jnp.inf)
        bits = pltpu.bitcast(sc, jnp.int32)
        key = jnp.where(bits < 0, bits ^ jnp.int32(0x7FFFFFFF), bits)
        key_ref[j] = key
        hi_ref[j] = (key >> 16).astype(jnp.int16)
        return 0

    lax.fori_loop(0, nch, score_chunk, 0)

    i16 = jnp.int16
    rows16 = 16
    lowest = jnp.full((t, t), -32768, i16)
    npairs = (nch + 1) // 2

    @pl.when(nch % 2 == 1)
    def _():
        hi_ref[nch] = lowest
        lo_ref[nch] = lowest

    def count16(ref, pred):
        def body(j2, c):
            pieces = []
            for j in (2 * j2, 2 * j2 + 1):
                ind = jnp.where(pred(ref[j]), i16(1), i16(0))
                pieces += [ind[r * rows16:(r + 1) * rows16] for r in range(t // rows16)]
            while len(pieces) > 1:
                pieces = [a + b for a, b in zip(pieces[0::2], pieces[1::2])]
            return c + pieces[0]
        per_lane = lax.fori_loop(0, npairs, body, jnp.zeros((rows16, t), i16))
        return jnp.sum(per_lane.astype(F32), axis=0, keepdims=True)

    def search16(ref, base):
        def bit_step(n, thr):
            cand = thr + lax.shift_left(jnp.int32(1), 15 - n)
            c16 = cand.astype(i16)
            cnt = base + count16(ref, lambda v: v >= c16)
            return jnp.where(cnt >= float(n_sel), cand, thr)
        return lax.fori_loop(0, 16, bit_step, jnp.full((1, t), -32768, jnp.int32))

    thr_hi = search16(hi_ref, 0.0)
    thr_hi16 = thr_hi.astype(i16)
    above = count16(hi_ref, lambda v: v > thr_hi16)

    def low_halves(j, _):
        low = ((key_ref[j] & 0xFFFF) - 32768).astype(i16)
        lo_ref[j] = jnp.where(hi_ref[j] == thr_hi16, low, i16(-32768))
        return 0

    lax.fori_loop(0, nch, low_halves, 0)
    thr_lo = search16(lo_ref, above)
    thr = lax.shift_left(thr_hi, 16) | (thr_lo + 32768)

    def count_keys(pred):
        def body(j, c):
            return c + jnp.sum(jnp.where(pred(key_ref[j]), 1.0, 0.0), axis=0, keepdims=True)
        return lax.fori_loop(0, nch, body, jnp.zeros((1, t), F32))

    need = float(n_sel) - count_keys(lambda k: k > thr)

    upto = (qcol <= krow).astype(BF16)

    def mask_chunk(j, seen):
        key = key_ref[j]
        eq = key == thr
        rank = jnp.dot(upto, jnp.where(eq, 1.0, 0.0).astype(BF16), preferred_element_type=F32) + seen
        sel = (key > thr) | (eq & (rank <= need))
        sel = sel & (j * t + krow <= i * t + qcol)
        madd_ref[j] = jnp.where(sel, 0.0, NEG)
        return rank[t - 1:t, :]

    lax.fori_loop(0, nch, mask_chunk, jnp.zeros((1, t), F32))

    for pair in range(N_PAIRS):
        lo, hi_ = pair * LANES, (pair + 1) * LANES
        qp = dq_ref[0, :, lo:hi_].astype(F32) * (HEAD_DIM ** -0.5 * LOG2E)
        qms = [jnp.where(_half_mask((t, LANES), half), qp, 0.0).astype(BF16) for half in range(2)]

        def scores(j, pair=pair, lo=lo, hi_=hi_, qms=qms):
            ks = pl.multiple_of(j * t, t)
            slot = jnp.clip(j - i + 2, 0, 2)
            k = dk_ref[0, pl.ds(ks, t), lo:hi_]
            madd = madd_ref[j]
            return tuple(_nt_dot(k, qms[half]) + bias_ref[2 * pair + half, slot] + madd for half in range(2))

        def weighted_values(j, half, p, pair=pair):
            return jnp.dot(dvt_ref[0, pair, j, HEAD_DIM * half:HEAD_DIM * (half + 1), :], p,
                           preferred_element_type=F32)

        o0, o1 = _online_softmax_pair(i, scores, weighted_values, None, t)
        o_ref[0, :, lo:hi_] = _pair_out(o0, o1).astype(o_ref.dtype)


def _dsa(z3, zs3, dk, dvt, bias, t):
    b, s, _ = z3.shape
    n_sel = min(TOPK_MAX, s // 4)
    nt = s // t
    return pl.pallas_call(
        functools.partial(_dsa_kernel, t=t, n_sel=n_sel),
        grid=(b, s // t),
        in_specs=[pl.BlockSpec((1, t, W_ATT), lambda bi, i: (bi, i, U_DQ // 4)),
                  pl.BlockSpec((1, t, 2 * LANES), lambda bi, i: (bi, i, U_DIQ // 2)),
                  pl.BlockSpec((1, t, LANES), lambda bi, i: (bi, i, 0)),
                  pl.BlockSpec((1, s, LANES), lambda bi, i: (bi, 0, U_DIK)),
                  pl.BlockSpec((1, s, W_ATT), lambda bi, i: (bi, 0, 0)),
                  pl.BlockSpec((1, N_PAIRS, s // t, LANES, t), lambda bi, i: (bi, 0, 0, 0, 0)),
                  pl.BlockSpec((N_HEADS, 3, t, t), lambda bi, i: (0, 0, 0, 0))],
        out_specs=pl.BlockSpec((1, t, W_ATT), lambda bi, i: (bi, i, 0)),
        out_shape=jax.ShapeDtypeStruct((b, s, W_ATT), BF16),
        scratch_shapes=[pltpu.VMEM((nt, t, t), jnp.int32), pltpu.VMEM((nt + nt % 2, t, t), jnp.int16),
                        pltpu.VMEM((nt + nt % 2, t, t), jnp.int16), pltpu.VMEM((nt, t, t), F32)],
        compiler_params=_cparams(("parallel", "arbitrary")),
        name="dsa_attention",
    )(z3, z3, zs3, z3, dk, dvt, bias)


def _merge_kernel(yf_ref, ys_ref, yd_ref, wf_ref, ws_ref, wd_ref, g0_ref, g1_ref, g2_ref, o_ref):
    def branch(y_ref, w_ref, g_ref):
        proj = jnp.dot(y_ref[...], w_ref[...].astype(BF16), preferred_element_type=F32)
        gate = 0.5 * jnp.tanh(0.5 * g_ref[...].astype(F32)) + 0.5
        return gate * proj

    o_ref[...] = (branch(yf_ref, wf_ref, g0_ref) + branch(ys_ref, ws_ref, g1_ref)
                  + branch(yd_ref, wd_ref, g2_ref)).astype(o_ref.dtype)


def _merge(yf, ys, yd, wf, ws, wd, z, d, tm, tn):
    m = yf.shape[0]
    yblk = pl.BlockSpec((tm, W_ATT), lambda i, j: (i, 0))
    wblk = pl.BlockSpec((W_ATT, tn), lambda i, j: (0, j))
    gblk = lambda g: pl.BlockSpec((tm, tn), lambda i, j: (i, (U_GATES * LANES + g * d) // tn + j))
    return pl.pallas_call(
        _merge_kernel,
        grid=(m // tm, d // tn),
        in_specs=[yblk, yblk, yblk, wblk, wblk, wblk, gblk(0), gblk(1), gblk(2)],
        out_specs=pl.BlockSpec((tm, tn), lambda i, j: (i, j)),
        out_shape=jax.ShapeDtypeStruct((m, d), BF16),
        compiler_params=_cparams(("parallel", "arbitrary")),
        name="branch_merge",
    )(yf, ys, yd, wf, ws, wd, z, z, z)


def _matmul_res_kernel(a_ref, w_ref, r_ref, o_ref):
    o_ref[...] = r_ref[...] + jnp.dot(a_ref[...], w_ref[...].astype(BF16), preferred_element_type=F32)


def _matmul_res(a, w, res, tm, tn):
    m, k = a.shape
    n = w.shape[1]
    return pl.pallas_call(
        _matmul_res_kernel,
        grid=(m // tm, n // tn),
        in_specs=[pl.BlockSpec((tm, k), lambda i, j: (i, 0)), pl.BlockSpec((k, tn), lambda i, j: (0, j)),
                  pl.BlockSpec((tm, tn), lambda i, j: (i, j))],
        out_specs=pl.BlockSpec((tm, tn), lambda i, j: (i, j)),
        out_shape=jax.ShapeDtypeStruct((m, n), F32),
        compiler_params=_cparams(("parallel", "arbitrary")),
        name="out_proj_residual",
    )(a, w, res)


def _row_copy(src_hbm, row, dst_vmem, slot, sem):
    return pltpu.make_async_copy(src_hbm.at[pl.ds(row, 1)], dst_vmem.at[pl.ds(slot, 1)], sem)


def _ffn_kernel(te_ref, nv_ref, src_ref, rows_ref, x_ref, wg_ref, wu_ref, wd_ref, *rest, grouped, tm, nf):
    ti = pl.program_id(0)
    f = pl.program_id(1)
    if grouped:
        o_ref, gbuf, xs, sem = rest
    else:
        (o_ref,) = rest

    @pl.when(f == 0)
    def _():
        o_ref[...] = jnp.zeros_like(o_ref)

    if grouped:
        steps = max(k for k in range(1, nf) if tm % k == 0)
        per = tm // steps

        def issue_rows(tile, lo, n):
            def issue(r, _):
                _row_copy(x_ref, src_ref[tile * tm + lo + r], gbuf, lo + r, sem).start()
                return 0
            lax.fori_loop(0, n, issue, 0, unroll=ISSUE_UNROLL)

        @pl.when((ti == 0) & (f == 0))
        def _():
            issue_rows(0, 0, tm)

        @pl.when((f == 0) & (ti < nv_ref[0]))
        def _():
            def wait(r, _):
                _row_copy(x_ref, 0, gbuf, r, sem).wait()
                return 0
            lax.fori_loop(0, tm, wait, 0, unroll=ISSUE_UNROLL)
            xs[...] = gbuf[...].astype(BF16)

        @pl.when((f > 0) & (f <= steps) & (ti + 1 < nv_ref[0]))
        def _():
            issue_rows(ti + 1, (f - 1) * per, per)

    def swiglu(x, wgate, wup, wdown):
        g = jnp.dot(x, wgate, preferred_element_type=F32)
        u = jnp.dot(x, wup, preferred_element_type=F32)
        a = (g * jax.nn.sigmoid(g) * u).astype(BF16)
        return jnp.dot(a, wdown, preferred_element_type=F32)

    def whole_tile(x):
        o_ref[...] += swiglu(x, wg_ref[0].astype(BF16), wu_ref[0].astype(BF16), wd_ref[0].astype(BF16))

    @pl.when(ti < nv_ref[0])
    def _():
        if not grouped:
            whole_tile(x_ref[...])
            return
        rows = rows_ref[ti]

        @pl.when(rows > tm - SUB_ROWS)
        def _():
            whole_tile(xs[...])

        @pl.when(rows <= tm - SUB_ROWS)
        def _():
            for lo in range(0, tm, SUB_ROWS):
                @pl.when(lo < rows)
                def _(lo=lo):
                    o_ref[lo:lo + SUB_ROWS, :] += swiglu(
                        xs[lo:lo + SUB_ROWS, :], wg_ref[0].astype(BF16), wu_ref[0].astype(BF16),
                        wd_ref[0].astype(BF16))


def _ffn(x, wg, wu, wd, tile_expert, n_valid, tm, fc, src=None, tile_rows=None):
    grouped = src is not None
    d = x.shape[1]
    p = src.shape[0] if grouped else x.shape[0]
    nf = wg.shape[2] // fc

    def chunk(ti, f, nv):
        return jnp.where(ti < nv[0], f, nf - 1)

    wspecs = [pl.BlockSpec((1, d, fc), lambda ti, f, te, nv, sr, rw: (te[ti], 0, chunk(ti, f, nv))),
              pl.BlockSpec((1, d, fc), lambda ti, f, te, nv, sr, rw: (te[ti], 0, chunk(ti, f, nv))),
              pl.BlockSpec((1, fc, d), lambda ti, f, te, nv, sr, rw: (te[ti], chunk(ti, f, nv), 0))]
    scratch = []
    if grouped:
        in_specs = [pl.BlockSpec(memory_space=pl.ANY)] + wspecs
        args = [x, wg, wu, wd]
        scratch += [pltpu.VMEM((tm, d), F32), pltpu.VMEM((tm, d), BF16), pltpu.SemaphoreType.DMA(())]
    else:
        in_specs = [pl.BlockSpec((tm, d), lambda ti, f, te, nv, sr, rw: (ti, 0))] + wspecs
        args = [x, wg, wu, wd]
        src = tile_rows = jnp.zeros((1,), jnp.int32)
    grid_spec = pltpu.PrefetchScalarGridSpec(
        num_scalar_prefetch=4,
        grid=(p // tm, nf),
        in_specs=in_specs,
        out_specs=pl.BlockSpec((tm, d), lambda ti, f, te, nv, sr, rw: (ti, 0),
                               **({"pipeline_mode": pl.Buffered(1)} if grouped else {})),
        scratch_shapes=scratch,
    )
    return pl.pallas_call(
        functools.partial(_ffn_kernel, grouped=grouped, tm=tm, nf=nf),
        grid_spec=grid_spec,
        out_shape=jax.ShapeDtypeStruct((p, d), F32),
        compiler_params=_cparams(("arbitrary", "arbitrary")),
        name="grouped_swiglu" if grouped else "dense_swiglu",
    )(tile_expert, n_valid, src, tile_rows, *args)


def _combine_kernel(pos_ref, x_ref, info_ref, y_ref, g_ref, o_ref, buf, sem, *, tt, n, norm):
    i = pl.program_id(0)

    def issue_tile(tile, slot):
        def issue(r, _):
            for k in range(2):
                _row_copy(y_ref, pos_ref[k * n + tile * tt + r], buf.at[slot, k], r, sem.at[slot]).start()
            return 0
        lax.fori_loop(0, tt, issue, 0, unroll=ISSUE_UNROLL)

    @pl.when(i == 0)
    def _():
        issue_tile(0, 0)

    @pl.when(i + 1 < pl.num_programs(0))
    def _():
        issue_tile(i + 1, (i + 1) % 2)

    slot = i % 2

    def wait(r, _):
        for k in range(2):
            _row_copy(y_ref, 0, buf.at[slot, k], r, sem.at[slot]).wait()
        return 0

    lax.fori_loop(0, tt, wait, 0)
    info = info_ref[...]
    o = x_ref[...] + (info[:, 0:1] * buf[slot, 0] + info[:, 1:2] * buf[slot, 1])
    if norm:
        o = o * lax.rsqrt(jnp.mean(o * o, axis=-1, keepdims=True) + EPS) * g_ref[...]
    o_ref[...] = o


def _combine(x2, info, y, pos, tt, gain=None):
    n, d = x2.shape
    norm = gain is not None
    grid_spec = pltpu.PrefetchScalarGridSpec(
        num_scalar_prefetch=1,
        grid=(n // tt,),
        in_specs=[pl.BlockSpec((tt, d), lambda i, ps: (i, 0)), pl.BlockSpec((tt, LANES), lambda i, ps: (i, 0)),
                  pl.BlockSpec(memory_space=pl.ANY), pl.BlockSpec((1, d), lambda i, ps: (0, 0))],
        out_specs=pl.BlockSpec((tt, d), lambda i, ps: (i, 0)),
        scratch_shapes=[pltpu.VMEM((2, 2, tt, d), F32), pltpu.SemaphoreType.DMA((2,))],
    )
    return pl.pallas_call(
        functools.partial(_combine_kernel, tt=tt, n=n, norm=norm),
        grid_spec=grid_spec,
        out_shape=jax.ShapeDtypeStruct((n, d), F32),
        compiler_params=_cparams(("arbitrary",)),
        name="moe_combine",
    )(pos, x2, info, y, (gain if norm else jnp.ones((d,), F32)).reshape(1, d))


def _moe(x2, h, info, wg, wu, wd, tm, fc, out_gain=None):
    n = h.shape[0]
    e = wg.shape[0]
    eid = jnp.concatenate([info[:, 2], info[:, 3]]).astype(jnp.int32)
    tok = jnp.tile(jnp.arange(n, dtype=jnp.int32), 2)
    onehot = (eid[:, None] == jnp.arange(e, dtype=jnp.int32)[None, :]).astype(jnp.int32)
    rank = jnp.sum((jnp.cumsum(onehot, axis=0) - 1) * onehot, axis=1)
    counts = jnp.sum(onehot, axis=0)
    padded = ((counts + tm - 1) // tm) * tm
    ends = jnp.cumsum(padded)
    starts = ends - padded
    pos = starts[eid] + rank
    n_tiles = (2 * n) // tm + e
    p = n_tiles * tm
    src = jnp.zeros((p,), jnp.int32).at[pos].set(tok)
    n_valid = (ends[e - 1] // tm).astype(jnp.int32)
    tile_start = jnp.arange(n_tiles, dtype=jnp.int32) * tm
    tile_start = jnp.minimum(tile_start, (n_valid - 1) * tm)
    tile_expert = jnp.sum((tile_start[:, None] >= ends[None, :]).astype(jnp.int32), axis=1)
    tile_rows = jnp.clip((starts + counts)[tile_expert] - tile_start, 0, tm).astype(jnp.int32)
    y = _ffn(h, wg, wu, wd, tile_expert, n_valid.reshape(1), tm, fc, src=src, tile_rows=tile_rows)
    return _combine(x2, info, y, pos, _tile(n, 256), out_gain)


def _tile(total, want):
    t = min(total, want)
    assert total % t == 0
    return t


def kernel(x, w_in, b_forget, q_norm, k_norm, kv_norm, w_ukv, w_br_fox, w_br_sb, w_br_dsa, w_out, rel_bias, norm_mix, norm_ffn, w_ffn_gate, w_ffn_up, w_ffn_down, w_router, w_moe_gate, w_moe_up, w_moe_down, norm_final):
    b, s, d = x.shape
    m = b * s
    depth = w_in.shape[0]
    ta = _tile(s, 256)
    tm_norm = _tile(m, 512)
    tm = _tile(m, 1024)
    tn = 512
    fc = 256
    tm_moe = _tile(2 * m, 1024)

    bias = _bias_tiles(rel_bias, ta)
    x2 = x.reshape(m, d)
    pending = None
    for l in range(depth):
        if pending is None:
            h = _rmsnorm(x2, norm_mix[l], BF16, tm)
        else:
            x2, h = _add_rmsnorm(x2, pending, norm_mix[l], tm_norm)
            pending = None
        z, zs = _inproj(h, _relayout_w_in(w_in, l), tm, 2 * tn)
        z3 = z.reshape(b, s, Z_COLS)
        zs3 = zs.reshape(b, s, LANES)
        qn, kn, aq, ak, dk, fvt, svt, dvt = _prep(z3, zs3, q_norm[l], k_norm[l], kv_norm[l], b_forget[l],
                                                  w_ukv[l], ta)
        y_fox = _fox(qn, kn, fvt, aq, ak, z3, ta).reshape(m, W_ATT)
        y_sb = _sb(z3, svt, ta).reshape(m, W_ATT)
        y_dsa = _dsa(z3, zs3, dk, dvt, bias, ta).reshape(m, W_ATT)
        mix = _merge(y_fox, y_sb, y_dsa, w_br_fox[l], w_br_sb[l], w_br_dsa[l], z, d, tm, tn)
        x2 = _matmul_res(mix, w_out[l], x2, tm, 2 * tn)
        j = l // 2
        if l % 2 == 0:
            h = _rmsnorm(x2, norm_ffn[l], BF16, tm)
            n_tiles = m // tm
            pending = _ffn(h, w_ffn_gate[j][None], w_ffn_up[j][None], w_ffn_down[j][None],
                           jnp.zeros((n_tiles,), jnp.int32), jnp.full((1,), n_tiles, jnp.int32), tm, 2 * fc)
        else:
            h, info = _rmsnorm_router(x2, norm_ffn[l], w_router[j], tm)
            last = l == depth - 1
            x2 = _moe(x2, h, info, w_moe_gate[j], w_moe_up[j], w_moe_down[j], tm_moe, 2 * fc,
                      norm_final if last else None)
    if pending is not None:
        x2 = _rmsnorm(x2 + pending, norm_final, F32, tm_norm)
    return x2.reshape(b, s, d)
```

```python
import functools
import math

import jax
import jax.numpy as jnp
import numpy as np
from jax import lax
from jax.experimental import pallas as pl
from jax.experimental.pallas import tpu as pltpu

F32 = jnp.float32
BF16 = jnp.bfloat16

HEAD_DIM = 64
N_HEADS = 8
N_PAIRS = N_HEADS // 2
W_ATT = N_HEADS * HEAD_DIM
KV_RANK = 256
IDX_HEADS = 4
IDX_DIM = 64
TOPK_MAX = 256
N_BUCKETS = 32
MAX_DISTANCE = 128
N_EXPERTS = 8
N_BRANCH = 3
EPS = 1e-6
LOG2E = 1.4426950408889634
LANES = 128
NEG = -1e30
SB_UNDERFLOW = -104.0
ISSUE_UNROLL = 16
SUB_ROWS = 256

U_FQ, U_FK, U_FV, U_FO, U_SQ, U_SK, U_SV, U_DQ = 0, 4, 8, 12, 16, 20, 24, 28
U_DLAT, U_DIQ, U_DIK, U_SMALL, U_GATES = 32, 34, 36, 37, 40
N_UNITS = 88
Z_COLS = N_UNITS * LANES
SM_FF, SM_DIW = 0, 8

VMEM_LIMIT = 56 * 1024 * 1024


def _cparams(sem):
    return pltpu.CompilerParams(dimension_semantics=sem, vmem_limit_bytes=VMEM_LIMIT)


def _log_sigmoid(x):
    return jnp.minimum(x, 0.0) - jnp.log(1.0 + jnp.exp(-jnp.abs(x)))


def _rmsnorm_kernel(x_ref, g_ref, o_ref):
    x = x_ref[...]
    ms = jnp.mean(x * x, axis=-1, keepdims=True)
    o_ref[...] = (x * lax.rsqrt(ms + EPS) * g_ref[...]).astype(o_ref.dtype)


def _rmsnorm(x, g, out_dtype, tm):
    m, d = x.shape
    return pl.pallas_call(
        _rmsnorm_kernel,
        grid=(m // tm,),
        in_specs=[pl.BlockSpec((tm, d), lambda i: (i, 0)), pl.BlockSpec((1, d), lambda i: (0, 0))],
        out_specs=pl.BlockSpec((tm, d), lambda i: (i, 0)),
        out_shape=jax.ShapeDtypeStruct((m, d), out_dtype),
        compiler_params=_cparams(("parallel",)),
        name="rmsnorm",
    )(x, g.reshape(1, d))


def _add_rmsnorm_kernel(x_ref, y_ref, g_ref, xo_ref, h_ref):
    x = x_ref[...] + y_ref[...]
    xo_ref[...] = x
    ms = jnp.mean(x * x, axis=-1, keepdims=True)
    h_ref[...] = (x * lax.rsqrt(ms + EPS) * g_ref[...]).astype(h_ref.dtype)


def _add_rmsnorm(x, y, g, tm):
    m, d = x.shape
    blk = pl.BlockSpec((tm, d), lambda i: (i, 0))
    return pl.pallas_call(
        _add_rmsnorm_kernel,
        grid=(m // tm,),
        in_specs=[blk, blk, pl.BlockSpec((1, d), lambda i: (0, 0))],
        out_specs=[blk, blk],
        out_shape=[jax.ShapeDtypeStruct((m, d), F32), jax.ShapeDtypeStruct((m, d), BF16)],
        compiler_params=_cparams(("parallel",)),
        name="add_rmsnorm",
    )(x, y, g.reshape(1, d))


def _rmsnorm_router_kernel(x_ref, g_ref, wr_ref, o_ref, info_ref):
    x = x_ref[...]
    ms = jnp.mean(x * x, axis=-1, keepdims=True)
    h = x * lax.rsqrt(ms + EPS) * g_ref[...]
    o_ref[...] = h.astype(o_ref.dtype)
    logits = jnp.dot(h, wr_ref[...], precision=lax.Precision.HIGHEST, preferred_element_type=F32)
    lane = lax.broadcasted_iota(jnp.int32, logits.shape, 1).astype(F32)
    lg = jnp.where(lane < N_EXPERTS, logits, -jnp.inf)
    v1 = jnp.max(lg, axis=-1, keepdims=True)
    i1 = jnp.min(jnp.where(lg == v1, lane, float(LANES)), axis=-1, keepdims=True)
    lg2 = jnp.where(lane == i1, -jnp.inf, lg)
    v2 = jnp.max(lg2, axis=-1, keepdims=True)
    i2 = jnp.min(jnp.where(lg2 == v2, lane, float(LANES)), axis=-1, keepdims=True)
    e2 = jnp.exp(v2 - v1)
    w1 = 1.0 / (1.0 + e2)
    w2 = e2 / (1.0 + e2)
    info = jnp.where(lane == 0.0, w1, jnp.where(lane == 1.0, w2, jnp.where(
        lane == 2.0, i1, jnp.where(lane == 3.0, i2, 0.0))))
    info_ref[...] = info


def _rmsnorm_router(x, g, w_router, tm):
    m, d = x.shape
    wr = jnp.pad(w_router, ((0, 0), (0, LANES - w_router.shape[1])))
    return pl.pallas_call(
        _rmsnorm_router_kernel,
        grid=(m // tm,),
        in_specs=[pl.BlockSpec((tm, d), lambda i: (i, 0)), pl.BlockSpec((1, d), lambda i: (0, 0)),
                  pl.BlockSpec((d, LANES), lambda i: (0, 0))],
        out_specs=[pl.BlockSpec((tm, d), lambda i: (i, 0)), pl.BlockSpec((tm, LANES), lambda i: (i, 0))],
        out_shape=[jax.ShapeDtypeStruct((m, d), F32), jax.ShapeDtypeStruct((m, LANES), F32)],
        compiler_params=_cparams(("parallel",)),
        name="rmsnorm_router",
    )(x, g.reshape(1, d), wr)


def _inproj_kernel(a_ref, w_ref, z_ref, zs_ref, *, small_tile):
    acc = jnp.dot(a_ref[...], w_ref[...], preferred_element_type=F32)
    z_ref[...] = acc.astype(z_ref.dtype)

    @pl.when(pl.program_id(1) == small_tile)
    def _():
        off = (U_SMALL * LANES) % acc.shape[1]
        zs_ref[...] = acc[:, off:off + LANES]


def _inproj(h, w, tm, tn):
    m, d = h.shape
    return pl.pallas_call(
        functools.partial(_inproj_kernel, small_tile=(U_SMALL * LANES) // tn),
        grid=(m // tm, Z_COLS // tn),
        in_specs=[pl.BlockSpec((tm, d), lambda i, j: (i, 0)), pl.BlockSpec((d, tn), lambda i, j: (0, j))],
        out_specs=[pl.BlockSpec((tm, tn), lambda i, j: (i, j)), pl.BlockSpec((tm, LANES), lambda i, j: (i, 0))],
        out_shape=[jax.ShapeDtypeStruct((m, Z_COLS), BF16), jax.ShapeDtypeStruct((m, LANES), F32)],
        compiler_params=_cparams(("parallel", "arbitrary")),
        name="inproj",
    )(h, w)


W_IN_ORDER = (("fq", W_ATT), ("fk", W_ATT), ("fv", W_ATT), ("ff", N_HEADS), ("fo", W_ATT),
              ("sq", W_ATT), ("sk", W_ATT), ("sv", W_ATT), ("dq", W_ATT), ("dlat", KV_RANK),
              ("diq", IDX_HEADS * IDX_DIM), ("dik", IDX_DIM), ("diw", IDX_HEADS))
RELAYOUT_COLS = 4 * LANES


def _relayout_kernel(tbl_ref, w_ref, ff_ref, o_ref, *, layer, special):
    j = pl.program_id(0)

    @pl.when(j != special)
    def _():
        o_ref[...] = jnp.transpose(w_ref[:, layer, :]).astype(o_ref.dtype)

    @pl.when(j == special)
    def _():
        xt = jnp.transpose(w_ref[0:LANES, layer, :])
        fft = jnp.transpose(ff_ref[:, layer, :])
        lane = lax.broadcasted_iota(jnp.int32, xt.shape, 1)
        dik2 = jnp.where(lane < IDX_DIM, xt, pltpu.roll(xt, IDX_DIM, axis=1))
        diw = pltpu.roll(xt, LANES - IDX_DIM + SM_DIW, axis=1)
        small = jnp.where(lane < N_HEADS, fft,
                          jnp.where((lane >= SM_DIW) & (lane < SM_DIW + IDX_HEADS), diw, 0.0))
        o_ref[:, 0:LANES] = dik2.astype(o_ref.dtype)
        o_ref[:, LANES:2 * LANES] = small.astype(o_ref.dtype)
        o_ref[:, 2 * LANES:] = jnp.zeros((o_ref.shape[0], o_ref.shape[1] - 2 * LANES), o_ref.dtype)


def _relayout_w_in(w_in, layer):
    _, d, n_in = w_in.shape
    src, o = {}, 0
    for name, width in W_IN_ORDER + (("gates", N_BRANCH * d),):
        src[name] = o
        o += width
    assert o == n_in and U_GATES * LANES + N_BRANCH * d == Z_COLS and SM_FF == 0
    assert src["diq"] == src["dlat"] + KV_RANK and src["diw"] == src["dik"] + IDX_DIM
    units = {U_FQ: "fq", U_FK: "fk", U_FV: "fv", U_FO: "fo", U_SQ: "sq", U_SK: "sk", U_SV: "sv",
             U_DQ: "dq", U_DLAT: "dlat", U_DIK: "dik"}
    per = RELAYOUT_COLS // LANES
    tbl = [src[units[u]] for u in range(0, U_GATES, per)]
    tbl += [src["gates"] + k * RELAYOUT_COLS for k in range(N_BRANCH * d // RELAYOUT_COLS)]
    elem = lambda rows: (pl.Element(rows), pl.Element(w_in.shape[0]), pl.Element(d))
    grid_spec = pltpu.PrefetchScalarGridSpec(
        num_scalar_prefetch=1,
        grid=(Z_COLS // RELAYOUT_COLS,),
        in_specs=[pl.BlockSpec(elem(RELAYOUT_COLS), lambda j, tbl: (tbl[j], 0, 0)),
                  pl.BlockSpec(elem(LANES), lambda j, tbl: (src["ff"], 0, 0))],
        out_specs=pl.BlockSpec((d, RELAYOUT_COLS), lambda j, tbl: (0, j)),
    )
    wt = jnp.transpose(w_in, (2, 0, 1))
    return pl.pallas_call(
        functools.partial(_relayout_kernel, layer=layer, special=U_DIK // per),
        grid_spec=grid_spec,
        out_shape=jax.ShapeDtypeStruct((d, Z_COLS), BF16),
        compiler_params=_cparams(("arbitrary",)),
        name="w_in_relayout",
    )(jnp.asarray(tbl, jnp.int32), wt, wt)


def _prep_kernel(zq_ref, zk_ref, zl_ref, zfv_ref, zsv_ref, zs_ref, gq_ref, gk_ref, gkv_ref, bf_ref, wukv_ref,
                 grp_ref, eq_ref, ek_ref, oneq_ref, onek_ref,
                 qn_ref, kn_ref, aq_ref, ak_ref, dk_ref, fvt_ref, svt_ref, dvt_ref, carry_ref, *, ts):
    def value_tile(v):
        return jnp.transpose(v).reshape(N_PAIRS, LANES, ts).astype(BF16)

    fvt_ref[0, :, 0] = value_tile(zfv_ref[0].astype(F32))
    svt_ref[0, :, 0] = value_tile(zsv_ref[0].astype(F32))

    @pl.when(pl.program_id(1) == 0)
    def _():
        carry_ref[...] = jnp.zeros_like(carry_ref)

    def head_norm(z_ref, g_ref):
        x = z_ref[0].astype(F32)
        sq = x * x
        hi = sq.astype(BF16)
        lo = (sq - hi.astype(F32)).astype(BF16)
        ms = (jnp.dot(hi, grp_ref[...], preferred_element_type=F32)
              + jnp.dot(lo, grp_ref[...], preferred_element_type=F32)) * (1.0 / HEAD_DIM)
        return x * lax.rsqrt(ms + EPS) * g_ref[...]

    qn_ref[0] = head_norm(zq_ref, gq_ref).astype(BF16)
    kn_ref[0] = head_norm(zk_ref, gk_ref).astype(BF16)

    lane = lax.broadcasted_iota(jnp.int32, (ts, LANES), 1)
    lf = jnp.where(lane < N_HEADS, _log_sigmoid(zs_ref[0] + bf_ref[...]), 0.0)
    r = lax.broadcasted_iota(jnp.int32, (ts, ts), 0)
    c_ = lax.broadcasted_iota(jnp.int32, (ts, ts), 1)
    tri = (c_ <= r).astype(F32)
    c = jnp.dot(tri, lf, precision=lax.Precision.HIGHEST, preferred_element_type=F32) + carry_ref[...]
    carry_ref[...] = c[ts - 1:ts, :]
    cs = c * LOG2E
    c0 = cs.astype(BF16)
    r1 = cs - c0.astype(F32)
    c1 = r1.astype(BF16)
    c2 = (r1 - c1.astype(F32)).astype(BF16)
    pieces = (c0, c1, c2)
    aq = oneq_ref[...]
    ak = onek_ref[...]
    for k in range(3):
        aq = aq + jnp.dot(pieces[k], eq_ref[k], preferred_element_type=F32)
        ak = ak - jnp.dot(pieces[k], ek_ref[k], preferred_element_type=F32)
    aq_ref[0] = aq.astype(BF16)
    ak_ref[0] = ak.astype(BF16)

    lat = zl_ref[0, :, :KV_RANK].astype(F32)
    msl = jnp.mean(lat * lat, axis=-1, keepdims=True)
    latn = (lat * lax.rsqrt(msl + EPS) * gkv_ref[...]).astype(BF16)
    kv = jnp.dot(latn, wukv_ref[...], preferred_element_type=F32)
    dk_ref[0] = kv[:, :W_ATT].astype(BF16)
    dvt_ref[0, :, 0] = value_tile(kv[:, W_ATT:].astype(BF16).astype(F32))


def _aug_constants():
    eq = np.zeros((3, LANES, LANES), np.float32)
    ek = np.zeros((3, LANES, LANES), np.float32)
    oneq = np.zeros((1, LANES), np.float32)
    onek = np.zeros((1, LANES), np.float32)
    for h in range(N_HEADS):
        for k in range(3):
            eq[k, h, 8 * h + k] = 1.0
            ek[k, h, 8 * h + 3 + k] = 1.0
            oneq[0, 8 * h + 3 + k] = 1.0
            onek[0, 8 * h + k] = 1.0
    grp = np.kron(np.eye(N_HEADS, dtype=np.float32), np.ones((HEAD_DIM, HEAD_DIM), np.float32))
    return (jnp.asarray(grp, BF16), jnp.asarray(eq, BF16), jnp.asarray(ek, BF16),
            jnp.asarray(oneq), jnp.asarray(onek))


def _prep(z3, zs3, q_norm, k_norm, kv_norm, b_forget, w_ukv, ts):
    b, s, _ = z3.shape
    grp, eq, ek, oneq, onek = _aug_constants()
    gq = (jnp.tile(q_norm, N_HEADS) * (HEAD_DIM ** -0.5 * LOG2E)).reshape(1, W_ATT)
    gk = jnp.tile(k_norm, N_HEADS).reshape(1, W_ATT)
    bf = jnp.pad(b_forget, (SM_FF, LANES - N_HEADS - SM_FF)).reshape(1, LANES)
    const = lambda shape: pl.BlockSpec(shape, lambda bi, si: (0,) * len(shape))
    zblk = lambda unit: pl.BlockSpec((1, ts, W_ATT), lambda bi, si: (bi, si, unit // 4))
    seq_out = lambda w: pl.BlockSpec((1, ts, w), lambda bi, si: (bi, si, 0))
    vt_out = pl.BlockSpec((1, N_PAIRS, 1, LANES, ts), lambda bi, si: (bi, 0, si, 0, 0))
    vt_shape = jax.ShapeDtypeStruct((b, N_PAIRS, s // ts, LANES, ts), BF16)
    return pl.pallas_call(
        functools.partial(_prep_kernel, ts=ts),
        grid=(b, s // ts),
        in_specs=[zblk(U_FQ), zblk(U_FK), zblk(U_DLAT), zblk(U_FV), zblk(U_SV),
                  pl.BlockSpec((1, ts, LANES), lambda bi, si: (bi, si, 0)),
                  const((1, W_ATT)), const((1, W_ATT)), const((1, KV_RANK)), const((1, LANES)),
                  const((KV_RANK, 2 * W_ATT)), const((W_ATT, W_ATT)),
                  const((3, LANES, LANES)), const((3, LANES, LANES)), const((1, LANES)), const((1, LANES))],
        out_specs=[seq_out(W_ATT), seq_out(W_ATT), seq_out(LANES), seq_out(LANES), seq_out(W_ATT),
                   vt_out, vt_out, vt_out],
        out_shape=[jax.ShapeDtypeStruct((b, s, W_ATT), BF16), jax.ShapeDtypeStruct((b, s, W_ATT), BF16),
                   jax.ShapeDtypeStruct((b, s, LANES), BF16), jax.ShapeDtypeStruct((b, s, LANES), BF16),
                   jax.ShapeDtypeStruct((b, s, W_ATT), BF16), vt_shape, vt_shape, vt_shape],
        scratch_shapes=[pltpu.VMEM((1, LANES), F32)],
        compiler_params=_cparams(("parallel", "arbitrary")),
        name="mixer_prep",
    )(z3, z3, z3, z3, z3, zs3, gq, gk, kv_norm.reshape(1, KV_RANK), bf, w_ukv.astype(BF16), grp, eq, ek, oneq, onek)


def _nt_dot(a, b):
    return lax.dot_general(a, b, (((1,), (1,)), ((), ())), preferred_element_type=F32)


def _half_mask(shape, half):
    lane = lax.broadcasted_iota(jnp.int32, shape, 1)
    return (lane >= HEAD_DIM * half) & (lane < HEAD_DIM * (half + 1))


def _pair_out(acc0, acc1):
    return jnp.transpose(jnp.concatenate([acc0, acc1], axis=0))


def _online_softmax_pair(last, scores, weighted_values, mask_last, t):
    def soft(s, m, l):
        m_new = jnp.maximum(m, jnp.max(s, axis=0, keepdims=True))
        alpha = jnp.exp2(m - m_new)
        p = jnp.exp2(s - m_new)
        return m_new, alpha * l + jnp.sum(p, axis=0, keepdims=True), alpha, p.astype(BF16)

    def drain(j, p, alpha, acc):
        return tuple(alpha[h] * acc[h] + weighted_values(j, h, p[h]) for h in range(2))

    def body(n, c):
        s, p_prev, a_prev, m, l, acc = c
        s_next = scores(n + 1)
        acc = drain(jnp.maximum(n - 1, 0), p_prev, a_prev, acc)
        r = [soft(s[h], m[h], l[h]) for h in range(2)]
        return (s_next, (r[0][3], r[1][3]), (r[0][2], r[1][2]), (r[0][0], r[1][0]), (r[0][1], r[1][1]), acc)

    two = lambda x: (x, x)
    init = (scores(0), two(jnp.zeros((t, t), BF16)), two(jnp.ones((1, t), F32)),
            two(jnp.full((1, t), NEG, F32)), two(jnp.zeros((1, t), F32)), two(jnp.zeros((HEAD_DIM, t), F32)))
    s, p_prev, a_prev, m, l, acc = lax.fori_loop(0, last, body, init)
    acc = drain(jnp.maximum(last - 1, 0), p_prev, a_prev, acc)
    if mask_last is not None:
        s = tuple(mask_last(x) for x in s)
    r = [soft(s[h], m[h], l[h]) for h in range(2)]
    acc = drain(last, (r[0][3], r[1][3]), (r[0][2], r[1][2]), acc)
    return acc[0] / r[0][1], acc[1] / r[1][1]


def _fox_kernel(q_ref, k_ref, vt_ref, aq_ref, ak_ref, fo_ref, o_ref, *, t):
    pair = pl.program_id(1)
    i = pl.program_id(2)
    q = q_ref[0].astype(F32)
    aq = aq_ref[0].astype(F32)
    lane = lax.broadcasted_iota(jnp.int32, (t, LANES), 1)
    krow = lax.broadcasted_iota(jnp.int32, (t, t), 0)
    qcol = lax.broadcasted_iota(jnp.int32, (t, t), 1)
    causal = krow <= qcol
    qcs = []
    for half in range(2):
        head = 2 * pair + half
        qm = jnp.where(_half_mask((t, LANES), half), q, 0.0).astype(BF16)
        am = jnp.where((lane >= 8 * head) & (lane < 8 * head + 6), aq, 0.0).astype(BF16)
        qcs.append(jnp.concatenate([qm, am], axis=1))

    def scores(j):
        ks = pl.multiple_of(j * t, t)
        kc = jnp.concatenate([k_ref[0, pl.ds(ks, t), :], ak_ref[0, pl.ds(ks, t), :]], axis=1)
        return tuple(_nt_dot(kc, qcs[half]) for half in range(2))

    def weighted_values(j, half, p):
        return jnp.dot(vt_ref[0, 0, j, HEAD_DIM * half:HEAD_DIM * (half + 1), :], p, preferred_element_type=F32)

    o0, o1 = _online_softmax_pair(i, scores, weighted_values, lambda s: jnp.where(causal, s, NEG), t)
    o = _pair_out(o0, o1)
    o_ref[0] = (o * jax.nn.sigmoid(fo_ref[0].astype(F32))).astype(o_ref.dtype)


def _fox(qn, kn, vt, aq, ak, z3, t):
    b, s, _ = qn.shape
    qblk = lambda unit: pl.BlockSpec((1, t, LANES), lambda bi, p, i: (bi, i, unit + p))
    return pl.pallas_call(
        functools.partial(_fox_kernel, t=t),
        grid=(b, N_PAIRS, s // t),
        in_specs=[qblk(0),
                  pl.BlockSpec((1, s, LANES), lambda bi, p, i: (bi, 0, p)),
                  pl.BlockSpec((1, 1, s // t, LANES, t), lambda bi, p, i: (bi, p, 0, 0, 0)),
                  pl.BlockSpec((1, t, LANES), lambda bi, p, i: (bi, i, 0)),
                  pl.BlockSpec((1, s, LANES), lambda bi, p, i: (bi, 0, 0)),
                  qblk(U_FO)],
        out_specs=qblk(0),
        out_shape=jax.ShapeDtypeStruct((b, s, W_ATT), BF16),
        compiler_params=_cparams(("parallel", "parallel", "arbitrary")),
        name="fox_attention",
    )(qn, kn, vt, aq, ak, z3)


def _sb_kernel(q_ref, k_ref, vt_ref, o_ref, *, t):
    i = pl.program_id(2)
    q = q_ref[0].astype(F32) * (HEAD_DIM ** -0.5)
    krow = lax.broadcasted_iota(jnp.int32, (t, t), 0)
    qcol = lax.broadcasted_iota(jnp.int32, (t, t), 1)
    strict = krow < qcol
    after = (qcol > krow).astype(BF16)
    qms = [jnp.where(_half_mask((t, LANES), half), q, 0.0).astype(BF16) for half in range(2)]

    def local(j, diag):
        ks = pl.multiple_of(j * t, t)
        k = k_ref[0, pl.ds(ks, t), :]
        out = []
        for half in range(2):
            z = _nt_dot(k, qms[half])
            lz = _log_sigmoid(z)
            l1m = lz - z
            if diag:
                l1m = jnp.where(strict, l1m, 0.0)
            hi = l1m.astype(BF16)
            lo = (l1m - hi.astype(F32)).astype(BF16)
            suffix = (jnp.dot(after, hi, preferred_element_type=F32)
                      + jnp.dot(after, lo, preferred_element_type=F32))
            out.append((lz + suffix, jnp.sum(l1m, axis=0, keepdims=True)))
        return out

    def finish(j, loc, carry, keep):
        new = []
        for half in range(2):
            logw, colsum = loc[half]
            rsum, acc = carry[half]
            a = jnp.exp(logw + rsum)
            if keep is not None:
                a = jnp.where(keep, a, 0.0)
            pv = jnp.dot(vt_ref[0, 0, j, HEAD_DIM * half:HEAD_DIM * (half + 1), :], a.astype(BF16),
                         preferred_element_type=F32)
            new.append((rsum + colsum, acc + pv))
        return tuple(new)

    prev = jnp.maximum(i - 1, 0)
    loc_diag = local(i, True)
    loc_prev = local(prev, False)
    zero = (jnp.zeros((1, t), F32), jnp.zeros((HEAD_DIM, t), F32))
    carry = finish(i, loc_diag, (zero, zero), strict)
    carry = finish(prev, loc_prev, carry, i > 0)

    def more(c):
        n, ((r0, _), (r1, _)) = c
        return (n < i) & (jnp.maximum(jnp.max(r0), jnp.max(r1)) > SB_UNDERFLOW)

    def body(c):
        n, carry = c
        j = i - 1 - n
        return n + 1, finish(j, local(j, False), carry, None)

    _, ((_, acc0), (_, acc1)) = lax.while_loop(more, body, (jnp.int32(1), carry))
    o_ref[0] = _pair_out(acc0, acc1).astype(o_ref.dtype)


def _sb(z3, vt, t):
    b, s, _ = z3.shape
    return pl.pallas_call(
        functools.partial(_sb_kernel, t=t),
        grid=(b, N_PAIRS, s // t),
        in_specs=[pl.BlockSpec((1, t, LANES), lambda bi, p, i: (bi, i, U_SQ + p)),
                  pl.BlockSpec((1, s, LANES), lambda bi, p, i: (bi, 0, U_SK + p)),
                  pl.BlockSpec((1, 1, s // t, LANES, t), lambda bi, p, i: (bi, p, 0, 0, 0))],
        out_specs=pl.BlockSpec((1, t, LANES), lambda bi, p, i: (bi, i, p)),
        out_shape=jax.ShapeDtypeStruct((b, s, W_ATT), BF16),
        compiler_params=_cparams(("parallel", "parallel", "arbitrary")),
        name="stickbreak_attention",
    )(z3, z3, vt)


def _t5_bucket(n):
    max_exact = N_BUCKETS // 2
    nf = jnp.maximum(n, 1).astype(F32)
    large = max_exact + (jnp.log(nf / max_exact) / math.log(MAX_DISTANCE / max_exact)
                         * (N_BUCKETS - max_exact)).astype(jnp.int32)
    large = jnp.minimum(large, N_BUCKETS - 1)
    return jnp.where(n < max_exact, n, large)


def _bias_tiles_kernel(relb_ref, o_ref, *, t):
    h = pl.program_id(0)
    krow = lax.broadcasted_iota(jnp.int32, (t, t), 0)
    qcol = lax.broadcasted_iota(jnp.int32, (t, t), 1)
    o_ref[0, 0] = jnp.full((t, t), relb_ref[N_BUCKETS - 1, h] * LOG2E, F32)
    for slot, shift in ((1, t), (2, 0)):
        bucket = _t5_bucket(jnp.maximum(qcol - krow + shift, 0))
        val = jnp.full((t, t), relb_ref[0, h], F32)
        for k in range(1, N_BUCKETS):
            val = jnp.where(bucket == k, relb_ref[k, h], val)
        o_ref[0, slot] = val * LOG2E


def _bias_tiles(rel_bias, t):
    assert t >= MAX_DISTANCE
    return pl.pallas_call(
        functools.partial(_bias_tiles_kernel, t=t),
        grid=(N_HEADS,),
        in_specs=[pl.BlockSpec(memory_space=pltpu.SMEM)],
        out_specs=pl.BlockSpec((1, 3, t, t), lambda h: (h, 0, 0, 0)),
        out_shape=jax.ShapeDtypeStruct((N_HEADS, 3, t, t), F32),
        compiler_params=_cparams(("arbitrary",)),
        name="t5_bias_tiles",
    )(rel_bias)


def _dsa_kernel(dq_ref, qi_ref, zs_ref, kidx_ref, dk_ref, dvt_ref, bias_ref, o_ref, key_ref, hi_ref, lo_ref,
                madd_ref, *, t, n_sel):
    i = pl.program_id(1)
    nch = i + 1
    krow = lax.broadcasted_iota(jnp.int32, (t, t), 0)
    qcol = lax.broadcasted_iota(jnp.int32, (t, t), 1)
    idx_scale = (IDX_DIM ** -0.5) * (IDX_HEADS ** -0.5)

    zst = jnp.transpose(zs_ref[0])
    qi = qi_ref[0].astype(F32)
    qih, wih = [], []
    for h in range(IDX_HEADS):
        blk = qi[:, (h // 2) * LANES:(h // 2 + 1) * LANES]
        qih.append(jnp.where(_half_mask((t, LANES), h % 2), blk, 0.0).astype(BF16))
        wih.append(zst[SM_DIW + h:SM_DIW + h + 1, :] * idx_scale)

    def score_chunk(j, _):
        ks = pl.multiple_of(j * t, t)
        kc = kidx_ref[0, pl.ds(ks, t), :]
        sc = jnp.zeros((t, t), F32)
        for h in range(IDX_HEADS):
            sc = sc + jnp.maximum(_nt_dot(kc, qih[h]), 0.0) * wih[h]
        sc = jnp.where(sc == 0.0, 0.0, sc)
        sc = jnp.where(j * t + krow <= i * t + qcol, sc, -jnp.inf)
        bits = pltpu.bitcast(sc, jnp.int32)
        key = jnp.where(bits < 0, bits ^ jnp.int32(0x7FFFFFFF), bits)
        key_ref[j] = key
        hi_ref[j] = (key >> 16).astype(jnp.int16)
        return 0

    lax.fori_loop(0, nch, score_chunk, 0)

    i16 = jnp.int16
    rows16 = 16
    lowest = jnp.full((t, t), -32768, i16)
    npairs = (nch + 1) // 2

    @pl.when(nch % 2 == 1)
    def _():
        hi_ref[nch] = lowest
        lo_ref[nch] = lowest

    def count16(ref, pred):
        def body(j2, c):
            pieces = []
            for j in (2 * j2, 2 * j2 + 1):
                ind = jnp.where(pred(ref[j]), i16(1), i16(0))
                pieces += [ind[r * rows16:(r + 1) * rows16] for r in range(t // rows16)]
            while len(pieces) > 1:
                pieces = [a + b for a, b in zip(pieces[0::2], pieces[1::2])]
            return c + pieces[0]
        per_lane = lax.fori_loop(0, npairs, body, jnp.zeros((rows16, t), i16))
        return jnp.sum(per_lane.astype(F32), axis=0, keepdims=True)

    def search16(ref, base):
        def bit_step(n, thr):
            cand = thr + lax.shift_left(jnp.int32(1), 15 - n)
            c16 = cand.astype(i16)
            cnt = base + count16(ref, lambda v: v >= c16)
            return jnp.where(cnt >= float(n_sel), cand, thr)
        return lax.fori_loop(0, 16, bit_step, jnp.full((1, t), -32768, jnp.int32))

    thr_hi = search16(hi_ref, 0.0)
    thr_hi16 = thr_hi.astype(i16)
    above = count16(hi_ref, lambda v: v > thr_hi16)

    def low_halves(j, _):
        low = ((key_ref[j] & 0xFFFF) - 32768).astype(i16)
        lo_ref[j] = jnp.where(hi_ref[j] == thr_hi16, low, i16(-32768))
        return 0

    lax.fori_loop(0, nch, low_halves, 0)
    thr_lo = search16(lo_ref, above)
    thr = lax.shift_left(thr_hi, 16) | (thr_lo + 32768)

    def count_keys(pred):
        def body(j, c):
            return c + jnp.sum(jnp.where(pred(key_ref[j]), 1.0, 0.0), axis=0, keepdims=True)
        return lax.fori_loop(0, nch, body, jnp.zeros((1, t), F32))

    need = float(n_sel) - count_keys(lambda k: k > thr)

    upto = (qcol <= krow).astype(BF16)

    def mask_chunk(j, seen):
        key = key_ref[j]
        eq = key == thr
        rank = jnp.dot(upto, jnp.where(eq, 1.0, 0.0).astype(BF16), preferred_element_type=F32) + seen
        sel = (key > thr) | (eq & (rank <= need))
        sel = sel & (j * t + krow <= i * t + qcol)
        madd_ref[j] = jnp.where(sel, 0.0, NEG)
        return rank[t - 1:t, :]

    lax.fori_loop(0, nch, mask_chunk, jnp.zeros((1, t), F32))

    for pair in range(N_PAIRS):
        lo, hi_ = pair * LANES, (pair + 1) * LANES
        qp = dq_ref[0, :, lo:hi_].astype(F32) * (HEAD_DIM ** -0.5 * LOG2E)
        qms = [jnp.where(_half_mask((t, LANES), half), qp, 0.0).astype(BF16) for half in range(2)]

        def scores(j, pair=pair, lo=lo, hi_=hi_, qms=qms):
            ks = pl.multiple_of(j * t, t)
            slot = jnp.clip(j - i + 2, 0, 2)
            k = dk_ref[0, pl.ds(ks, t), lo:hi_]
            madd = madd_ref[j]
            return tuple(_nt_dot(k, qms[half]) + bias_ref[2 * pair + half, slot] + madd for half in range(2))

        def weighted_values(j, half, p, pair=pair):
            return jnp.dot(dvt_ref[0, pair, j, HEAD_DIM * half:HEAD_DIM * (half + 1), :], p,
                           preferred_element_type=F32)

        o0, o1 = _online_softmax_pair(i, scores, weighted_values, None, t)
        o_ref[0, :, lo:hi_] = _pair_out(o0, o1).astype(o_ref.dtype)


def _dsa(z3, zs3, dk, dvt, bias, t):
    b, s, _ = z3.shape
    n_sel = min(TOPK_MAX, s // 4)
    nt = s // t
    return pl.pallas_call(
        functools.partial(_dsa_kernel, t=t, n_sel=n_sel),
        grid=(b, s // t),
        in_specs=[pl.BlockSpec((1, t, W_ATT), lambda bi, i: (bi, i, U_DQ // 4)),
                  pl.BlockSpec((1, t, 2 * LANES), lambda bi, i: (bi, i, U_DIQ // 2)),
                  pl.BlockSpec((1, t, LANES), lambda bi, i: (bi, i, 0)),
                  pl.BlockSpec((1, s, LANES), lambda bi, i: (bi, 0, U_DIK)),
                  pl.BlockSpec((1, s, W_ATT), lambda bi, i: (bi, 0, 0)),
                  pl.BlockSpec((1, N_PAIRS, s // t, LANES, t), lambda bi, i: (bi, 0, 0, 0, 0)),
                  pl.BlockSpec((N_HEADS, 3, t, t), lambda bi, i: (0, 0, 0, 0))],
        out_specs=pl.BlockSpec((1, t, W_ATT), lambda bi, i: (bi, i, 0)),
        out_shape=jax.ShapeDtypeStruct((b, s, W_ATT), BF16),
        scratch_shapes=[pltpu.VMEM((nt, t, t), jnp.int32), pltpu.VMEM((nt + nt % 2, t, t), jnp.int16),
                        pltpu.VMEM((nt + nt % 2, t, t), jnp.int16), pltpu.VMEM((nt, t, t), F32)],
        compiler_params=_cparams(("parallel", "arbitrary")),
        name="dsa_attention",
    )(z3, z3, zs3, z3, dk, dvt, bias)


def _merge_kernel(yf_ref, ys_ref, yd_ref, wf_ref, ws_ref, wd_ref, g0_ref, g1_ref, g2_ref, o_ref):
    def branch(y_ref, w_ref, g_ref):
        proj = jnp.dot(y_ref[...], w_ref[...].astype(BF16), preferred_element_type=F32)
        gate = 0.5 * jnp.tanh(0.5 * g_ref[...].astype(F32)) + 0.5
        return gate * proj

    o_ref[...] = (branch(yf_ref, wf_ref, g0_ref) + branch(ys_ref, ws_ref, g1_ref)
                  + branch(yd_ref, wd_ref, g2_ref)).astype(o_ref.dtype)


def _merge(yf, ys, yd, wf, ws, wd, z, d, tm, tn):
    m = yf.shape[0]
    yblk = pl.BlockSpec((tm, W_ATT), lambda i, j: (i, 0))
    wblk = pl.BlockSpec((W_ATT, tn), lambda i, j: (0, j))
    gblk = lambda g: pl.BlockSpec((tm, tn), lambda i, j: (i, (U_GATES * LANES + g * d) // tn + j))
    return pl.pallas_call(
        _merge_kernel,
        grid=(m // tm, d // tn),
        in_specs=[yblk, yblk, yblk, wblk, wblk, wblk, gblk(0), gblk(1), gblk(2)],
        out_specs=pl.BlockSpec((tm, tn), lambda i, j: (i, j)),
        out_shape=jax.ShapeDtypeStruct((m, d), BF16),
        compiler_params=_cparams(("parallel", "arbitrary")),
        name="branch_merge",
    )(yf, ys, yd, wf, ws, wd, z, z, z)


def _matmul_res_kernel(a_ref, w_ref, r_ref, o_ref):
    o_ref[...] = r_ref[...] + jnp.dot(a_ref[...], w_ref[...].astype(BF16), preferred_element_type=F32)


def _matmul_res(a, w, res, tm, tn):
    m, k = a.shape
    n = w.shape[1]
    return pl.pallas_call(
        _matmul_res_kernel,
        grid=(m // tm, n // tn),
        in_specs=[pl.BlockSpec((tm, k), lambda i, j: (i, 0)), pl.BlockSpec((k, tn), lambda i, j: (0, j)),
                  pl.BlockSpec((tm, tn), lambda i, j: (i, j))],
        out_specs=pl.BlockSpec((tm, tn), lambda i, j: (i, j)),
        out_shape=jax.ShapeDtypeStruct((m, n), F32),
        compiler_params=_cparams(("parallel", "arbitrary")),
        name="out_proj_residual",
    )(a, w, res)


def _row_copy(src_hbm, row, dst_vmem, slot, sem):
    return pltpu.make_async_copy(src_hbm.at[pl.ds(row, 1)], dst_vmem.at[pl.ds(slot, 1)], sem)


def _ffn_kernel(te_ref, nv_ref, src_ref, rows_ref, x_ref, wg_ref, wu_ref, wd_ref, *rest, grouped, tm, nf):
    ti = pl.program_id(0)
    f = pl.program_id(1)
    if grouped:
        o_ref, gbuf, xs, sem = rest
    else:
        (o_ref,) = rest

    @pl.when(f == 0)
    def _():
        o_ref[...] = jnp.zeros_like(o_ref)

    if grouped:
        steps = max(k for k in range(1, nf) if tm % k == 0)
        per = tm // steps

        def issue_rows(tile, lo, n):
            def issue(r, _):
                _row_copy(x_ref, src_ref[tile * tm + lo + r], gbuf, lo + r, sem).start()
                return 0
            lax.fori_loop(0, n, issue, 0, unroll=ISSUE_UNROLL)

        @pl.when((ti == 0) & (f == 0))
        def _():
            issue_rows(0, 0, tm)

        @pl.when((f == 0) & (ti < nv_ref[0]))
        def _():
            def wait(r, _):
                _row_copy(x_ref, 0, gbuf, r, sem).wait()
                return 0
            lax.fori_loop(0, tm, wait, 0, unroll=ISSUE_UNROLL)
            xs[...] = gbuf[...].astype(BF16)

        @pl.when((f > 0) & (f <= steps) & (ti + 1 < nv_ref[0]))
        def _():
            issue_rows(ti + 1, (f - 1) * per, per)

    def swiglu(x, wgate, wup, wdown):
        g = jnp.dot(x, wgate, preferred_element_type=F32)
        u = jnp.dot(x, wup, preferred_element_type=F32)
        a = (g * jax.nn.sigmoid(g) * u).astype(BF16)
        return jnp.dot(a, wdown, preferred_element_type=F32)

    def whole_tile(x):
        o_ref[...] += swiglu(x, wg_ref[0].astype(BF16), wu_ref[0].astype(BF16), wd_ref[0].astype(BF16))

    @pl.when(ti < nv_ref[0])
    def _():
        if not grouped:
            whole_tile(x_ref[...])
            return
        rows = rows_ref[ti]

        @pl.when(rows > tm - SUB_ROWS)
        def _():
            whole_tile(xs[...])

        @pl.when(rows <= tm - SUB_ROWS)
        def _():
            for lo in range(0, tm, SUB_ROWS):
                @pl.when(lo < rows)
                def _(lo=lo):
                    o_ref[lo:lo + SUB_ROWS, :] += swiglu(
                        xs[lo:lo + SUB_ROWS, :], wg_ref[0].astype(BF16), wu_ref[0].astype(BF16),
                        wd_ref[0].astype(BF16))


def _ffn(x, wg, wu, wd, tile_expert, n_valid, tm, fc, src=None, tile_rows=None):
    grouped = src is not None
    d = x.shape[1]
    p = src.shape[0] if grouped else x.shape[0]
    nf = wg.shape[2] // fc

    def chunk(ti, f, nv):
        return jnp.where(ti < nv[0], f, nf - 1)

    wspecs = [pl.BlockSpec((1, d, fc), lambda ti, f, te, nv, sr, rw: (te[ti], 0, chunk(ti, f, nv))),
              pl.BlockSpec((1, d, fc), lambda ti, f, te, nv, sr, rw: (te[ti], 0, chunk(ti, f, nv))),
              pl.BlockSpec((1, fc, d), lambda ti, f, te, nv, sr, rw: (te[ti], chunk(ti, f, nv), 0))]
    scratch = []
    if grouped:
        in_specs = [pl.BlockSpec(memory_space=pl.ANY)] + wspecs
        args = [x, wg, wu, wd]
        scratch += [pltpu.VMEM((tm, d), F32), pltpu.VMEM((tm, d), BF16), pltpu.SemaphoreType.DMA(())]
    else:
        in_specs = [pl.BlockSpec((tm, d), lambda ti, f, te, nv, sr, rw: (ti, 0))] + wspecs
        args = [x, wg, wu, wd]
        src = tile_rows = jnp.zeros((1,), jnp.int32)
    grid_spec = pltpu.PrefetchScalarGridSpec(
        num_scalar_prefetch=4,
        grid=(p // tm, nf),
        in_specs=in_specs,
        out_specs=pl.BlockSpec((tm, d), lambda ti, f, te, nv, sr, rw: (ti, 0),
                               **({"pipeline_mode": pl.Buffered(1)} if grouped else {})),
        scratch_shapes=scratch,
    )
    return pl.pallas_call(
        functools.partial(_ffn_kernel, grouped=grouped, tm=tm, nf=nf),
        grid_spec=grid_spec,
        out_shape=jax.ShapeDtypeStruct((p, d), F32),
        compiler_params=_cparams(("arbitrary", "arbitrary")),
        name="grouped_swiglu" if grouped else "dense_swiglu",
    )(tile_expert, n_valid, src, tile_rows, *args)


def _combine_kernel(pos_ref, x_ref, info_ref, y_ref, g_ref, o_ref, buf, sem, *, tt, n, norm):
    i = pl.program_id(0)

    def issue_tile(tile, slot):
        def issue(r, _):
            for k in range(2):
                _row_copy(y_ref, pos_ref[k * n + tile * tt + r], buf.at[slot, k], r, sem.at[slot]).start()
            return 0
        lax.fori_loop(0, tt, issue, 0, unroll=ISSUE_UNROLL)

    @pl.when(i == 0)
    def _():
        issue_tile(0, 0)

    @pl.when(i + 1 < pl.num_programs(0))
    def _():
        issue_tile(i + 1, (i + 1) % 2)

    slot = i % 2

    def wait(r, _):
        for k in range(2):
            _row_copy(y_ref, 0, buf.at[slot, k], r, sem.at[slot]).wait()
        return 0

    lax.fori_loop(0, tt, wait, 0)
    info = info_ref[...]
    o = x_ref[...] + (info[:, 0:1] * buf[slot, 0] + info[:, 1:2] * buf[slot, 1])
    if norm:
        o = o * lax.rsqrt(jnp.mean(o * o, axis=-1, keepdims=True) + EPS) * g_ref[...]
    o_ref[...] = o


def _combine(x2, info, y, pos, tt, gain=None):
    n, d = x2.shape
    norm = gain is not None
    grid_spec = pltpu.PrefetchScalarGridSpec(
        num_scalar_prefetch=1,
        grid=(n // tt,),
        in_specs=[pl.BlockSpec((tt, d), lambda i, ps: (i, 0)), pl.BlockSpec((tt, LANES), lambda i, ps: (i, 0)),
                  pl.BlockSpec(memory_space=pl.ANY), pl.BlockSpec((1, d), lambda i, ps: (0, 0))],
        out_specs=pl.BlockSpec((tt, d), lambda i, ps: (i, 0)),
        scratch_shapes=[pltpu.VMEM((2, 2, tt, d), F32), pltpu.SemaphoreType.DMA((2,))],
    )
    return pl.pallas_call(
        functools.partial(_combine_kernel, tt=tt, n=n, norm=norm),
        grid_spec=grid_spec,
        out_shape=jax.ShapeDtypeStruct((n, d), F32),
        compiler_params=_cparams(("arbitrary",)),
        name="moe_combine",
    )(pos, x2, info, y, (gain if norm else jnp.ones((d,), F32)).reshape(1, d))


def _moe(x2, h, info, wg, wu, wd, tm, fc, out_gain=None):
    n = h.shape[0]
    e = wg.shape[0]
    eid = jnp.concatenate([info[:, 2], info[:, 3]]).astype(jnp.int32)
    tok = jnp.tile(jnp.arange(n, dtype=jnp.int32), 2)
    onehot = (eid[:, None] == jnp.arange(e, dtype=jnp.int32)[None, :]).astype(jnp.int32)
    rank = jnp.sum((jnp.cumsum(onehot, axis=0) - 1) * onehot, axis=1)
    counts = jnp.sum(onehot, axis=0)
    padded = ((counts + tm - 1) // tm) * tm
    ends = jnp.cumsum(padded)
    starts = ends - padded
    pos = starts[eid] + rank
    n_tiles = (2 * n) // tm + e
    p = n_tiles * tm
    src = jnp.zeros((p,), jnp.int32).at[pos].set(tok)
    n_valid = (ends[e - 1] // tm).astype(jnp.int32)
    tile_start = jnp.arange(n_tiles, dtype=jnp.int32) * tm
    tile_start = jnp.minimum(tile_start, (n_valid - 1) * tm)
    tile_expert = jnp.sum((tile_start[:, None] >= ends[None, :]).astype(jnp.int32), axis=1)
    tile_rows = jnp.clip((starts + counts)[tile_expert] - tile_start, 0, tm).astype(jnp.int32)
    y = _ffn(h, wg, wu, wd, tile_expert, n_valid.reshape(1), tm, fc, src=src, tile_rows=tile_rows)
    return _combine(x2, info, y, pos, _tile(n, 256), out_gain)


def _tile(total, want):
    t = min(total, want)
    assert total % t == 0
    return t


def kernel(x, w_in, b_forget, q_norm, k_norm, kv_norm, w_ukv, w_br_fox, w_br_sb, w_br_dsa, w_out, rel_bias, norm_mix, norm_ffn, w_ffn_gate, w_ffn_up, w_ffn_down, w_router, w_moe_gate, w_moe_up, w_moe_down, norm_final):
    b, s, d = x.shape
    m = b * s
    depth = w_in.shape[0]
    ta = _tile(s, 256)
    tm_norm = _tile(m, 512)
    tm = _tile(m, 1024)
    tn = 512
    fc = 256
    tm_moe = _tile(2 * m, 1024)

    bias = _bias_tiles(rel_bias, ta)
    x2 = x.reshape(m, d)
    pending = None
    for l in range(depth):
        if pending is None:
            h = _rmsnorm(x2, norm_mix[l], BF16, tm)
        else:
            x2, h = _add_rmsnorm(x2, pending, norm_mix[l], tm_norm)
            pending = None
        z, zs = _inproj(h, _relayout_w_in(w_in, l), tm, 2 * tn)
        z3 = z.reshape(b, s, Z_COLS)
        zs3 = zs.reshape(b, s, LANES)
        qn, kn, aq, ak, dk, fvt, svt, dvt = _prep(z3, zs3, q_norm[l], k_norm[l], kv_norm[l], b_forget[l],
                                                  w_ukv[l], ta)
        y_fox = _fox(qn, kn, fvt, aq, ak, z3, ta).reshape(m, W_ATT)
        y_sb = _sb(z3, svt, ta).reshape(m, W_ATT)
        y_dsa = _dsa(z3, zs3, dk, dvt, bias, ta).reshape(m, W_ATT)
        mix = _merge(y_fox, y_sb, y_dsa, w_br_fox[l], w_br_sb[l], w_br_dsa[l], z, d, tm, 2 * tn)
        x2 = _matmul_res(mix, w_out[l], x2, tm, 2 * tn)
        j = l // 2
        if l % 2 == 0:
            h = _rmsnorm(x2, norm_ffn[l], BF16, tm)
            n_tiles = m // tm
            pending = _ffn(h, w_ffn_gate[j][None], w_ffn_up[j][None], w_ffn_down[j][None],
                           jnp.zeros((n_tiles,), jnp.int32), jnp.full((1,), n_tiles, jnp.int32), tm, 2 * fc)
        else:
            h, info = _rmsnorm_router(x2, norm_ffn[l], w_router[j], tm)
            last = l == depth - 1
            x2 = _moe(x2, h, info, w_moe_gate[j], w_moe_up[j], w_moe_down[j], tm_moe, 2 * fc,
                      norm_final if last else None)
    if pending is not None:
        x2 = _rmsnorm(x2 + pending, norm_final, F32, tm_norm)
    return x2.reshape(b, s, d)
```
